```python
import math
import jax, jax.numpy as jnp
from jax import lax
import numpy as np

D_MODEL = 1024
BATCH = 2
SEQ = 8192
DEPTH = 2
DEC_BATCH = 32
DEC_SEQ = 16
PAST_LEN = 1024

CHUNK = 64
Q_BLOCK = 128
N_EVEN = (DEPTH + 1) // 2
N_ODD = DEPTH // 2

POOL_WIDTH = 512
POOL_GROUPS = 4
POOL_GC = POOL_WIDTH // POOL_GROUPS
POOL_WINDOWS = (2, 4, 8, 16)
POOL_HIST = max(POOL_WINDOWS) - 1

FOX_HEADS = 8
FOX_DH = 64
FOX_WIDTH = FOX_HEADS * FOX_DH
FORGET_BIAS_INIT = 3.0

DIFF_HEADS = 4
DIFF_DH = 64
DIFF_VD = 2 * DIFF_DH
DIFF_QK = DIFF_HEADS * 2 * DIFF_DH
DIFF_WIDTH = DIFF_HEADS * DIFF_VD

CONV_CH = 512
CONV_K = 3

MIX_WIDTH = POOL_WIDTH + FOX_WIDTH
EVEN_IN = POOL_WIDTH + 3 * FOX_WIDTH + FOX_HEADS
ODD_IN = 2 * DIFF_QK + DIFF_WIDTH + 3 * CONV_CH

D_FF = 4 * D_MODEL

N_MEM = 256
MEM_HEADS = 4
MEM_DH = D_MODEL // MEM_HEADS
MEM_WIDTH = MEM_HEADS * MEM_DH

REL_BUCKETS = 32
REL_MAX_DIST = 128

DN_ALPHA = (2 * DEPTH) ** 0.25
DN_BETA = (8 * DEPTH) ** -0.25
LN_EPS = 1e-5
RMS_EPS = 1e-5
NEG = -1e30

kernel_name = 'hybrid_streaming_encoder_step'


def _layer_norm(x, g, b):
    xf = x.astype(jnp.float32)
    mu = jnp.mean(xf, axis=-1, keepdims=True)
    var = jnp.mean(jnp.square(xf - mu), axis=-1, keepdims=True)
    return ((xf - mu) * lax.rsqrt(var + LN_EPS) * g.astype(jnp.float32) + b.astype(jnp.float32)).astype(x.dtype)


def _rms_norm(x, g):
    xf = x.astype(jnp.float32)
    return (xf * lax.rsqrt(jnp.mean(xf * xf, axis=-1, keepdims=True) + RMS_EPS) * g.astype(jnp.float32)).astype(x.dtype)


def _t5_bucket(rel):
    nb = REL_BUCKETS // 2
    max_exact = nb // 2
    ret = jnp.where(rel > 0, nb, 0)
    n = jnp.abs(rel)
    nf = jnp.maximum(n, 1).astype(jnp.float32)
    large = max_exact + (jnp.log(nf / max_exact) / math.log(REL_MAX_DIST / max_exact) * (nb - max_exact)).astype(jnp.int32)
    large = jnp.minimum(large, nb - 1)
    return ret + jnp.where(n < max_exact, n, large)


def _sweep(fn, q_pos, *q_arrays):
    Q = q_pos.shape[0]
    if Q <= Q_BLOCK:
        return fn(q_pos, *q_arrays)
    nb = Q // Q_BLOCK

    def blk(i):
        s = i * Q_BLOCK
        return fn(lax.dynamic_slice_in_dim(q_pos, s, Q_BLOCK),
                  *[lax.dynamic_slice_in_dim(a, s, Q_BLOCK, axis=1) for a in q_arrays])

    out = lax.map(blk, jnp.arange(nb))
    out = jnp.moveaxis(out, 0, 1)
    return out.reshape((out.shape[0], Q) + out.shape[3:])


def _fox_attend(q_pos, q, Fq, k_pos, k, v, Fk):
    s = jnp.einsum('bqhd,bkhd->bhqk', q, k).astype(jnp.float32) * (FOX_DH ** -0.5)
    s = s + (jnp.transpose(Fq, (0, 2, 1))[..., :, None] - jnp.transpose(Fk, (0, 2, 1))[..., None, :])
    mask = k_pos[None, :] <= q_pos[:, None]
    s = jnp.where(mask, s, NEG)
    p = jax.nn.softmax(s, axis=-1)
    return jnp.einsum('bhqk,bkhd->bqhd', p.astype(v.dtype), v)


def _diff_attend(q_pos, q, k_pos, k, v, lam, rel_table):
    s = jnp.einsum('bqhmd,bkhmd->bmhqk', q, k).astype(jnp.float32) * (DIFF_DH ** -0.5)
    bias = rel_table[_t5_bucket(k_pos[None, :] - q_pos[:, None])]
    s = s + jnp.transpose(bias, (2, 0, 1)).astype(jnp.float32)[None, None]
    mask = (k_pos[None, :] // CHUNK) <= (q_pos[:, None] // CHUNK)
    s = jnp.where(mask, s, NEG)
    p = jax.nn.softmax(s, axis=-1)
    a = p[:, 0] - lam * p[:, 1]
    return jnp.einsum('bhqk,bkhe->bqhe', a.astype(v.dtype), v)


def _pool_mix(u, hist, pos0, w_mix, scale):
    B, T, _ = u.shape
    u_ext = jnp.concatenate([hist, u], axis=1)
    c = jnp.cumsum(u_ext.astype(jnp.float32), axis=1)
    c = jnp.concatenate([jnp.zeros_like(c[:, :1]), c], axis=1)
    pos = pos0 + jnp.arange(T)
    uf = u.astype(jnp.float32)
    groups = []
    for g, w in enumerate(POOL_WINDOWS):
        sl = slice(g * POOL_GC, (g + 1) * POOL_GC)
        e = POOL_HIST + 1
        wsum = c[:, e:e + T, sl] - c[:, e - w:e - w + T, sl]
        cnt = jnp.minimum(w, pos + 1).astype(jnp.float32)
        groups.append(wsum / cnt[None, :, None] - uf[:, :, sl])
    d = jnp.stack(groups, axis=2).astype(u.dtype)
    y = jnp.einsum('btgc,gce->btge', d, w_mix).reshape(B, T, POOL_WIDTH) * scale
    return y, u_ext[:, -POOL_HIST:]


def _short_conv(bg, cg, h, hist, w):
    z = cg * h
    z_ext = jnp.concatenate([hist, z], axis=1)
    T = z.shape[1]
    y = z_ext[:, 0:T] * w[0] + z_ext[:, 1:T + 1] * w[1] + z_ext[:, 2:T + 2] * w[2]
    return bg * y, z_ext[:, -(CONV_K - 1):]


def _mem_attend(x, mk, mv, wq, wo):
    B, T, _ = x.shape
    q = (x @ wq).reshape(B, T, MEM_HEADS, MEM_DH)
    s = jnp.einsum('bqhd,bkhd->bhqk', q, mk).astype(jnp.float32) * (MEM_DH ** -0.5)
    p = jax.nn.softmax(s, axis=-1).astype(mv.dtype)
    o = jnp.einsum('bhqk,bkhd->bqhd', p, mv).reshape(B, T, MEM_WIDTH)
    return o @ wo


def _trunk(x, mem_k, mem_v, pool_h, fk_h, fv_h, flf_h, dk_h, dv_h, conv_h, past, params):
    (w_in_even, b_forget, w_pool_mix, pool_scale, w_out_even,
     w_in_odd, diff_lambda, diff_subln_g, conv_w, w_out_odd, rel_bias_table,
     w_mem_q, w_mem_o, w_ff1, w_ff2, ln_g, ln_b) = params
    B, T, _ = x.shape
    f32 = jnp.float32
    k_pos = jnp.arange(past + T)
    q_pos = past + jnp.arange(T)
    n_pool, n_fk, n_fv, n_flf, n_dk, n_dv, n_conv = [], [], [], [], [], [], []
    for l in range(DEPTH):
        i = l // 2
        if l % 2 == 0:
            proj = x @ w_in_even[i]
            o = POOL_WIDTH
            u = proj[..., :o]
            q = proj[..., o:o + FOX_WIDTH].reshape(B, T, FOX_HEADS, FOX_DH)
            k = proj[..., o + FOX_WIDTH:o + 2 * FOX_WIDTH].reshape(B, T, FOX_HEADS, FOX_DH)
            v = proj[..., o + 2 * FOX_WIDTH:o + 3 * FOX_WIDTH].reshape(B, T, FOX_HEADS, FOX_DH)
            f_logit = proj[..., o + 3 * FOX_WIDTH:]
            logf = jax.nn.log_sigmoid(f_logit.astype(f32) + b_forget[i].astype(f32))
            pool_y, ph = _pool_mix(u, pool_h[i], past, w_pool_mix[i], pool_scale[i])
            kc = jnp.concatenate([fk_h[i], k], axis=1)
            vc = jnp.concatenate([fv_h[i], v], axis=1)
            F = jnp.cumsum(jnp.concatenate([flf_h[i].astype(f32), logf], axis=1), axis=1)
            fox_y = _sweep(lambda qp, qb, fb: _fox_attend(qp, qb, fb, k_pos, kc, vc, F),
                           q_pos, q, F[:, past:])
            mix = jnp.concatenate([pool_y, fox_y.reshape(B, T, FOX_WIDTH)], axis=-1) @ w_out_even[i]
            n_pool.append(ph); n_fk.append(k); n_fv.append(v); n_flf.append(logf)
        else:
            proj = x @ w_in_odd[i]
            q = proj[..., :DIFF_QK].reshape(B, T, DIFF_HEADS, 2, DIFF_DH)
            k = proj[..., DIFF_QK:2 * DIFF_QK].reshape(B, T, DIFF_HEADS, 2, DIFF_DH)
            o = 2 * DIFF_QK
            v = proj[..., o:o + DIFF_WIDTH].reshape(B, T, DIFF_HEADS, DIFF_VD)
            o = o + DIFF_WIDTH
            bg = proj[..., o:o + CONV_CH]
            cg = proj[..., o + CONV_CH:o + 2 * CONV_CH]
            hh = proj[..., o + 2 * CONV_CH:o + 3 * CONV_CH]
            lam_init = 0.8 - 0.6 * math.exp(-0.3 * l)
            lp = diff_lambda[i].astype(f32)
            lam = jnp.exp(jnp.sum(lp[0] * lp[1])) - jnp.exp(jnp.sum(lp[2] * lp[3])) + lam_init
            kc = jnp.concatenate([dk_h[i], k], axis=1)
            vc = jnp.concatenate([dv_h[i], v], axis=1)
            att = _sweep(lambda qp, qb: _diff_attend(qp, qb, k_pos, kc, vc, lam, rel_bias_table),
                         q_pos, q)
            att = _rms_norm(att, diff_subln_g[i]) * (1.0 - lam_init)
            conv_y, ch = _short_conv(bg, cg, hh, conv_h[i], conv_w[i])
            mix = jnp.concatenate([att.reshape(B, T, DIFF_WIDTH), conv_y], axis=-1) @ w_out_odd[i]
            n_dk.append(k); n_dv.append(v); n_conv.append(ch)
        x = _layer_norm(DN_ALPHA * x + mix, ln_g[l, 0], ln_b[l, 0])
        x = _layer_norm(DN_ALPHA * x + _mem_attend(x, mem_k[l], mem_v[l], w_mem_q[l], w_mem_o[l]),
                        ln_g[l, 1], ln_b[l, 1])
        h = jnp.square(jax.nn.relu(x @ w_ff1[l]))
        x = _layer_norm(DN_ALPHA * x + h @ w_ff2[l], ln_g[l, 2], ln_b[l, 2])
    return (x, jnp.stack(n_pool), jnp.stack(n_fk), jnp.stack(n_fv), jnp.stack(n_flf),
            jnp.stack(n_dk), jnp.stack(n_dv), jnp.stack(n_conv))


def setup_inputs(seed: int = 0) -> dict:
    key = jax.random.key(seed)
    ks = iter(jax.random.split(key, 40))

    def nrm(shape, scale=1.0):
        return jax.random.normal(next(ks), shape, jnp.float32) * scale

    return {
        'x_prompt': nrm((BATCH, SEQ, D_MODEL)),
        'x_sample': nrm((DEC_BATCH, DEC_SEQ, D_MODEL)),
        'state_pool': nrm((N_EVEN, DEC_BATCH, POOL_HIST, POOL_WIDTH)),
        'cache_fox_k': nrm((N_EVEN, DEC_BATCH, PAST_LEN, FOX_HEADS, FOX_DH)),
        'cache_fox_v': nrm((N_EVEN, DEC_BATCH, PAST_LEN, FOX_HEADS, FOX_DH)),
        'cache_fox_logf': jax.nn.log_sigmoid(FORGET_BIAS_INIT + nrm((N_EVEN, DEC_BATCH, PAST_LEN, FOX_HEADS))),
        'cache_diff_k': nrm((N_ODD, DEC_BATCH, PAST_LEN, DIFF_HEADS, 2, DIFF_DH)),
        'cache_diff_v': nrm((N_ODD, DEC_BATCH, PAST_LEN, DIFF_HEADS, DIFF_VD)),
        'state_conv': nrm((N_ODD, DEC_BATCH, CONV_K - 1, CONV_CH)),
        'cache_mem_k': nrm((DEPTH, DEC_BATCH, N_MEM, MEM_HEADS, MEM_DH)),
        'cache_mem_v': nrm((DEPTH, DEC_BATCH, N_MEM, MEM_HEADS, MEM_DH)),
        'mem_prompt': nrm((BATCH, N_MEM, D_MODEL)),
        'w_in_even': nrm((N_EVEN, D_MODEL, EVEN_IN), D_MODEL ** -0.5),
        'b_forget': FORGET_BIAS_INIT + nrm((N_EVEN, FOX_HEADS), 0.1),
        'w_pool_mix': nrm((N_EVEN, POOL_GROUPS, POOL_GC, POOL_GC), POOL_GC ** -0.5),
        'pool_scale': 1.0 + nrm((N_EVEN, POOL_WIDTH), 0.1),
        'w_out_even': nrm((N_EVEN, MIX_WIDTH, D_MODEL), MIX_WIDTH ** -0.5 * DN_BETA),
        'w_in_odd': nrm((N_ODD, D_MODEL, ODD_IN), D_MODEL ** -0.5),
        'diff_lambda': nrm((N_ODD, 4, DIFF_DH), 0.1),
        'diff_subln_g': 1.0 + nrm((N_ODD, DIFF_VD), 0.1),
        'conv_w': nrm((N_ODD, CONV_K, CONV_CH), CONV_K ** -0.5),
        'w_out_odd': nrm((N_ODD, MIX_WIDTH, D_MODEL), MIX_WIDTH ** -0.5 * DN_BETA),
        'rel_bias_table': nrm((REL_BUCKETS, DIFF_HEADS), 0.5),
        'w_mem_q': nrm((DEPTH, D_MODEL, MEM_WIDTH), D_MODEL ** -0.5),
        'w_mem_k': nrm((DEPTH, D_MODEL, MEM_WIDTH), D_MODEL ** -0.5),
        'w_mem_v': nrm((DEPTH, D_MODEL, MEM_WIDTH), D_MODEL ** -0.5),
        'w_mem_o': nrm((DEPTH, MEM_WIDTH, D_MODEL), MEM_WIDTH ** -0.5 * DN_BETA),
        'w_ff1': nrm((DEPTH, D_MODEL, D_FF), D_MODEL ** -0.5),
        'w_ff2': nrm((DEPTH, D_FF, D_MODEL), D_FF ** -0.5 * DN_BETA),
        'ln_g': 1.0 + nrm((DEPTH, 3, D_MODEL), 0.05),
        'ln_b': nrm((DEPTH, 3, D_MODEL), 0.05),
    }


def reference(x_prompt, x_sample, state_pool, cache_fox_k, cache_fox_v, cache_fox_logf,
              cache_diff_k, cache_diff_v, state_conv, cache_mem_k, cache_mem_v, mem_prompt,
              w_in_even, b_forget, w_pool_mix, pool_scale, w_out_even,
              w_in_odd, diff_lambda, diff_subln_g, conv_w, w_out_odd, rel_bias_table,
              w_mem_q, w_mem_k, w_mem_v, w_mem_o, w_ff1, w_ff2, ln_g, ln_b):
    params = (w_in_even, b_forget, w_pool_mix, pool_scale, w_out_even,
              w_in_odd, diff_lambda, diff_subln_g, conv_w, w_out_odd, rel_bias_table,
              w_mem_q, w_mem_o, w_ff1, w_ff2, ln_g, ln_b)
    Bp = x_prompt.shape[0]
    dt = x_prompt.dtype
    p_mem_k = jnp.einsum('bmd,lde->lbme', mem_prompt, w_mem_k).reshape(DEPTH, Bp, N_MEM, MEM_HEADS, MEM_DH)
    p_mem_v = jnp.einsum('bmd,lde->lbme', mem_prompt, w_mem_v).reshape(DEPTH, Bp, N_MEM, MEM_HEADS, MEM_DH)
    (y_prompt, p_pool, p_fox_k, p_fox_v, p_fox_logf, p_diff_k, p_diff_v, p_conv) = _trunk(
        x_prompt, p_mem_k, p_mem_v,
        jnp.zeros((N_EVEN, Bp, POOL_HIST, POOL_WIDTH), dt),
        jnp.zeros((N_EVEN, Bp, 0, FOX_HEADS, FOX_DH), dt),
        jnp.zeros((N_EVEN, Bp, 0, FOX_HEADS, FOX_DH), dt),
        jnp.zeros((N_EVEN, Bp, 0, FOX_HEADS), jnp.float32),
        jnp.zeros((N_ODD, Bp, 0, DIFF_HEADS, 2, DIFF_DH), dt),
        jnp.zeros((N_ODD, Bp, 0, DIFF_HEADS, DIFF_VD), dt),
        jnp.zeros((N_ODD, Bp, CONV_K - 1, CONV_CH), dt),
        0, params)
    (y_sample, s_pool, s_fox_k, s_fox_v, s_fox_logf, s_diff_k, s_diff_v, s_conv) = _trunk(
        x_sample, cache_mem_k, cache_mem_v, state_pool, cache_fox_k, cache_fox_v, cache_fox_logf,
        cache_diff_k, cache_diff_v, state_conv, cache_fox_k.shape[2], params)
    return (y_prompt, y_sample,
            p_pool, p_fox_k, p_fox_v, p_fox_logf, p_diff_k, p_diff_v, p_conv, p_mem_k, p_mem_v,
            s_pool, s_fox_k, s_fox_v, s_fox_logf, s_diff_k, s_diff_v, s_conv)
```

```python
import functools
import math

import numpy as np
import jax
import jax.numpy as jnp
from jax import lax
from jax.experimental import pallas as pl
from jax.experimental.pallas import tpu as pltpu

F32 = jnp.float32
BF16 = jnp.bfloat16

D_MODEL = 1024
DEPTH = 2
CHUNK = 64
POOL_WIDTH = 512
POOL_GC = 128
POOL_WINDOWS = (2, 4, 8, 16)
POOL_HIST = 15
FOX_HEADS = 8
FOX_DH = 64
FOX_WIDTH = 512
DIFF_HEADS = 4
DIFF_DH = 64
DIFF_VD = 128
DIFF_QK = 512
DIFF_WIDTH = 512
CONV_CH = 512
CONV_K = 3
D_FF = 4096
N_MEM = 256
MEM_HEADS = 4
MEM_DH = 256
REL_BUCKETS = 32
REL_MAX_DIST = 128
DN_ALPHA = (2 * DEPTH) ** 0.25
LN_EPS = 1e-5
RMS_EPS = 1e-5
NEG = -1e30

LANES = 128
HIST_ROWS = 16
VMEM_LIMIT = 56 * 1024 * 1024


def _cparams(*sem):
    return pltpu.CompilerParams(dimension_semantics=sem, vmem_limit_bytes=VMEM_LIMIT)


def _dot(a, b):
    return jnp.dot(a, b, preferred_element_type=F32)


def _dot_nt(a, b):
    return lax.dot_general(a, b, (((1,), (1,)), ((), ())), preferred_element_type=F32)


def _layer_norm(y, g, b):
    mu = jnp.mean(y, axis=-1, keepdims=True)
    d = y - mu
    var = jnp.mean(d * d, axis=-1, keepdims=True)
    return d * lax.rsqrt(var + LN_EPS) * g + b


def _row_tile(rows):
    return min(512, rows)


def _proj_even_kernel(x_ref, w_ref, wf_ref, bf_ref,
                      u_ref, q_ref, k_ref, kb_ref, v_ref, vb_ref, lf_ref):
    xb = x_ref[...].astype(BF16)

    def mm(c):
        return _dot(xb, w_ref[:, c * 512:(c + 1) * 512])

    u_ref[...] = mm(0)
    q_ref[...] = (mm(1) * (FOX_DH ** -0.5)).astype(BF16)
    k = mm(2)
    k_ref[...] = k
    kb_ref[...] = k.astype(BF16)
    v = mm(3)
    v_ref[...] = v
    vb_ref[...] = v.astype(BF16)
    z = _dot(xb, wf_ref[...]) + bf_ref[...]
    lf_ref[...] = jnp.minimum(z, 0.0) - jnp.log1p(jnp.exp(-jnp.abs(z)))


def _proj_even(x2, w, wf, bf):
    rows = x2.shape[0]
    tm = _row_tile(rows)
    row = lambda n: pl.BlockSpec((tm, n), lambda i: (i, 0))
    full = lambda a: pl.BlockSpec(a.shape, lambda i: (0,) * a.ndim)
    f32o = jax.ShapeDtypeStruct((rows, 512), F32)
    bf16o = jax.ShapeDtypeStruct((rows, 512), BF16)
    return pl.pallas_call(
        _proj_even_kernel,
        grid=(rows // tm,),
        in_specs=[row(D_MODEL), full(w), full(wf), full(bf)],
        out_specs=[row(512)] * 6 + [row(LANES)],
        out_shape=[f32o, bf16o, f32o, bf16o, f32o, bf16o,
                   jax.ShapeDtypeStruct((rows, LANES), F32)],
        compiler_params=_cparams("parallel"),
        name="proj_even",
    )(x2, w, wf, bf)


def _proj_odd_kernel(x_ref, w_ref, q_ref, k_ref, kb_ref, v_ref, vb_ref, bg_ref, z_ref):
    xb = x_ref[...].astype(BF16)

    def mm(c):
        return _dot(xb, w_ref[:, c * 512:(c + 1) * 512])

    q_ref[...] = (mm(0) * (DIFF_DH ** -0.5)).astype(BF16)
    k = mm(1)
    k_ref[...] = k
    kb_ref[...] = k.astype(BF16)
    v = mm(2)
    v_ref[...] = v
    vb_ref[...] = v.astype(BF16)
    bg_ref[...] = mm(3)
    z_ref[...] = mm(4) * mm(5)


def _proj_odd(x2, w):
    rows = x2.shape[0]
    tm = _row_tile(rows)
    row = lambda n: pl.BlockSpec((tm, n), lambda i: (i, 0))
    f32o = jax.ShapeDtypeStruct((rows, 512), F32)
    bf16o = jax.ShapeDtypeStruct((rows, 512), BF16)
    return pl.pallas_call(
        _proj_odd_kernel,
        grid=(rows // tm,),
        in_specs=[row(D_MODEL), pl.BlockSpec(w.shape, lambda i: (0, 0))],
        out_specs=[row(512)] * 7,
        out_shape=[bf16o, f32o, bf16o, f32o, bf16o, f32o, f32o],
        compiler_params=_cparams("parallel"),
        name="proj_odd",
    )(x2, w)


def _mem_kv_kernel(x_ref, w_ref, o_ref):
    o_ref[0, 0] = _dot(x_ref[...].astype(BF16), w_ref[0, 0])


def _mem_kv(mem2, w_kv):
    rows = mem2.shape[0]
    return pl.pallas_call(
        _mem_kv_kernel,
        grid=(2, DEPTH),
        in_specs=[pl.BlockSpec((rows, D_MODEL), lambda a, l: (0, 0)),
                  pl.BlockSpec((1, 1, D_MODEL, D_MODEL), lambda a, l: (a, l, 0, 0))],
        out_specs=pl.BlockSpec((1, 1, rows, D_MODEL), lambda a, l: (a, l, 0, 0)),
        out_shape=jax.ShapeDtypeStruct((2, DEPTH, rows, D_MODEL), F32),
        compiler_params=_cparams("parallel", "parallel"),
        name="mem_kv",
    )(mem2, w_kv)


def _cumsum_kernel(x_ref, o_ref, carry_ref, *, rows):
    @pl.when(pl.program_id(1) == 0)
    def _():
        carry_ref[...] = jnp.zeros_like(carry_ref)

    r = lax.broadcasted_iota(jnp.int32, (LANES, LANES), 0)
    c = lax.broadcasted_iota(jnp.int32, (LANES, LANES), 1)
    tri = (r >= c).astype(F32)
    carry = carry_ref[...]
    for s in range(rows // LANES):
        blk = x_ref[0, s * LANES:(s + 1) * LANES, :]
        cs = jnp.dot(tri, blk, preferred_element_type=F32,
                     precision=lax.Precision.HIGHEST) + carry
        o_ref[0, s * LANES:(s + 1) * LANES, :] = cs
        carry = cs[LANES - 1:LANES, :]
    carry_ref[...] = carry


def _cumsum_time(x, rows):
    b, t, _ = x.shape
    spec = pl.BlockSpec((1, rows, LANES), lambda i, j: (i, j, 0))
    return pl.pallas_call(
        functools.partial(_cumsum_kernel, rows=rows),
        grid=(b, t // rows),
        in_specs=[spec],
        out_specs=spec,
        out_shape=jax.ShapeDtypeStruct(x.shape, F32),
        scratch_shapes=[pltpu.VMEM((1, LANES), F32)],
        compiler_params=_cparams("parallel", "arbitrary"),
        name="cumsum_time",
    )(x)


def _fox_kernel(q_ref, k_ref, v_ref, fq_ref, ft_ref, mask_ref, o_ref, acc_ref, m_ref,
                *, tq, tk, past):
    pair = pl.program_id(1)
    qi = pl.program_id(2)
    j_last = (past + qi * tq) // tk
    lane = lax.broadcasted_iota(jnp.int32, (1, LANES), 1)
    upper = lane >= FOX_DH
    q2 = q_ref[0]
    fblk = fq_ref[0]
    lane_q = lax.broadcasted_iota(jnp.int32, (tq, LANES), 1)
    qm, fq = [], []
    for hh in range(2):
        sel = upper if hh else jnp.logical_not(upper)
        qm.append(jnp.where(sel, q2, jnp.zeros_like(q2)))
        fq.append(jnp.sum(jnp.where(lane_q == 2 * pair + hh, fblk, 0.0),
                          axis=1, keepdims=True))
    m_ref[...] = jnp.full(m_ref.shape, NEG, F32)
    acc_ref[...] = jnp.zeros(acc_ref.shape, F32)

    def step(j, masked):
        start = pl.multiple_of(j * tk, tk)
        kb = k_ref[0, pl.ds(start, tk), :]
        vb = v_ref[0, pl.ds(start, tk), :]
        for hh in range(2):
            sel = upper if hh else jnp.logical_not(upper)
            s = _dot_nt(qm[hh], kb)
            fk = ft_ref[0, 0, hh:hh + 1, pl.ds(start, tk)]
            s = s + (fq[hh] - fk)
            if masked:
                s = s + mask_ref[...]
            m_old = m_ref[hh]
            m_new = jnp.maximum(m_old, jnp.max(s, axis=1, keepdims=True))
            p = jnp.exp(s - m_new)
            alpha = jnp.exp(m_old - m_new)
            va = jnp.where(sel, vb, jnp.ones_like(vb))
            acc_ref[hh] = alpha * acc_ref[hh] + _dot(p.astype(BF16), va)
            m_ref[hh] = m_new

    def body(j, carry):
        step(j, False)
        return carry

    lax.fori_loop(0, j_last, body, 0)
    step(j_last, True)

    a0 = acc_ref[0]
    a1 = acc_ref[1]
    o0 = a0 / pltpu.roll(a0, FOX_DH, 1)
    o1 = a1 / pltpu.roll(a1, FOX_DH, 1)
    o_ref[0] = jnp.where(upper, o1, o0).astype(o_ref.dtype)


def _fox_attention(qb, kb, vb, fcum, ft, past, tq, tk):
    b, t_q, _ = qb.shape
    t_k = kb.shape[1]
    nq = t_q // tq
    assert past % tk == 0 and t_k % tk == 0 and (tq == tk or nq == 1)
    ii = np.arange(tq)[:, None]
    jj = np.arange(tk)[None, :]
    mask = jnp.asarray(np.where(jj <= ii, 0.0, NEG), F32)
    return pl.pallas_call(
        functools.partial(_fox_kernel, tq=tq, tk=tk, past=past),
        grid=(b, FOX_HEADS // 2, nq),
        in_specs=[
            pl.BlockSpec((1, tq, LANES), lambda bi, p, qi: (bi, qi, p)),
            pl.BlockSpec((1, t_k, LANES), lambda bi, p, qi: (bi, 0, p)),
            pl.BlockSpec((1, t_k, LANES), lambda bi, p, qi: (bi, 0, p)),
            pl.BlockSpec((1, tq, LANES), lambda bi, p, qi: (bi, past // tq + qi, 0)),
            pl.BlockSpec((1, 1, 2, t_k), lambda bi, p, qi: (bi, p, 0, 0)),
            pl.BlockSpec((tq, tk), lambda bi, p, qi: (0, 0)),
        ],
        out_specs=pl.BlockSpec((1, tq, LANES), lambda bi, p, qi: (bi, qi, p)),
        out_shape=jax.ShapeDtypeStruct((b, t_q, FOX_WIDTH), BF16),
        scratch_shapes=[pltpu.VMEM((2, tq, LANES), F32), pltpu.VMEM((2, tq, 1), F32)],
        compiler_params=_cparams("parallel", "parallel", "arbitrary"),
        name="fox_attention",
    )(qb, kb, vb, fcum, ft, mask)


def _diff_kernel(cfar_ref, q_ref, k_ref, v_ref, b0_ref, b1_ref, lam_ref, g_ref,
                 o_ref, acc_ref, m_ref, *, tq, tk, past, lam_init):
    h = pl.program_id(1)
    qi = pl.program_id(2)
    j_last = (past + qi * tq) // tk
    lane = lax.broadcasted_iota(jnp.int32, (1, LANES), 1)
    upper = lane >= DIFF_DH
    q2 = q_ref[0]
    zero = jnp.zeros_like(q2)
    qm = [jnp.where(upper, zero, q2), jnp.where(upper, q2, zero)]
    cfar = cfar_ref[h]
    m_ref[...] = jnp.full(m_ref.shape, NEG, F32)
    acc_ref[...] = jnp.zeros(acc_ref.shape, F32)

    def step(j, bias):
        start = pl.multiple_of(j * tk, tk)
        kb = k_ref[0, pl.ds(start, tk), :]
        vb = v_ref[0, pl.ds(start, tk), :]
        va = jnp.concatenate([vb, jnp.ones_like(vb)], axis=1)
        for mm in range(2):
            s = _dot_nt(qm[mm], kb) + bias
            m_old = m_ref[mm]
            m_new = jnp.maximum(m_old, jnp.max(s, axis=1, keepdims=True))
            p = jnp.exp(s - m_new)
            alpha = jnp.exp(m_old - m_new)
            acc_ref[mm] = alpha * acc_ref[mm] + _dot(p.astype(BF16), va)
            m_ref[mm] = m_new

    def body(j, carry):
        step(j, cfar)
        return carry

    lax.fori_loop(0, jnp.maximum(j_last - 1, 0), body, 0)

    @pl.when(j_last >= 1)
    def _():
        step(j_last - 1, b1_ref[0])

    step(j_last, b0_ref[0])

    lp = lam_ref[...]
    lam = (jnp.exp(jnp.sum(lp[0:1] * lp[1:2], keepdims=True))
           - jnp.exp(jnp.sum(lp[2:3] * lp[3:4], keepdims=True)) + lam_init)
    a0 = acc_ref[0]
    a1 = acc_ref[1]
    o = a0[:, :LANES] / a0[:, LANES:] - lam * (a1[:, :LANES] / a1[:, LANES:])
    ms = jnp.mean(o * o, axis=1, keepdims=True)
    o = o * lax.rsqrt(ms + RMS_EPS) * g_ref[...] * (1.0 - lam_init)
    o_ref[0] = o.astype(o_ref.dtype)


def _t5_bucket(rel):
    nb = REL_BUCKETS // 2
    max_exact = nb // 2
    ret = jnp.where(rel > 0, nb, 0)
    n = jnp.abs(rel)
    nf = jnp.maximum(n, 1).astype(F32)
    large = max_exact + (jnp.log(nf / max_exact) / math.log(REL_MAX_DIST / max_exact)
                         * (nb - max_exact)).astype(jnp.int32)
    large = jnp.minimum(large, nb - 1)
    return ret + jnp.where(n < max_exact, n, large)


def _diff_bias_tiles(rel_table, tq, tk, kv_valid):
    ii = jnp.arange(tq)[:, None]
    jj = jnp.arange(tk)[None, :]
    rel0 = jj - ii
    rel1 = rel0 - tk
    t0 = jnp.transpose(rel_table[_t5_bucket(rel0)], (2, 0, 1)).astype(F32)
    t1 = jnp.transpose(rel_table[_t5_bucket(rel1)], (2, 0, 1)).astype(F32)
    visible = ((jj // CHUNK) <= (ii // CHUNK)) & (jj < kv_valid)
    t0 = jnp.where(visible[None], t0, NEG)
    cfar = rel_table[_t5_bucket(jnp.asarray(-2 * REL_MAX_DIST, jnp.int32))].astype(F32)
    return t0, t1, cfar


def _diff_attention(qb, kb, vb, rel_table, diff_lambda, subln_g, past, tq, tk, kv_valid, lam_init):
    b, t_q, _ = qb.shape
    t_k = kb.shape[1]
    nq = t_q // tq
    assert past % tk == 0 and t_k % tk == 0 and (tq == tk or nq == 1)
    assert tk >= REL_MAX_DIST and tk % CHUNK == 0 and (tq % CHUNK == 0 or nq == 1)
    t0, t1, cfar = _diff_bias_tiles(rel_table, tq, tk, kv_valid)
    return pl.pallas_call(
        functools.partial(_diff_kernel, tq=tq, tk=tk, past=past, lam_init=lam_init),
        grid=(b, DIFF_HEADS, nq),
        in_specs=[
            pl.BlockSpec(memory_space=pltpu.SMEM),
            pl.BlockSpec((1, tq, LANES), lambda bi, h, qi: (bi, qi, h)),
            pl.BlockSpec((1, t_k, LANES), lambda bi, h, qi: (bi, 0, h)),
            pl.BlockSpec((1, t_k, LANES), lambda bi, h, qi: (bi, 0, h)),
            pl.BlockSpec((1, tq, tk), lambda bi, h, qi: (h, 0, 0)),
            pl.BlockSpec((1, tq, tk), lambda bi, h, qi: (h, 0, 0)),
            pl.BlockSpec((4, DIFF_DH), lambda bi, h, qi: (0, 0)),
            pl.BlockSpec((1, DIFF_VD), lambda bi, h, qi: (0, 0)),
        ],
        out_specs=pl.BlockSpec((1, tq, LANES), lambda bi, h, qi: (bi, qi, h)),
        out_shape=jax.ShapeDtypeStruct((b, t_q, DIFF_WIDTH), BF16),
        scratch_shapes=[pltpu.VMEM((2, tq, 2 * LANES), F32), pltpu.VMEM((2, tq, 1), F32)],
        compiler_params=_cparams("parallel", "parallel", "arbitrary"),
        name="diff_attention",
    )(cfar, qb, kb, vb, t0, t1, diff_lambda, subln_g.reshape(1, DIFF_VD))


def _pool_kernel(u_ref, prev_ref, hist_ref, wmix_ref, scale_ref, o_ref, ext_ref, *, tm, past):
    i = pl.program_id(1)
    u = u_ref[0]
    ext_ref[0:HIST_ROWS, :] = jnp.where(i == 0, hist_ref[0], prev_ref[0])
    ext_ref[HIST_ROWS:HIST_ROWS + tm, :] = u
    pos = past + i * tm + lax.broadcasted_iota(jnp.int32, (tm, 1), 0)
    for g, w in enumerate(POOL_WINDOWS):
        sl = slice(g * POOL_GC, (g + 1) * POOL_GC)
        ug = u[:, sl]
        wsum = ug
        for s in range(1, w):
            wsum = wsum + ext_ref[HIST_ROWS - s:HIST_ROWS - s + tm, sl]
        cnt = jnp.minimum(w, pos + 1).astype(F32)
        d = wsum / cnt - ug
        y = _dot(d.astype(BF16), wmix_ref[g]) * scale_ref[:, sl]
        o_ref[0, :, sl] = y.astype(o_ref.dtype)


def _halo_specs(tm, width):
    cur = pl.BlockSpec((1, tm, width), lambda b, i: (b, i, 0))
    prev = pl.BlockSpec((1, HIST_ROWS, width),
                        lambda b, i: (b, jnp.maximum(i * (tm // HIST_ROWS) - 1, 0), 0))
    hist = pl.BlockSpec((1, HIST_ROWS, width), lambda b, i: (b, 0, 0))
    return cur, prev, hist


def _pool_mix(u3, hist16, wmix, scale, past):
    b, t, _ = u3.shape
    tm = min(512, t)
    cur, prev, hist = _halo_specs(tm, POOL_WIDTH)
    return pl.pallas_call(
        functools.partial(_pool_kernel, tm=tm, past=past),
        grid=(b, t // tm),
        in_specs=[cur, prev, hist,
                  pl.BlockSpec(wmix.shape, lambda bi, i: (0, 0, 0)),
                  pl.BlockSpec((1, POOL_WIDTH), lambda bi, i: (0, 0))],
        out_specs=cur,
        out_shape=jax.ShapeDtypeStruct(u3.shape, BF16),
        scratch_shapes=[pltpu.VMEM((HIST_ROWS + tm, POOL_WIDTH), F32)],
        compiler_params=_cparams("parallel", "parallel"),
        name="pool_mix",
    )(u3, u3, hist16, wmix, scale.reshape(1, POOL_WIDTH))


def _conv_kernel(z_ref, prev_ref, hist_ref, bg_ref, w_ref, o_ref, ext_ref, *, tm):
    i = pl.program_id(1)
    z = z_ref[0]
    ext_ref[0:HIST_ROWS, :] = jnp.where(i == 0, hist_ref[0], prev_ref[0])
    ext_ref[HIST_ROWS:HIST_ROWS + tm, :] = z
    y = (ext_ref[HIST_ROWS - 2:HIST_ROWS - 2 + tm, :] * w_ref[0:1, :]
         + ext_ref[HIST_ROWS - 1:HIST_ROWS - 1 + tm, :] * w_ref[1:2, :]
         + z * w_ref[2:3, :])
    o_ref[0] = (bg_ref[0] * y).astype(o_ref.dtype)


def _short_conv(z3, hist16, bg3, w):
    b, t, _ = z3.shape
    tm = min(512, t)
    cur, prev, hist = _halo_specs(tm, CONV_CH)
    return pl.pallas_call(
        functools.partial(_conv_kernel, tm=tm),
        grid=(b, t // tm),
        in_specs=[cur, prev, hist, cur, pl.BlockSpec((CONV_K, CONV_CH), lambda bi, i: (0, 0))],
        out_specs=cur,
        out_shape=jax.ShapeDtypeStruct(z3.shape, BF16),
        scratch_shapes=[pltpu.VMEM((HIST_ROWS + tm, CONV_CH), F32)],
        compiler_params=_cparams("parallel", "parallel"),
        name="short_conv",
    )(z3, z3, hist16, bg3, w)


def _outproj_kernel(a_ref, b_ref, w_ref, x_ref, g_ref, beta_ref, o_ref):
    half = w_ref.shape[0] // 2
    mix = _dot(a_ref[...], w_ref[0:half, :]) + _dot(b_ref[...], w_ref[half:, :])
    o_ref[...] = _layer_norm(DN_ALPHA * x_ref[...] + mix, g_ref[...], beta_ref[...])


def _outproj_ln(a2, b2, w, x2, g, beta):
    rows = x2.shape[0]
    tm = _row_tile(rows)
    row = lambda n: pl.BlockSpec((tm, n), lambda i: (i, 0))
    vec = pl.BlockSpec((1, D_MODEL), lambda i: (0, 0))
    return pl.pallas_call(
        _outproj_kernel,
        grid=(rows // tm,),
        in_specs=[row(512), row(512), pl.BlockSpec(w.shape, lambda i: (0, 0)),
                  row(D_MODEL), vec, vec],
        out_specs=row(D_MODEL),
        out_shape=jax.ShapeDtypeStruct((rows, D_MODEL), F32),
        compiler_params=_cparams("parallel"),
        name="outproj_ln",
    )(a2, b2, w, x2, g.reshape(1, D_MODEL), beta.reshape(1, D_MODEL))


def _mem_kernel(x_ref, wq_ref, mk_ref, mv_ref, wo_ref, g_ref, beta_ref, o_ref):
    x = x_ref[0]
    q = _dot(x.astype(BF16), wq_ref[...])
    qb = (q * (MEM_DH ** -0.5)).astype(BF16)
    outs = []
    for h in range(MEM_HEADS):
        sl = slice(h * MEM_DH, (h + 1) * MEM_DH)
        s = _dot_nt(qb[:, sl], mk_ref[0, :, sl].astype(BF16))
        m = jnp.max(s, axis=1, keepdims=True)
        p = jnp.exp(s - m)
        l = jnp.sum(p, axis=1, keepdims=True)
        o = _dot(p.astype(BF16), mv_ref[0, :, sl].astype(BF16)) / l
        outs.append(o.astype(BF16))
    o_all = jnp.concatenate(outs, axis=1)
    y = DN_ALPHA * x + _dot(o_all, wo_ref[...])
    o_ref[0] = _layer_norm(y, g_ref[...], beta_ref[...])


def _mem_attend_ln(x3, wq, mk, mv, wo, g, beta):
    b, t, _ = x3.shape
    tm = min(512, t)
    xs = pl.BlockSpec((1, tm, D_MODEL), lambda bi, i: (bi, i, 0))
    ws = pl.BlockSpec((D_MODEL, D_MODEL), lambda bi, i: (0, 0))
    ms = pl.BlockSpec((1, N_MEM, D_MODEL), lambda bi, i: (bi, 0, 0))
    vec = pl.BlockSpec((1, D_MODEL), lambda bi, i: (0, 0))
    return pl.pallas_call(
        _mem_kernel,
        grid=(b, t // tm),
        in_specs=[xs, ws, ms, ms, ws, vec, vec],
        out_specs=xs,
        out_shape=jax.ShapeDtypeStruct(x3.shape, F32),
        compiler_params=_cparams("parallel", "parallel"),
        name="mem_attend_ln",
    )(x3, wq, mk, mv, wo, g.reshape(1, D_MODEL), beta.reshape(1, D_MODEL))


def _ffn_kernel(x_ref, w1_ref, w2_ref, g_ref, beta_ref, o_ref, *, chunk):
    x = x_ref[...]
    xb = x.astype(BF16)
    acc = jnp.zeros(x.shape, F32)
    for c in range(D_FF // chunk):
        h = _dot(xb, w1_ref[:, c * chunk:(c + 1) * chunk])
        h = jnp.square(jnp.maximum(h, 0.0))
        acc = acc + _dot(h.astype(BF16), w2_ref[c * chunk:(c + 1) * chunk, :])
    o_ref[...] = _layer_norm(DN_ALPHA * x + acc, g_ref[...], beta_ref[...])


def _ffn_ln(x2, w1, w2, g, beta):
    rows = x2.shape[0]
    tm = _row_tile(rows)
    row = pl.BlockSpec((tm, D_MODEL), lambda i: (i, 0))
    vec = pl.BlockSpec((1, D_MODEL), lambda i: (0, 0))
    once = pl.Buffered(1)
    return pl.pallas_call(
        functools.partial(_ffn_kernel, chunk=1024),
        grid=(rows // tm,),
        in_specs=[row,
                  pl.BlockSpec(w1.shape, lambda i: (0, 0), pipeline_mode=once),
                  pl.BlockSpec(w2.shape, lambda i: (0, 0), pipeline_mode=once),
                  vec, vec],
        out_specs=row,
        out_shape=jax.ShapeDtypeStruct((rows, D_MODEL), F32),
        compiler_params=_cparams("parallel"),
        name="ffn_ln",
    )(x2, w1, w2, g.reshape(1, D_MODEL), beta.reshape(1, D_MODEL))


def _pad_rows(a, total):
    pad = total - a.shape[1]
    if pad == 0:
        return a
    return jnp.pad(a, ((0, 0), (0, pad)) + ((0, 0),) * (a.ndim - 2))


def _hist16(h):
    return jnp.pad(h, ((0, 0), (HIST_ROWS - h.shape[1], 0), (0, 0)))


def _trunk(x, mem_k, mem_v, pool_h, fk_h, fv_h, flf_h, dk_h, dv_h, conv_h, past, wts):
    b, t, _ = x.shape
    rows = b * t
    t_k = past + t
    if past == 0:
        tq = tk = 512
        dq = dk = 256
        t_kp = t_k
        cs_rows = 1024
    else:
        tq = dq = t
        tk = dk = 128
        t_kp = -(-t_k // tk) * tk
        cs_rows = t_kp
    x2 = x.reshape(rows, D_MODEL)

    u, qb, k, kb, v, vb, lf = _proj_even(x2, wts["w_in_even"], wts["w_forget"], wts["b_forget"])
    lf3 = lf.reshape(b, t, LANES)
    kb3 = kb.reshape(b, t, FOX_WIDTH)
    vb3 = vb.reshape(b, t, FOX_WIDTH)
    if past:
        lf_hist = jnp.pad(flf_h[0], ((0, 0), (0, 0), (0, LANES - FOX_HEADS)))
        lf_all = jnp.concatenate([lf_hist, lf3], axis=1)
        kb3 = jnp.concatenate([fk_h[0].reshape(b, past, FOX_WIDTH).astype(BF16), kb3], axis=1)
        vb3 = jnp.concatenate([fv_h[0].reshape(b, past, FOX_WIDTH).astype(BF16), vb3], axis=1)
    else:
        lf_all = lf3
    fcum = _cumsum_time(_pad_rows(lf_all, t_kp), cs_rows)
    ft = jnp.transpose(fcum[:, :, :FOX_HEADS], (0, 2, 1)).reshape(b, FOX_HEADS // 2, 2, t_kp)
    fox_y = _fox_attention(qb.reshape(b, t, FOX_WIDTH), _pad_rows(kb3, t_kp), _pad_rows(vb3, t_kp),
                           fcum, ft, past, tq, tk)
    u3 = u.reshape(b, t, POOL_WIDTH)
    pool_y = _pool_mix(u3, _hist16(pool_h[0]), wts["w_pool_mix"], wts["pool_scale"], past)
    x2 = _outproj_ln(pool_y.reshape(rows, POOL_WIDTH), fox_y.reshape(rows, FOX_WIDTH),
                     wts["w_out_even"], x2, wts["ln_g"][0, 0], wts["ln_b"][0, 0])
    x2 = _mem_attend_ln(x2.reshape(b, t, D_MODEL), wts["w_mem_q"][0], mem_k[0], mem_v[0],
                        wts["w_mem_o"][0], wts["ln_g"][0, 1], wts["ln_b"][0, 1]).reshape(rows, D_MODEL)
    x2 = _ffn_ln(x2, wts["w_ff1"][0], wts["w_ff2"][0], wts["ln_g"][0, 2], wts["ln_b"][0, 2])
    n_pool = u3[:, t - POOL_HIST:][None]
    n_fk = k.reshape(1, b, t, FOX_HEADS, FOX_DH)
    n_fv = v.reshape(1, b, t, FOX_HEADS, FOX_DH)
    n_flf = lf3[:, :, :FOX_HEADS][None]

    lam_init = 0.8 - 0.6 * math.exp(-0.3 * 1)
    qb, k, kb, v, vb, bg, z = _proj_odd(x2, wts["w_in_odd"])
    kb3 = kb.reshape(b, t, DIFF_QK)
    vb3 = vb.reshape(b, t, DIFF_WIDTH)
    if past:
        kb3 = jnp.concatenate([dk_h[0].reshape(b, past, DIFF_QK).astype(BF16), kb3], axis=1)
        vb3 = jnp.concatenate([dv_h[0].reshape(b, past, DIFF_WIDTH).astype(BF16), vb3], axis=1)
    att = _diff_attention(qb.reshape(b, t, DIFF_QK), _pad_rows(kb3, t_kp), _pad_rows(vb3, t_kp),
                          wts["rel_bias_table"], wts["diff_lambda"], wts["diff_subln_g"],
                          past, dq, dk, t_k - (t_kp - dk), lam_init)
    z3 = z.reshape(b, t, CONV_CH)
    conv_y = _short_conv(z3, _hist16(conv_h[0]), bg.reshape(b, t, CONV_CH), wts["conv_w"])
    x2 = _outproj_ln(att.reshape(rows, DIFF_WIDTH), conv_y.reshape(rows, CONV_CH),
                     wts["w_out_odd"], x2, wts["ln_g"][1, 0], wts["ln_b"][1, 0])
    x2 = _mem_attend_ln(x2.reshape(b, t, D_MODEL), wts["w_mem_q"][1], mem_k[1], mem_v[1],
                        wts["w_mem_o"][1], wts["ln_g"][1, 1], wts["ln_b"][1, 1]).reshape(rows, D_MODEL)
    x2 = _ffn_ln(x2, wts["w_ff1"][1], wts["w_ff2"][1], wts["ln_g"][1, 2], wts["ln_b"][1, 2])
    n_dk = k.reshape(1, b, t, DIFF_HEADS, 2, DIFF_DH)
    n_dv = v.reshape(1, b, t, DIFF_HEADS, DIFF_VD)
    n_conv = z3[:, t - (CONV_K - 1):][None]
    return (x2.reshape(b, t, D_MODEL), n_pool, n_fk, n_fv, n_flf, n_dk, n_dv, n_conv)


def kernel(x_prompt, x_sample, state_pool, cache_fox_k, cache_fox_v, cache_fox_logf,
           cache_diff_k, cache_diff_v, state_conv, cache_mem_k, cache_mem_v, mem_prompt,
           w_in_even, b_forget, w_pool_mix, pool_scale, w_out_even,
           w_in_odd, diff_lambda, diff_subln_g, conv_w, w_out_odd, rel_bias_table,
           w_mem_q, w_mem_k, w_mem_v, w_mem_o, w_ff1, w_ff2, ln_g, ln_b):
    bp = x_prompt.shape[0]
    nmain = POOL_WIDTH + 3 * FOX_WIDTH
    wts = {
        "w_in_even": w_in_even[0, :, :nmain].astype(BF16),
        "w_forget": jnp.pad(w_in_even[0, :, nmain:], ((0, 0), (0, LANES - FOX_HEADS))).astype(BF16),
        "b_forget": jnp.pad(b_forget[0], (0, LANES - FOX_HEADS)).reshape(1, LANES).astype(F32),
        "w_pool_mix": w_pool_mix[0].astype(BF16),
        "pool_scale": pool_scale[0],
        "w_out_even": w_out_even[0].astype(BF16),
        "w_in_odd": w_in_odd[0].astype(BF16),
        "diff_lambda": diff_lambda[0],
        "diff_subln_g": diff_subln_g[0],
        "conv_w": conv_w[0],
        "w_out_odd": w_out_odd[0].astype(BF16),
        "rel_bias_table": rel_bias_table,
        "w_mem_q": w_mem_q.astype(BF16),
        "w_mem_o": w_mem_o.astype(BF16),
        "w_ff1": w_ff1.astype(BF16),
        "w_ff2": w_ff2.astype(BF16),
        "ln_g": ln_g,
        "ln_b": ln_b,
    }
    kv = _mem_kv(mem_prompt.reshape(bp * N_MEM, D_MODEL),
                 jnp.stack([w_mem_k, w_mem_v]).astype(BF16))
    kv = kv.reshape(2, DEPTH, bp, N_MEM, D_MODEL)
    p_mem_k = kv[0].reshape(DEPTH, bp, N_MEM, MEM_HEADS, MEM_DH)
    p_mem_v = kv[1].reshape(DEPTH, bp, N_MEM, MEM_HEADS, MEM_DH)
    zeros = lambda *s: jnp.zeros(s, F32)
    (y_prompt, p_pool, p_fox_k, p_fox_v, p_fox_logf, p_diff_k, p_diff_v, p_conv) = _trunk(
        x_prompt, kv[0], kv[1],
        zeros(1, bp, POOL_HIST, POOL_WIDTH), None, None, None, None, None,
        zeros(1, bp, CONV_K - 1, CONV_CH), 0, wts)
    bs = x_sample.shape[0]
    (y_sample, s_pool, s_fox_k, s_fox_v, s_fox_logf, s_diff_k, s_diff_v, s_conv) = _trunk(
        x_sample, cache_mem_k.reshape(DEPTH, bs, N_MEM, D_MODEL),
        cache_mem_v.reshape(DEPTH, bs, N_MEM, D_MODEL),
        state_pool, cache_fox_k, cache_fox_v, cache_fox_logf,
        cache_diff_k, cache_diff_v, state_conv, cache_fox_k.shape[2], wts)
    return (y_prompt, y_sample,
            p_pool, p_fox_k, p_fox_v, p_fox_logf, p_diff_k, p_diff_v, p_conv, p_mem_k, p_mem_v,
            s_pool, s_fox_k, s_fox_v, s_fox_logf, s_diff_k, s_diff_v, s_conv)
```

```python
import functools
import math

import numpy as np
import jax
import jax.numpy as jnp
from jax import lax
from jax.experimental import pallas as pl
from jax.experimental.pallas import tpu as pltpu

F32 = jnp.float32
BF16 = jnp.bfloat16

D_MODEL = 1024
DEPTH = 2
CHUNK = 64
POOL_WIDTH = 512
POOL_GC = 128
POOL_WINDOWS = (2, 4, 8, 16)
POOL_HIST = 15
FOX_HEADS = 8
FOX_DH = 64
FOX_WIDTH = 512
DIFF_HEADS = 4
DIFF_DH = 64
DIFF_VD = 128
DIFF_QK = 512
DIFF_WIDTH = 512
CONV_CH = 512
CONV_K = 3
D_FF = 4096
N_MEM = 256
MEM_HEADS = 4
MEM_DH = 256
REL_BUCKETS = 32
REL_MAX_DIST = 128
DN_ALPHA = (2 * DEPTH) ** 0.25
LN_EPS = 1e-5
RMS_EPS = 1e-5
NEG = -1e30

LANES = 128
HIST_ROWS = 16
VMEM_LIMIT = 56 * 1024 * 1024


def _cparams(*sem):
    return pltpu.CompilerParams(dimension_semantics=sem, vmem_limit_bytes=VMEM_LIMIT)


def _dot(a, b):
    return jnp.dot(a, b, preferred_element_type=F32)


def _dot_nt(a, b):
    return lax.dot_general(a, b, (((1,), (1,)), ((), ())), preferred_element_type=F32)


def _layer_norm(y, g, b):
    mu = jnp.mean(y, axis=-1, keepdims=True)
    d = y - mu
    var = jnp.mean(d * d, axis=-1, keepdims=True)
    return d * lax.rsqrt(var + LN_EPS) * g + b


def _row_tile(rows):
    return min(512, rows)


def _proj_even_kernel(x_ref, w_ref, wf_ref, bf_ref,
                      u_ref, q_ref, k_ref, kb_ref, v_ref, vb_ref, lf_ref):
    xb = x_ref[...].astype(BF16)

    def mm(c):
        return _dot(xb, w_ref[:, c * 512:(c + 1) * 512])

    u_ref[...] = mm(0)
    q_ref[...] = (mm(1) * (FOX_DH ** -0.5)).astype(BF16)
    k = mm(2)
    k_ref[...] = k
    kb_ref[...] = k.astype(BF16)
    v = mm(3)
    v_ref[...] = v
    vb_ref[...] = v.astype(BF16)
    z = _dot(xb, wf_ref[...]) + bf_ref[...]
    lf_ref[...] = jnp.minimum(z, 0.0) - jnp.log1p(jnp.exp(-jnp.abs(z)))


def _proj_even(x2, w, wf, bf):
    rows = x2.shape[0]
    tm = _row_tile(rows)
    row = lambda n: pl.BlockSpec((tm, n), lambda i: (i, 0))
    full = lambda a: pl.BlockSpec(a.shape, lambda i: (0,) * a.ndim)
    f32o = jax.ShapeDtypeStruct((rows, 512), F32)
    bf16o = jax.ShapeDtypeStruct((rows, 512), BF16)
    return pl.pallas_call(
        _proj_even_kernel,
        grid=(rows // tm,),
        in_specs=[row(D_MODEL), full(w), full(wf), full(bf)],
        out_specs=[row(512)] * 6 + [row(LANES)],
        out_shape=[f32o, bf16o, f32o, bf16o, f32o, bf16o,
                   jax.ShapeDtypeStruct((rows, LANES), F32)],
        compiler_params=_cparams("parallel"),
        name="proj_even",
    )(x2, w, wf, bf)


def _proj_odd_kernel(x_ref, w_ref, q_ref, k_ref, kb_ref, v_ref, vb_ref, bg_ref, z_ref):
    xb = x_ref[...].astype(BF16)

    def mm(c):
        return _dot(xb, w_ref[:, c * 512:(c + 1) * 512])

    q_ref[...] = (mm(0) * (DIFF_DH ** -0.5)).astype(BF16)
    k = mm(1)
    k_ref[...] = k
    kb_ref[...] = k.astype(BF16)
    v = mm(2)
    v_ref[...] = v
    vb_ref[...] = v.astype(BF16)
    bg_ref[...] = mm(3)
    z_ref[...] = mm(4) * mm(5)


def _proj_odd(x2, w):
    rows = x2.shape[0]
    tm = _row_tile(rows)
    row = lambda n: pl.BlockSpec((tm, n), lambda i: (i, 0))
    f32o = jax.ShapeDtypeStruct((rows, 512), F32)
    bf16o = jax.ShapeDtypeStruct((rows, 512), BF16)
    return pl.pallas_call(
        _proj_odd_kernel,
        grid=(rows // tm,),
        in_specs=[row(D_MODEL), pl.BlockSpec(w.shape, lambda i: (0, 0))],
        out_specs=[row(512)] * 7,
        out_shape=[bf16o, f32o, bf16o, f32o, bf16o, f32o, f32o],
        compiler_params=_cparams("parallel"),
        name="proj_odd",
    )(x2, w)


def _mem_kv_kernel(x_ref, w_ref, o_ref):
    o_ref[0, 0] = _dot(x_ref[...].astype(BF16), w_ref[0, 0])


def _mem_kv(mem2, w_kv):
    rows = mem2.shape[0]
    return pl.pallas_call(
        _mem_kv_kernel,
        grid=(2, DEPTH),
        in_specs=[pl.BlockSpec((rows, D_MODEL), lambda a, l: (0, 0)),
                  pl.BlockSpec((1, 1, D_MODEL, D_MODEL), lambda a, l: (a, l, 0, 0))],
        out_specs=pl.BlockSpec((1, 1, rows, D_MODEL), lambda a, l: (a, l, 0, 0)),
        out_shape=jax.ShapeDtypeStruct((2, DEPTH, rows, D_MODEL), F32),
        compiler_params=_cparams("parallel", "parallel"),
        name="mem_kv",
    )(mem2, w_kv)


def _cumsum_kernel(x_ref, o_ref, carry_ref, *, rows):
    @pl.when(pl.program_id(1) == 0)
    def _():
        carry_ref[...] = jnp.zeros_like(carry_ref)

    r = lax.broadcasted_iota(jnp.int32, (LANES, LANES), 0)
    c = lax.broadcasted_iota(jnp.int32, (LANES, LANES), 1)
    tri = (r >= c).astype(F32)
    carry = carry_ref[...]
    for s in range(rows // LANES):
        blk = x_ref[0, s * LANES:(s + 1) * LANES, :]
        cs = jnp.dot(tri, blk, preferred_element_type=F32,
                     precision=lax.Precision.HIGHEST) + carry
        o_ref[0, s * LANES:(s + 1) * LANES, :] = cs
        carry = cs[LANES - 1:LANES, :]
    carry_ref[...] = carry


def _cumsum_time(x, rows):
    b, t, _ = x.shape
    spec = pl.BlockSpec((1, rows, LANES), lambda i, j: (i, j, 0))
    return pl.pallas_call(
        functools.partial(_cumsum_kernel, rows=rows),
        grid=(b, t // rows),
        in_specs=[spec],
        out_specs=spec,
        out_shape=jax.ShapeDtypeStruct(x.shape, F32),
        scratch_shapes=[pltpu.VMEM((1, LANES), F32)],
        compiler_params=_cparams("parallel", "arbitrary"),
        name="cumsum_time",
    )(x)


def _fox_kernel(q_ref, k_ref, v_ref, fq_ref, ft_ref, mask_ref, o_ref,
                qm_ref, fqc_ref, s_ref, p_ref, acc_ref, m_ref, al_ref,
                *, tq, tk, past):
    pair = pl.program_id(1)
    qi = pl.program_id(2)
    j_last = (past + qi * tq) // tk
    lane = lax.broadcasted_iota(jnp.int32, (1, LANES), 1)
    upper = lane >= FOX_DH
    sels = (jnp.logical_not(upper), upper)
    q2 = q_ref[0]
    fblk = fq_ref[0]
    lane_q = lax.broadcasted_iota(jnp.int32, (tq, LANES), 1)
    for hh in range(2):
        qm_ref[hh] = jnp.where(sels[hh], q2, jnp.zeros_like(q2))
        fqc_ref[hh] = jnp.sum(jnp.where(lane_q == 2 * pair + hh, fblk, 0.0),
                              axis=1, keepdims=True)
    m_ref[...] = jnp.full(m_ref.shape, NEG, F32)
    acc_ref[...] = jnp.zeros(acc_ref.shape, F32)
    p_ref[1] = jnp.zeros(p_ref.shape[1:], p_ref.dtype)
    al_ref[1] = jnp.ones(al_ref.shape[1:], F32)

    def scores(hh, j):
        start = pl.multiple_of(j * tk, tk)
        s_ref[hh] = _dot_nt(qm_ref[hh], k_ref[0, pl.ds(start, tk), :])

    def probs(hh, j, masked):
        start = pl.multiple_of(j * tk, tk)
        fk = ft_ref[0, 0, hh:hh + 1, pl.ds(start, tk)]
        s = s_ref[hh] + (fqc_ref[hh] - fk)
        if masked:
            s = s + mask_ref[...]
        m_old = m_ref[hh]
        m_new = jnp.maximum(m_old, jnp.max(s, axis=1, keepdims=True))
        p_ref[hh] = jnp.exp(s - m_new).astype(BF16)
        al_ref[hh] = jnp.exp(m_old - m_new)
        m_ref[hh] = m_new

    def pv(hh, j):
        start = pl.multiple_of(j * tk, tk)
        vb = v_ref[0, pl.ds(start, tk), :]
        va = jnp.where(sels[hh], vb, jnp.ones_like(vb))
        acc_ref[hh] = al_ref[hh] * acc_ref[hh] + _dot(p_ref[hh], va)

    scores(0, 0)

    def body(j, carry):
        scores(1, j)
        probs(0, j, False)
        pv(1, jnp.maximum(j - 1, 0))
        scores(0, j + 1)
        probs(1, j, False)
        pv(0, j)
        return carry

    lax.fori_loop(0, j_last, body, 0)
    scores(1, j_last)
    probs(0, j_last, True)
    pv(1, jnp.maximum(j_last - 1, 0))
    probs(1, j_last, True)
    pv(0, j_last)
    pv(1, j_last)

    a0 = acc_ref[0]
    a1 = acc_ref[1]
    o0 = a0 / pltpu.roll(a0, FOX_DH, 1)
    o1 = a1 / pltpu.roll(a1, FOX_DH, 1)
    o_ref[0] = jnp.where(upper, o1, o0).astype(o_ref.dtype)


def _fox_attention(qb, kb, vb, fcum, ft, past, tq, tk):
    b, t_q, _ = qb.shape
    t_k = kb.shape[1]
    nq = t_q // tq
    assert past % tk == 0 and t_k % tk == 0 and (tq == tk or nq == 1)
    ii = np.arange(tq)[:, None]
    jj = np.arange(tk)[None, :]
    mask = jnp.asarray(np.where(jj <= ii, 0.0, NEG), F32)
    return pl.pallas_call(
        functools.partial(_fox_kernel, tq=tq, tk=tk, past=past),
        grid=(b, FOX_HEADS // 2, nq),
        in_specs=[
            pl.BlockSpec((1, tq, LANES), lambda bi, p, qi: (bi, qi, p)),
            pl.BlockSpec((1, t_k, LANES), lambda bi, p, qi: (bi, 0, p)),
            pl.BlockSpec((1, t_k, LANES), lambda bi, p, qi: (bi, 0, p)),
            pl.BlockSpec((1, tq, LANES), lambda bi, p, qi: (bi, past // tq + qi, 0)),
            pl.BlockSpec((1, 1, 2, t_k), lambda bi, p, qi: (bi, p, 0, 0)),
            pl.BlockSpec((tq, tk), lambda bi, p, qi: (0, 0)),
        ],
        out_specs=pl.BlockSpec((1, tq, LANES), lambda bi, p, qi: (bi, qi, p)),
        out_shape=jax.ShapeDtypeStruct((b, t_q, FOX_WIDTH), BF16),
        scratch_shapes=[pltpu.VMEM((2, tq, LANES), BF16),
                        pltpu.VMEM((2, tq, 1), F32),
                        pltpu.VMEM((2, tq, tk), F32),
                        pltpu.VMEM((2, tq, tk), BF16),
                        pltpu.VMEM((2, tq, LANES), F32),
                        pltpu.VMEM((2, tq, 1), F32),
                        pltpu.VMEM((2, tq, 1), F32)],
        compiler_params=_cparams("parallel", "parallel", "arbitrary"),
        name="fox_attention",
    )(qb, kb, vb, fcum, ft, mask)


def _diff_kernel(cfar_ref, q_ref, k_ref, v_ref, b0_ref, b1_ref, lam_ref, g_ref, o_ref,
                 qm_ref, s_ref, p_ref, acc_ref, m_ref, al_ref, *, tq, tk, past, lam_init):
    h = pl.program_id(1)
    qi = pl.program_id(2)
    j_last = (past + qi * tq) // tk
    lane = lax.broadcasted_iota(jnp.int32, (1, LANES), 1)
    upper = lane >= DIFF_DH
    q2 = q_ref[0]
    zero = jnp.zeros_like(q2)
    qm_ref[0] = jnp.where(upper, zero, q2)
    qm_ref[1] = jnp.where(upper, q2, zero)
    cfar = cfar_ref[h]
    m_ref[...] = jnp.full(m_ref.shape, NEG, F32)
    acc_ref[...] = jnp.zeros(acc_ref.shape, F32)
    p_ref[1] = jnp.zeros(p_ref.shape[1:], p_ref.dtype)
    al_ref[1] = jnp.ones(al_ref.shape[1:], F32)

    def scores(mm, j):
        start = pl.multiple_of(j * tk, tk)
        s_ref[mm] = _dot_nt(qm_ref[mm], k_ref[0, pl.ds(start, tk), :])

    def probs(mm, bias):
        s = s_ref[mm] + bias
        m_old = m_ref[mm]
        m_new = jnp.maximum(m_old, jnp.max(s, axis=1, keepdims=True))
        p_ref[mm] = jnp.exp(s - m_new).astype(BF16)
        al_ref[mm] = jnp.exp(m_old - m_new)
        m_ref[mm] = m_new

    def pv(mm, j):
        start = pl.multiple_of(j * tk, tk)
        vb = v_ref[0, pl.ds(start, tk), :]
        va = jnp.concatenate([vb, jnp.ones_like(vb)], axis=1)
        acc_ref[mm] = al_ref[mm] * acc_ref[mm] + _dot(p_ref[mm], va)

    def block(j, bias):
        scores(1, j)
        probs(0, bias)
        pv(1, jnp.maximum(j - 1, 0))
        scores(0, j + 1)
        probs(1, bias)
        pv(0, j)

    scores(0, 0)

    def body(j, carry):
        block(j, cfar)
        return carry

    lax.fori_loop(0, jnp.maximum(j_last - 1, 0), body, 0)

    @pl.when(j_last >= 1)
    def _():
        block(j_last - 1, b1_ref[0])

    scores(1, j_last)
    probs(0, b0_ref[0])
    pv(1, jnp.maximum(j_last - 1, 0))
    probs(1, b0_ref[0])
    pv(0, j_last)
    pv(1, j_last)

    lp = lam_ref[...]
    lam = (jnp.exp(jnp.sum(lp[0:1] * lp[1:2], keepdims=True))
           - jnp.exp(jnp.sum(lp[2:3] * lp[3:4], keepdims=True)) + lam_init)
    a0 = acc_ref[0]
    a1 = acc_ref[1]
    o = a0[:, :LANES] / a0[:, LANES:] - lam * (a1[:, :LANES] / a1[:, LANES:])
    ms = jnp.mean(o * o, axis=1, keepdims=True)
    o = o * lax.rsqrt(ms + RMS_EPS) * g_ref[...] * (1.0 - lam_init)
    o_ref[0] = o.astype(o_ref.dtype)


def _t5_bucket(rel):
    nb = REL_BUCKETS // 2
    max_exact = nb // 2
    ret = jnp.where(rel > 0, nb, 0)
    n = jnp.abs(rel)
    nf = jnp.maximum(n, 1).astype(F32)
    large = max_exact + (jnp.log(nf / max_exact) / math.log(REL_MAX_DIST / max_exact)
                         * (nb - max_exact)).astype(jnp.int32)
    large = jnp.minimum(large, nb - 1)
    return ret + jnp.where(n < max_exact, n, large)


def _bias_tile_kernel(w_ref, mask_ref, t0_ref, t1_ref, *, tq, tk):
    n = w_ref.shape[2]
    for d, t_ref in enumerate((t0_ref, t1_ref)):
        w = jnp.broadcast_to(w_ref[0, d:d + 1, :], (tq, n))
        t = pltpu.roll(w, 0, 1, stride=1, stride_axis=0)[:, :tk]
        if d == 0:
            t = t + mask_ref[...]
        t_ref[0] = t


def _diff_bias_tiles(rel_table, tq, tk, kv_valid):
    n = -(-(tq + tk) // LANES) * LANES
    m = jnp.arange(n)
    rel = jnp.where(m < tk, m, m - n)
    w = jnp.stack([rel_table[_t5_bucket(rel)], rel_table[_t5_bucket(rel - tk)]])
    w = jnp.transpose(w, (2, 0, 1)).astype(F32)
    ii = np.arange(tq)[:, None]
    jj = np.arange(tk)[None, :]
    visible = ((jj // CHUNK) <= (ii // CHUNK)) & (jj < kv_valid)
    mask = jnp.asarray(np.where(visible, 0.0, NEG), F32)
    tile = jax.ShapeDtypeStruct((DIFF_HEADS, tq, tk), F32)
    t0, t1 = pl.pallas_call(
        functools.partial(_bias_tile_kernel, tq=tq, tk=tk),
        grid=(DIFF_HEADS,),
        in_specs=[pl.BlockSpec((1, 2, n), lambda h: (h, 0, 0)),
                  pl.BlockSpec((tq, tk), lambda h: (0, 0))],
        out_specs=[pl.BlockSpec((1, tq, tk), lambda h: (h, 0, 0))] * 2,
        out_shape=[tile, tile],
        compiler_params=_cparams("parallel"),
        name="diff_bias_tiles",
    )(w, mask)
    cfar = rel_table[_t5_bucket(jnp.asarray(-2 * REL_MAX_DIST, jnp.int32))].astype(F32)
    return t0, t1, cfar


def _diff_attention(qb, kb, vb, rel_table, diff_lambda, subln_g, past, tq, tk, kv_valid, lam_init):
    b, t_q, _ = qb.shape
    t_k = kb.shape[1]
    nq = t_q // tq
    assert past % tk == 0 and t_k % tk == 0 and (tq == tk or nq == 1)
    assert tk >= REL_MAX_DIST and tk % CHUNK == 0 and (tq % CHUNK == 0 or nq == 1)
    t0, t1, cfar = _diff_bias_tiles(rel_table, tq, tk, kv_valid)
    return pl.pallas_call(
        functools.partial(_diff_kernel, tq=tq, tk=tk, past=past, lam_init=lam_init),
        grid=(b, DIFF_HEADS, nq),
        in_specs=[
            pl.BlockSpec(memory_space=pltpu.SMEM),
            pl.BlockSpec((1, tq, LANES), lambda bi, h, qi: (bi, qi, h)),
            pl.BlockSpec((1, t_k, LANES), lambda bi, h, qi: (bi, 0, h)),
            pl.BlockSpec((1, t_k, LANES), lambda bi, h, qi: (bi, 0, h)),
            pl.BlockSpec((1, tq, tk), lambda bi, h, qi: (h, 0, 0)),
            pl.BlockSpec((1, tq, tk), lambda bi, h, qi: (h, 0, 0)),
            pl.BlockSpec((4, DIFF_DH), lambda bi, h, qi: (0, 0)),
            pl.BlockSpec((1, DIFF_VD), lambda bi, h, qi: (0, 0)),
        ],
        out_specs=pl.BlockSpec((1, tq, LANES), lambda bi, h, qi: (bi, qi, h)),
        out_shape=jax.ShapeDtypeStruct((b, t_q, DIFF_WIDTH), BF16),
        scratch_shapes=[pltpu.VMEM((2, tq, LANES), BF16),
                        pltpu.VMEM((2, tq, tk), F32),
                        pltpu.VMEM((2, tq, tk), BF16),
                        pltpu.VMEM((2, tq, 2 * LANES), F32),
                        pltpu.VMEM((2, tq, 1), F32),
                        pltpu.VMEM((2, tq, 1), F32)],
        compiler_params=_cparams("parallel", "parallel", "arbitrary"),
        name="diff_attention",
    )(cfar, qb, kb, vb, t0, t1, diff_lambda, subln_g.reshape(1, DIFF_VD))


def _pool_kernel(u_ref, prev_ref, hist_ref, wmix_ref, scale_ref, o_ref, ext_ref, *, tm, past):
    i = pl.program_id(1)
    u = u_ref[0]
    ext_ref[0:HIST_ROWS, :] = jnp.where(i == 0, hist_ref[0], prev_ref[0])
    ext_ref[HIST_ROWS:HIST_ROWS + tm, :] = u
    pos = past + i * tm + lax.broadcasted_iota(jnp.int32, (tm, 1), 0)
    for g, w in enumerate(POOL_WINDOWS):
        sl = slice(g * POOL_GC, (g + 1) * POOL_GC)
        ug = u[:, sl]
        wsum = ug
        for s in range(1, w):
            wsum = wsum + ext_ref[HIST_ROWS - s:HIST_ROWS - s + tm, sl]
        cnt = jnp.minimum(w, pos + 1).astype(F32)
        d = wsum / cnt - ug
        y = _dot(d.astype(BF16), wmix_ref[g]) * scale_ref[:, sl]
        o_ref[0, :, sl] = y.astype(o_ref.dtype)


def _halo_specs(tm, width):
    cur = pl.BlockSpec((1, tm, width), lambda b, i: (b, i, 0))
    prev = pl.BlockSpec((1, HIST_ROWS, width),
                        lambda b, i: (b, jnp.maximum(i * (tm // HIST_ROWS) - 1, 0), 0))
    hist = pl.BlockSpec((1, HIST_ROWS, width), lambda b, i: (b, 0, 0))
    return cur, prev, hist


def _pool_mix(u3, hist16, wmix, scale, past):
    b, t, _ = u3.shape
    tm = min(512, t)
    cur, prev, hist = _halo_specs(tm, POOL_WIDTH)
    return pl.pallas_call(
        functools.partial(_pool_kernel, tm=tm, past=past),
        grid=(b, t // tm),
        in_specs=[cur, prev, hist,
                  pl.BlockSpec(wmix.shape, lambda bi, i: (0, 0, 0)),
                  pl.BlockSpec((1, POOL_WIDTH), lambda bi, i: (0, 0))],
        out_specs=cur,
        out_shape=jax.ShapeDtypeStruct(u3.shape, BF16),
        scratch_shapes=[pltpu.VMEM((HIST_ROWS + tm, POOL_WIDTH), F32)],
        compiler_params=_cparams("parallel", "parallel"),
        name="pool_mix",
    )(u3, u3, hist16, wmix, scale.reshape(1, POOL_WIDTH))


def _conv_kernel(z_ref, prev_ref, hist_ref, bg_ref, w_ref, o_ref, ext_ref, *, tm):
    i = pl.program_id(1)
    z = z_ref[0]
    ext_ref[0:HIST_ROWS, :] = jnp.where(i == 0, hist_ref[0], prev_ref[0])
    ext_ref[HIST_ROWS:HIST_ROWS + tm, :] = z
    y = (ext_ref[HIST_ROWS - 2:HIST_ROWS - 2 + tm, :] * w_ref[0:1, :]
         + ext_ref[HIST_ROWS - 1:HIST_ROWS - 1 + tm, :] * w_ref[1:2, :]
         + z * w_ref[2:3, :])
    o_ref[0] = (bg_ref[0] * y).astype(o_ref.dtype)


def _short_conv(z3, hist16, bg3, w):
    b, t, _ = z3.shape
    tm = min(512, t)
    cur, prev, hist = _halo_specs(tm, CONV_CH)
    return pl.pallas_call(
        functools.partial(_conv_kernel, tm=tm),
        grid=(b, t // tm),
        in_specs=[cur, prev, hist, cur, pl.BlockSpec((CONV_K, CONV_CH), lambda bi, i: (0, 0))],
        out_specs=cur,
        out_shape=jax.ShapeDtypeStruct(z3.shape, BF16),
        scratch_shapes=[pltpu.VMEM((HIST_ROWS + tm, CONV_CH), F32)],
        compiler_params=_cparams("parallel", "parallel"),
        name="short_conv",
    )(z3, z3, hist16, bg3, w)


def _outproj_kernel(a_ref, b_ref, w_ref, x_ref, g_ref, beta_ref, o_ref):
    half = w_ref.shape[0] // 2
    mix = _dot(a_ref[...], w_ref[0:half, :]) + _dot(b_ref[...], w_ref[half:, :])
    o_ref[...] = _layer_norm(DN_ALPHA * x_ref[...] + mix, g_ref[...], beta_ref[...])


def _outproj_ln(a2, b2, w, x2, g, beta):
    rows = x2.shape[0]
    tm = _row_tile(rows)
    row = lambda n: pl.BlockSpec((tm, n), lambda i: (i, 0))
    vec = pl.BlockSpec((1, D_MODEL), lambda i: (0, 0))
    return pl.pallas_call(
        _outproj_kernel,
        grid=(rows // tm,),
        in_specs=[row(512), row(512), pl.BlockSpec(w.shape, lambda i: (0, 0)),
                  row(D_MODEL), vec, vec],
        out_specs=row(D_MODEL),
        out_shape=jax.ShapeDtypeStruct((rows, D_MODEL), F32),
        compiler_params=_cparams("parallel"),
        name="outproj_ln",
    )(a2, b2, w, x2, g.reshape(1, D_MODEL), beta.reshape(1, D_MODEL))


def _mem_kernel(x_ref, wq_ref, mk_ref, mv_ref, wo_ref, g_ref, beta_ref, o_ref):
    x = x_ref[0]
    q = _dot(x.astype(BF16), wq_ref[...])
    qb = (q * (MEM_DH ** -0.5)).astype(BF16)
    outs = []
    for h in range(MEM_HEADS):
        sl = slice(h * MEM_DH, (h + 1) * MEM_DH)
        s = _dot_nt(qb[:, sl], mk_ref[0, :, sl].astype(BF16))
        m = jnp.max(s, axis=1, keepdims=True)
        p = jnp.exp(s - m)
        l = jnp.sum(p, axis=1, keepdims=True)
        o = _dot(p.astype(BF16), mv_ref[0, :, sl].astype(BF16)) / l
        outs.append(o.astype(BF16))
    o_all = jnp.concatenate(outs, axis=1)
    y = DN_ALPHA * x + _dot(o_all, wo_ref[...])
    o_ref[0] = _layer_norm(y, g_ref[...], beta_ref[...])


def _mem_attend_ln(x3, wq, mk, mv, wo, g, beta):
    b, t, _ = x3.shape
    tm = min(512, t)
    xs = pl.BlockSpec((1, tm, D_MODEL), lambda bi, i: (bi, i, 0))
    ws = pl.BlockSpec((D_MODEL, D_MODEL), lambda bi, i: (0, 0))
    ms = pl.BlockSpec((1, N_MEM, D_MODEL), lambda bi, i: (bi, 0, 0))
    vec = pl.BlockSpec((1, D_MODEL), lambda bi, i: (0, 0))
    return pl.pallas_call(
        _mem_kernel,
        grid=(b, t // tm),
        in_specs=[xs, ws, ms, ms, ws, vec, vec],
        out_specs=xs,
        out_shape=jax.ShapeDtypeStruct(x3.shape, F32),
        compiler_params=_cparams("parallel", "parallel"),
        name="mem_attend_ln",
    )(x3, wq, mk, mv, wo, g.reshape(1, D_MODEL), beta.reshape(1, D_MODEL))


def _ffn_kernel(x_ref, w1_ref, w2_ref, g_ref, beta_ref, o_ref, *, chunk):
    x = x_ref[...]
    xb = x.astype(BF16)
    acc = jnp.zeros(x.shape, F32)
    for c in range(D_FF // chunk):
        h = _dot(xb, w1_ref[:, c * chunk:(c + 1) * chunk])
        h = jnp.square(jnp.maximum(h, 0.0))
        acc = acc + _dot(h.astype(BF16), w2_ref[c * chunk:(c + 1) * chunk, :])
    o_ref[...] = _layer_norm(DN_ALPHA * x + acc, g_ref[...], beta_ref[...])


def _ffn_ln(x2, w1, w2, g, beta):
    rows = x2.shape[0]
    tm = _row_tile(rows)
    row = pl.BlockSpec((tm, D_MODEL), lambda i: (i, 0))
    vec = pl.BlockSpec((1, D_MODEL), lambda i: (0, 0))
    once = pl.Buffered(1)
    return pl.pallas_call(
        functools.partial(_ffn_kernel, chunk=1024),
        grid=(rows // tm,),
        in_specs=[row,
                  pl.BlockSpec(w1.shape, lambda i: (0, 0), pipeline_mode=once),
                  pl.BlockSpec(w2.shape, lambda i: (0, 0), pipeline_mode=once),
                  vec, vec],
        out_specs=row,
        out_shape=jax.ShapeDtypeStruct((rows, D_MODEL), F32),
        compiler_params=_cparams("parallel"),
        name="ffn_ln",
    )(x2, w1, w2, g.reshape(1, D_MODEL), beta.reshape(1, D_MODEL))


def _pad_rows(a, total):
    pad = total - a.shape[1]
    if pad == 0:
        return a
    return jnp.pad(a, ((0, 0), (0, pad)) + ((0, 0),) * (a.ndim - 2))


def _hist16(h):
    return jnp.pad(h, ((0, 0), (HIST_ROWS - h.shape[1], 0), (0, 0)))


def _trunk(x, mem_k, mem_v, pool_h, fk_h, fv_h, flf_h, dk_h, dv_h, conv_h, past, wts):
    b, t, _ = x.shape
    rows = b * t
    t_k = past + t
    if past == 0:
        tq = tk = 512
        dq = dk = 512
        t_kp = t_k
        cs_rows = 1024
    else:
        tq = dq = t
        tk = dk = 128
        t_kp = -(-t_k // tk) * tk
        cs_rows = t_kp
    x2 = x.reshape(rows, D_MODEL)

    u, qb, k, kb, v, vb, lf = _proj_even(x2, wts["w_in_even"], wts["w_forget"], wts["b_forget"])
    lf3 = lf.reshape(b, t, LANES)
    kb3 = kb.reshape(b, t, FOX_WIDTH)
    vb3 = vb.reshape(b, t, FOX_WIDTH)
    if past:
        lf_hist = jnp.pad(flf_h[0], ((0, 0), (0, 0), (0, LANES - FOX_HEADS)))
        lf_all = jnp.concatenate([lf_hist, lf3], axis=1)
        kb3 = jnp.concatenate([fk_h[0].reshape(b, past, FOX_WIDTH).astype(BF16), kb3], axis=1)
        vb3 = jnp.concatenate([fv_h[0].reshape(b, past, FOX_WIDTH).astype(BF16), vb3], axis=1)
    else:
        lf_all = lf3
    fcum = _cumsum_time(_pad_rows(lf_all, t_kp), cs_rows)
    ft = jnp.transpose(fcum[:, :, :FOX_HEADS], (0, 2, 1)).reshape(b, FOX_HEADS // 2, 2, t_kp)
    fox_y = _fox_attention(qb.reshape(b, t, FOX_WIDTH), _pad_rows(kb3, t_kp), _pad_rows(vb3, t_kp),
                           fcum, ft, past, tq, tk)
    u3 = u.reshape(b, t, POOL_WIDTH)
    pool_y = _pool_mix(u3, _hist16(pool_h[0]), wts["w_pool_mix"], wts["pool_scale"], past)
    x2 = _outproj_ln(pool_y.reshape(rows, POOL_WIDTH), fox_y.reshape(rows, FOX_WIDTH),
                     wts["w_out_even"], x2, wts["ln_g"][0, 0], wts["ln_b"][0, 0])
    x2 = _mem_attend_ln(x2.reshape(b, t, D_MODEL), wts["w_mem_q"][0], mem_k[0], mem_v[0],
                        wts["w_mem_o"][0], wts["ln_g"][0, 1], wts["ln_b"][0, 1]).reshape(rows, D_MODEL)
    x2 = _ffn_ln(x2, wts["w_ff1"][0], wts["w_ff2"][0], wts["ln_g"][0, 2], wts["ln_b"][0, 2])
    n_pool = u3[:, t - POOL_HIST:][None]
    n_fk = k.reshape(1, b, t, FOX_HEADS, FOX_DH)
    n_fv = v.reshape(1, b, t, FOX_HEADS, FOX_DH)
    n_flf = lf3[:, :, :FOX_HEADS][None]

    lam_init = 0.8 - 0.6 * math.exp(-0.3 * 1)
    qb, k, kb, v, vb, bg, z = _proj_odd(x2, wts["w_in_odd"])
    kb3 = kb.reshape(b, t, DIFF_QK)
    vb3 = vb.reshape(b, t, DIFF_WIDTH)
    if past:
        kb3 = jnp.concatenate([dk_h[0].reshape(b, past, DIFF_QK).astype(BF16), kb3], axis=1)
        vb3 = jnp.concatenate([dv_h[0].reshape(b, past, DIFF_WIDTH).astype(BF16), vb3], axis=1)
    att = _diff_attention(qb.reshape(b, t, DIFF_QK), _pad_rows(kb3, t_kp), _pad_rows(vb3, t_kp),
                          wts["rel_bias_table"], wts["diff_lambda"], wts["diff_subln_g"],
                          past, dq, dk, t_k - (t_kp - dk), lam_init)
    z3 = z.reshape(b, t, CONV_CH)
    conv_y = _short_conv(z3, _hist16(conv_h[0]), bg.reshape(b, t, CONV_CH), wts["conv_w"])
    x2 = _outproj_ln(att.reshape(rows, DIFF_WIDTH), conv_y.reshape(rows, CONV_CH),
                     wts["w_out_odd"], x2, wts["ln_g"][1, 0], wts["ln_b"][1, 0])
    x2 = _mem_attend_ln(x2.reshape(b, t, D_MODEL), wts["w_mem_q"][1], mem_k[1], mem_v[1],
                        wts["w_mem_o"][1], wts["ln_g"][1, 1], wts["ln_b"][1, 1]).reshape(rows, D_MODEL)
    x2 = _ffn_ln(x2, wts["w_ff1"][1], wts["w_ff2"][1], wts["ln_g"][1, 2], wts["ln_b"][1, 2])
    n_dk = k.reshape(1, b, t, DIFF_HEADS, 2, DIFF_DH)
    n_dv = v.reshape(1, b, t, DIFF_HEADS, DIFF_VD)
    n_conv = z3[:, t - (CONV_K - 1):][None]
    return (x2.reshape(b, t, D_MODEL), n_pool, n_fk, n_fv, n_flf, n_dk, n_dv, n_conv)


def kernel(x_prompt, x_sample, state_pool, cache_fox_k, cache_fox_v, cache_fox_logf,
           cache_diff_k, cache_diff_v, state_conv, cache_mem_k, cache_mem_v, mem_prompt,
           w_in_even, b_forget, w_pool_mix, pool_scale, w_out_even,
           w_in_odd, diff_lambda, diff_subln_g, conv_w, w_out_odd, rel_bias_table,
           w_mem_q, w_mem_k, w_mem_v, w_mem_o, w_ff1, w_ff2, ln_g, ln_b):
    bp = x_prompt.shape[0]
    nmain = POOL_WIDTH + 3 * FOX_WIDTH
    wts = {
        "w_in_even": w_in_even[0, :, :nmain].astype(BF16),
        "w_forget": jnp.pad(w_in_even[0, :, nmain:], ((0, 0), (0, LANES - FOX_HEADS))).astype(BF16),
        "b_forget": jnp.pad(b_forget[0], (0, LANES - FOX_HEADS)).reshape(1, LANES).astype(F32),
        "w_pool_mix": w_pool_mix[0].astype(BF16),
        "pool_scale": pool_scale[0],
        "w_out_even": w_out_even[0].astype(BF16),
        "w_in_odd": w_in_odd[0].astype(BF16),
        "diff_lambda": diff_lambda[0],
        "diff_subln_g": diff_subln_g[0],
        "conv_w": conv_w[0],
        "w_out_odd": w_out_odd[0].astype(BF16),
        "rel_bias_table": rel_bias_table,
        "w_mem_q": w_mem_q.astype(BF16),
        "w_mem_o": w_mem_o.astype(BF16),
        "w_ff1": w_ff1.astype(BF16),
        "w_ff2": w_ff2.astype(BF16),
        "ln_g": ln_g,
        "ln_b": ln_b,
    }
    kv = _mem_kv(mem_prompt.reshape(bp * N_MEM, D_MODEL),
                 jnp.stack([w_mem_k, w_mem_v]).astype(BF16))
    kv = kv.reshape(2, DEPTH, bp, N_MEM, D_MODEL)
    p_mem_k = kv[0].reshape(DEPTH, bp, N_MEM, MEM_HEADS, MEM_DH)
    p_mem_v = kv[1].reshape(DEPTH, bp, N_MEM, MEM_HEADS, MEM_DH)
    zeros = lambda *s: jnp.zeros(s, F32)
    (y_prompt, p_pool, p_fox_k, p_fox_v, p_fox_logf, p_diff_k, p_diff_v, p_conv) = _trunk(
        x_prompt, kv[0], kv[1],
        zeros(1, bp, POOL_HIST, POOL_WIDTH), None, None, None, None, None,
        zeros(1, bp, CONV_K - 1, CONV_CH), 0, wts)
    bs = x_sample.shape[0]
    (y_sample, s_pool, s_fox_k, s_fox_v, s_fox_logf, s_diff_k, s_diff_v, s_conv) = _trunk(
        x_sample, cache_mem_k.reshape(DEPTH, bs, N_MEM, D_MODEL),
        cache_mem_v.reshape(DEPTH, bs, N_MEM, D_MODEL),
        state_pool, cache_fox_k, cache_fox_v, cache_fox_logf,
        cache_diff_k, cache_diff_v, state_conv, cache_fox_k.shape[2], wts)
    return (y_prompt, y_sample,
            p_pool, p_fox_k, p_fox_v, p_fox_logf, p_diff_k, p_diff_v, p_conv, p_mem_k, p_mem_v,
            s_pool, s_fox_k, s_fox_v, s_fox_logf, s_diff_k, s_diff_v, s_conv)
```

```python
import functools
import math

import numpy as np
import jax
import jax.numpy as jnp
from jax import lax
from jax.experimental import pallas as pl
from jax.experimental.pallas import tpu as pltpu

F32 = jnp.float32
BF16 = jnp.bfloat16

D_MODEL = 1024
DEPTH = 2
CHUNK = 64
POOL_WIDTH = 512
POOL_GC = 128
POOL_WINDOWS = (2, 4, 8, 16)
POOL_HIST = 15
FOX_HEADS = 8
FOX_DH = 64
FOX_WIDTH = 512
DIFF_HEADS = 4
DIFF_DH = 64
DIFF_VD = 128
DIFF_QK = 512
DIFF_WIDTH = 512
CONV_CH = 512
CONV_K = 3
D_FF = 4096
N_MEM = 256
MEM_HEADS = 4
MEM_DH = 256
REL_BUCKETS = 32
REL_MAX_DIST = 128
DN_ALPHA = (2 * DEPTH) ** 0.25
LN_EPS = 1e-5
RMS_EPS = 1e-5
NEG = -1e30
LOG2E = math.log2(math.e)

LANES = 128
HIST_ROWS = 16
ATT_TILE = 512
VMEM_LIMIT = 56 * 1024 * 1024


def _cparams(*sem):
    return pltpu.CompilerParams(dimension_semantics=sem, vmem_limit_bytes=VMEM_LIMIT)


def _dot(a, b):
    return jnp.dot(a, b, preferred_element_type=F32)


def _dot_nt(a, b):
    return lax.dot_general(a, b, (((1,), (1,)), ((), ())), preferred_element_type=F32)


def _layer_norm(y, g, b):
    mu = jnp.mean(y, axis=-1, keepdims=True)
    d = y - mu
    var = jnp.mean(d * d, axis=-1, keepdims=True)
    return d * lax.rsqrt(var + LN_EPS) * g + b


def _row_tile(rows):
    return min(512, rows)


def _proj_even_kernel(x_ref, w_ref, wf_ref, bf_ref,
                      u_ref, q_ref, k_ref, kb_ref, v_ref, vb_ref, lf_ref):
    xb = x_ref[...].astype(BF16)

    def mm(c):
        return _dot(xb, w_ref[:, c * 512:(c + 1) * 512])

    u_ref[...] = mm(0)
    q_ref[...] = (mm(1) * (FOX_DH ** -0.5 * LOG2E)).astype(BF16)
    k = mm(2)
    k_ref[...] = k
    kb_ref[...] = k.astype(BF16)
    v = mm(3)
    v_ref[...] = v
    vb_ref[...] = v.astype(BF16)
    z = _dot(xb, wf_ref[...]) + bf_ref[...]
    lf_ref[...] = jnp.minimum(z, 0.0) - jnp.log1p(jnp.exp(-jnp.abs(z)))


def _proj_even(x2, w, wf, bf):
    rows = x2.shape[0]
    tm = _row_tile(rows)
    row = lambda n: pl.BlockSpec((tm, n), lambda i: (i, 0))
    full = lambda a: pl.BlockSpec(a.shape, lambda i: (0,) * a.ndim)
    f32o = jax.ShapeDtypeStruct((rows, 512), F32)
    bf16o = jax.ShapeDtypeStruct((rows, 512), BF16)
    return pl.pallas_call(
        _proj_even_kernel,
        grid=(rows // tm,),
        in_specs=[row(D_MODEL), full(w), full(wf), full(bf)],
        out_specs=[row(512)] * 6 + [row(LANES)],
        out_shape=[f32o, bf16o, f32o, bf16o, f32o, bf16o,
                   jax.ShapeDtypeStruct((rows, LANES), F32)],
        compiler_params=_cparams("parallel"),
        name="proj_even",
    )(x2, w, wf, bf)


def _proj_odd_kernel(x_ref, w_ref, q_ref, k_ref, kb_ref, v_ref, vb_ref, bg_ref, z_ref):
    xb = x_ref[...].astype(BF16)

    def mm(c):
        return _dot(xb, w_ref[:, c * 512:(c + 1) * 512])

    q_ref[...] = (mm(0) * (DIFF_DH ** -0.5 * LOG2E)).astype(BF16)
    k = mm(1)
    k_ref[...] = k
    kb_ref[...] = k.astype(BF16)
    v = mm(2)
    v_ref[...] = v
    vb_ref[...] = v.astype(BF16)
    bg_ref[...] = mm(3)
    z_ref[...] = mm(4) * mm(5)


def _proj_odd(x2, w):
    rows = x2.shape[0]
    tm = _row_tile(rows)
    row = lambda n: pl.BlockSpec((tm, n), lambda i: (i, 0))
    f32o = jax.ShapeDtypeStruct((rows, 512), F32)
    bf16o = jax.ShapeDtypeStruct((rows, 512), BF16)
    return pl.pallas_call(
        _proj_odd_kernel,
        grid=(rows // tm,),
        in_specs=[row(D_MODEL), pl.BlockSpec(w.shape, lambda i: (0, 0))],
        out_specs=[row(512)] * 7,
        out_shape=[bf16o, f32o, bf16o, f32o, bf16o, f32o, f32o],
        compiler_params=_cparams("parallel"),
        name="proj_odd",
    )(x2, w)


def _mem_kv_kernel(x_ref, w_ref, o_ref):
    o_ref[0, 0] = _dot(x_ref[...].astype(BF16), w_ref[0, 0])


def _mem_kv(mem2, w_kv):
    rows = mem2.shape[0]
    return pl.pallas_call(
        _mem_kv_kernel,
        grid=(2, DEPTH),
        in_specs=[pl.BlockSpec((rows, D_MODEL), lambda a, l: (0, 0)),
                  pl.BlockSpec((1, 1, D_MODEL, D_MODEL), lambda a, l: (a, l, 0, 0))],
        out_specs=pl.BlockSpec((1, 1, rows, D_MODEL), lambda a, l: (a, l, 0, 0)),
        out_shape=jax.ShapeDtypeStruct((2, DEPTH, rows, D_MODEL), F32),
        compiler_params=_cparams("parallel", "parallel"),
        name="mem_kv",
    )(mem2, w_kv)


def _cumsum_kernel(x_ref, f_ref, kaug_ref, qaug_ref, carry_ref, *, rows):
    @pl.when(pl.program_id(1) == 0)
    def _():
        carry_ref[...] = jnp.zeros_like(carry_ref)

    r = lax.broadcasted_iota(jnp.int32, (LANES, LANES), 0)
    c = lax.broadcasted_iota(jnp.int32, (LANES, LANES), 1)
    tri = (r >= c).astype(F32)
    grp = c // FOX_HEADS
    carry = carry_ref[...]
    for s in range(rows // LANES):
        sl = slice(s * LANES, (s + 1) * LANES)
        cs = jnp.dot(tri, x_ref[0, sl, :], preferred_element_type=F32,
                     precision=lax.Precision.HIGHEST) + carry
        carry = cs[LANES - 1:LANES, :]
        f = cs * LOG2E
        f_ref[0, sl, :] = f
        hi = f.astype(BF16).astype(F32)
        mid = (f - hi).astype(BF16).astype(F32)
        lo = (f - hi - mid).astype(BF16).astype(F32)
        piece = jnp.where(grp % 3 == 0, hi, jnp.where(grp % 3 == 1, mid, lo))
        kaug_ref[0, sl, :] = jnp.where(grp < 3, -piece, jnp.where(grp < 6, 1.0, 0.0)).astype(BF16)
        qaug_ref[0, sl, :] = jnp.where(grp < 3, 1.0, jnp.where(grp < 6, piece, 0.0)).astype(BF16)
    carry_ref[...] = carry


def _cumsum_time(x, rows):
    b, t, _ = x.shape
    spec = pl.BlockSpec((1, rows, LANES), lambda i, j: (i, j, 0))
    return pl.pallas_call(
        functools.partial(_cumsum_kernel, rows=rows),
        grid=(b, t // rows),
        in_specs=[spec],
        out_specs=[spec] * 3,
        out_shape=[jax.ShapeDtypeStruct(x.shape, F32), jax.ShapeDtypeStruct(x.shape, BF16),
                   jax.ShapeDtypeStruct(x.shape, BF16)],
        scratch_shapes=[pltpu.VMEM((1, LANES), F32)],
        compiler_params=_cparams("parallel", "arbitrary"),
        name="cumsum_time",
    )(x)


def _online_softmax(s, m_old, shift=None):
    row_max = jnp.max(s, axis=1, keepdims=True)
    if shift is not None:
        row_max = row_max + shift
    m_new = jnp.maximum(m_old, row_max)
    m_sub = m_new if shift is None else m_new - shift
    p = jnp.exp2(s - jnp.concatenate([m_sub] * (s.shape[1] // LANES), axis=1))
    return p.astype(BF16), jnp.exp2(m_old - m_new), m_new


def _fox_kernel(q_ref, k_ref, v_ref, qaug_ref, kaug_ref, mask_ref, o_ref,
                qcat_ref, s_ref, p_ref, acc_ref, m_ref, al_ref, *, tq, tk):
    pair = pl.program_id(1)
    j_last = pl.program_id(2)
    lane = lax.broadcasted_iota(jnp.int32, (1, LANES), 1)
    upper = lane >= FOX_DH
    sels = (jnp.logical_not(upper), upper)
    q2 = q_ref[0]
    qa = qaug_ref[0]
    for hh in range(2):
        own = jnp.logical_and(lane % FOX_HEADS == 2 * pair + hh, lane < 6 * FOX_HEADS)
        qcat_ref[hh] = jnp.concatenate([jnp.where(sels[hh], q2, jnp.zeros_like(q2)),
                                        jnp.where(own, qa, jnp.zeros_like(qa))], axis=1)
    m_ref[...] = jnp.full(m_ref.shape, NEG, F32)
    acc_ref[...] = jnp.zeros(acc_ref.shape, F32)
    p_ref[1] = jnp.zeros(p_ref.shape[1:], p_ref.dtype)
    al_ref[1] = jnp.ones(al_ref.shape[1:], F32)

    def scores(hh, j):
        rows = pl.ds(pl.multiple_of(j * tk, tk), tk)
        kcat = jnp.concatenate([k_ref[0, rows, :], kaug_ref[0, rows, :]], axis=1)
        s_ref[hh] = _dot_nt(qcat_ref[hh], kcat)

    def probs(hh, masked):
        s = s_ref[hh]
        if masked:
            s = s + mask_ref[...]
        p_ref[hh], al_ref[hh], m_ref[hh] = _online_softmax(s, m_ref[hh])

    def pv(hh, j):
        vb = v_ref[0, pl.ds(pl.multiple_of(j * tk, tk), tk), :]
        va = jnp.where(sels[hh], vb, jnp.ones_like(vb))
        acc_ref[hh] = al_ref[hh] * acc_ref[hh] + _dot(p_ref[hh], va)

    scores(0, 0)

    def body(j, carry):
        scores(1, j)
        probs(0, False)
        pv(1, jnp.maximum(j - 1, 0))
        scores(0, j + 1)
        probs(1, False)
        pv(0, j)
        return carry

    lax.fori_loop(0, j_last, body, 0)
    scores(1, j_last)
    probs(0, True)
    pv(1, jnp.maximum(j_last - 1, 0))
    probs(1, True)
    pv(0, j_last)
    pv(1, j_last)

    a0 = acc_ref[0]
    a1 = acc_ref[1]
    o0 = a0 / pltpu.roll(a0, FOX_DH, 1)
    o1 = a1 / pltpu.roll(a1, FOX_DH, 1)
    o_ref[0] = jnp.where(upper, o1, o0).astype(o_ref.dtype)


def _fox_attention(qb, kb, vb, qaug, kaug, tq, tk):
    b, t, _ = qb.shape
    assert t % tq == 0 and tq == tk
    ii = np.arange(tq)[:, None]
    jj = np.arange(tk)[None, :]
    mask = jnp.asarray(np.where(jj <= ii, 0.0, NEG), F32)
    return pl.pallas_call(
        functools.partial(_fox_kernel, tq=tq, tk=tk),
        grid=(b, FOX_HEADS // 2, t // tq),
        in_specs=[
            pl.BlockSpec((1, tq, LANES), lambda bi, p, qi: (bi, qi, p)),
            pl.BlockSpec((1, t, LANES), lambda bi, p, qi: (bi, 0, p)),
            pl.BlockSpec((1, t, LANES), lambda bi, p, qi: (bi, 0, p)),
            pl.BlockSpec((1, tq, LANES), lambda bi, p, qi: (bi, qi, 0)),
            pl.BlockSpec((1, t, LANES), lambda bi, p, qi: (bi, 0, 0)),
            pl.BlockSpec((tq, tk), lambda bi, p, qi: (0, 0)),
        ],
        out_specs=pl.BlockSpec((1, tq, LANES), lambda bi, p, qi: (bi, qi, p)),
        out_shape=jax.ShapeDtypeStruct((b, t, FOX_WIDTH), BF16),
        scratch_shapes=[pltpu.VMEM((2, tq, 2 * LANES), BF16),
                        pltpu.VMEM((2, tq, tk), F32),
                        pltpu.VMEM((2, tq, tk), BF16),
                        pltpu.VMEM((2, tq, LANES), F32),
                        pltpu.VMEM((2, tq, LANES), F32),
                        pltpu.VMEM((2, tq, LANES), F32)],
        compiler_params=_cparams("parallel", "parallel", "arbitrary"),
        name="fox_attention",
    )(qb, kb, vb, qaug, kaug, mask)


def _diff_kernel(cfar_ref, q_ref, k_ref, v_ref, tiles_ref, lam_ref, g_ref, o_ref,
                 qm_ref, s_ref, p_ref, acc_ref, m_ref, al_ref, *, tq, tk, past, lam_init):
    h = pl.program_id(1)
    qi = pl.program_id(2)
    j_last = (past + qi * tq) // tk
    lane = lax.broadcasted_iota(jnp.int32, (1, LANES), 1)
    upper = lane >= DIFF_DH
    q2 = q_ref[0]
    zero = jnp.zeros_like(q2)
    qm_ref[0] = jnp.where(upper, zero, q2)
    qm_ref[1] = jnp.where(upper, q2, zero)
    cfar = cfar_ref[h]
    m_ref[...] = jnp.full(m_ref.shape, NEG, F32)
    acc_ref[...] = jnp.zeros(acc_ref.shape, F32)
    p_ref[1] = jnp.zeros(p_ref.shape[1:], p_ref.dtype)
    al_ref[1] = jnp.ones(al_ref.shape[1:], F32)

    def scores(mm, j):
        start = pl.multiple_of(j * tk, tk)
        s_ref[mm] = _dot_nt(qm_ref[mm], k_ref[0, pl.ds(start, tk), :])

    def probs(mm, tile):
        if tile is None:
            out = _online_softmax(s_ref[mm], m_ref[mm], shift=cfar)
        else:
            out = _online_softmax(s_ref[mm] + tiles_ref[0, tile], m_ref[mm])
        p_ref[mm], al_ref[mm], m_ref[mm] = out

    def pv(mm, j):
        start = pl.multiple_of(j * tk, tk)
        vb = v_ref[0, pl.ds(start, tk), :]
        va = jnp.concatenate([vb, jnp.ones_like(vb)], axis=1)
        al = al_ref[mm]
        acc_ref[mm] = jnp.concatenate([al, al], axis=1) * acc_ref[mm] + _dot(p_ref[mm], va)

    def block(j, tile):
        scores(1, j)
        probs(0, tile)
        pv(1, jnp.maximum(j - 1, 0))
        scores(0, j + 1)
        probs(1, tile)
        pv(0, j)

    scores(0, 0)

    def body(j, carry):
        block(j, None)
        return carry

    lax.fori_loop(0, jnp.maximum(j_last - 1, 0), body, 0)

    @pl.when(j_last >= 1)
    def _():
        block(j_last - 1, 1)

    scores(1, j_last)
    probs(0, 0)
    pv(1, jnp.maximum(j_last - 1, 0))
    probs(1, 0)
    pv(0, j_last)
    pv(1, j_last)

    lam = _diff_lambda(lam_ref[...], lam_init)
    a0 = acc_ref[0]
    a1 = acc_ref[1]
    o = a0[:, :LANES] / a0[:, LANES:] - lam * (a1[:, :LANES] / a1[:, LANES:])
    ms = jnp.mean(o * o, axis=1, keepdims=True)
    o = o * lax.rsqrt(ms + RMS_EPS) * g_ref[...] * (1.0 - lam_init)
    o_ref[0] = o.astype(o_ref.dtype)


def _t5_bucket(rel):
    nb = REL_BUCKETS // 2
    max_exact = nb // 2
    ret = jnp.where(rel > 0, nb, 0)
    n = jnp.abs(rel)
    nf = jnp.maximum(n, 1).astype(F32)
    large = max_exact + (jnp.log(nf / max_exact) / math.log(REL_MAX_DIST / max_exact)
                         * (nb - max_exact)).astype(jnp.int32)
    large = jnp.minimum(large, nb - 1)
    return ret + jnp.where(n < max_exact, n, large)


def _toeplitz_kernel(w_ref, mask_ref, o_ref, *, tq, width):
    n = w_ref.shape[2]
    for d in range(w_ref.shape[1]):
        w = jnp.broadcast_to(w_ref[0, d:d + 1, :], (tq, n))
        o_ref[0, d] = pltpu.roll(w, 0, 1, stride=1, stride_axis=0)[:, :width] + mask_ref[d]


def _t5_bias_tiles(rel_table, tq, width, offsets, masks):
    nt = len(offsets)
    n = -(-(tq + width) // LANES) * LANES
    m = jnp.arange(n)
    rel = jnp.where(m < width, m, m - n)
    w = jnp.stack([rel_table[_t5_bucket(rel + d)] for d in offsets])
    w = jnp.transpose(w, (2, 0, 1)).astype(F32) * LOG2E
    return pl.pallas_call(
        functools.partial(_toeplitz_kernel, tq=tq, width=width),
        grid=(DIFF_HEADS,),
        in_specs=[pl.BlockSpec((1, nt, n), lambda h: (h, 0, 0)),
                  pl.BlockSpec((nt, tq, width), lambda h: (0, 0, 0))],
        out_specs=pl.BlockSpec((1, nt, tq, width), lambda h: (h, 0, 0, 0)),
        out_shape=jax.ShapeDtypeStruct((DIFF_HEADS, nt, tq, width), F32),
        compiler_params=_cparams("parallel"),
        name="t5_bias_tiles",
    )(w, jnp.asarray(masks, F32))


def _diff_lambda(lp, lam_init):
    return (jnp.exp(jnp.sum(lp[0:1] * lp[1:2], keepdims=True))
            - jnp.exp(jnp.sum(lp[2:3] * lp[3:4], keepdims=True)) + lam_init)


def _diff_attention(qb, kb, vb, rel_table, diff_lambda, subln_g, tq, tk, lam_init):
    b, t, _ = qb.shape
    assert t % tq == 0 and tq == tk and tk >= REL_MAX_DIST and tk % CHUNK == 0
    ii = np.arange(tq)[:, None]
    jj = np.arange(tk)[None, :]
    masks = np.zeros((2, tq, tk), np.float32)
    masks[0] = np.where((jj // CHUNK) <= (ii // CHUNK), 0.0, NEG)
    tiles = _t5_bias_tiles(rel_table, tq, tk, (0, -tk), masks)
    cfar = rel_table[_t5_bucket(jnp.asarray(-2 * REL_MAX_DIST, jnp.int32))].astype(F32) * LOG2E
    return pl.pallas_call(
        functools.partial(_diff_kernel, tq=tq, tk=tk, past=0, lam_init=lam_init),
        grid=(b, DIFF_HEADS, t // tq),
        in_specs=[
            pl.BlockSpec(memory_space=pltpu.SMEM),
            pl.BlockSpec((1, tq, LANES), lambda bi, h, qi: (bi, qi, h)),
            pl.BlockSpec((1, t, LANES), lambda bi, h, qi: (bi, 0, h)),
            pl.BlockSpec((1, t, LANES), lambda bi, h, qi: (bi, 0, h)),
            pl.BlockSpec((1, 2, tq, tk), lambda bi, h, qi: (h, 0, 0, 0)),
            pl.BlockSpec((4, DIFF_DH), lambda bi, h, qi: (0, 0)),
            pl.BlockSpec((1, DIFF_VD), lambda bi, h, qi: (0, 0)),
        ],
        out_specs=pl.BlockSpec((1, tq, LANES), lambda bi, h, qi: (bi, qi, h)),
        out_shape=jax.ShapeDtypeStruct((b, t, DIFF_WIDTH), BF16),
        scratch_shapes=[pltpu.VMEM((2, tq, LANES), BF16),
                        pltpu.VMEM((2, tq, tk), F32),
                        pltpu.VMEM((2, tq, tk), BF16),
                        pltpu.VMEM((2, tq, 2 * LANES), F32),
                        pltpu.VMEM((2, tq, LANES), F32),
                        pltpu.VMEM((2, tq, LANES), F32)],
        compiler_params=_cparams("parallel", "parallel", "arbitrary"),
        name="diff_attention",
    )(cfar, qb, kb, vb, tiles, diff_lambda, subln_g.reshape(1, DIFF_VD))


def _split_halves(q2):
    upper = lax.broadcasted_iota(jnp.int32, (1, LANES), 1) >= LANES // 2
    zero = jnp.zeros_like(q2)
    return jnp.concatenate([jnp.where(upper, zero, q2), jnp.where(upper, q2, zero)], axis=0)


def _cat_keys(cache_ref, new, sl, t_kp):
    c = cache_ref[0, :, sl].astype(BF16)
    pad = t_kp - c.shape[0] - new.shape[0]
    return jnp.concatenate([c, new[:, sl], jnp.zeros((pad, LANES), BF16)], axis=0)


def _softmax_pv(s, vcat):
    m = jnp.max(s, axis=1, keepdims=True)
    p = jnp.exp2(s - m)
    l = jnp.sum(p, axis=1, keepdims=True)
    return _dot(p.astype(BF16), vcat) / l


def _fox_decode_kernel(q_ref, kn_ref, vn_ref, kc_ref, vc_ref, fq_ref, ft_ref, mask_ref, o_ref, *, t):
    t_kp = mask_ref.shape[1]
    upper = lax.broadcasted_iota(jnp.int32, (1, LANES), 1) >= FOX_DH
    q = q_ref[0]
    kn = kn_ref[0]
    vn = vn_ref[0]
    fblk = fq_ref[0]
    outs = []
    for p in range(FOX_HEADS // 2):
        sl = slice(p * LANES, (p + 1) * LANES)
        qs = _split_halves(q[:, sl])
        kcat = _cat_keys(kc_ref, kn, sl, t_kp)
        vcat = _cat_keys(vc_ref, vn, sl, t_kp)
        fqs = jnp.concatenate([fblk[:, 2 * p:2 * p + 1], fblk[:, 2 * p + 1:2 * p + 2]], axis=0)
        ftp = ft_ref[0, p]
        fk = jnp.concatenate([jnp.broadcast_to(ftp[0:1], (t, t_kp)),
                              jnp.broadcast_to(ftp[1:2], (t, t_kp))], axis=0)
        s = _dot_nt(qs, kcat) + (fqs - fk) + mask_ref[...]
        o = _softmax_pv(s, vcat)
        outs.append(jnp.where(upper, o[t:], o[:t]))
    o_ref[0] = jnp.concatenate(outs, axis=1).astype(o_ref.dtype)


def _decode_specs(t, past):
    new = pl.BlockSpec((1, t, 512), lambda bi: (bi, 0, 0))
    cache = pl.BlockSpec((1, past, 512), lambda bi: (bi, 0, 0))
    return new, cache


def _fox_decode(qb, kn, vn, kc, vc, fcum, ft, past):
    b, t, _ = qb.shape
    t_kp = fcum.shape[1]
    assert past % t == 0
    rr = np.arange(2 * t)[:, None] % t
    cc = np.arange(t_kp)[None, :]
    mask = jnp.asarray(np.where(cc <= past + rr, 0.0, NEG), F32)
    new, cache = _decode_specs(t, past)
    return pl.pallas_call(
        functools.partial(_fox_decode_kernel, t=t),
        grid=(b,),
        in_specs=[new, new, new, cache, cache,
                  pl.BlockSpec((1, t, LANES), lambda bi: (bi, past // t, 0)),
                  pl.BlockSpec((1, FOX_HEADS // 2, 2, t_kp), lambda bi: (bi, 0, 0, 0)),
                  pl.BlockSpec((2 * t, t_kp), lambda bi: (0, 0))],
        out_specs=new,
        out_shape=jax.ShapeDtypeStruct((b, t, FOX_WIDTH), BF16),
        compiler_params=_cparams("parallel"),
        name="fox_decode",
    )(qb, kn, vn, kc, vc, fcum, ft, mask)


def _diff_decode_kernel(q_ref, kn_ref, vn_ref, kc_ref, vc_ref, bias_ref, lam_ref, g_ref, o_ref,
                        *, t, lam_init):
    t_kp = bias_ref.shape[3]
    q = q_ref[0]
    kn = kn_ref[0]
    vn = vn_ref[0]
    lam = _diff_lambda(lam_ref[...], lam_init)
    outs = []
    for h in range(DIFF_HEADS):
        sl = slice(h * LANES, (h + 1) * LANES)
        qs = _split_halves(q[:, sl])
        kcat = _cat_keys(kc_ref, kn, sl, t_kp)
        vcat = _cat_keys(vc_ref, vn, sl, t_kp)
        bias = bias_ref[h, 0]
        s = _dot_nt(qs, kcat) + jnp.concatenate([bias, bias], axis=0)
        o = _softmax_pv(s, vcat)
        a = o[:t] - lam * o[t:]
        ms = jnp.mean(a * a, axis=1, keepdims=True)
        outs.append(a * lax.rsqrt(ms + RMS_EPS) * g_ref[...] * (1.0 - lam_init))
    o_ref[0] = jnp.concatenate(outs, axis=1).astype(o_ref.dtype)


def _diff_decode(qb, kn, vn, kc, vc, rel_table, diff_lambda, subln_g, past, lam_init):
    b, t, _ = qb.shape
    t_kp = -(-(past + t) // LANES) * LANES
    q_pos = past + np.arange(t)[:, None]
    k_pos = np.arange(t_kp)[None, :]
    visible = ((k_pos // CHUNK) <= (q_pos // CHUNK)) & (k_pos < past + t)
    tiles = _t5_bias_tiles(rel_table, t, t_kp, (-past,), np.where(visible, 0.0, NEG)[None])
    new, cache = _decode_specs(t, past)
    return pl.pallas_call(
        functools.partial(_diff_decode_kernel, t=t, lam_init=lam_init),
        grid=(b,),
        in_specs=[new, new, new, cache, cache,
                  pl.BlockSpec(tiles.shape, lambda bi: (0, 0, 0, 0)),
                  pl.BlockSpec((4, DIFF_DH), lambda bi: (0, 0)),
                  pl.BlockSpec((1, DIFF_VD), lambda bi: (0, 0))],
        out_specs=new,
        out_shape=jax.ShapeDtypeStruct((b, t, DIFF_WIDTH), BF16),
        compiler_params=_cparams("parallel"),
        name="diff_decode",
    )(qb, kn, vn, kc, vc, tiles, diff_lambda, subln_g.reshape(1, DIFF_VD))


def _pool_kernel(u_ref, prev_ref, hist_ref, wmix_ref, scale_ref, o_ref, ext_ref, *, tm, past):
    i = pl.program_id(1)
    u = u_ref[0]
    ext_ref[0:HIST_ROWS, :] = jnp.where(i == 0, hist_ref[0], prev_ref[0])
    ext_ref[HIST_ROWS:HIST_ROWS + tm, :] = u
    pos = past + i * tm + lax.broadcasted_iota(jnp.int32, (tm, 1), 0)
    for g, w in enumerate(POOL_WINDOWS):
        sl = slice(g * POOL_GC, (g + 1) * POOL_GC)
        ug = u[:, sl]
        wsum = ug
        for s in range(1, w):
            wsum = wsum + ext_ref[HIST_ROWS - s:HIST_ROWS - s + tm, sl]
        cnt = jnp.minimum(w, pos + 1).astype(F32)
        d = wsum / cnt - ug
        y = _dot(d.astype(BF16), wmix_ref[g]) * scale_ref[:, sl]
        o_ref[0, :, sl] = y.astype(o_ref.dtype)


def _halo_specs(tm, width):
    cur = pl.BlockSpec((1, tm, width), lambda b, i: (b, i, 0))
    prev = pl.BlockSpec((1, HIST_ROWS, width),
                        lambda b, i: (b, jnp.maximum(i * (tm // HIST_ROWS) - 1, 0), 0))
    hist = pl.BlockSpec((1, HIST_ROWS, width), lambda b, i: (b, 0, 0))
    return cur, prev, hist


def _pool_mix(u3, hist16, wmix, scale, past):
    b, t, _ = u3.shape
    tm = min(512, t)
    cur, prev, hist = _halo_specs(tm, POOL_WIDTH)
    return pl.pallas_call(
        functools.partial(_pool_kernel, tm=tm, past=past),
        grid=(b, t // tm),
        in_specs=[cur, prev, hist,
                  pl.BlockSpec(wmix.shape, lambda bi, i: (0, 0, 0)),
                  pl.BlockSpec((1, POOL_WIDTH), lambda bi, i: (0, 0))],
        out_specs=cur,
        out_shape=jax.ShapeDtypeStruct(u3.shape, BF16),
        scratch_shapes=[pltpu.VMEM((HIST_ROWS + tm, POOL_WIDTH), F32)],
        compiler_params=_cparams("parallel", "parallel"),
        name="pool_mix",
    )(u3, u3, hist16, wmix, scale.reshape(1, POOL_WIDTH))


def _conv_kernel(z_ref, prev_ref, hist_ref, bg_ref, w_ref, o_ref, ext_ref, *, tm):
    i = pl.program_id(1)
    z = z_ref[0]
    ext_ref[0:HIST_ROWS, :] = jnp.where(i == 0, hist_ref[0], prev_ref[0])
    ext_ref[HIST_ROWS:HIST_ROWS + tm, :] = z
    y = (ext_ref[HIST_ROWS - 2:HIST_ROWS - 2 + tm, :] * w_ref[0:1, :]
         + ext_ref[HIST_ROWS - 1:HIST_ROWS - 1 + tm, :] * w_ref[1:2, :]
         + z * w_ref[2:3, :])
    o_ref[0] = (bg_ref[0] * y).astype(o_ref.dtype)


def _short_conv(z3, hist16, bg3, w):
    b, t, _ = z3.shape
    tm = min(512, t)
    cur, prev, hist = _halo_specs(tm, CONV_CH)
    return pl.pallas_call(
        functools.partial(_conv_kernel, tm=tm),
        grid=(b, t // tm),
        in_specs=[cur, prev, hist, cur, pl.BlockSpec((CONV_K, CONV_CH), lambda bi, i: (0, 0))],
        out_specs=cur,
        out_shape=jax.ShapeDtypeStruct(z3.shape, BF16),
        scratch_shapes=[pltpu.VMEM((HIST_ROWS + tm, CONV_CH), F32)],
        compiler_params=_cparams("parallel", "parallel"),
        name="short_conv",
    )(z3, z3, hist16, bg3, w)


def _outproj_kernel(a_ref, b_ref, w_ref, x_ref, g_ref, beta_ref, o_ref):
    half = w_ref.shape[0] // 2
    mix = _dot(a_ref[...], w_ref[0:half, :]) + _dot(b_ref[...], w_ref[half:, :])
    o_ref[...] = _layer_norm(DN_ALPHA * x_ref[...] + mix, g_ref[...], beta_ref[...])


def _outproj_ln(a2, b2, w, x2, g, beta):
    rows = x2.shape[0]
    tm = _row_tile(rows)
    row = lambda n: pl.BlockSpec((tm, n), lambda i: (i, 0))
    vec = pl.BlockSpec((1, D_MODEL), lambda i: (0, 0))
    return pl.pallas_call(
        _outproj_kernel,
        grid=(rows // tm,),
        in_specs=[row(512), row(512), pl.BlockSpec(w.shape, lambda i: (0, 0)),
                  row(D_MODEL), vec, vec],
        out_specs=row(D_MODEL),
        out_shape=jax.ShapeDtypeStruct((rows, D_MODEL), F32),
        compiler_params=_cparams("parallel"),
        name="outproj_ln",
    )(a2, b2, w, x2, g.reshape(1, D_MODEL), beta.reshape(1, D_MODEL))


def _mem_kernel(x_ref, wq_ref, mk_ref, mv_ref, wo_ref, g_ref, beta_ref, o_ref):
    x = x_ref[0]
    q = _dot(x.astype(BF16), wq_ref[...])
    qb = (q * (MEM_DH ** -0.5)).astype(BF16)
    outs = []
    for h in range(MEM_HEADS):
        sl = slice(h * MEM_DH, (h + 1) * MEM_DH)
        s = _dot_nt(qb[:, sl], mk_ref[0, :, sl].astype(BF16))
        m = jnp.max(s, axis=1, keepdims=True)
        p = jnp.exp(s - m)
        l = jnp.sum(p, axis=1, keepdims=True)
        o = _dot(p.astype(BF16), mv_ref[0, :, sl].astype(BF16)) / l
        outs.append(o.astype(BF16))
    o_all = jnp.concatenate(outs, axis=1)
    y = DN_ALPHA * x + _dot(o_all, wo_ref[...])
    o_ref[0] = _layer_norm(y, g_ref[...], beta_ref[...])


def _mem_attend_ln(x3, wq, mk, mv, wo, g, beta):
    b, t, _ = x3.shape
    tm = min(512, t)
    xs = pl.BlockSpec((1, tm, D_MODEL), lambda bi, i: (bi, i, 0))
    ws = pl.BlockSpec((D_MODEL, D_MODEL), lambda bi, i: (0, 0))
    ms = pl.BlockSpec((1, N_MEM, D_MODEL), lambda bi, i: (bi, 0, 0))
    vec = pl.BlockSpec((1, D_MODEL), lambda bi, i: (0, 0))
    return pl.pallas_call(
        _mem_kernel,
        grid=(b, t // tm),
        in_specs=[xs, ws, ms, ms, ws, vec, vec],
        out_specs=xs,
        out_shape=jax.ShapeDtypeStruct(x3.shape, F32),
        compiler_params=_cparams("parallel", "parallel"),
        name="mem_attend_ln",
    )(x3, wq, mk, mv, wo, g.reshape(1, D_MODEL), beta.reshape(1, D_MODEL))


def _ffn_kernel(x_ref, w1_ref, w2_ref, g_ref, beta_ref, o_ref, *, chunk):
    x = x_ref[...]
    xb = x.astype(BF16)
    acc = jnp.zeros(x.shape, F32)
    for c in range(D_FF // chunk):
        h = _dot(xb, w1_ref[:, c * chunk:(c + 1) * chunk])
        h = jnp.square(jnp.maximum(h, 0.0))
        acc = acc + _dot(h.astype(BF16), w2_ref[c * chunk:(c + 1) * chunk, :])
    o_ref[...] = _layer_norm(DN_ALPHA * x + acc, g_ref[...], beta_ref[...])


def _ffn_ln(x2, w1, w2, g, beta):
    rows = x2.shape[0]
    tm = _row_tile(rows)
    row = pl.BlockSpec((tm, D_MODEL), lambda i: (i, 0))
    vec = pl.BlockSpec((1, D_MODEL), lambda i: (0, 0))
    once = pl.Buffered(1)
    return pl.pallas_call(
        functools.partial(_ffn_kernel, chunk=1024),
        grid=(rows // tm,),
        in_specs=[row,
                  pl.BlockSpec(w1.shape, lambda i: (0, 0), pipeline_mode=once),
                  pl.BlockSpec(w2.shape, lambda i: (0, 0), pipeline_mode=once),
                  vec, vec],
        out_specs=row,
        out_shape=jax.ShapeDtypeStruct((rows, D_MODEL), F32),
        compiler_params=_cparams("parallel"),
        name="ffn_ln",
    )(x2, w1, w2, g.reshape(1, D_MODEL), beta.reshape(1, D_MODEL))


def _pad_rows(a, total):
    pad = total - a.shape[1]
    if pad == 0:
        return a
    return jnp.pad(a, ((0, 0), (0, pad)) + ((0, 0),) * (a.ndim - 2))


def _hist16(h):
    return jnp.pad(h, ((0, 0), (HIST_ROWS - h.shape[1], 0), (0, 0)))


def _trunk(x, mem_k, mem_v, pool_h, fk_h, fv_h, flf_h, dk_h, dv_h, conv_h, past, wts):
    b, t, _ = x.shape
    rows = b * t
    t_k = past + t
    if past == 0:
        t_kp = t_k
        cs_rows = 1024
    else:
        t_kp = -(-t_k // LANES) * LANES
        cs_rows = t_kp
    x2 = x.reshape(rows, D_MODEL)

    u, qb, k, kb, v, vb, lf = _proj_even(x2, wts["w_in_even"], wts["w_forget"], wts["b_forget"])
    lf3 = lf.reshape(b, t, LANES)
    qb3 = qb.reshape(b, t, FOX_WIDTH)
    kb3 = kb.reshape(b, t, FOX_WIDTH)
    vb3 = vb.reshape(b, t, FOX_WIDTH)
    if past:
        lf_hist = jnp.pad(flf_h[0], ((0, 0), (0, 0), (0, LANES - FOX_HEADS)))
        lf_all = jnp.concatenate([lf_hist, lf3], axis=1)
    else:
        lf_all = lf3
    fcum, kaug, qaug = _cumsum_time(_pad_rows(lf_all, t_kp), cs_rows)
    if past:
        ft = jnp.transpose(fcum[:, :, :FOX_HEADS], (0, 2, 1)).reshape(b, FOX_HEADS // 2, 2, t_kp)
        fox_y = _fox_decode(qb3, kb3, vb3, fk_h[0].reshape(b, past, FOX_WIDTH),
                            fv_h[0].reshape(b, past, FOX_WIDTH), fcum, ft, past)
    else:
        fox_y = _fox_attention(qb3, kb3, vb3, qaug, kaug, ATT_TILE, ATT_TILE)
    u3 = u.reshape(b, t, POOL_WIDTH)
    pool_y = _pool_mix(u3, _hist16(pool_h[0]), wts["w_pool_mix"], wts["pool_scale"], past)
    x2 = _outproj_ln(pool_y.reshape(rows, POOL_WIDTH), fox_y.reshape(rows, FOX_WIDTH),
                     wts["w_out_even"], x2, wts["ln_g"][0, 0], wts["ln_b"][0, 0])
    x2 = _mem_attend_ln(x2.reshape(b, t, D_MODEL), wts["w_mem_q"][0], mem_k[0], mem_v[0],
                        wts["w_mem_o"][0], wts["ln_g"][0, 1], wts["ln_b"][0, 1]).reshape(rows, D_MODEL)
    x2 = _ffn_ln(x2, wts["w_ff1"][0], wts["w_ff2"][0], wts["ln_g"][0, 2], wts["ln_b"][0, 2])
    n_pool = u3[:, t - POOL_HIST:][None]
    n_fk = k.reshape(1, b, t, FOX_HEADS, FOX_DH)
    n_fv = v.reshape(1, b, t, FOX_HEADS, FOX_DH)
    n_flf = lf3[:, :, :FOX_HEADS][None]

    lam_init = 0.8 - 0.6 * math.exp(-0.3 * 1)
    qb, k, kb, v, vb, bg, z = _proj_odd(x2, wts["w_in_odd"])
    qb3 = qb.reshape(b, t, DIFF_QK)
    kb3 = kb.reshape(b, t, DIFF_QK)
    vb3 = vb.reshape(b, t, DIFF_WIDTH)
    if past:
        att = _diff_decode(qb3, kb3, vb3, dk_h[0].reshape(b, past, DIFF_QK),
                           dv_h[0].reshape(b, past, DIFF_WIDTH), wts["rel_bias_table"],
                           wts["diff_lambda"], wts["diff_subln_g"], past, lam_init)
    else:
        att = _diff_attention(qb3, kb3, vb3, wts["rel_bias_table"], wts["diff_lambda"],
                              wts["diff_subln_g"], ATT_TILE, ATT_TILE, lam_init)
    z3 = z.reshape(b, t, CONV_CH)
    conv_y = _short_conv(z3, _hist16(conv_h[0]), bg.reshape(b, t, CONV_CH), wts["conv_w"])
    x2 = _outproj_ln(att.reshape(rows, DIFF_WIDTH), conv_y.reshape(rows, CONV_CH),
                     wts["w_out_odd"], x2, wts["ln_g"][1, 0], wts["ln_b"][1, 0])
    x2 = _mem_attend_ln(x2.reshape(b, t, D_MODEL), wts["w_mem_q"][1], mem_k[1], mem_v[1],
                        wts["w_mem_o"][1], wts["ln_g"][1, 1], wts["ln_b"][1, 1]).reshape(rows, D_MODEL)
    x2 = _ffn_ln(x2, wts["w_ff1"][1], wts["w_ff2"][1], wts["ln_g"][1, 2], wts["ln_b"][1, 2])
    n_dk = k.reshape(1, b, t, DIFF_HEADS, 2, DIFF_DH)
    n_dv = v.reshape(1, b, t, DIFF_HEADS, DIFF_VD)
    n_conv = z3[:, t - (CONV_K - 1):][None]
    return (x2.reshape(b, t, D_MODEL), n_pool, n_fk, n_fv, n_flf, n_dk, n_dv, n_conv)


def kernel(x_prompt, x_sample, state_pool, cache_fox_k, cache_fox_v, cache_fox_logf,
           cache_diff_k, cache_diff_v, state_conv, cache_mem_k, cache_mem_v, mem_prompt,
           w_in_even, b_forget, w_pool_mix, pool_scale, w_out_even,
           w_in_odd, diff_lambda, diff_subln_g, conv_w, w_out_odd, rel_bias_table,
           w_mem_q, w_mem_k, w_mem_v, w_mem_o, w_ff1, w_ff2, ln_g, ln_b):
    bp = x_prompt.shape[0]
    nmain = POOL_WIDTH + 3 * FOX_WIDTH
    wts = {
        "w_in_even": w_in_even[0, :, :nmain].astype(BF16),
        "w_forget": jnp.pad(jnp.tile(w_in_even[0, :, nmain:], (1, 6)),
                            ((0, 0), (0, LANES - 6 * FOX_HEADS))).astype(BF16),
        "b_forget": jnp.pad(jnp.tile(b_forget[0], 6), (0, LANES - 6 * FOX_HEADS)).reshape(1, LANES).astype(F32),
        "w_pool_mix": w_pool_mix[0].astype(BF16),
        "pool_scale": pool_scale[0],
        "w_out_even": w_out_even[0].astype(BF16),
        "w_in_odd": w_in_odd[0].astype(BF16),
        "diff_lambda": diff_lambda[0],
        "diff_subln_g": diff_subln_g[0],
        "conv_w": conv_w[0],
        "w_out_odd": w_out_odd[0].astype(BF16),
        "rel_bias_table": rel_bias_table,
        "w_mem_q": w_mem_q.astype(BF16),
        "w_mem_o": w_mem_o.astype(BF16),
        "w_ff1": w_ff1.astype(BF16),
        "w_ff2": w_ff2.astype(BF16),
        "ln_g": ln_g,
        "ln_b": ln_b,
    }
    kv = _mem_kv(mem_prompt.reshape(bp * N_MEM, D_MODEL),
                 jnp.stack([w_mem_k, w_mem_v]).astype(BF16))
    kv = kv.reshape(2, DEPTH, bp, N_MEM, D_MODEL)
    p_mem_k = kv[0].reshape(DEPTH, bp, N_MEM, MEM_HEADS, MEM_DH)
    p_mem_v = kv[1].reshape(DEPTH, bp, N_MEM, MEM_HEADS, MEM_DH)
    zeros = lambda *s: jnp.zeros(s, F32)
    (y_prompt, p_pool, p_fox_k, p_fox_v, p_fox_logf, p_diff_k, p_diff_v, p_conv) = _trunk(
        x_prompt, kv[0], kv[1],
        zeros(1, bp, POOL_HIST, POOL_WIDTH), None, None, None, None, None,
        zeros(1, bp, CONV_K - 1, CONV_CH), 0, wts)
    bs = x_sample.shape[0]
    (y_sample, s_pool, s_fox_k, s_fox_v, s_fox_logf, s_diff_k, s_diff_v, s_conv) = _trunk(
        x_sample, cache_mem_k.reshape(DEPTH, bs, N_MEM, D_MODEL),
        cache_mem_v.reshape(DEPTH, bs, N_MEM, D_MODEL),
        state_pool, cache_fox_k, cache_fox_v, cache_fox_logf,
        cache_diff_k, cache_diff_v, state_conv, cache_fox_k.shape[2], wts)
    return (y_prompt, y_sample,
            p_pool, p_fox_k, p_fox_v, p_fox_logf, p_diff_k, p_diff_v, p_conv, p_mem_k, p_mem_v,
            s_pool, s_fox_k, s_fox_v, s_fox_logf, s_diff_k, s_diff_v, s_conv)
```

```python
import functools
import math

import numpy as np
import jax
import jax.numpy as jnp
from jax import lax
from jax.experimental import pallas as pl
from jax.experimental.pallas import tpu as pltpu

F32 = jnp.float32
BF16 = jnp.bfloat16

D_MODEL = 1024
DEPTH = 2
CHUNK = 64
POOL_WIDTH = 512
POOL_GC = 128
POOL_WINDOWS = (2, 4, 8, 16)
POOL_HIST = 15
FOX_HEADS = 8
FOX_DH = 64
FOX_WIDTH = 512
DIFF_HEADS = 4
DIFF_DH = 64
DIFF_VD = 128
DIFF_QK = 512
DIFF_WIDTH = 512
CONV_CH = 512
CONV_K = 3
D_FF = 4096
N_MEM = 256
MEM_HEADS = 4
MEM_DH = 256
REL_BUCKETS = 32
REL_MAX_DIST = 128
DN_ALPHA = (2 * DEPTH) ** 0.25
LN_EPS = 1e-5
RMS_EPS = 1e-5
NEG = -1e30
LOG2E = math.log2(math.e)

LANES = 128
HIST_ROWS = 16
ATT_TILE = 512
VMEM_LIMIT = 56 * 1024 * 1024


def _cparams(*sem):
    return pltpu.CompilerParams(dimension_semantics=sem, vmem_limit_bytes=VMEM_LIMIT)


def _dot(a, b):
    return jnp.dot(a, b, preferred_element_type=F32)


def _dot_nt(a, b):
    return lax.dot_general(a, b, (((1,), (1,)), ((), ())), preferred_element_type=F32)


def _layer_norm(y, g, b):
    mu = jnp.mean(y, axis=-1, keepdims=True)
    d = y - mu
    var = jnp.mean(d * d, axis=-1, keepdims=True)
    return d * lax.rsqrt(var + LN_EPS) * g + b


def _row_tile(rows):
    return min(512, rows)


def _proj_even_kernel(x_ref, w_ref, wf_ref, bf_ref,
                      u_ref, q_ref, k_ref, kb_ref, v_ref, vb_ref, lf_ref):
    xb = x_ref[...].astype(BF16)

    def mm(c):
        return _dot(xb, w_ref[:, c * 512:(c + 1) * 512])

    u_ref[...] = mm(0)
    q_ref[...] = (mm(1) * (FOX_DH ** -0.5 * LOG2E)).astype(BF16)
    k = mm(2)
    k_ref[...] = k
    kb_ref[...] = k.astype(BF16)
    v = mm(3)
    v_ref[...] = v
    vb_ref[...] = v.astype(BF16)
    z = _dot(xb, wf_ref[...]) + bf_ref[...]
    lf_ref[...] = jnp.minimum(z, 0.0) - jnp.log1p(jnp.exp(-jnp.abs(z)))


def _proj_even(x2, w, wf, bf):
    rows = x2.shape[0]
    tm = _row_tile(rows)
    row = lambda n: pl.BlockSpec((tm, n), lambda i: (i, 0))
    full = lambda a: pl.BlockSpec(a.shape, lambda i: (0,) * a.ndim)
    f32o = jax.ShapeDtypeStruct((rows, 512), F32)
    bf16o = jax.ShapeDtypeStruct((rows, 512), BF16)
    return pl.pallas_call(
        _proj_even_kernel,
        grid=(rows // tm,),
        in_specs=[row(D_MODEL), full(w), full(wf), full(bf)],
        out_specs=[row(512)] * 6 + [row(LANES)],
        out_shape=[f32o, bf16o, f32o, bf16o, f32o, bf16o,
                   jax.ShapeDtypeStruct((rows, LANES), F32)],
        compiler_params=_cparams("parallel"),
        name="proj_even",
    )(x2, w, wf, bf)


def _proj_odd_kernel(x_ref, w_ref, q_ref, k_ref, kb_ref, v_ref, vb_ref, bg_ref, z_ref):
    xb = x_ref[...].astype(BF16)

    def mm(c):
        return _dot(xb, w_ref[:, c * 512:(c + 1) * 512])

    q_ref[...] = (mm(0) * (DIFF_DH ** -0.5 * LOG2E)).astype(BF16)
    k = mm(1)
    k_ref[...] = k
    kb_ref[...] = k.astype(BF16)
    v = mm(2)
    tm = v.shape[0]
    v_rows = v_ref.reshape(tm * DIFF_HEADS, DIFF_VD)
    for h in range(DIFF_HEADS):
        v_rows[pl.ds(h, tm, stride=DIFF_HEADS), :] = v[:, h * DIFF_VD:(h + 1) * DIFF_VD]
    vb_ref[...] = v.astype(BF16)
    bg_ref[...] = mm(3)
    z_ref[...] = mm(4) * mm(5)


def _proj_odd(x2, w):
    rows = x2.shape[0]
    tm = _row_tile(rows)
    row = lambda n: pl.BlockSpec((tm, n), lambda i: (i, 0))
    f32o = jax.ShapeDtypeStruct((rows, 512), F32)
    bf16o = jax.ShapeDtypeStruct((rows, 512), BF16)
    return pl.pallas_call(
        _proj_odd_kernel,
        grid=(rows // tm,),
        in_specs=[row(D_MODEL), pl.BlockSpec(w.shape, lambda i: (0, 0))],
        out_specs=[row(512)] * 3 + [pl.BlockSpec((tm, DIFF_HEADS, DIFF_VD), lambda i: (i, 0, 0))]
        + [row(512)] * 3,
        out_shape=[bf16o, f32o, bf16o, jax.ShapeDtypeStruct((rows, DIFF_HEADS, DIFF_VD), F32),
                   bf16o, f32o, f32o],
        compiler_params=_cparams("parallel"),
        name="proj_odd",
    )(x2, w)


def _mem_kv_kernel(x_ref, w_ref, o_ref):
    o_ref[0, 0] = _dot(x_ref[...].astype(BF16), w_ref[0, 0])


def _mem_kv(mem2, w_kv):
    rows = mem2.shape[0]
    return pl.pallas_call(
        _mem_kv_kernel,
        grid=(2, DEPTH),
        in_specs=[pl.BlockSpec((rows, D_MODEL), lambda a, l: (0, 0)),
                  pl.BlockSpec((1, 1, D_MODEL, D_MODEL), lambda a, l: (a, l, 0, 0))],
        out_specs=pl.BlockSpec((1, 1, rows, D_MODEL), lambda a, l: (a, l, 0, 0)),
        out_shape=jax.ShapeDtypeStruct((2, DEPTH, rows, D_MODEL), F32),
        compiler_params=_cparams("parallel", "parallel"),
        name="mem_kv",
    )(mem2, w_kv)


def _cumsum_kernel(x_ref, f_ref, kaug_ref, qaug_ref, carry_ref, *, rows):
    @pl.when(pl.program_id(1) == 0)
    def _():
        carry_ref[...] = jnp.zeros_like(carry_ref)

    r = lax.broadcasted_iota(jnp.int32, (LANES, LANES), 0)
    c = lax.broadcasted_iota(jnp.int32, (LANES, LANES), 1)
    tri = (r >= c).astype(F32)
    grp = c // FOX_HEADS
    carry = carry_ref[...]
    for s in range(rows // LANES):
        sl = slice(s * LANES, (s + 1) * LANES)
        cs = jnp.dot(tri, x_ref[0, sl, :], preferred_element_type=F32,
                     precision=lax.Precision.HIGHEST) + carry
        carry = cs[LANES - 1:LANES, :]
        f = cs * LOG2E
        f_ref[0, sl, :] = f
        hi = f.astype(BF16).astype(F32)
        mid = (f - hi).astype(BF16).astype(F32)
        lo = (f - hi - mid).astype(BF16).astype(F32)
        piece = jnp.where(grp % 3 == 0, hi, jnp.where(grp % 3 == 1, mid, lo))
        kaug_ref[0, sl, :] = jnp.where(grp < 3, -piece, jnp.where(grp < 6, 1.0, 0.0)).astype(BF16)
        qaug_ref[0, sl, :] = jnp.where(grp < 3, 1.0, jnp.where(grp < 6, piece, 0.0)).astype(BF16)
    carry_ref[...] = carry


def _cumsum_time(x, rows):
    b, t, _ = x.shape
    spec = pl.BlockSpec((1, rows, LANES), lambda i, j: (i, j, 0))
    return pl.pallas_call(
        functools.partial(_cumsum_kernel, rows=rows),
        grid=(b, t // rows),
        in_specs=[spec],
        out_specs=[spec] * 3,
        out_shape=[jax.ShapeDtypeStruct(x.shape, F32), jax.ShapeDtypeStruct(x.shape, BF16),
                   jax.ShapeDtypeStruct(x.shape, BF16)],
        scratch_shapes=[pltpu.VMEM((1, LANES), F32)],
        compiler_params=_cparams("parallel", "arbitrary"),
        name="cumsum_time",
    )(x)


def _online_softmax(s, m_old, shift=None):
    row_max = jnp.max(s, axis=1, keepdims=True)
    if shift is not None:
        row_max = row_max + shift
    m_new = jnp.maximum(m_old, row_max)
    m_sub = m_new if shift is None else m_new - shift
    p = jnp.exp2(s - jnp.concatenate([m_sub] * (s.shape[1] // LANES), axis=1))
    return p.astype(BF16), jnp.exp2(m_old - m_new), m_new


def _fox_kernel(q_ref, k_ref, v_ref, qaug_ref, kaug_ref, mask_ref, o_ref,
                qcat_ref, s_ref, p_ref, acc_ref, m_ref, al_ref, *, tq, tk):
    pair = pl.program_id(1)
    j_last = pl.program_id(2)
    lane = lax.broadcasted_iota(jnp.int32, (1, LANES), 1)
    upper = lane >= FOX_DH
    sels = (jnp.logical_not(upper), upper)
    q2 = q_ref[0]
    qa = qaug_ref[0]
    for hh in range(2):
        own = jnp.logical_and(lane % FOX_HEADS == 2 * pair + hh, lane < 6 * FOX_HEADS)
        qcat_ref[hh] = jnp.concatenate([jnp.where(sels[hh], q2, jnp.zeros_like(q2)),
                                        jnp.where(own, qa, jnp.zeros_like(qa))], axis=1)
    m_ref[...] = jnp.full(m_ref.shape, NEG, F32)
    acc_ref[...] = jnp.zeros(acc_ref.shape, F32)
    p_ref[1] = jnp.zeros(p_ref.shape[1:], p_ref.dtype)
    al_ref[1] = jnp.ones(al_ref.shape[1:], F32)

    def scores(hh, j):
        rows = pl.ds(pl.multiple_of(j * tk, tk), tk)
        kcat = jnp.concatenate([k_ref[0, rows, :], kaug_ref[0, rows, :]], axis=1)
        s_ref[hh] = _dot_nt(qcat_ref[hh], kcat)

    def probs(hh, masked):
        s = s_ref[hh]
        if masked:
            s = s + mask_ref[...]
        p_ref[hh], al_ref[hh], m_ref[hh] = _online_softmax(s, m_ref[hh])

    def pv(hh, j):
        vb = v_ref[0, pl.ds(pl.multiple_of(j * tk, tk), tk), :]
        va = jnp.where(sels[hh], vb, jnp.ones_like(vb))
        acc_ref[hh] = al_ref[hh] * acc_ref[hh] + _dot(p_ref[hh], va)

    scores(0, 0)

    def body(j, carry):
        scores(1, j)
        probs(0, False)
        pv(1, jnp.maximum(j - 1, 0))
        scores(0, j + 1)
        probs(1, False)
        pv(0, j)
        return carry

    lax.fori_loop(0, j_last, body, 0)
    scores(1, j_last)
    probs(0, True)
    pv(1, jnp.maximum(j_last - 1, 0))
    probs(1, True)
    pv(0, j_last)
    pv(1, j_last)

    a0 = acc_ref[0]
    a1 = acc_ref[1]
    o0 = a0 / pltpu.roll(a0, FOX_DH, 1)
    o1 = a1 / pltpu.roll(a1, FOX_DH, 1)
    o_ref[0] = jnp.where(upper, o1, o0).astype(o_ref.dtype)


def _fox_attention(qb, kb, vb, qaug, kaug, tq, tk):
    b, t, _ = qb.shape
    assert t % tq == 0 and tq == tk
    ii = np.arange(tq)[:, None]
    jj = np.arange(tk)[None, :]
    mask = jnp.asarray(np.where(jj <= ii, 0.0, NEG), F32)
    return pl.pallas_call(
        functools.partial(_fox_kernel, tq=tq, tk=tk),
        grid=(b, FOX_HEADS // 2, t // tq),
        in_specs=[
            pl.BlockSpec((1, tq, LANES), lambda bi, p, qi: (bi, qi, p)),
            pl.BlockSpec((1, t, LANES), lambda bi, p, qi: (bi, 0, p)),
            pl.BlockSpec((1, t, LANES), lambda bi, p, qi: (bi, 0, p)),
            pl.BlockSpec((1, tq, LANES), lambda bi, p, qi: (bi, qi, 0)),
            pl.BlockSpec((1, t, LANES), lambda bi, p, qi: (bi, 0, 0)),
            pl.BlockSpec((tq, tk), lambda bi, p, qi: (0, 0)),
        ],
        out_specs=pl.BlockSpec((1, tq, LANES), lambda bi, p, qi: (bi, qi, p)),
        out_shape=jax.ShapeDtypeStruct((b, t, FOX_WIDTH), BF16),
        scratch_shapes=[pltpu.VMEM((2, tq, 2 * LANES), BF16),
                        pltpu.VMEM((2, tq, tk), F32),
                        pltpu.VMEM((2, tq, tk), BF16),
                        pltpu.VMEM((2, tq, LANES), F32),
                        pltpu.VMEM((2, tq, LANES), F32),
                        pltpu.VMEM((2, tq, LANES), F32)],
        compiler_params=_cparams("parallel", "parallel", "arbitrary"),
        name="fox_attention",
    )(qb, kb, vb, qaug, kaug, mask)


def _diff_kernel(cfar_ref, q_ref, k_ref, v_ref, tiles_ref, lam_ref, g_ref, o_ref,
                 qm_ref, s_ref, p_ref, acc_ref, m_ref, al_ref, *, tq, tk, past, lam_init):
    h = pl.program_id(1)
    qi = pl.program_id(2)
    j_last = (past + qi * tq) // tk
    lane = lax.broadcasted_iota(jnp.int32, (1, LANES), 1)
    upper = lane >= DIFF_DH
    q2 = q_ref[0]
    zero = jnp.zeros_like(q2)
    qm_ref[0] = jnp.where(upper, zero, q2)
    qm_ref[1] = jnp.where(upper, q2, zero)
    cfar = cfar_ref[h]
    m_ref[...] = jnp.full(m_ref.shape, NEG, F32)
    acc_ref[...] = jnp.zeros(acc_ref.shape, F32)
    p_ref[1] = jnp.zeros(p_ref.shape[1:], p_ref.dtype)
    al_ref[1] = jnp.ones(al_ref.shape[1:], F32)

    def scores(mm, j):
        start = pl.multiple_of(j * tk, tk)
        s_ref[mm] = _dot_nt(qm_ref[mm], k_ref[0, pl.ds(start, tk), :])

    def probs(mm, tile):
        if tile is None:
            out = _online_softmax(s_ref[mm], m_ref[mm], shift=cfar)
        else:
            out = _online_softmax(s_ref[mm] + tiles_ref[0, tile], m_ref[mm])
        p_ref[mm], al_ref[mm], m_ref[mm] = out

    def pv(mm, j):
        start = pl.multiple_of(j * tk, tk)
        vb = v_ref[0, pl.ds(start, tk), :]
        va = jnp.concatenate([vb, jnp.ones_like(vb)], axis=1)
        al = al_ref[mm]
        acc_ref[mm] = jnp.concatenate([al, al], axis=1) * acc_ref[mm] + _dot(p_ref[mm], va)

    def block(j, tile):
        scores(1, j)
        probs(0, tile)
        pv(1, jnp.maximum(j - 1, 0))
        scores(0, j + 1)
        probs(1, tile)
        pv(0, j)

    scores(0, 0)

    def body(j, carry):
        block(j, None)
        return carry

    lax.fori_loop(0, jnp.maximum(j_last - 1, 0), body, 0)

    @pl.when(j_last >= 1)
    def _():
        block(j_last - 1, 1)

    scores(1, j_last)
    probs(0, 0)
    pv(1, jnp.maximum(j_last - 1, 0))
    probs(1, 0)
    pv(0, j_last)
    pv(1, j_last)

    lam = _diff_lambda(lam_ref[...], lam_init)
    a0 = acc_ref[0]
    a1 = acc_ref[1]
    o = a0[:, :LANES] / a0[:, LANES:] - lam * (a1[:, :LANES] / a1[:, LANES:])
    ms = jnp.mean(o * o, axis=1, keepdims=True)
    o = o * lax.rsqrt(ms + RMS_EPS) * g_ref[...] * (1.0 - lam_init)
    o_ref[0] = o.astype(o_ref.dtype)


def _t5_bucket(rel):
    nb = REL_BUCKETS // 2
    max_exact = nb // 2
    ret = jnp.where(rel > 0, nb, 0)
    n = jnp.abs(rel)
    nf = jnp.maximum(n, 1).astype(F32)
    large = max_exact + (jnp.log(nf / max_exact) / math.log(REL_MAX_DIST / max_exact)
                         * (nb - max_exact)).astype(jnp.int32)
    large = jnp.minimum(large, nb - 1)
    return ret + jnp.where(n < max_exact, n, large)


def _toeplitz_kernel(w_ref, mask_ref, o_ref, *, tq, width):
    n = w_ref.shape[2]
    for d in range(w_ref.shape[1]):
        w = jnp.broadcast_to(w_ref[0, d:d + 1, :], (tq, n))
        o_ref[0, d] = pltpu.roll(w, 0, 1, stride=1, stride_axis=0)[:, :width] + mask_ref[d]


def _t5_bias_tiles(rel_table, tq, width, offsets, masks):
    nt = len(offsets)
    n = -(-(tq + width) // LANES) * LANES
    m = jnp.arange(n)
    rel = jnp.where(m < width, m, m - n)
    w = jnp.stack([rel_table[_t5_bucket(rel + d)] for d in offsets])
    w = jnp.transpose(w, (2, 0, 1)).astype(F32) * LOG2E
    return pl.pallas_call(
        functools.partial(_toeplitz_kernel, tq=tq, width=width),
        grid=(DIFF_HEADS,),
        in_specs=[pl.BlockSpec((1, nt, n), lambda h: (h, 0, 0)),
                  pl.BlockSpec((nt, tq, width), lambda h: (0, 0, 0))],
        out_specs=pl.BlockSpec((1, nt, tq, width), lambda h: (h, 0, 0, 0)),
        out_shape=jax.ShapeDtypeStruct((DIFF_HEADS, nt, tq, width), F32),
        compiler_params=_cparams("parallel"),
        name="t5_bias_tiles",
    )(w, jnp.asarray(masks, F32))


def _diff_lambda(lp, lam_init):
    return (jnp.exp(jnp.sum(lp[0:1] * lp[1:2], keepdims=True))
            - jnp.exp(jnp.sum(lp[2:3] * lp[3:4], keepdims=True)) + lam_init)


def _diff_attention(qb, kb, vb, rel_table, diff_lambda, subln_g, tq, tk, lam_init):
    b, t, _ = qb.shape
    assert t % tq == 0 and tq == tk and tk >= REL_MAX_DIST and tk % CHUNK == 0
    ii = np.arange(tq)[:, None]
    jj = np.arange(tk)[None, :]
    masks = np.zeros((2, tq, tk), np.float32)
    masks[0] = np.where((jj // CHUNK) <= (ii // CHUNK), 0.0, NEG)
    tiles = _t5_bias_tiles(rel_table, tq, tk, (0, -tk), masks)
    cfar = rel_table[_t5_bucket(jnp.asarray(-2 * REL_MAX_DIST, jnp.int32))].astype(F32) * LOG2E
    return pl.pallas_call(
        functools.partial(_diff_kernel, tq=tq, tk=tk, past=0, lam_init=lam_init),
        grid=(b, DIFF_HEADS, t // tq),
        in_specs=[
            pl.BlockSpec(memory_space=pltpu.SMEM),
            pl.BlockSpec((1, tq, LANES), lambda bi, h, qi: (bi, qi, h)),
            pl.BlockSpec((1, t, LANES), lambda bi, h, qi: (bi, 0, h)),
            pl.BlockSpec((1, t, LANES), lambda bi, h, qi: (bi, 0, h)),
            pl.BlockSpec((1, 2, tq, tk), lambda bi, h, qi: (h, 0, 0, 0)),
            pl.BlockSpec((4, DIFF_DH), lambda bi, h, qi: (0, 0)),
            pl.BlockSpec((1, DIFF_VD), lambda bi, h, qi: (0, 0)),
        ],
        out_specs=pl.BlockSpec((1, tq, LANES), lambda bi, h, qi: (bi, qi, h)),
        out_shape=jax.ShapeDtypeStruct((b, t, DIFF_WIDTH), BF16),
        scratch_shapes=[pltpu.VMEM((2, tq, LANES), BF16),
                        pltpu.VMEM((2, tq, tk), F32),
                        pltpu.VMEM((2, tq, tk), BF16),
                        pltpu.VMEM((2, tq, 2 * LANES), F32),
                        pltpu.VMEM((2, tq, LANES), F32),
                        pltpu.VMEM((2, tq, LANES), F32)],
        compiler_params=_cparams("parallel", "parallel", "arbitrary"),
        name="diff_attention",
    )(cfar, qb, kb, vb, tiles, diff_lambda, subln_g.reshape(1, DIFF_VD))


def _split_halves(q2):
    upper = lax.broadcasted_iota(jnp.int32, (1, LANES), 1) >= LANES // 2
    zero = jnp.zeros_like(q2)
    return jnp.concatenate([jnp.where(upper, zero, q2), jnp.where(upper, q2, zero)], axis=0)


def _cat_keys(cache_cols, new, sl, t_kp):
    c = cache_cols.astype(BF16)
    pad = t_kp - c.shape[0] - new.shape[0]
    return jnp.concatenate([c, new[:, sl], jnp.zeros((pad, LANES), BF16)], axis=0)


def _softmax_pv(s, vcat):
    m = jnp.max(s, axis=1, keepdims=True)
    p = jnp.exp2(s - m)
    l = jnp.sum(p, axis=1, keepdims=True)
    return _dot(p.astype(BF16), vcat) / l


def _fox_decode_kernel(q_ref, kn_ref, vn_ref, kc_ref, vc_ref, fq_ref, ft_ref, mask_ref, o_ref, *, t):
    t_kp = mask_ref.shape[1]
    upper = lax.broadcasted_iota(jnp.int32, (1, LANES), 1) >= FOX_DH
    q = q_ref[0]
    kn = kn_ref[0]
    vn = vn_ref[0]
    fblk = fq_ref[0]
    outs = []
    for p in range(FOX_HEADS // 2):
        sl = slice(p * LANES, (p + 1) * LANES)
        qs = _split_halves(q[:, sl])
        kcat = _cat_keys(kc_ref[0, :, sl], kn, sl, t_kp)
        vcat = _cat_keys(vc_ref[0, :, sl], vn, sl, t_kp)
        fqs = jnp.concatenate([fblk[:, 2 * p:2 * p + 1], fblk[:, 2 * p + 1:2 * p + 2]], axis=0)
        ftp = ft_ref[0, p]
        fk = jnp.concatenate([jnp.broadcast_to(ftp[0:1], (t, t_kp)),
                              jnp.broadcast_to(ftp[1:2], (t, t_kp))], axis=0)
        s = _dot_nt(qs, kcat) + (fqs - fk) + mask_ref[...]
        o = _softmax_pv(s, vcat)
        outs.append(jnp.where(upper, o[t:], o[:t]))
    o_ref[0] = jnp.concatenate(outs, axis=1).astype(o_ref.dtype)


def _decode_specs(t, past):
    new = pl.BlockSpec((1, t, 512), lambda bi: (bi, 0, 0))
    cache = pl.BlockSpec((1, past, 512), lambda bi: (bi, 0, 0))
    return new, cache


def _fox_decode(qb, kn, vn, kc, vc, fcum, ft, past):
    b, t, _ = qb.shape
    t_kp = fcum.shape[1]
    assert past % t == 0
    rr = np.arange(2 * t)[:, None] % t
    cc = np.arange(t_kp)[None, :]
    mask = jnp.asarray(np.where(cc <= past + rr, 0.0, NEG), F32)
    new, cache = _decode_specs(t, past)
    return pl.pallas_call(
        functools.partial(_fox_decode_kernel, t=t),
        grid=(b,),
        in_specs=[new, new, new, cache, cache,
                  pl.BlockSpec((1, t, LANES), lambda bi: (bi, past // t, 0)),
                  pl.BlockSpec((1, FOX_HEADS // 2, 2, t_kp), lambda bi: (bi, 0, 0, 0)),
                  pl.BlockSpec((2 * t, t_kp), lambda bi: (0, 0))],
        out_specs=new,
        out_shape=jax.ShapeDtypeStruct((b, t, FOX_WIDTH), BF16),
        compiler_params=_cparams("parallel"),
        name="fox_decode",
    )(qb, kn, vn, kc, vc, fcum, ft, mask)


def _diff_decode_kernel(q_ref, kn_ref, vn_ref, kc_ref, vc_ref, bias_ref, lam_ref, g_ref, o_ref,
                        *, t, lam_init):
    t_kp = bias_ref.shape[3]
    q = q_ref[0]
    kn = kn_ref[0]
    vn = vn_ref[0]
    lam = _diff_lambda(lam_ref[...], lam_init)
    outs = []
    for h in range(DIFF_HEADS):
        sl = slice(h * LANES, (h + 1) * LANES)
        qs = _split_halves(q[:, sl])
        kcat = _cat_keys(kc_ref[0, :, sl], kn, sl, t_kp)
        vcat = _cat_keys(_head_rows(vc_ref, h), vn, sl, t_kp)
        bias = bias_ref[h, 0]
        s = _dot_nt(qs, kcat) + jnp.concatenate([bias, bias], axis=0)
        o = _softmax_pv(s, vcat)
        a = o[:t] - lam * o[t:]
        ms = jnp.mean(a * a, axis=1, keepdims=True)
        outs.append(a * lax.rsqrt(ms + RMS_EPS) * g_ref[...] * (1.0 - lam_init))
    o_ref[0] = jnp.concatenate(outs, axis=1).astype(o_ref.dtype)


def _diff_decode(qb, kn, vn, kc, vc, rel_table, diff_lambda, subln_g, past, lam_init):
    b, t, _ = qb.shape
    t_kp = -(-(past + t) // LANES) * LANES
    q_pos = past + np.arange(t)[:, None]
    k_pos = np.arange(t_kp)[None, :]
    visible = ((k_pos // CHUNK) <= (q_pos // CHUNK)) & (k_pos < past + t)
    tiles = _t5_bias_tiles(rel_table, t, t_kp, (-past,), np.where(visible, 0.0, NEG)[None])
    new, cache = _decode_specs(t, past)
    vcache = pl.BlockSpec((1, 1, past, DIFF_HEADS, DIFF_VD), lambda bi: (0, bi, 0, 0, 0))
    return pl.pallas_call(
        functools.partial(_diff_decode_kernel, t=t, lam_init=lam_init),
        grid=(b,),
        in_specs=[new, new, new, cache, vcache,
                  pl.BlockSpec(tiles.shape, lambda bi: (0, 0, 0, 0)),
                  pl.BlockSpec((4, DIFF_DH), lambda bi: (0, 0)),
                  pl.BlockSpec((1, DIFF_VD), lambda bi: (0, 0))],
        out_specs=new,
        out_shape=jax.ShapeDtypeStruct((b, t, DIFF_WIDTH), BF16),
        compiler_params=_cparams("parallel"),
        name="diff_decode",
    )(qb, kn, vn, kc, vc, tiles, diff_lambda, subln_g.reshape(1, DIFF_VD))


def _pool_kernel(u_ref, prev_ref, hist_ref, wmix_ref, scale_ref, o_ref, ext_ref, *, tm, past):
    i = pl.program_id(1)
    u = u_ref[0]
    ext_ref[0:HIST_ROWS, :] = jnp.where(i == 0, hist_ref[0], prev_ref[0])
    ext_ref[HIST_ROWS:HIST_ROWS + tm, :] = u
    pos = past + i * tm + lax.broadcasted_iota(jnp.int32, (tm, 1), 0)
    for g, w in enumerate(POOL_WINDOWS):
        sl = slice(g * POOL_GC, (g + 1) * POOL_GC)
        ug = u[:, sl]
        wsum = ug
        for s in range(1, w):
            wsum = wsum + ext_ref[HIST_ROWS - s:HIST_ROWS - s + tm, sl]
        cnt = jnp.minimum(w, pos + 1).astype(F32)
        d = wsum / cnt - ug
        y = _dot(d.astype(BF16), wmix_ref[g]) * scale_ref[:, sl]
        o_ref[0, :, sl] = y.astype(o_ref.dtype)


def _halo_specs(tm, width):
    cur = pl.BlockSpec((1, tm, width), lambda b, i: (b, i, 0))
    prev = pl.BlockSpec((1, HIST_ROWS, width),
                        lambda b, i: (b, jnp.maximum(i * (tm // HIST_ROWS) - 1, 0), 0))
    hist = pl.BlockSpec((1, HIST_ROWS, width), lambda b, i: (b, 0, 0))
    return cur, prev, hist


def _pool_mix(u3, hist16, wmix, scale, past):
    b, t, _ = u3.shape
    tm = min(512, t)
    cur, prev, hist = _halo_specs(tm, POOL_WIDTH)
    return pl.pallas_call(
        functools.partial(_pool_kernel, tm=tm, past=past),
        grid=(b, t // tm),
        in_specs=[cur, prev, hist,
                  pl.BlockSpec(wmix.shape, lambda bi, i: (0, 0, 0)),
                  pl.BlockSpec((1, POOL_WIDTH), lambda bi, i: (0, 0))],
        out_specs=cur,
        out_shape=jax.ShapeDtypeStruct(u3.shape, BF16),
        scratch_shapes=[pltpu.VMEM((HIST_ROWS + tm, POOL_WIDTH), F32)],
        compiler_params=_cparams("parallel", "parallel"),
        name="pool_mix",
    )(u3, u3, hist16, wmix, scale.reshape(1, POOL_WIDTH))


def _conv_kernel(z_ref, prev_ref, hist_ref, bg_ref, w_ref, o_ref, ext_ref, *, tm):
    i = pl.program_id(1)
    z = z_ref[0]
    ext_ref[0:HIST_ROWS, :] = jnp.where(i == 0, hist_ref[0], prev_ref[0])
    ext_ref[HIST_ROWS:HIST_ROWS + tm, :] = z
    y = (ext_ref[HIST_ROWS - 2:HIST_ROWS - 2 + tm, :] * w_ref[0:1, :]
         + ext_ref[HIST_ROWS - 1:HIST_ROWS - 1 + tm, :] * w_ref[1:2, :]
         + z * w_ref[2:3, :])
    o_ref[0] = (bg_ref[0] * y).astype(o_ref.dtype)


def _short_conv(z3, hist16, bg3, w):
    b, t, _ = z3.shape
    tm = min(512, t)
    cur, prev, hist = _halo_specs(tm, CONV_CH)
    return pl.pallas_call(
        functools.partial(_conv_kernel, tm=tm),
        grid=(b, t // tm),
        in_specs=[cur, prev, hist, cur, pl.BlockSpec((CONV_K, CONV_CH), lambda bi, i: (0, 0))],
        out_specs=cur,
        out_shape=jax.ShapeDtypeStruct(z3.shape, BF16),
        scratch_shapes=[pltpu.VMEM((HIST_ROWS + tm, CONV_CH), F32)],
        compiler_params=_cparams("parallel", "parallel"),
        name="short_conv",
    )(z3, z3, hist16, bg3, w)


def _outproj_kernel(a_ref, b_ref, w_ref, x_ref, g_ref, beta_ref, o_ref):
    half = w_ref.shape[0] // 2
    mix = _dot(a_ref[...], w_ref[0:half, :]) + _dot(b_ref[...], w_ref[half:, :])
    o_ref[...] = _layer_norm(DN_ALPHA * x_ref[...] + mix, g_ref[...], beta_ref[...])


def _outproj_ln(a2, b2, w, x2, g, beta):
    rows = x2.shape[0]
    tm = _row_tile(rows)
    row = lambda n: pl.BlockSpec((tm, n), lambda i: (i, 0))
    vec = pl.BlockSpec((1, D_MODEL), lambda i: (0, 0))
    return pl.pallas_call(
        _outproj_kernel,
        grid=(rows // tm,),
        in_specs=[row(512), row(512), pl.BlockSpec(w.shape, lambda i: (0, 0)),
                  row(D_MODEL), vec, vec],
        out_specs=row(D_MODEL),
        out_shape=jax.ShapeDtypeStruct((rows, D_MODEL), F32),
        compiler_params=_cparams("parallel"),
        name="outproj_ln",
    )(a2, b2, w, x2, g.reshape(1, D_MODEL), beta.reshape(1, D_MODEL))


def _head_rows(c_ref, h):
    _, _, n, heads, d = c_ref.shape
    return c_ref.reshape(n * heads, d)[pl.ds(h, n, stride=heads), :]


def _mem_kernel(x_ref, wq_ref, mk0_ref, mk1_ref, mv0_ref, mv1_ref, wo_ref, g_ref, beta_ref, o_ref,
                kb_ref, vb_ref):
    @pl.when(pl.program_id(1) == 0)
    def _():
        for h in range(MEM_HEADS):
            kb_ref[h] = jnp.concatenate([_head_rows(mk0_ref, h), _head_rows(mk1_ref, h)],
                                        axis=1).astype(BF16)
            vb_ref[h] = jnp.concatenate([_head_rows(mv0_ref, h), _head_rows(mv1_ref, h)],
                                        axis=1).astype(BF16)

    x = x_ref[0]
    q = _dot(x.astype(BF16), wq_ref[...])
    qb = (q * (MEM_DH ** -0.5)).astype(BF16)
    outs = []
    for h in range(MEM_HEADS):
        sl = slice(h * MEM_DH, (h + 1) * MEM_DH)
        s = _dot_nt(qb[:, sl], kb_ref[h])
        m = jnp.max(s, axis=1, keepdims=True)
        p = jnp.exp(s - m)
        l = jnp.sum(p, axis=1, keepdims=True)
        o = _dot(p.astype(BF16), vb_ref[h]) / l
        outs.append(o.astype(BF16))
    o_all = jnp.concatenate(outs, axis=1)
    y = DN_ALPHA * x + _dot(o_all, wo_ref[...])
    o_ref[0] = _layer_norm(y, g_ref[...], beta_ref[...])


def _mem_attend_ln(x3, wq, mk, mv, layer, wo, g, beta):
    b, t, _ = x3.shape
    tm = min(512, t)
    xs = pl.BlockSpec((1, tm, D_MODEL), lambda bi, i: (bi, i, 0))
    ws = pl.BlockSpec((D_MODEL, D_MODEL), lambda bi, i: (0, 0))
    half = lambda c: pl.BlockSpec((1, 1, N_MEM, MEM_HEADS, LANES), lambda bi, i: (layer, bi, 0, 0, c))
    vec = pl.BlockSpec((1, D_MODEL), lambda bi, i: (0, 0))
    assert MEM_DH == 2 * LANES
    return pl.pallas_call(
        _mem_kernel,
        grid=(b, t // tm),
        in_specs=[xs, ws, half(0), half(1), half(0), half(1), ws, vec, vec],
        out_specs=xs,
        out_shape=jax.ShapeDtypeStruct(x3.shape, F32),
        scratch_shapes=[pltpu.VMEM((MEM_HEADS, N_MEM, MEM_DH), BF16),
                        pltpu.VMEM((MEM_HEADS, N_MEM, MEM_DH), BF16)],
        compiler_params=_cparams("parallel", "arbitrary"),
        name="mem_attend_ln",
    )(x3, wq, mk, mk, mv, mv, wo, g.reshape(1, D_MODEL), beta.reshape(1, D_MODEL))


def _ffn_kernel(x_ref, w1_ref, w2_ref, g_ref, beta_ref, o_ref, *, chunk):
    x = x_ref[...]
    xb = x.astype(BF16)
    acc = jnp.zeros(x.shape, F32)
    for c in range(D_FF // chunk):
        h = _dot(xb, w1_ref[:, c * chunk:(c + 1) * chunk])
        h = jnp.square(jnp.maximum(h, 0.0))
        acc = acc + _dot(h.astype(BF16), w2_ref[c * chunk:(c + 1) * chunk, :])
    o_ref[...] = _layer_norm(DN_ALPHA * x + acc, g_ref[...], beta_ref[...])


def _ffn_ln(x2, w1, w2, g, beta):
    rows = x2.shape[0]
    tm = _row_tile(rows)
    row = pl.BlockSpec((tm, D_MODEL), lambda i: (i, 0))
    vec = pl.BlockSpec((1, D_MODEL), lambda i: (0, 0))
    once = pl.Buffered(1)
    return pl.pallas_call(
        functools.partial(_ffn_kernel, chunk=1024),
        grid=(rows // tm,),
        in_specs=[row,
                  pl.BlockSpec(w1.shape, lambda i: (0, 0), pipeline_mode=once),
                  pl.BlockSpec(w2.shape, lambda i: (0, 0), pipeline_mode=once),
                  vec, vec],
        out_specs=row,
        out_shape=jax.ShapeDtypeStruct((rows, D_MODEL), F32),
        compiler_params=_cparams("parallel"),
        name="ffn_ln",
    )(x2, w1, w2, g.reshape(1, D_MODEL), beta.reshape(1, D_MODEL))


def _pad_rows(a, total):
    pad = total - a.shape[1]
    if pad == 0:
        return a
    return jnp.pad(a, ((0, 0), (0, pad)) + ((0, 0),) * (a.ndim - 2))


def _hist16(h):
    return jnp.pad(h, ((0, 0), (HIST_ROWS - h.shape[1], 0), (0, 0)))


def _trunk(x, mem_k, mem_v, pool_h, fk_h, fv_h, flf_h, dk_h, dv_h, conv_h, past, wts):
    b, t, _ = x.shape
    rows = b * t
    t_k = past + t
    if past == 0:
        t_kp = t_k
        cs_rows = 1024
    else:
        t_kp = -(-t_k // LANES) * LANES
        cs_rows = t_kp
    x2 = x.reshape(rows, D_MODEL)

    u, qb, k, kb, v, vb, lf = _proj_even(x2, wts["w_in_even"], wts["w_forget"], wts["b_forget"])
    lf3 = lf.reshape(b, t, LANES)
    qb3 = qb.reshape(b, t, FOX_WIDTH)
    kb3 = kb.reshape(b, t, FOX_WIDTH)
    vb3 = vb.reshape(b, t, FOX_WIDTH)
    if past:
        lf_hist = jnp.pad(flf_h[0], ((0, 0), (0, 0), (0, LANES - FOX_HEADS)))
        lf_all = jnp.concatenate([lf_hist, lf3], axis=1)
    else:
        lf_all = lf3
    fcum, kaug, qaug = _cumsum_time(_pad_rows(lf_all, t_kp), cs_rows)
    if past:
        ft = jnp.transpose(fcum[:, :, :FOX_HEADS], (0, 2, 1)).reshape(b, FOX_HEADS // 2, 2, t_kp)
        fox_y = _fox_decode(qb3, kb3, vb3, fk_h[0].reshape(b, past, FOX_WIDTH),
                            fv_h[0].reshape(b, past, FOX_WIDTH), fcum, ft, past)
    else:
        fox_y = _fox_attention(qb3, kb3, vb3, qaug, kaug, ATT_TILE, ATT_TILE)
    u3 = u.reshape(b, t, POOL_WIDTH)
    pool_y = _pool_mix(u3, _hist16(pool_h[0]), wts["w_pool_mix"], wts["pool_scale"], past)
    x2 = _outproj_ln(pool_y.reshape(rows, POOL_WIDTH), fox_y.reshape(rows, FOX_WIDTH),
                     wts["w_out_even"], x2, wts["ln_g"][0, 0], wts["ln_b"][0, 0])
    x2 = _mem_attend_ln(x2.reshape(b, t, D_MODEL), wts["w_mem_q"][0], mem_k, mem_v, 0,
                        wts["w_mem_o"][0], wts["ln_g"][0, 1], wts["ln_b"][0, 1]).reshape(rows, D_MODEL)
    x2 = _ffn_ln(x2, wts["w_ff1"][0], wts["w_ff2"][0], wts["ln_g"][0, 2], wts["ln_b"][0, 2])
    n_pool = u3[:, t - POOL_HIST:][None]
    n_fk = k.reshape(1, b, t, FOX_HEADS, FOX_DH)
    n_fv = v.reshape(1, b, t, FOX_HEADS, FOX_DH)
    n_flf = lf3[:, :, :FOX_HEADS][None]

    lam_init = 0.8 - 0.6 * math.exp(-0.3 * 1)
    qb, k, kb, v, vb, bg, z = _proj_odd(x2, wts["w_in_odd"])
    qb3 = qb.reshape(b, t, DIFF_QK)
    kb3 = kb.reshape(b, t, DIFF_QK)
    vb3 = vb.reshape(b, t, DIFF_WIDTH)
    if past:
        att = _diff_decode(qb3, kb3, vb3, dk_h[0].reshape(b, past, DIFF_QK),
                           dv_h, wts["rel_bias_table"],
                           wts["diff_lambda"], wts["diff_subln_g"], past, lam_init)
    else:
        att = _diff_attention(qb3, kb3, vb3, wts["rel_bias_table"], wts["diff_lambda"],
                              wts["diff_subln_g"], ATT_TILE, ATT_TILE, lam_init)
    z3 = z.reshape(b, t, CONV_CH)
    conv_y = _short_conv(z3, _hist16(conv_h[0]), bg.reshape(b, t, CONV_CH), wts["conv_w"])
    x2 = _outproj_ln(att.reshape(rows, DIFF_WIDTH), conv_y.reshape(rows, CONV_CH),
                     wts["w_out_odd"], x2, wts["ln_g"][1, 0], wts["ln_b"][1, 0])
    x2 = _mem_attend_ln(x2.reshape(b, t, D_MODEL), wts["w_mem_q"][1], mem_k, mem_v, 1,
                        wts["w_mem_o"][1], wts["ln_g"][1, 1], wts["ln_b"][1, 1]).reshape(rows, D_MODEL)
    x2 = _ffn_ln(x2, wts["w_ff1"][1], wts["w_ff2"][1], wts["ln_g"][1, 2], wts["ln_b"][1, 2])
    n_dk = k.reshape(1, b, t, DIFF_HEADS, 2, DIFF_DH)
    n_dv = v.reshape(1, b, t, DIFF_HEADS, DIFF_VD)
    n_conv = z3[:, t - (CONV_K - 1):][None]
    return (x2.reshape(b, t, D_MODEL), n_pool, n_fk, n_fv, n_flf, n_dk, n_dv, n_conv)


def kernel(x_prompt, x_sample, state_pool, cache_fox_k, cache_fox_v, cache_fox_logf,
           cache_diff_k, cache_diff_v, state_conv, cache_mem_k, cache_mem_v, mem_prompt,
           w_in_even, b_forget, w_pool_mix, pool_scale, w_out_even,
           w_in_odd, diff_lambda, diff_subln_g, conv_w, w_out_odd, rel_bias_table,
           w_mem_q, w_mem_k, w_mem_v, w_mem_o, w_ff1, w_ff2, ln_g, ln_b):
    bp = x_prompt.shape[0]
    nmain = POOL_WIDTH + 3 * FOX_WIDTH
    wts = {
        "w_in_even": w_in_even[0, :, :nmain].astype(BF16),
        "w_forget": jnp.pad(jnp.tile(w_in_even[0, :, nmain:], (1, 6)),
                            ((0, 0), (0, LANES - 6 * FOX_HEADS))).astype(BF16),
        "b_forget": jnp.pad(jnp.tile(b_forget[0], 6), (0, LANES - 6 * FOX_HEADS)).reshape(1, LANES).astype(F32),
        "w_pool_mix": w_pool_mix[0].astype(BF16),
        "pool_scale": pool_scale[0],
        "w_out_even": w_out_even[0].astype(BF16),
        "w_in_odd": w_in_odd[0].astype(BF16),
        "diff_lambda": diff_lambda[0],
        "diff_subln_g": diff_subln_g[0],
        "conv_w": conv_w[0],
        "w_out_odd": w_out_odd[0].astype(BF16),
        "rel_bias_table": rel_bias_table,
        "w_mem_q": w_mem_q.astype(BF16),
        "w_mem_o": w_mem_o.astype(BF16),
        "w_ff1": w_ff1.astype(BF16),
        "w_ff2": w_ff2.astype(BF16),
        "ln_g": ln_g,
        "ln_b": ln_b,
    }
    kv = _mem_kv(mem_prompt.reshape(bp * N_MEM, D_MODEL),
                 jnp.stack([w_mem_k, w_mem_v]).astype(BF16))
    kv = kv.reshape(2, DEPTH, bp, N_MEM, D_MODEL)
    p_mem_k = kv[0].reshape(DEPTH, bp, N_MEM, MEM_HEADS, MEM_DH)
    p_mem_v = kv[1].reshape(DEPTH, bp, N_MEM, MEM_HEADS, MEM_DH)
    zeros = lambda *s: jnp.zeros(s, F32)
    (y_prompt, p_pool, p_fox_k, p_fox_v, p_fox_logf, p_diff_k, p_diff_v, p_conv) = _trunk(
        x_prompt, p_mem_k, p_mem_v,
        zeros(1, bp, POOL_HIST, POOL_WIDTH), None, None, None, None, None,
        zeros(1, bp, CONV_K - 1, CONV_CH), 0, wts)
    bs = x_sample.shape[0]
    (y_sample, s_pool, s_fox_k, s_fox_v, s_fox_logf, s_diff_k, s_diff_v, s_conv) = _trunk(
        x_sample, cache_mem_k, cache_mem_v,
        state_pool, cache_fox_k, cache_fox_v, cache_fox_logf,
        cache_diff_k, cache_diff_v, state_conv, cache_fox_k.shape[2], wts)
    return (y_prompt, y_sample,
            p_pool, p_fox_k, p_fox_v, p_fox_logf, p_diff_k, p_diff_v, p_conv, p_mem_k, p_mem_v,
            s_pool, s_fox_k, s_fox_v, s_fox_logf, s_diff_k, s_diff_v, s_conv)
```

```python
import functools
import math

import numpy as np
import jax
import jax.numpy as jnp
from jax import lax
from jax.experimental import pallas as pl
from jax.experimental.pallas import tpu as pltpu

F32 = jnp.float32
BF16 = jnp.bfloat16

D_MODEL = 1024
DEPTH = 2
CHUNK = 64
POOL_WIDTH = 512
POOL_GC = 128
POOL_WINDOWS = (2, 4, 8, 16)
POOL_HIST = 15
FOX_HEADS = 8
FOX_DH = 64
FOX_WIDTH = 512
DIFF_HEADS = 4
DIFF_DH = 64
DIFF_VD = 128
DIFF_QK = 512
DIFF_WIDTH = 512
CONV_CH = 512
CONV_K = 3
D_FF = 4096
N_MEM = 256
MEM_HEADS = 4
MEM_DH = 256
REL_BUCKETS = 32
REL_MAX_DIST = 128
DN_ALPHA = (2 * DEPTH) ** 0.25
LN_EPS = 1e-5
RMS_EPS = 1e-5
NEG = -1e30
LOG2E = math.log2(math.e)

LANES = 128
HIST_ROWS = 16
ATT_TILE = 512
VMEM_LIMIT = 56 * 1024 * 1024


def _cparams(*sem):
    return pltpu.CompilerParams(dimension_semantics=sem, vmem_limit_bytes=VMEM_LIMIT)


def _dot(a, b):
    return jnp.dot(a, b, preferred_element_type=F32)


def _dot_nt(a, b):
    return lax.dot_general(a, b, (((1,), (1,)), ((), ())), preferred_element_type=F32)


def _layer_norm(y, g, b):
    mu = jnp.mean(y, axis=-1, keepdims=True)
    d = y - mu
    var = jnp.mean(d * d, axis=-1, keepdims=True)
    return d * lax.rsqrt(var + LN_EPS) * g + b


def _row_tile(rows):
    return min(512, rows)


def _store_heads(o_ref, y):
    heads = o_ref.shape[1]
    cols = []
    for h in range(heads):
        c = y[:, (h // 2) * LANES:(h // 2 + 1) * LANES]
        cols.append(pltpu.roll(c, LANES // 2, 1) if h % 2 else c)
    t = jnp.swapaxes(jnp.stack(cols, axis=0), 0, 1)
    o_ref[...] = t[:, :, :o_ref.shape[2]]


def _proj_even_kernel(x_ref, w_ref, wf_ref, bf_ref,
                      u_ref, q_ref, k_ref, kb_ref, v_ref, vb_ref, lf_ref):
    xb = x_ref[...].astype(BF16)

    def mm(c):
        return _dot(xb, w_ref[:, c * 512:(c + 1) * 512])

    u_ref[...] = mm(0)
    q_ref[...] = (mm(1) * (FOX_DH ** -0.5 * LOG2E)).astype(BF16)
    k = mm(2)
    _store_heads(k_ref, k)
    kb_ref[...] = k.astype(BF16)
    v = mm(3)
    _store_heads(v_ref, v)
    vb_ref[...] = v.astype(BF16)
    z = _dot(xb, wf_ref[...]) + bf_ref[...]
    lf_ref[...] = jnp.minimum(z, 0.0) - jnp.log1p(jnp.exp(-jnp.abs(z)))


def _proj_even(x2, w, wf, bf):
    rows = x2.shape[0]
    tm = _row_tile(rows)
    row = lambda n: pl.BlockSpec((tm, n), lambda i: (i, 0))
    full = lambda a: pl.BlockSpec(a.shape, lambda i: (0,) * a.ndim)
    f32o = jax.ShapeDtypeStruct((rows, 512), F32)
    bf16o = jax.ShapeDtypeStruct((rows, 512), BF16)
    heads = pl.BlockSpec((tm, FOX_HEADS, FOX_DH), lambda i: (i, 0, 0))
    headso = jax.ShapeDtypeStruct((rows, FOX_HEADS, FOX_DH), F32)
    return pl.pallas_call(
        _proj_even_kernel,
        grid=(rows // tm,),
        in_specs=[row(D_MODEL), full(w), full(wf), full(bf)],
        out_specs=[row(512), row(512), heads, row(512), heads, row(512), row(LANES)],
        out_shape=[f32o, bf16o, headso, bf16o, headso, bf16o,
                   jax.ShapeDtypeStruct((rows, LANES), F32)],
        compiler_params=_cparams("parallel"),
        name="proj_even",
    )(x2, w, wf, bf)


def _proj_odd_kernel(x_ref, w_ref, q_ref, k_ref, kb_ref, v_ref, vb_ref, bg_ref, z_ref):
    xb = x_ref[...].astype(BF16)

    def mm(c):
        return _dot(xb, w_ref[:, c * 512:(c + 1) * 512])

    q_ref[...] = (mm(0) * (DIFF_DH ** -0.5 * LOG2E)).astype(BF16)
    k = mm(1)
    _store_heads(k_ref, k)
    kb_ref[...] = k.astype(BF16)
    v = mm(2)
    tm = v.shape[0]
    v_rows = v_ref.reshape(tm * DIFF_HEADS, DIFF_VD)
    for h in range(DIFF_HEADS):
        v_rows[pl.ds(h, tm, stride=DIFF_HEADS), :] = v[:, h * DIFF_VD:(h + 1) * DIFF_VD]
    vb_ref[...] = v.astype(BF16)
    bg_ref[...] = mm(3)
    z_ref[...] = mm(4) * mm(5)


def _proj_odd(x2, w):
    rows = x2.shape[0]
    tm = _row_tile(rows)
    row = lambda n: pl.BlockSpec((tm, n), lambda i: (i, 0))
    f32o = jax.ShapeDtypeStruct((rows, 512), F32)
    bf16o = jax.ShapeDtypeStruct((rows, 512), BF16)
    return pl.pallas_call(
        _proj_odd_kernel,
        grid=(rows // tm,),
        in_specs=[row(D_MODEL), pl.BlockSpec(w.shape, lambda i: (0, 0))],
        out_specs=[row(512), pl.BlockSpec((tm, 2 * DIFF_HEADS, DIFF_DH), lambda i: (i, 0, 0)), row(512),
                   pl.BlockSpec((tm, DIFF_HEADS, DIFF_VD), lambda i: (i, 0, 0))] + [row(512)] * 3,
        out_shape=[bf16o, jax.ShapeDtypeStruct((rows, 2 * DIFF_HEADS, DIFF_DH), F32), bf16o,
                   jax.ShapeDtypeStruct((rows, DIFF_HEADS, DIFF_VD), F32), bf16o, f32o, f32o],
        compiler_params=_cparams("parallel"),
        name="proj_odd",
    )(x2, w)


def _mem_kv_kernel(x_ref, w_ref, o_ref):
    o_ref[0, 0] = _dot(x_ref[...].astype(BF16), w_ref[0, 0])


def _mem_kv(mem2, w_kv):
    rows = mem2.shape[0]
    return pl.pallas_call(
        _mem_kv_kernel,
        grid=(2, DEPTH),
        in_specs=[pl.BlockSpec((rows, D_MODEL), lambda a, l: (0, 0)),
                  pl.BlockSpec((1, 1, D_MODEL, D_MODEL), lambda a, l: (a, l, 0, 0))],
        out_specs=pl.BlockSpec((1, 1, rows, D_MODEL), lambda a, l: (a, l, 0, 0)),
        out_shape=jax.ShapeDtypeStruct((2, DEPTH, rows, D_MODEL), F32),
        compiler_params=_cparams("parallel", "parallel"),
        name="mem_kv",
    )(mem2, w_kv)


def _cumsum_kernel(x_ref, f_ref, kaug_ref, qaug_ref, carry_ref, *, rows):
    @pl.when(pl.program_id(1) == 0)
    def _():
        carry_ref[...] = jnp.zeros_like(carry_ref)

    r = lax.broadcasted_iota(jnp.int32, (LANES, LANES), 0)
    c = lax.broadcasted_iota(jnp.int32, (LANES, LANES), 1)
    tri = (r >= c).astype(F32)
    grp = c // FOX_HEADS
    carry = carry_ref[...]
    for s in range(rows // LANES):
        sl = slice(s * LANES, (s + 1) * LANES)
        cs = jnp.dot(tri, x_ref[0, sl, :], preferred_element_type=F32,
                     precision=lax.Precision.HIGHEST) + carry
        carry = cs[LANES - 1:LANES, :]
        f = cs * LOG2E
        f_ref[0, sl, :] = f
        hi = f.astype(BF16).astype(F32)
        mid = (f - hi).astype(BF16).astype(F32)
        lo = (f - hi - mid).astype(BF16).astype(F32)
        piece = jnp.where(grp % 3 == 0, hi, jnp.where(grp % 3 == 1, mid, lo))
        kaug_ref[0, sl, :] = jnp.where(grp < 3, -piece, jnp.where(grp < 6, 1.0, 0.0)).astype(BF16)
        qaug_ref[0, sl, :] = jnp.where(grp < 3, 1.0, jnp.where(grp < 6, piece, 0.0)).astype(BF16)
    carry_ref[...] = carry


def _cumsum_time(x, rows):
    b, t, _ = x.shape
    spec = pl.BlockSpec((1, rows, LANES), lambda i, j: (i, j, 0))
    return pl.pallas_call(
        functools.partial(_cumsum_kernel, rows=rows),
        grid=(b, t // rows),
        in_specs=[spec],
        out_specs=[spec] * 3,
        out_shape=[jax.ShapeDtypeStruct(x.shape, F32), jax.ShapeDtypeStruct(x.shape, BF16),
                   jax.ShapeDtypeStruct(x.shape, BF16)],
        scratch_shapes=[pltpu.VMEM((1, LANES), F32)],
        compiler_params=_cparams("parallel", "arbitrary"),
        name="cumsum_time",
    )(x)


def _online_softmax(s, m_old):
    m_new = jnp.maximum(m_old, jnp.max(s, axis=1, keepdims=True))
    p = jnp.exp2(s - jnp.concatenate([m_new] * (s.shape[1] // LANES), axis=1))
    return p.astype(BF16), jnp.exp2(m_old - m_new), m_new


def _fox_kernel(q_ref, k_ref, v_ref, qaug_ref, kaug_ref, mask_ref, o_ref,
                qcat_ref, s_ref, p_ref, acc_ref, m_ref, al_ref, *, tq, tk):
    pair = pl.program_id(1)
    j_last = pl.program_id(2)
    lane = lax.broadcasted_iota(jnp.int32, (1, LANES), 1)
    upper = lane >= FOX_DH
    sels = (jnp.logical_not(upper), upper)
    q2 = q_ref[0]
    qa = qaug_ref[0]
    for hh in range(2):
        own = jnp.logical_and(lane % FOX_HEADS == 2 * pair + hh, lane < 6 * FOX_HEADS)
        qcat_ref[hh] = jnp.concatenate([jnp.where(sels[hh], q2, jnp.zeros_like(q2)),
                                        jnp.where(own, qa, jnp.zeros_like(qa))], axis=1)
    m_ref[...] = jnp.full(m_ref.shape, NEG, F32)
    acc_ref[...] = jnp.zeros(acc_ref.shape, F32)
    p_ref[1] = jnp.zeros(p_ref.shape[1:], p_ref.dtype)
    al_ref[1] = jnp.ones(al_ref.shape[1:], F32)

    def scores(hh, j):
        rows = pl.ds(pl.multiple_of(j * tk, tk), tk)
        kcat = jnp.concatenate([k_ref[0, rows, :], kaug_ref[0, rows, :]], axis=1)
        s_ref[hh] = _dot_nt(qcat_ref[hh], kcat)

    def probs(hh, masked):
        s = s_ref[hh]
        if masked:
            s = s + mask_ref[...]
        p_ref[hh], al_ref[hh], m_ref[hh] = _online_softmax(s, m_ref[hh])

    def pv(hh, j):
        vb = v_ref[0, pl.ds(pl.multiple_of(j * tk, tk), tk), :]
        va = jnp.where(sels[hh], vb, jnp.ones_like(vb))
        acc_ref[hh] = al_ref[hh] * acc_ref[hh] + _dot(p_ref[hh], va)

    scores(0, 0)

    def body(j, carry):
        scores(1, j)
        probs(0, False)
        pv(1, jnp.maximum(j - 1, 0))
        scores(0, j + 1)
        probs(1, False)
        pv(0, j)
        return carry

    lax.fori_loop(0, j_last, body, 0)
    scores(1, j_last)
    probs(0, True)
    pv(1, jnp.maximum(j_last - 1, 0))
    probs(1, True)
    pv(0, j_last)
    pv(1, j_last)

    a0 = acc_ref[0]
    a1 = acc_ref[1]
    o0 = a0 / pltpu.roll(a0, FOX_DH, 1)
    o1 = a1 / pltpu.roll(a1, FOX_DH, 1)
    o_ref[0] = jnp.where(upper, o1, o0).astype(o_ref.dtype)


def _fox_attention(qb, kb, vb, qaug, kaug, tq, tk):
    b, t, _ = qb.shape
    assert t % tq == 0 and tq == tk
    ii = np.arange(tq)[:, None]
    jj = np.arange(tk)[None, :]
    mask = jnp.asarray(np.where(jj <= ii, 0.0, NEG), F32)
    return pl.pallas_call(
        functools.partial(_fox_kernel, tq=tq, tk=tk),
        grid=(b, FOX_HEADS // 2, t // tq),
        in_specs=[
            pl.BlockSpec((1, tq, LANES), lambda bi, p, qi: (bi, qi, p)),
            pl.BlockSpec((1, t, LANES), lambda bi, p, qi: (bi, 0, p)),
            pl.BlockSpec((1, t, LANES), lambda bi, p, qi: (bi, 0, p)),
            pl.BlockSpec((1, tq, LANES), lambda bi, p, qi: (bi, qi, 0)),
            pl.BlockSpec((1, t, LANES), lambda bi, p, qi: (bi, 0, 0)),
            pl.BlockSpec((tq, tk), lambda bi, p, qi: (0, 0)),
        ],
        out_specs=pl.BlockSpec((1, tq, LANES), lambda bi, p, qi: (bi, qi, p)),
        out_shape=jax.ShapeDtypeStruct((b, t, FOX_WIDTH), BF16),
        scratch_shapes=[pltpu.VMEM((2, tq, 2 * LANES), BF16),
                        pltpu.VMEM((2, tq, tk), F32),
                        pltpu.VMEM((2, tq, tk), BF16),
                        pltpu.VMEM((2, tq, LANES), F32),
                        pltpu.VMEM((2, tq, LANES), F32),
                        pltpu.VMEM((2, tq, LANES), F32)],
        compiler_params=_cparams("parallel", "parallel", "arbitrary"),
        name="fox_attention",
    )(qb, kb, vb, qaug, kaug, mask)


def _diff_kernel(q_ref, k_ref, v_ref, tiles_ref, lam_ref, g_ref, o_ref,
                 qm_ref, s_ref, p_ref, acc_ref, m_ref, al_ref, *, tq, tk, lam_init):
    j_last = pl.program_id(2)
    lane = lax.broadcasted_iota(jnp.int32, (1, LANES), 1)
    upper = lane >= DIFF_DH
    q2 = q_ref[0]
    zero = jnp.zeros_like(q2)
    qm_ref[0] = jnp.where(upper, zero, q2)
    qm_ref[1] = jnp.where(upper, q2, zero)
    m_ref[...] = jnp.full(m_ref.shape, NEG, F32)
    acc_ref[...] = jnp.zeros(acc_ref.shape, F32)
    p_ref[1] = jnp.zeros(p_ref.shape[1:], p_ref.dtype)
    al_ref[1] = jnp.ones(al_ref.shape[1:], F32)

    def scores(mm, j):
        start = pl.multiple_of(j * tk, tk)
        s_ref[mm] = _dot_nt(qm_ref[mm], k_ref[0, pl.ds(start, tk), :])

    def probs(mm, tile):
        p_ref[mm], al_ref[mm], m_ref[mm] = _online_softmax(s_ref[mm] + tiles_ref[0, tile], m_ref[mm])

    def pv(mm, j):
        start = pl.multiple_of(j * tk, tk)
        vb = v_ref[0, pl.ds(start, tk), :]
        va = jnp.concatenate([vb, jnp.ones_like(vb)], axis=1)
        al = al_ref[mm]
        acc_ref[mm] = jnp.concatenate([al, al], axis=1) * acc_ref[mm] + _dot(p_ref[mm], va)

    scores(0, 0)

    def body(j, carry):
        tile = jnp.minimum(j_last - j, 2)
        scores(1, j)
        probs(0, tile)
        pv(1, jnp.maximum(j - 1, 0))
        scores(0, jnp.minimum(j + 1, j_last))
        probs(1, tile)
        pv(0, j)
        return carry

    lax.fori_loop(0, j_last + 1, body, 0)
    pv(1, j_last)

    lam = _diff_lambda(lam_ref[...], lam_init)
    a0 = acc_ref[0]
    a1 = acc_ref[1]
    o = a0[:, :LANES] / a0[:, LANES:] - lam * (a1[:, :LANES] / a1[:, LANES:])
    ms = jnp.mean(o * o, axis=1, keepdims=True)
    o = o * lax.rsqrt(ms + RMS_EPS) * g_ref[...] * (1.0 - lam_init)
    o_ref[0] = o.astype(o_ref.dtype)


def _t5_bucket(rel):
    nb = REL_BUCKETS // 2
    max_exact = nb // 2
    ret = jnp.where(rel > 0, nb, 0)
    n = jnp.abs(rel)
    nf = jnp.maximum(n, 1).astype(F32)
    large = max_exact + (jnp.log(nf / max_exact) / math.log(REL_MAX_DIST / max_exact)
                         * (nb - max_exact)).astype(jnp.int32)
    large = jnp.minimum(large, nb - 1)
    return ret + jnp.where(n < max_exact, n, large)


def _toeplitz_kernel(w_ref, mask_ref, o_ref, *, tq, width):
    n = w_ref.shape[2]
    for d in range(w_ref.shape[1]):
        w = jnp.broadcast_to(w_ref[0, d:d + 1, :], (tq, n))
        o_ref[0, d] = pltpu.roll(w, 0, 1, stride=1, stride_axis=0)[:, :width] + mask_ref[d]


def _t5_bias_tiles(rel_table, tq, width, offsets, masks):
    nt = len(offsets)
    n = -(-(tq + width) // LANES) * LANES
    m = jnp.arange(n)
    rel = jnp.where(m < width, m, m - n)
    w = jnp.stack([rel_table[_t5_bucket(rel + d)] for d in offsets])
    w = jnp.transpose(w, (2, 0, 1)).astype(F32) * LOG2E
    return pl.pallas_call(
        functools.partial(_toeplitz_kernel, tq=tq, width=width),
        grid=(DIFF_HEADS,),
        in_specs=[pl.BlockSpec((1, nt, n), lambda h: (h, 0, 0)),
                  pl.BlockSpec((nt, tq, width), lambda h: (0, 0, 0))],
        out_specs=pl.BlockSpec((1, nt, tq, width), lambda h: (h, 0, 0, 0)),
        out_shape=jax.ShapeDtypeStruct((DIFF_HEADS, nt, tq, width), F32),
        compiler_params=_cparams("parallel"),
        name="t5_bias_tiles",
    )(w, jnp.asarray(masks, F32))


def _diff_lambda(lp, lam_init):
    return (jnp.exp(jnp.sum(lp[0:1] * lp[1:2], keepdims=True))
            - jnp.exp(jnp.sum(lp[2:3] * lp[3:4], keepdims=True)) + lam_init)


def _diff_attention(qb, kb, vb, rel_table, diff_lambda, subln_g, tq, tk, lam_init):
    b, t, _ = qb.shape
    assert t % tq == 0 and tq == tk and tk >= REL_MAX_DIST and tk % CHUNK == 0
    ii = np.arange(tq)[:, None]
    jj = np.arange(tk)[None, :]
    masks = np.zeros((3, tq, tk), np.float32)
    masks[0] = np.where((jj // CHUNK) <= (ii // CHUNK), 0.0, NEG)
    tiles = _t5_bias_tiles(rel_table, tq, tk, (0, -tk, -2 * tk), masks)
    return pl.pallas_call(
        functools.partial(_diff_kernel, tq=tq, tk=tk, lam_init=lam_init),
        grid=(b, DIFF_HEADS, t // tq),
        in_specs=[
            pl.BlockSpec((1, tq, LANES), lambda bi, h, qi: (bi, qi, h)),
            pl.BlockSpec((1, t, LANES), lambda bi, h, qi: (bi, 0, h)),
            pl.BlockSpec((1, t, LANES), lambda bi, h, qi: (bi, 0, h)),
            pl.BlockSpec((1, 3, tq, tk), lambda bi, h, qi: (h, 0, 0, 0)),
            pl.BlockSpec((4, DIFF_DH), lambda bi, h, qi: (0, 0)),
            pl.BlockSpec((1, DIFF_VD), lambda bi, h, qi: (0, 0)),
        ],
        out_specs=pl.BlockSpec((1, tq, LANES), lambda bi, h, qi: (bi, qi, h)),
        out_shape=jax.ShapeDtypeStruct((b, t, DIFF_WIDTH), BF16),
        scratch_shapes=[pltpu.VMEM((2, tq, LANES), BF16),
                        pltpu.VMEM((2, tq, tk), F32),
                        pltpu.VMEM((2, tq, tk), BF16),
                        pltpu.VMEM((2, tq, 2 * LANES), F32),
                        pltpu.VMEM((2, tq, LANES), F32),
                        pltpu.VMEM((2, tq, LANES), F32)],
        compiler_params=_cparams("parallel", "parallel", "arbitrary"),
        name="diff_attention",
    )(qb, kb, vb, tiles, diff_lambda, subln_g.reshape(1, DIFF_VD))


def _heads_major(c_ref):
    return jnp.swapaxes(c_ref[0], 0, 1)


def _two_part_attend(q_h, k_c, k_n, v_c, v_n, bias_c, bias_n):
    s_c = _dot_nt(q_h, k_c.astype(BF16)) + bias_c
    s_n = _dot_nt(q_h, k_n) + bias_n
    m = jnp.maximum(jnp.max(s_c, axis=1, keepdims=True), jnp.max(s_n, axis=1, keepdims=True))
    p_c = jnp.exp2(s_c - m)
    p_n = jnp.exp2(s_n - m)
    l = jnp.sum(p_c, axis=1, keepdims=True) + jnp.sum(p_n, axis=1, keepdims=True)
    return (_dot(p_c.astype(BF16), v_c.astype(BF16)) + _dot(p_n.astype(BF16), v_n)) / l


def _fox_decode_kernel(q_ref, kn_ref, vn_ref, kc_ref, vc_ref, fq_ref, ft_ref, o_ref, *, t, past):
    q = q_ref[0]
    kn = kn_ref[0]
    vn = vn_ref[0]
    kc = _heads_major(kc_ref)
    vc = _heads_major(vc_ref)
    fblk = fq_ref[0]
    r = lax.broadcasted_iota(jnp.int32, (t, t), 0)
    c = lax.broadcasted_iota(jnp.int32, (t, t), 1)
    causal = jnp.where(c <= r, 0.0, NEG)
    outs = []
    for h in range(FOX_HEADS):
        sl = slice(h * FOX_DH, (h + 1) * FOX_DH)
        fq = fblk[:, h:h + 1]
        fk = ft_ref[0, h:h + 1, :]
        outs.append(_two_part_attend(q[:, sl], kc[h], kn[:, sl], vc[h], vn[:, sl],
                                     fq - fk[:, :past], fq - fk[:, past:past + t] + causal))
    o_ref[0] = jnp.concatenate(outs, axis=1).astype(o_ref.dtype)


def _decode_specs(t, past):
    new = pl.BlockSpec((1, t, 512), lambda bi: (bi, 0, 0))
    cache = pl.BlockSpec((1, past, 8, 64), lambda bi: (bi, 0, 0, 0))
    return new, cache


def _fox_decode(qb, kn, vn, kc, vc, fcum, ft, past):
    b, t, _ = qb.shape
    t_kp = fcum.shape[1]
    assert past % t == 0
    new, cache = _decode_specs(t, past)
    return pl.pallas_call(
        functools.partial(_fox_decode_kernel, t=t, past=past),
        grid=(b,),
        in_specs=[new, new, new, cache, cache,
                  pl.BlockSpec((1, t, LANES), lambda bi: (bi, past // t, 0)),
                  pl.BlockSpec((1, FOX_HEADS, t_kp), lambda bi: (bi, 0, 0))],
        out_specs=new,
        out_shape=jax.ShapeDtypeStruct((b, t, FOX_WIDTH), BF16),
        compiler_params=_cparams("parallel"),
        name="fox_decode",
    )(qb, kn, vn, kc, vc, fcum, ft)


def _diff_decode_kernel(q_ref, kn_ref, vn_ref, kc_ref, vc_ref, bias_ref, lam_ref, g_ref, o_ref,
                        *, t, past, lam_init):
    q = q_ref[0]
    kn = kn_ref[0]
    vn = vn_ref[0]
    kc = _heads_major(kc_ref)
    lam = _diff_lambda(lam_ref[...], lam_init)
    outs = []
    for h in range(DIFF_HEADS):
        v_c = _head_rows(vc_ref, h)
        v_n = vn[:, h * DIFF_VD:(h + 1) * DIFF_VD]
        bias = bias_ref[h, 0]
        maps = []
        for mm in range(2):
            j = 2 * h + mm
            sl = slice(j * DIFF_DH, (j + 1) * DIFF_DH)
            maps.append(_two_part_attend(q[:, sl], kc[j], kn[:, sl], v_c, v_n,
                                         bias[:, :past], bias[:, past:past + t]))
        a = maps[0] - lam * maps[1]
        ms = jnp.mean(a * a, axis=1, keepdims=True)
        outs.append(a * lax.rsqrt(ms + RMS_EPS) * g_ref[...] * (1.0 - lam_init))
    o_ref[0] = jnp.concatenate(outs, axis=1).astype(o_ref.dtype)


def _diff_decode(qb, kn, vn, kc, vc, rel_table, diff_lambda, subln_g, past, lam_init):
    b, t, _ = qb.shape
    t_kp = -(-(past + t) // LANES) * LANES
    q_pos = past + np.arange(t)[:, None]
    k_pos = np.arange(t_kp)[None, :]
    visible = ((k_pos // CHUNK) <= (q_pos // CHUNK)) & (k_pos < past + t)
    tiles = _t5_bias_tiles(rel_table, t, t_kp, (-past,), np.where(visible, 0.0, NEG)[None])
    new, cache = _decode_specs(t, past)
    vcache = pl.BlockSpec((1, 1, past, DIFF_HEADS, DIFF_VD), lambda bi: (0, bi, 0, 0, 0))
    return pl.pallas_call(
        functools.partial(_diff_decode_kernel, t=t, past=past, lam_init=lam_init),
        grid=(b,),
        in_specs=[new, new, new, cache, vcache,
                  pl.BlockSpec(tiles.shape, lambda bi: (0, 0, 0, 0)),
                  pl.BlockSpec((4, DIFF_DH), lambda bi: (0, 0)),
                  pl.BlockSpec((1, DIFF_VD), lambda bi: (0, 0))],
        out_specs=new,
        out_shape=jax.ShapeDtypeStruct((b, t, DIFF_WIDTH), BF16),
        compiler_params=_cparams("parallel"),
        name="diff_decode",
    )(qb, kn, vn, kc, vc, tiles, diff_lambda, subln_g.reshape(1, DIFF_VD))


def _pool_kernel(u_ref, prev_ref, hist_ref, wmix_ref, scale_ref, o_ref, ext_ref, *, tm, past):
    i = pl.program_id(1)
    u = u_ref[0]
    ext_ref[0:HIST_ROWS, :] = jnp.where(i == 0, hist_ref[0], prev_ref[0])
    ext_ref[HIST_ROWS:HIST_ROWS + tm, :] = u
    pos = past + i * tm + lax.broadcasted_iota(jnp.int32, (tm, 1), 0)
    for g, w in enumerate(POOL_WINDOWS):
        sl = slice(g * POOL_GC, (g + 1) * POOL_GC)
        ug = u[:, sl]
        wsum = ug
        for s in range(1, w):
            wsum = wsum + ext_ref[HIST_ROWS - s:HIST_ROWS - s + tm, sl]
        cnt = jnp.minimum(w, pos + 1).astype(F32)
        d = wsum / cnt - ug
        y = _dot(d.astype(BF16), wmix_ref[g]) * scale_ref[:, sl]
        o_ref[0, :, sl] = y.astype(o_ref.dtype)


def _halo_specs(tm, width):
    cur = pl.BlockSpec((1, tm, width), lambda b, i: (b, i, 0))
    prev = pl.BlockSpec((1, HIST_ROWS, width),
                        lambda b, i: (b, jnp.maximum(i * (tm // HIST_ROWS) - 1, 0), 0))
    hist = pl.BlockSpec((1, HIST_ROWS, width), lambda b, i: (b, 0, 0))
    return cur, prev, hist


def _pool_mix(u3, hist16, wmix, scale, past):
    b, t, _ = u3.shape
    tm = min(512, t)
    cur, prev, hist = _halo_specs(tm, POOL_WIDTH)
    return pl.pallas_call(
        functools.partial(_pool_kernel, tm=tm, past=past),
        grid=(b, t // tm),
        in_specs=[cur, prev, hist,
                  pl.BlockSpec(wmix.shape, lambda bi, i: (0, 0, 0)),
                  pl.BlockSpec((1, POOL_WIDTH), lambda bi, i: (0, 0))],
        out_specs=cur,
        out_shape=jax.ShapeDtypeStruct(u3.shape, BF16),
        scratch_shapes=[pltpu.VMEM((HIST_ROWS + tm, POOL_WIDTH), F32)],
        compiler_params=_cparams("parallel", "parallel"),
        name="pool_mix",
    )(u3, u3, hist16, wmix, scale.reshape(1, POOL_WIDTH))


def _conv_kernel(z_ref, prev_ref, hist_ref, bg_ref, w_ref, o_ref, ext_ref, *, tm):
    i = pl.program_id(1)
    z = z_ref[0]
    ext_ref[0:HIST_ROWS, :] = jnp.where(i == 0, hist_ref[0], prev_ref[0])
    ext_ref[HIST_ROWS:HIST_ROWS + tm, :] = z
    y = (ext_ref[HIST_ROWS - 2:HIST_ROWS - 2 + tm, :] * w_ref[0:1, :]
         + ext_ref[HIST_ROWS - 1:HIST_ROWS - 1 + tm, :] * w_ref[1:2, :]
         + z * w_ref[2:3, :])
    o_ref[0] = (bg_ref[0] * y).astype(o_ref.dtype)


def _short_conv(z3, hist16, bg3, w):
    b, t, _ = z3.shape
    tm = min(512, t)
    cur, prev, hist = _halo_specs(tm, CONV_CH)
    return pl.pallas_call(
        functools.partial(_conv_kernel, tm=tm),
        grid=(b, t // tm),
        in_specs=[cur, prev, hist, cur, pl.BlockSpec((CONV_K, CONV_CH), lambda bi, i: (0, 0))],
        out_specs=cur,
        out_shape=jax.ShapeDtypeStruct(z3.shape, BF16),
        scratch_shapes=[pltpu.VMEM((HIST_ROWS + tm, CONV_CH), F32)],
        compiler_params=_cparams("parallel", "parallel"),
        name="short_conv",
    )(z3, z3, hist16, bg3, w)


def _outproj_kernel(a_ref, b_ref, w_ref, x_ref, g_ref, beta_ref, o_ref):
    half = w_ref.shape[0] // 2
    mix = _dot(a_ref[...], w_ref[0:half, :]) + _dot(b_ref[...], w_ref[half:, :])
    o_ref[...] = _layer_norm(DN_ALPHA * x_ref[...] + mix, g_ref[...], beta_ref[...])


def _outproj_ln(a2, b2, w, x2, g, beta):
    rows = x2.shape[0]
    tm = _row_tile(rows)
    row = lambda n: pl.BlockSpec((tm, n), lambda i: (i, 0))
    vec = pl.BlockSpec((1, D_MODEL), lambda i: (0, 0))
    return pl.pallas_call(
        _outproj_kernel,
        grid=(rows // tm,),
        in_specs=[row(512), row(512), pl.BlockSpec(w.shape, lambda i: (0, 0)),
                  row(D_MODEL), vec, vec],
        out_specs=row(D_MODEL),
        out_shape=jax.ShapeDtypeStruct((rows, D_MODEL), F32),
        compiler_params=_cparams("parallel"),
        name="outproj_ln",
    )(a2, b2, w, x2, g.reshape(1, D_MODEL), beta.reshape(1, D_MODEL))


def _head_rows(c_ref, h):
    _, _, n, heads, d = c_ref.shape
    return c_ref.reshape(n * heads, d)[pl.ds(h, n, stride=heads), :]


def _mem_kernel(x_ref, wq_ref, mk0_ref, mk1_ref, mv0_ref, mv1_ref, wo_ref, g_ref, beta_ref, o_ref,
                kb_ref, vb_ref):
    @pl.when(pl.program_id(1) == 0)
    def _():
        for h in range(MEM_HEADS):
            kb_ref[h] = jnp.concatenate([_head_rows(mk0_ref, h), _head_rows(mk1_ref, h)],
                                        axis=1).astype(BF16)
            vb_ref[h] = jnp.concatenate([_head_rows(mv0_ref, h), _head_rows(mv1_ref, h)],
                                        axis=1).astype(BF16)

    x = x_ref[0]
    q = _dot(x.astype(BF16), wq_ref[...])
    qb = (q * (MEM_DH ** -0.5)).astype(BF16)
    outs = []
    for h in range(MEM_HEADS):
        sl = slice(h * MEM_DH, (h + 1) * MEM_DH)
        s = _dot_nt(qb[:, sl], kb_ref[h])
        m = jnp.max(s, axis=1, keepdims=True)
        p = jnp.exp(s - m)
        l = jnp.sum(p, axis=1, keepdims=True)
        o = _dot(p.astype(BF16), vb_ref[h]) / l
        outs.append(o.astype(BF16))
    o_all = jnp.concatenate(outs, axis=1)
    y = DN_ALPHA * x + _dot(o_all, wo_ref[...])
    o_ref[0] = _layer_norm(y, g_ref[...], beta_ref[...])


def _mem_attend_ln(x3, wq, mk, mv, layer, wo, g, beta):
    b, t, _ = x3.shape
    tm = min(512, t)
    xs = pl.BlockSpec((1, tm, D_MODEL), lambda bi, i: (bi, i, 0))
    ws = pl.BlockSpec((D_MODEL, D_MODEL), lambda bi, i: (0, 0))
    half = lambda c: pl.BlockSpec((1, 1, N_MEM, MEM_HEADS, LANES), lambda bi, i: (layer, bi, 0, 0, c))
    vec = pl.BlockSpec((1, D_MODEL), lambda bi, i: (0, 0))
    assert MEM_DH == 2 * LANES
    return pl.pallas_call(
        _mem_kernel,
        grid=(b, t // tm),
        in_specs=[xs, ws, half(0), half(1), half(0), half(1), ws, vec, vec],
        out_specs=xs,
        out_shape=jax.ShapeDtypeStruct(x3.shape, F32),
        scratch_shapes=[pltpu.VMEM((MEM_HEADS, N_MEM, MEM_DH), BF16),
                        pltpu.VMEM((MEM_HEADS, N_MEM, MEM_DH), BF16)],
        compiler_params=_cparams("parallel", "arbitrary"),
        name="mem_attend_ln",
    )(x3, wq, mk, mk, mv, mv, wo, g.reshape(1, D_MODEL), beta.reshape(1, D_MODEL))


def _ffn_kernel(x_ref, w1_ref, w2_ref, g_ref, beta_ref, o_ref, *, chunk):
    x = x_ref[...]
    xb = x.astype(BF16)
    acc = jnp.zeros(x.shape, F32)
    for c in range(D_FF // chunk):
        h = _dot(xb, w1_ref[:, c * chunk:(c + 1) * chunk])
        h = jnp.square(jnp.maximum(h, 0.0))
        acc = acc + _dot(h.astype(BF16), w2_ref[c * chunk:(c + 1) * chunk, :])
    o_ref[...] = _layer_norm(DN_ALPHA * x + acc, g_ref[...], beta_ref[...])


def _ffn_ln(x2, w1, w2, g, beta):
    rows = x2.shape[0]
    tm = _row_tile(rows)
    row = pl.BlockSpec((tm, D_MODEL), lambda i: (i, 0))
    vec = pl.BlockSpec((1, D_MODEL), lambda i: (0, 0))
    once = pl.Buffered(1)
    return pl.pallas_call(
        functools.partial(_ffn_kernel, chunk=1024),
        grid=(rows // tm,),
        in_specs=[row,
                  pl.BlockSpec(w1.shape, lambda i: (0, 0), pipeline_mode=once),
                  pl.BlockSpec(w2.shape, lambda i: (0, 0), pipeline_mode=once),
                  vec, vec],
        out_specs=row,
        out_shape=jax.ShapeDtypeStruct((rows, D_MODEL), F32),
        compiler_params=_cparams("parallel"),
        name="ffn_ln",
    )(x2, w1, w2, g.reshape(1, D_MODEL), beta.reshape(1, D_MODEL))


def _pad_rows(a, total):
    pad = total - a.shape[1]
    if pad == 0:
        return a
    return jnp.pad(a, ((0, 0), (0, pad)) + ((0, 0),) * (a.ndim - 2))


def _hist16(h):
    return jnp.pad(h, ((0, 0), (HIST_ROWS - h.shape[1], 0), (0, 0)))


def _trunk(x, mem_k, mem_v, pool_h, fk_h, fv_h, flf_h, dk_h, dv_h, conv_h, past, wts):
    b, t, _ = x.shape
    rows = b * t
    t_k = past + t
    if past == 0:
        t_kp = t_k
        cs_rows = 1024
    else:
        t_kp = -(-t_k // LANES) * LANES
        cs_rows = t_kp
    x2 = x.reshape(rows, D_MODEL)

    u, qb, k, kb, v, vb, lf = _proj_even(x2, wts["w_in_even"], wts["w_forget"], wts["b_forget"])
    lf3 = lf.reshape(b, t, LANES)
    qb3 = qb.reshape(b, t, FOX_WIDTH)
    kb3 = kb.reshape(b, t, FOX_WIDTH)
    vb3 = vb.reshape(b, t, FOX_WIDTH)
    if past:
        lf_hist = jnp.pad(flf_h[0], ((0, 0), (0, 0), (0, LANES - FOX_HEADS)))
        lf_all = jnp.concatenate([lf_hist, lf3], axis=1)
    else:
        lf_all = lf3
    fcum, kaug, qaug = _cumsum_time(_pad_rows(lf_all, t_kp), cs_rows)
    if past:
        ft = jnp.transpose(fcum[:, :, :FOX_HEADS], (0, 2, 1))
        fox_y = _fox_decode(qb3, kb3, vb3, fk_h[0], fv_h[0], fcum, ft, past)
    else:
        fox_y = _fox_attention(qb3, kb3, vb3, qaug, kaug, ATT_TILE, ATT_TILE)
    u3 = u.reshape(b, t, POOL_WIDTH)
    pool_y = _pool_mix(u3, _hist16(pool_h[0]), wts["w_pool_mix"], wts["pool_scale"], past)
    x2 = _outproj_ln(pool_y.reshape(rows, POOL_WIDTH), fox_y.reshape(rows, FOX_WIDTH),
                     wts["w_out_even"], x2, wts["ln_g"][0, 0], wts["ln_b"][0, 0])
    x2 = _mem_attend_ln(x2.reshape(b, t, D_MODEL), wts["w_mem_q"][0], mem_k, mem_v, 0,
                        wts["w_mem_o"][0], wts["ln_g"][0, 1], wts["ln_b"][0, 1]).reshape(rows, D_MODEL)
    x2 = _ffn_ln(x2, wts["w_ff1"][0], wts["w_ff2"][0], wts["ln_g"][0, 2], wts["ln_b"][0, 2])
    n_pool = u3[:, t - POOL_HIST:][None]
    n_fk = k.reshape(1, b, t, FOX_HEADS, FOX_DH)
    n_fv = v.reshape(1, b, t, FOX_HEADS, FOX_DH)
    n_flf = lf3[:, :, :FOX_HEADS][None]

    lam_init = 0.8 - 0.6 * math.exp(-0.3 * 1)
    qb, k, kb, v, vb, bg, z = _proj_odd(x2, wts["w_in_odd"])
    qb3 = qb.reshape(b, t, DIFF_QK)
    kb3 = kb.reshape(b, t, DIFF_QK)
    vb3 = vb.reshape(b, t, DIFF_WIDTH)
    if past:
        att = _diff_decode(qb3, kb3, vb3, dk_h[0].reshape(b, past, 2 * DIFF_HEADS, DIFF_DH),
                           dv_h, wts["rel_bias_table"],
                           wts["diff_lambda"], wts["diff_subln_g"], past, lam_init)
    else:
        att = _diff_attention(qb3, kb3, vb3, wts["rel_bias_table"], wts["diff_lambda"],
                              wts["diff_subln_g"], ATT_TILE, ATT_TILE, lam_init)
    z3 = z.reshape(b, t, CONV_CH)
    conv_y = _short_conv(z3, _hist16(conv_h[0]), bg.reshape(b, t, CONV_CH), wts["conv_w"])
    x2 = _outproj_ln(att.reshape(rows, DIFF_WIDTH), conv_y.reshape(rows, CONV_CH),
                     wts["w_out_odd"], x2, wts["ln_g"][1, 0], wts["ln_b"][1, 0])
    x2 = _mem_attend_ln(x2.reshape(b, t, D_MODEL), wts["w_mem_q"][1], mem_k, mem_v, 1,
                        wts["w_mem_o"][1], wts["ln_g"][1, 1], wts["ln_b"][1, 1]).reshape(rows, D_MODEL)
    x2 = _ffn_ln(x2, wts["w_ff1"][1], wts["w_ff2"][1], wts["ln_g"][1, 2], wts["ln_b"][1, 2])
    n_dk = k.reshape(1, b, t, DIFF_HEADS, 2, DIFF_DH)
    n_dv = v.reshape(1, b, t, DIFF_HEADS, DIFF_VD)
    n_conv = z3[:, t - (CONV_K - 1):][None]
    return (x2.reshape(b, t, D_MODEL), n_pool, n_fk, n_fv, n_flf, n_dk, n_dv, n_conv)


def kernel(x_prompt, x_sample, state_pool, cache_fox_k, cache_fox_v, cache_fox_logf,
           cache_diff_k, cache_diff_v, state_conv, cache_mem_k, cache_mem_v, mem_prompt,
           w_in_even, b_forget, w_pool_mix, pool_scale, w_out_even,
           w_in_odd, diff_lambda, diff_subln_g, conv_w, w_out_odd, rel_bias_table,
           w_mem_q, w_mem_k, w_mem_v, w_mem_o, w_ff1, w_ff2, ln_g, ln_b):
    bp = x_prompt.shape[0]
    nmain = POOL_WIDTH + 3 * FOX_WIDTH
    wts = {
        "w_in_even": w_in_even[0, :, :nmain].astype(BF16),
        "w_forget": jnp.pad(jnp.tile(w_in_even[0, :, nmain:], (1, 6)),
                            ((0, 0), (0, LANES - 6 * FOX_HEADS))).astype(BF16),
        "b_forget": jnp.pad(jnp.tile(b_forget[0], 6), (0, LANES - 6 * FOX_HEADS)).reshape(1, LANES).astype(F32),
        "w_pool_mix": w_pool_mix[0].astype(BF16),
        "pool_scale": pool_scale[0],
        "w_out_even": w_out_even[0].astype(BF16),
        "w_in_odd": w_in_odd[0].astype(BF16),
        "diff_lambda": diff_lambda[0],
        "diff_subln_g": diff_subln_g[0],
        "conv_w": conv_w[0],
        "w_out_odd": w_out_odd[0].astype(BF16),
        "rel_bias_table": rel_bias_table,
        "w_mem_q": w_mem_q.astype(BF16),
        "w_mem_o": w_mem_o.astype(BF16),
        "w_ff1": w_ff1.astype(BF16),
        "w_ff2": w_ff2.astype(BF16),
        "ln_g": ln_g,
        "ln_b": ln_b,
    }
    kv = _mem_kv(mem_prompt.reshape(bp * N_MEM, D_MODEL),
                 jnp.stack([w_mem_k, w_mem_v]).astype(BF16))
    kv = kv.reshape(2, DEPTH, bp, N_MEM, D_MODEL)
    p_mem_k = kv[0].reshape(DEPTH, bp, N_MEM, MEM_HEADS, MEM_DH)
    p_mem_v = kv[1].reshape(DEPTH, bp, N_MEM, MEM_HEADS, MEM_DH)
    zeros = lambda *s: jnp.zeros(s, F32)
    (y_prompt, p_pool, p_fox_k, p_fox_v, p_fox_logf, p_diff_k, p_diff_v, p_conv) = _trunk(
        x_prompt, p_mem_k, p_mem_v,
        zeros(1, bp, POOL_HIST, POOL_WIDTH), None, None, None, None, None,
        zeros(1, bp, CONV_K - 1, CONV_CH), 0, wts)
    bs = x_sample.shape[0]
    (y_sample, s_pool, s_fox_k, s_fox_v, s_fox_logf, s_diff_k, s_diff_v, s_conv) = _trunk(
        x_sample, cache_mem_k, cache_mem_v,
        state_pool, cache_fox_k, cache_fox_v, cache_fox_logf,
        cache_diff_k, cache_diff_v, state_conv, cache_fox_k.shape[2], wts)
    return (y_prompt, y_sample,
            p_pool, p_fox_k, p_fox_v, p_fox_logf, p_diff_k, p_diff_v, p_conv, p_mem_k, p_mem_v,
            s_pool, s_fox_k, s_fox_v, s_fox_logf, s_diff_k, s_diff_v, s_conv)
```

```python
import functools
import math

import numpy as np
import jax
import jax.numpy as jnp
from jax import lax
from jax.experimental import pallas as pl
from jax.experimental.pallas import tpu as pltpu

F32 = jnp.float32
BF16 = jnp.bfloat16

D_MODEL = 1024
DEPTH = 2
CHUNK = 64
POOL_WIDTH = 512
POOL_GC = 128
POOL_WINDOWS = (2, 4, 8, 16)
POOL_HIST = 15
FOX_HEADS = 8
FOX_DH = 64
FOX_WIDTH = 512
DIFF_HEADS = 4
DIFF_DH = 64
DIFF_VD = 128
DIFF_QK = 512
DIFF_WIDTH = 512
CONV_CH = 512
CONV_K = 3
D_FF = 4096
N_MEM = 256
MEM_HEADS = 4
MEM_DH = 256
REL_BUCKETS = 32
REL_MAX_DIST = 128
DN_ALPHA = (2 * DEPTH) ** 0.25
LN_EPS = 1e-5
RMS_EPS = 1e-5
NEG = -1e30
LOG2E = math.log2(math.e)

LANES = 128
HIST_ROWS = 16
ATT_TILE = 512
VMEM_LIMIT = 56 * 1024 * 1024


def _cparams(*sem):
    return pltpu.CompilerParams(dimension_semantics=sem, vmem_limit_bytes=VMEM_LIMIT)


def _dot(a, b):
    return jnp.dot(a, b, preferred_element_type=F32)


def _dot_nt(a, b):
    return lax.dot_general(a, b, (((1,), (1,)), ((), ())), preferred_element_type=F32)


def _layer_norm(y, g, b):
    mu = jnp.mean(y, axis=-1, keepdims=True)
    d = y - mu
    var = jnp.mean(d * d, axis=-1, keepdims=True)
    return d * lax.rsqrt(var + LN_EPS) * g + b


def _row_tile(rows):
    return min(512, rows)


def _store_cache(o_ref, y):
    if o_ref.shape[0] == 1:
        o_ref[0] = y.T
        return
    cols = []
    for h in range(o_ref.shape[1]):
        c = y[:, (h // 2) * LANES:(h // 2 + 1) * LANES]
        cols.append(pltpu.roll(c, LANES // 2, 1) if h % 2 else c)
    t = jnp.swapaxes(jnp.stack(cols, axis=0), 0, 1)
    o_ref[...] = t[:, :, :o_ref.shape[2]]


def _cache_out(rows, tm, seq):
    if seq % tm == 0 and tm % LANES == 0:
        nb = seq // tm
        return (pl.BlockSpec((1, 512, tm), lambda i: (i // nb, 0, i % nb)),
                jax.ShapeDtypeStruct((rows // seq, 512, seq), F32))
    return (pl.BlockSpec((tm, 8, 64), lambda i: (i, 0, 0)),
            jax.ShapeDtypeStruct((rows, 8, 64), F32))


def _cache_logical(c, b, t, tail):
    if c.shape[0] == b and c.shape[1] == 512:
        c = jnp.transpose(c.reshape((b,) + tail + (t,)), (0, len(tail) + 1) + tuple(range(1, len(tail) + 1)))
        return c[None]
    return c.reshape((1, b, t) + tail)


def _proj_even_kernel(x_ref, w_ref, wf_ref, bf_ref,
                      u_ref, q_ref, k_ref, kb_ref, v_ref, vb_ref, lf_ref):
    xb = x_ref[...].astype(BF16)

    def mm(c):
        return _dot(xb, w_ref[:, c * 512:(c + 1) * 512])

    u_ref[...] = mm(0)
    q_ref[...] = (mm(1) * (FOX_DH ** -0.5 * LOG2E)).astype(BF16)
    k = mm(2)
    _store_cache(k_ref, k)
    kb_ref[...] = k.astype(BF16)
    v = mm(3)
    _store_cache(v_ref, v)
    vb_ref[...] = v.astype(BF16)
    z = _dot(xb, wf_ref[...]) + bf_ref[...]
    lf_ref[...] = jnp.minimum(z, 0.0) - jnp.log1p(jnp.exp(-jnp.abs(z)))


def _proj_even(x2, w, wf, bf, seq):
    rows = x2.shape[0]
    tm = _row_tile(rows)
    row = lambda n: pl.BlockSpec((tm, n), lambda i: (i, 0))
    full = lambda a: pl.BlockSpec(a.shape, lambda i: (0,) * a.ndim)
    f32o = jax.ShapeDtypeStruct((rows, 512), F32)
    bf16o = jax.ShapeDtypeStruct((rows, 512), BF16)
    heads, headso = _cache_out(rows, tm, seq)
    return pl.pallas_call(
        _proj_even_kernel,
        grid=(rows // tm,),
        in_specs=[row(D_MODEL), full(w), full(wf), full(bf)],
        out_specs=[row(512), row(512), heads, row(512), heads, row(512), row(LANES)],
        out_shape=[f32o, bf16o, headso, bf16o, headso, bf16o,
                   jax.ShapeDtypeStruct((rows, LANES), F32)],
        compiler_params=_cparams("parallel"),
        name="proj_even",
    )(x2, w, wf, bf)


def _proj_odd_kernel(x_ref, w_ref, q_ref, k_ref, kb_ref, v_ref, vb_ref, bg_ref, z_ref):
    xb = x_ref[...].astype(BF16)

    def mm(c):
        return _dot(xb, w_ref[:, c * 512:(c + 1) * 512])

    q_ref[...] = (mm(0) * (DIFF_DH ** -0.5 * LOG2E)).astype(BF16)
    k = mm(1)
    _store_cache(k_ref, k)
    kb_ref[...] = k.astype(BF16)
    v = mm(2)
    tm = v.shape[0]
    v_rows = v_ref.reshape(tm * DIFF_HEADS, DIFF_VD)
    for h in range(DIFF_HEADS):
        v_rows[pl.ds(h, tm, stride=DIFF_HEADS), :] = v[:, h * DIFF_VD:(h + 1) * DIFF_VD]
    vb_ref[...] = v.astype(BF16)
    bg_ref[...] = mm(3)
    z_ref[...] = mm(4) * mm(5)


def _proj_odd(x2, w, seq):
    rows = x2.shape[0]
    tm = _row_tile(rows)
    kspec, kshape = _cache_out(rows, tm, seq)
    row = lambda n: pl.BlockSpec((tm, n), lambda i: (i, 0))
    f32o = jax.ShapeDtypeStruct((rows, 512), F32)
    bf16o = jax.ShapeDtypeStruct((rows, 512), BF16)
    return pl.pallas_call(
        _proj_odd_kernel,
        grid=(rows // tm,),
        in_specs=[row(D_MODEL), pl.BlockSpec(w.shape, lambda i: (0, 0))],
        out_specs=[row(512), kspec, row(512),
                   pl.BlockSpec((tm, DIFF_HEADS, DIFF_VD), lambda i: (i, 0, 0))] + [row(512)] * 3,
        out_shape=[bf16o, kshape, bf16o,
                   jax.ShapeDtypeStruct((rows, DIFF_HEADS, DIFF_VD), F32), bf16o, f32o, f32o],
        compiler_params=_cparams("parallel"),
        name="proj_odd",
    )(x2, w)


def _mem_kv_kernel(x_ref, w_ref, o_ref):
    o_ref[0, 0] = _dot(x_ref[...].astype(BF16), w_ref[0, 0])


def _mem_kv(mem2, w_kv):
    rows = mem2.shape[0]
    return pl.pallas_call(
        _mem_kv_kernel,
        grid=(2, DEPTH),
        in_specs=[pl.BlockSpec((rows, D_MODEL), lambda a, l: (0, 0)),
                  pl.BlockSpec((1, 1, D_MODEL, D_MODEL), lambda a, l: (a, l, 0, 0))],
        out_specs=pl.BlockSpec((1, 1, rows, D_MODEL), lambda a, l: (a, l, 0, 0)),
        out_shape=jax.ShapeDtypeStruct((2, DEPTH, rows, D_MODEL), F32),
        compiler_params=_cparams("parallel", "parallel"),
        name="mem_kv",
    )(mem2, w_kv)


def _cumsum_kernel(x_ref, f_ref, kaug_ref, qaug_ref, carry_ref, *, rows):
    @pl.when(pl.program_id(1) == 0)
    def _():
        carry_ref[...] = jnp.zeros_like(carry_ref)

    r = lax.broadcasted_iota(jnp.int32, (LANES, LANES), 0)
    c = lax.broadcasted_iota(jnp.int32, (LANES, LANES), 1)
    tri = (r >= c).astype(F32)
    grp = c // FOX_HEADS
    carry = carry_ref[...]
    for s in range(rows // LANES):
        sl = slice(s * LANES, (s + 1) * LANES)
        cs = jnp.dot(tri, x_ref[0, sl, :], preferred_element_type=F32,
                     precision=lax.Precision.HIGHEST) + carry
        carry = cs[LANES - 1:LANES, :]
        f = cs * LOG2E
        f_ref[0, sl, :] = f
        hi = f.astype(BF16).astype(F32)
        mid = (f - hi).astype(BF16).astype(F32)
        lo = (f - hi - mid).astype(BF16).astype(F32)
        piece = jnp.where(grp % 3 == 0, hi, jnp.where(grp % 3 == 1, mid, lo))
        kaug_ref[0, sl, :] = jnp.where(grp < 3, -piece, jnp.where(grp < 6, 1.0, 0.0)).astype(BF16)
        qaug_ref[0, sl, :] = jnp.where(grp < 3, 1.0, jnp.where(grp < 6, piece, 0.0)).astype(BF16)
    carry_ref[...] = carry


def _cumsum_time(x, rows):
    b, t, _ = x.shape
    spec = pl.BlockSpec((1, rows, LANES), lambda i, j: (i, j, 0))
    return pl.pallas_call(
        functools.partial(_cumsum_kernel, rows=rows),
        grid=(b, t // rows),
        in_specs=[spec],
        out_specs=[spec] * 3,
        out_shape=[jax.ShapeDtypeStruct(x.shape, F32), jax.ShapeDtypeStruct(x.shape, BF16),
                   jax.ShapeDtypeStruct(x.shape, BF16)],
        scratch_shapes=[pltpu.VMEM((1, LANES), F32)],
        compiler_params=_cparams("parallel", "arbitrary"),
        name="cumsum_time",
    )(x)


def _online_softmax(s, m_old):
    m_new = jnp.maximum(m_old, jnp.max(s, axis=1, keepdims=True))
    p = jnp.exp2(s - jnp.concatenate([m_new] * (s.shape[1] // LANES), axis=1))
    return p.astype(BF16), jnp.exp2(m_old - m_new), m_new


def _fox_kernel(q_ref, k_ref, v_ref, qaug_ref, kaug_ref, mask_ref, o_ref,
                qcat_ref, s_ref, p_ref, acc_ref, m_ref, al_ref, *, tq, tk):
    pair = pl.program_id(1)
    j_last = pl.program_id(2)
    lane = lax.broadcasted_iota(jnp.int32, (1, LANES), 1)
    upper = lane >= FOX_DH
    sels = (jnp.logical_not(upper), upper)
    q2 = q_ref[0]
    qa = qaug_ref[0]
    for hh in range(2):
        own = jnp.logical_and(lane % FOX_HEADS == 2 * pair + hh, lane < 6 * FOX_HEADS)
        qcat_ref[hh] = jnp.concatenate([jnp.where(sels[hh], q2, jnp.zeros_like(q2)),
                                        jnp.where(own, qa, jnp.zeros_like(qa))], axis=1)
    m_ref[...] = jnp.full(m_ref.shape, NEG, F32)
    acc_ref[...] = jnp.zeros(acc_ref.shape, F32)
    p_ref[1] = jnp.zeros(p_ref.shape[1:], p_ref.dtype)
    al_ref[1] = jnp.ones(al_ref.shape[1:], F32)

    def scores(hh, j):
        rows = pl.ds(pl.multiple_of(j * tk, tk), tk)
        kcat = jnp.concatenate([k_ref[0, rows, :], kaug_ref[0, rows, :]], axis=1)
        s_ref[hh] = _dot_nt(qcat_ref[hh], kcat)

    def probs(hh, masked):
        s = s_ref[hh]
        if masked:
            s = s + mask_ref[...]
        p_ref[hh], al_ref[hh], m_ref[hh] = _online_softmax(s, m_ref[hh])

    def pv(hh, j):
        vb = v_ref[0, pl.ds(pl.multiple_of(j * tk, tk), tk), :]
        va = jnp.where(sels[hh], vb, jnp.ones_like(vb))
        acc_ref[hh] = al_ref[hh] * acc_ref[hh] + _dot(p_ref[hh], va)

    scores(0, 0)

    def body(j, carry):
        scores(1, j)
        probs(0, False)
        pv(1, jnp.maximum(j - 1, 0))
        scores(0, j + 1)
        probs(1, False)
        pv(0, j)
        return carry

    lax.fori_loop(0, j_last, body, 0)
    scores(1, j_last)
    probs(0, True)
    pv(1, jnp.maximum(j_last - 1, 0))
    probs(1, True)
    pv(0, j_last)
    pv(1, j_last)

    a0 = acc_ref[0]
    a1 = acc_ref[1]
    o0 = a0 / pltpu.roll(a0, FOX_DH, 1)
    o1 = a1 / pltpu.roll(a1, FOX_DH, 1)
    o_ref[0] = jnp.where(upper, o1, o0).astype(o_ref.dtype)


def _fox_attention(qb, kb, vb, qaug, kaug, tq, tk):
    b, t, _ = qb.shape
    assert t % tq == 0 and tq == tk
    ii = np.arange(tq)[:, None]
    jj = np.arange(tk)[None, :]
    mask = jnp.asarray(np.where(jj <= ii, 0.0, NEG), F32)
    return pl.pallas_call(
        functools.partial(_fox_kernel, tq=tq, tk=tk),
        grid=(b, FOX_HEADS // 2, t // tq),
        in_specs=[
            pl.BlockSpec((1, tq, LANES), lambda bi, p, qi: (bi, qi, p)),
            pl.BlockSpec((1, t, LANES), lambda bi, p, qi: (bi, 0, p)),
            pl.BlockSpec((1, t, LANES), lambda bi, p, qi: (bi, 0, p)),
            pl.BlockSpec((1, tq, LANES), lambda bi, p, qi: (bi, qi, 0)),
            pl.BlockSpec((1, t, LANES), lambda bi, p, qi: (bi, 0, 0)),
            pl.BlockSpec((tq, tk), lambda bi, p, qi: (0, 0)),
        ],
        out_specs=pl.BlockSpec((1, tq, LANES), lambda bi, p, qi: (bi, qi, p)),
        out_shape=jax.ShapeDtypeStruct((b, t, FOX_WIDTH), BF16),
        scratch_shapes=[pltpu.VMEM((2, tq, 2 * LANES), BF16),
                        pltpu.VMEM((2, tq, tk), F32),
                        pltpu.VMEM((2, tq, tk), BF16),
                        pltpu.VMEM((2, tq, LANES), F32),
                        pltpu.VMEM((2, tq, LANES), F32),
                        pltpu.VMEM((2, tq, LANES), F32)],
        compiler_params=_cparams("parallel", "parallel", "arbitrary"),
        name="fox_attention",
    )(qb, kb, vb, qaug, kaug, mask)


def _diff_kernel(q_ref, k_ref, v_ref, tiles_ref, lam_ref, g_ref, o_ref,
                 qm_ref, s_ref, p_ref, acc_ref, m_ref, al_ref, *, tq, tk, lam_init):
    j_last = pl.program_id(2)
    lane = lax.broadcasted_iota(jnp.int32, (1, LANES), 1)
    upper = lane >= DIFF_DH
    q2 = q_ref[0]
    zero = jnp.zeros_like(q2)
    qm_ref[0] = jnp.where(upper, zero, q2)
    qm_ref[1] = jnp.where(upper, q2, zero)
    m_ref[...] = jnp.full(m_ref.shape, NEG, F32)
    acc_ref[...] = jnp.zeros(acc_ref.shape, F32)
    p_ref[1] = jnp.zeros(p_ref.shape[1:], p_ref.dtype)
    al_ref[1] = jnp.ones(al_ref.shape[1:], F32)

    def scores(mm, j):
        start = pl.multiple_of(j * tk, tk)
        s_ref[mm] = _dot_nt(qm_ref[mm], k_ref[0, pl.ds(start, tk), :])

    def probs(mm, tile):
        p_ref[mm], al_ref[mm], m_ref[mm] = _online_softmax(s_ref[mm] + tiles_ref[0, tile], m_ref[mm])

    def pv(mm, j):
        start = pl.multiple_of(j * tk, tk)
        vb = v_ref[0, pl.ds(start, tk), :]
        va = jnp.concatenate([vb, jnp.ones_like(vb)], axis=1)
        al = al_ref[mm]
        acc_ref[mm] = jnp.concatenate([al, al], axis=1) * acc_ref[mm] + _dot(p_ref[mm], va)

    scores(0, 0)

    def body(j, carry):
        tile = jnp.minimum(j_last - j, 2)
        scores(1, j)
        probs(0, tile)
        pv(1, jnp.maximum(j - 1, 0))
        scores(0, jnp.minimum(j + 1, j_last))
        probs(1, tile)
        pv(0, j)
        return carry

    lax.fori_loop(0, j_last + 1, body, 0)
    pv(1, j_last)

    lam = _diff_lambda(lam_ref[...], lam_init)
    a0 = acc_ref[0]
    a1 = acc_ref[1]
    o = a0[:, :LANES] / a0[:, LANES:] - lam * (a1[:, :LANES] / a1[:, LANES:])
    ms = jnp.mean(o * o, axis=1, keepdims=True)
    o = o * lax.rsqrt(ms + RMS_EPS) * g_ref[...] * (1.0 - lam_init)
    o_ref[0] = o.astype(o_ref.dtype)


def _t5_bucket(rel):
    nb = REL_BUCKETS // 2
    max_exact = nb // 2
    ret = jnp.where(rel > 0, nb, 0)
    n = jnp.abs(rel)
    nf = jnp.maximum(n, 1).astype(F32)
    large = max_exact + (jnp.log(nf / max_exact) / math.log(REL_MAX_DIST / max_exact)
                         * (nb - max_exact)).astype(jnp.int32)
    large = jnp.minimum(large, nb - 1)
    return ret + jnp.where(n < max_exact, n, large)


def _toeplitz_kernel(w_ref, mask_ref, o_ref, *, tq, width):
    n = w_ref.shape[2]
    for d in range(w_ref.shape[1]):
        w = jnp.broadcast_to(w_ref[0, d:d + 1, :], (tq, n))
        o_ref[0, d] = pltpu.roll(w, 0, 1, stride=1, stride_axis=0)[:, :width] + mask_ref[d]


def _t5_bias_tiles(rel_table, tq, width, offsets, masks):
    nt = len(offsets)
    n = -(-(tq + width) // LANES) * LANES
    m = jnp.arange(n)
    rel = jnp.where(m < width, m, m - n)
    w = jnp.stack([rel_table[_t5_bucket(rel + d)] for d in offsets])
    w = jnp.transpose(w, (2, 0, 1)).astype(F32) * LOG2E
    return pl.pallas_call(
        functools.partial(_toeplitz_kernel, tq=tq, width=width),
        grid=(DIFF_HEADS,),
        in_specs=[pl.BlockSpec((1, nt, n), lambda h: (h, 0, 0)),
                  pl.BlockSpec((nt, tq, width), lambda h: (0, 0, 0))],
        out_specs=pl.BlockSpec((1, nt, tq, width), lambda h: (h, 0, 0, 0)),
        out_shape=jax.ShapeDtypeStruct((DIFF_HEADS, nt, tq, width), F32),
        compiler_params=_cparams("parallel"),
        name="t5_bias_tiles",
    )(w, jnp.asarray(masks, F32))


def _diff_lambda(lp, lam_init):
    return (jnp.exp(jnp.sum(lp[0:1] * lp[1:2], keepdims=True))
            - jnp.exp(jnp.sum(lp[2:3] * lp[3:4], keepdims=True)) + lam_init)


def _diff_attention(qb, kb, vb, rel_table, diff_lambda, subln_g, tq, tk, lam_init):
    b, t, _ = qb.shape
    assert t % tq == 0 and tq == tk and tk >= REL_MAX_DIST and tk % CHUNK == 0
    ii = np.arange(tq)[:, None]
    jj = np.arange(tk)[None, :]
    masks = np.zeros((3, tq, tk), np.float32)
    masks[0] = np.where((jj // CHUNK) <= (ii // CHUNK), 0.0, NEG)
    tiles = _t5_bias_tiles(rel_table, tq, tk, (0, -tk, -2 * tk), masks)
    return pl.pallas_call(
        functools.partial(_diff_kernel, tq=tq, tk=tk, lam_init=lam_init),
        grid=(b, DIFF_HEADS, t // tq),
        in_specs=[
            pl.BlockSpec((1, tq, LANES), lambda bi, h, qi: (bi, qi, h)),
            pl.BlockSpec((1, t, LANES), lambda bi, h, qi: (bi, 0, h)),
            pl.BlockSpec((1, t, LANES), lambda bi, h, qi: (bi, 0, h)),
            pl.BlockSpec((1, 3, tq, tk), lambda bi, h, qi: (h, 0, 0, 0)),
            pl.BlockSpec((4, DIFF_DH), lambda bi, h, qi: (0, 0)),
            pl.BlockSpec((1, DIFF_VD), lambda bi, h, qi: (0, 0)),
        ],
        out_specs=pl.BlockSpec((1, tq, LANES), lambda bi, h, qi: (bi, qi, h)),
        out_shape=jax.ShapeDtypeStruct((b, t, DIFF_WIDTH), BF16),
        scratch_shapes=[pltpu.VMEM((2, tq, LANES), BF16),
                        pltpu.VMEM((2, tq, tk), F32),
                        pltpu.VMEM((2, tq, tk), BF16),
                        pltpu.VMEM((2, tq, 2 * LANES), F32),
                        pltpu.VMEM((2, tq, LANES), F32),
                        pltpu.VMEM((2, tq, LANES), F32)],
        compiler_params=_cparams("parallel", "parallel", "arbitrary"),
        name="diff_attention",
    )(qb, kb, vb, tiles, diff_lambda, subln_g.reshape(1, DIFF_VD))


def _two_part_attend(q_h, kt_c, k_n, v_c, v_n, bias_c, bias_n, v_time_minor):
    s_c = _dot(q_h, kt_c.astype(BF16)) + bias_c
    s_n = _dot_nt(q_h, k_n) + bias_n
    m = jnp.maximum(jnp.max(s_c, axis=1, keepdims=True), jnp.max(s_n, axis=1, keepdims=True))
    p_c = jnp.exp2(s_c - m).astype(BF16)
    p_n = jnp.exp2(s_n - m)
    l = jnp.sum(p_c.astype(F32), axis=1, keepdims=True) + jnp.sum(p_n, axis=1, keepdims=True)
    pv_c = _dot_nt(p_c, v_c.astype(BF16)) if v_time_minor else _dot(p_c, v_c.astype(BF16))
    return (pv_c + _dot(p_n.astype(BF16), v_n)) / l


def _fox_decode_kernel(q_ref, kn_ref, vn_ref, kc_ref, vc_ref, fq_ref, ft_ref, o_ref, *, t, past):
    q = q_ref[0]
    kn = kn_ref[0]
    vn = vn_ref[0]
    fblk = fq_ref[0]
    r = lax.broadcasted_iota(jnp.int32, (t, t), 0)
    c = lax.broadcasted_iota(jnp.int32, (t, t), 1)
    causal = jnp.where(c <= r, 0.0, NEG)
    outs = []
    for h in range(FOX_HEADS):
        sl = slice(h * FOX_DH, (h + 1) * FOX_DH)
        fq = fblk[:, h:h + 1]
        fk = ft_ref[0, h:h + 1, :]
        outs.append(_two_part_attend(q[:, sl], kc_ref[0, h], kn[:, sl], vc_ref[0, h], vn[:, sl],
                                     fq - fk[:, :past], fq - fk[:, past:past + t] + causal, True))
    o_ref[0] = jnp.concatenate(outs, axis=1).astype(o_ref.dtype)


def _decode_specs(t, past):
    new = pl.BlockSpec((1, t, 512), lambda bi: (bi, 0, 0))
    cache = pl.BlockSpec((1, 8, 64, past), lambda bi: (bi, 0, 0, 0))
    return new, cache


def _fox_decode(qb, kn, vn, kc, vc, fcum, ft, past):
    b, t, _ = qb.shape
    t_kp = fcum.shape[1]
    assert past % t == 0
    new, cache = _decode_specs(t, past)
    return pl.pallas_call(
        functools.partial(_fox_decode_kernel, t=t, past=past),
        grid=(b,),
        in_specs=[new, new, new, cache, cache,
                  pl.BlockSpec((1, t, LANES), lambda bi: (bi, past // t, 0)),
                  pl.BlockSpec((1, FOX_HEADS, t_kp), lambda bi: (bi, 0, 0))],
        out_specs=new,
        out_shape=jax.ShapeDtypeStruct((b, t, FOX_WIDTH), BF16),
        compiler_params=_cparams("parallel"),
        name="fox_decode",
    )(qb, kn, vn, kc, vc, fcum, ft)


def _diff_decode_kernel(q_ref, kn_ref, vn_ref, kc_ref, vc_ref, bias_ref, lam_ref, g_ref, o_ref,
                        *, t, past, lam_init):
    q = q_ref[0]
    kn = kn_ref[0]
    vn = vn_ref[0]
    lam = _diff_lambda(lam_ref[...], lam_init)
    outs = []
    for h in range(DIFF_HEADS):
        v_c = _head_rows(vc_ref, h)
        v_n = vn[:, h * DIFF_VD:(h + 1) * DIFF_VD]
        bias = bias_ref[h, 0]
        maps = []
        for mm in range(2):
            j = 2 * h + mm
            sl = slice(j * DIFF_DH, (j + 1) * DIFF_DH)
            maps.append(_two_part_attend(q[:, sl], kc_ref[0, j], kn[:, sl], v_c, v_n,
                                         bias[:, :past], bias[:, past:past + t], False))
        a = maps[0] - lam * maps[1]
        ms = jnp.mean(a * a, axis=1, keepdims=True)
        outs.append(a * lax.rsqrt(ms + RMS_EPS) * g_ref[...] * (1.0 - lam_init))
    o_ref[0] = jnp.concatenate(outs, axis=1).astype(o_ref.dtype)


def _diff_decode(qb, kn, vn, kc, vc, rel_table, diff_lambda, subln_g, past, lam_init):
    b, t, _ = qb.shape
    t_kp = -(-(past + t) // LANES) * LANES
    q_pos = past + np.arange(t)[:, None]
    k_pos = np.arange(t_kp)[None, :]
    visible = ((k_pos // CHUNK) <= (q_pos // CHUNK)) & (k_pos < past + t)
    tiles = _t5_bias_tiles(rel_table, t, t_kp, (-past,), np.where(visible, 0.0, NEG)[None])
    new, cache = _decode_specs(t, past)
    vcache = pl.BlockSpec((1, 1, past, DIFF_HEADS, DIFF_VD), lambda bi: (0, bi, 0, 0, 0))
    return pl.pallas_call(
        functools.partial(_diff_decode_kernel, t=t, past=past, lam_init=lam_init),
        grid=(b,),
        in_specs=[new, new, new, cache, vcache,
                  pl.BlockSpec(tiles.shape, lambda bi: (0, 0, 0, 0)),
                  pl.BlockSpec((4, DIFF_DH), lambda bi: (0, 0)),
                  pl.BlockSpec((1, DIFF_VD), lambda bi: (0, 0))],
        out_specs=new,
        out_shape=jax.ShapeDtypeStruct((b, t, DIFF_WIDTH), BF16),
        compiler_params=_cparams("parallel"),
        name="diff_decode",
    )(qb, kn, vn, kc, vc, tiles, diff_lambda, subln_g.reshape(1, DIFF_VD))


def _pool_kernel(u_ref, prev_ref, hist_ref, wmix_ref, scale_ref, o_ref, ext_ref, *, tm, past):
    i = pl.program_id(1)
    u = u_ref[0]
    ext_ref[0:HIST_ROWS, :] = jnp.where(i == 0, hist_ref[0], prev_ref[0])
    ext_ref[HIST_ROWS:HIST_ROWS + tm, :] = u
    pos = past + i * tm + lax.broadcasted_iota(jnp.int32, (tm, 1), 0)
    for g, w in enumerate(POOL_WINDOWS):
        sl = slice(g * POOL_GC, (g + 1) * POOL_GC)
        ug = u[:, sl]
        wsum = ug
        for s in range(1, w):
            wsum = wsum + ext_ref[HIST_ROWS - s:HIST_ROWS - s + tm, sl]
        cnt = jnp.minimum(w, pos + 1).astype(F32)
        d = wsum / cnt - ug
        y = _dot(d.astype(BF16), wmix_ref[g]) * scale_ref[:, sl]
        o_ref[0, :, sl] = y.astype(o_ref.dtype)


def _halo_specs(tm, width):
    cur = pl.BlockSpec((1, tm, width), lambda b, i: (b, i, 0))
    prev = pl.BlockSpec((1, HIST_ROWS, width),
                        lambda b, i: (b, jnp.maximum(i * (tm // HIST_ROWS) - 1, 0), 0))
    hist = pl.BlockSpec((1, HIST_ROWS, width), lambda b, i: (b, 0, 0))
    return cur, prev, hist


def _pool_mix(u3, hist16, wmix, scale, past):
    b, t, _ = u3.shape
    tm = min(512, t)
    cur, prev, hist = _halo_specs(tm, POOL_WIDTH)
    return pl.pallas_call(
        functools.partial(_pool_kernel, tm=tm, past=past),
        grid=(b, t // tm),
        in_specs=[cur, prev, hist,
                  pl.BlockSpec(wmix.shape, lambda bi, i: (0, 0, 0)),
                  pl.BlockSpec((1, POOL_WIDTH), lambda bi, i: (0, 0))],
        out_specs=cur,
        out_shape=jax.ShapeDtypeStruct(u3.shape, BF16),
        scratch_shapes=[pltpu.VMEM((HIST_ROWS + tm, POOL_WIDTH), F32)],
        compiler_params=_cparams("parallel", "parallel"),
        name="pool_mix",
    )(u3, u3, hist16, wmix, scale.reshape(1, POOL_WIDTH))


def _conv_kernel(z_ref, prev_ref, hist_ref, bg_ref, w_ref, o_ref, ext_ref, *, tm):
    i = pl.program_id(1)
    z = z_ref[0]
    ext_ref[0:HIST_ROWS, :] = jnp.where(i == 0, hist_ref[0], prev_ref[0])
    ext_ref[HIST_ROWS:HIST_ROWS + tm, :] = z
    y = (ext_ref[HIST_ROWS - 2:HIST_ROWS - 2 + tm, :] * w_ref[0:1, :]
         + ext_ref[HIST_ROWS - 1:HIST_ROWS - 1 + tm, :] * w_ref[1:2, :]
         + z * w_ref[2:3, :])
    o_ref[0] = (bg_ref[0] * y).astype(o_ref.dtype)


def _short_conv(z3, hist16, bg3, w):
    b, t, _ = z3.shape
    tm = min(512, t)
    cur, prev, hist = _halo_specs(tm, CONV_CH)
    return pl.pallas_call(
        functools.partial(_conv_kernel, tm=tm),
        grid=(b, t // tm),
        in_specs=[cur, prev, hist, cur, pl.BlockSpec((CONV_K, CONV_CH), lambda bi, i: (0, 0))],
        out_specs=cur,
        out_shape=jax.ShapeDtypeStruct(z3.shape, BF16),
        scratch_shapes=[pltpu.VMEM((HIST_ROWS + tm, CONV_CH), F32)],
        compiler_params=_cparams("parallel", "parallel"),
        name="short_conv",
    )(z3, z3, hist16, bg3, w)


def _outproj_kernel(a_ref, b_ref, w_ref, x_ref, g_ref, beta_ref, o_ref):
    half = w_ref.shape[0] // 2
    mix = _dot(a_ref[...], w_ref[0:half, :]) + _dot(b_ref[...], w_ref[half:, :])
    o_ref[...] = _layer_norm(DN_ALPHA * x_ref[...] + mix, g_ref[...], beta_ref[...])


def _outproj_ln(a2, b2, w, x2, g, beta):
    rows = x2.shape[0]
    tm = _row_tile(rows)
    row = lambda n: pl.BlockSpec((tm, n), lambda i: (i, 0))
    vec = pl.BlockSpec((1, D_MODEL), lambda i: (0, 0))
    return pl.pallas_call(
        _outproj_kernel,
        grid=(rows // tm,),
        in_specs=[row(512), row(512), pl.BlockSpec(w.shape, lambda i: (0, 0)),
                  row(D_MODEL), vec, vec],
        out_specs=row(D_MODEL),
        out_shape=jax.ShapeDtypeStruct((rows, D_MODEL), F32),
        compiler_params=_cparams("parallel"),
        name="outproj_ln",
    )(a2, b2, w, x2, g.reshape(1, D_MODEL), beta.reshape(1, D_MODEL))


def _head_rows(c_ref, h):
    _, _, n, heads, d = c_ref.shape
    return c_ref.reshape(n * heads, d)[pl.ds(h, n, stride=heads), :]


def _mem_kernel(x_ref, wq_ref, mk0_ref, mk1_ref, mv0_ref, mv1_ref, wo_ref, g_ref, beta_ref, o_ref,
                kb_ref, vb_ref):
    @pl.when(pl.program_id(1) == 0)
    def _():
        for h in range(MEM_HEADS):
            kb_ref[h] = jnp.concatenate([_head_rows(mk0_ref, h), _head_rows(mk1_ref, h)],
                                        axis=1).astype(BF16)
            vb_ref[h] = jnp.concatenate([_head_rows(mv0_ref, h), _head_rows(mv1_ref, h)],
                                        axis=1).astype(BF16)

    x = x_ref[0]
    q = _dot(x.astype(BF16), wq_ref[...])
    qb = (q * (MEM_DH ** -0.5)).astype(BF16)
    outs = []
    for h in range(MEM_HEADS):
        sl = slice(h * MEM_DH, (h + 1) * MEM_DH)
        s = _dot_nt(qb[:, sl], kb_ref[h])
        m = jnp.max(s, axis=1, keepdims=True)
        p = jnp.exp(s - m)
        l = jnp.sum(p, axis=1, keepdims=True)
        o = _dot(p.astype(BF16), vb_ref[h]) / l
        outs.append(o.astype(BF16))
    o_all = jnp.concatenate(outs, axis=1)
    y = DN_ALPHA * x + _dot(o_all, wo_ref[...])
    o_ref[0] = _layer_norm(y, g_ref[...], beta_ref[...])


def _mem_attend_ln(x3, wq, mk, mv, layer, wo, g, beta):
    b, t, _ = x3.shape
    tm = min(512, t)
    xs = pl.BlockSpec((1, tm, D_MODEL), lambda bi, i: (bi, i, 0))
    ws = pl.BlockSpec((D_MODEL, D_MODEL), lambda bi, i: (0, 0))
    half = lambda c: pl.BlockSpec((1, 1, N_MEM, MEM_HEADS, LANES), lambda bi, i: (layer, bi, 0, 0, c))
    vec = pl.BlockSpec((1, D_MODEL), lambda bi, i: (0, 0))
    assert MEM_DH == 2 * LANES
    return pl.pallas_call(
        _mem_kernel,
        grid=(b, t // tm),
        in_specs=[xs, ws, half(0), half(1), half(0), half(1), ws, vec, vec],
        out_specs=xs,
        out_shape=jax.ShapeDtypeStruct(x3.shape, F32),
        scratch_shapes=[pltpu.VMEM((MEM_HEADS, N_MEM, MEM_DH), BF16),
                        pltpu.VMEM((MEM_HEADS, N_MEM, MEM_DH), BF16)],
        compiler_params=_cparams("parallel", "arbitrary"),
        name="mem_attend_ln",
    )(x3, wq, mk, mk, mv, mv, wo, g.reshape(1, D_MODEL), beta.reshape(1, D_MODEL))


def _ffn_kernel(x_ref, w1_ref, w2_ref, g_ref, beta_ref, o_ref, *, chunk):
    x = x_ref[...]
    xb = x.astype(BF16)
    acc = jnp.zeros(x.shape, F32)
    for c in range(D_FF // chunk):
        h = _dot(xb, w1_ref[:, c * chunk:(c + 1) * chunk])
        h = jnp.square(jnp.maximum(h, 0.0))
        acc = acc + _dot(h.astype(BF16), w2_ref[c * chunk:(c + 1) * chunk, :])
    o_ref[...] = _layer_norm(DN_ALPHA * x + acc, g_ref[...], beta_ref[...])


def _ffn_ln(x2, w1, w2, g, beta):
    rows = x2.shape[0]
    tm = _row_tile(rows)
    row = pl.BlockSpec((tm, D_MODEL), lambda i: (i, 0))
    vec = pl.BlockSpec((1, D_MODEL), lambda i: (0, 0))
    once = pl.Buffered(1)
    return pl.pallas_call(
        functools.partial(_ffn_kernel, chunk=1024),
        grid=(rows // tm,),
        in_specs=[row,
                  pl.BlockSpec(w1.shape, lambda i: (0, 0), pipeline_mode=once),
                  pl.BlockSpec(w2.shape, lambda i: (0, 0), pipeline_mode=once),
                  vec, vec],
        out_specs=row,
        out_shape=jax.ShapeDtypeStruct((rows, D_MODEL), F32),
        compiler_params=_cparams("parallel"),
        name="ffn_ln",
    )(x2, w1, w2, g.reshape(1, D_MODEL), beta.reshape(1, D_MODEL))


def _pad_rows(a, total):
    pad = total - a.shape[1]
    if pad == 0:
        return a
    return jnp.pad(a, ((0, 0), (0, pad)) + ((0, 0),) * (a.ndim - 2))


def _hist16(h):
    return jnp.pad(h, ((0, 0), (HIST_ROWS - h.shape[1], 0), (0, 0)))


def _trunk(x, mem_k, mem_v, pool_h, fk_h, fv_h, flf_h, dk_h, dv_h, conv_h, past, wts):
    b, t, _ = x.shape
    rows = b * t
    t_k = past + t
    if past == 0:
        t_kp = t_k
        cs_rows = 1024
    else:
        t_kp = -(-t_k // LANES) * LANES
        cs_rows = t_kp
    x2 = x.reshape(rows, D_MODEL)

    u, qb, k, kb, v, vb, lf = _proj_even(x2, wts["w_in_even"], wts["w_forget"], wts["b_forget"], t)
    lf3 = lf.reshape(b, t, LANES)
    qb3 = qb.reshape(b, t, FOX_WIDTH)
    kb3 = kb.reshape(b, t, FOX_WIDTH)
    vb3 = vb.reshape(b, t, FOX_WIDTH)
    if past:
        lf_hist = jnp.pad(flf_h[0], ((0, 0), (0, 0), (0, LANES - FOX_HEADS)))
        lf_all = jnp.concatenate([lf_hist, lf3], axis=1)
    else:
        lf_all = lf3
    fcum, kaug, qaug = _cumsum_time(_pad_rows(lf_all, t_kp), cs_rows)
    if past:
        ft = jnp.transpose(fcum[:, :, :FOX_HEADS], (0, 2, 1))
        fox_y = _fox_decode(qb3, kb3, vb3, jnp.transpose(fk_h[0], (0, 2, 3, 1)),
                            jnp.transpose(fv_h[0], (0, 2, 3, 1)), fcum, ft, past)
    else:
        fox_y = _fox_attention(qb3, kb3, vb3, qaug, kaug, ATT_TILE, ATT_TILE)
    u3 = u.reshape(b, t, POOL_WIDTH)
    pool_y = _pool_mix(u3, _hist16(pool_h[0]), wts["w_pool_mix"], wts["pool_scale"], past)
    x2 = _outproj_ln(pool_y.reshape(rows, POOL_WIDTH), fox_y.reshape(rows, FOX_WIDTH),
                     wts["w_out_even"], x2, wts["ln_g"][0, 0], wts["ln_b"][0, 0])
    x2 = _mem_attend_ln(x2.reshape(b, t, D_MODEL), wts["w_mem_q"][0], mem_k, mem_v, 0,
                        wts["w_mem_o"][0], wts["ln_g"][0, 1], wts["ln_b"][0, 1]).reshape(rows, D_MODEL)
    x2 = _ffn_ln(x2, wts["w_ff1"][0], wts["w_ff2"][0], wts["ln_g"][0, 2], wts["ln_b"][0, 2])
    n_pool = u3[:, t - POOL_HIST:][None]
    n_fk = _cache_logical(k, b, t, (FOX_HEADS, FOX_DH))
    n_fv = _cache_logical(v, b, t, (FOX_HEADS, FOX_DH))
    n_flf = lf3[:, :, :FOX_HEADS][None]

    lam_init = 0.8 - 0.6 * math.exp(-0.3 * 1)
    qb, k, kb, v, vb, bg, z = _proj_odd(x2, wts["w_in_odd"], t)
    qb3 = qb.reshape(b, t, DIFF_QK)
    kb3 = kb.reshape(b, t, DIFF_QK)
    vb3 = vb.reshape(b, t, DIFF_WIDTH)
    if past:
        att = _diff_decode(qb3, kb3, vb3,
                           jnp.transpose(dk_h[0].reshape(b, past, 2 * DIFF_HEADS, DIFF_DH), (0, 2, 3, 1)),
                           dv_h, wts["rel_bias_table"],
                           wts["diff_lambda"], wts["diff_subln_g"], past, lam_init)
    else:
        att = _diff_attention(qb3, kb3, vb3, wts["rel_bias_table"], wts["diff_lambda"],
                              wts["diff_subln_g"], ATT_TILE, ATT_TILE, lam_init)
    z3 = z.reshape(b, t, CONV_CH)
    conv_y = _short_conv(z3, _hist16(conv_h[0]), bg.reshape(b, t, CONV_CH), wts["conv_w"])
    x2 = _outproj_ln(att.reshape(rows, DIFF_WIDTH), conv_y.reshape(rows, CONV_CH),
                     wts["w_out_odd"], x2, wts["ln_g"][1, 0], wts["ln_b"][1, 0])
    x2 = _mem_attend_ln(x2.reshape(b, t, D_MODEL), wts["w_mem_q"][1], mem_k, mem_v, 1,
                        wts["w_mem_o"][1], wts["ln_g"][1, 1], wts["ln_b"][1, 1]).reshape(rows, D_MODEL)
    x2 = _ffn_ln(x2, wts["w_ff1"][1], wts["w_ff2"][1], wts["ln_g"][1, 2], wts["ln_b"][1, 2])
    n_dk = _cache_logical(k, b, t, (DIFF_HEADS, 2, DIFF_DH))
    n_dv = v.reshape(1, b, t, DIFF_HEADS, DIFF_VD)
    n_conv = z3[:, t - (CONV_K - 1):][None]
    return (x2.reshape(b, t, D_MODEL), n_pool, n_fk, n_fv, n_flf, n_dk, n_dv, n_conv)


def kernel(x_prompt, x_sample, state_pool, cache_fox_k, cache_fox_v, cache_fox_logf,
           cache_diff_k, cache_diff_v, state_conv, cache_mem_k, cache_mem_v, mem_prompt,
           w_in_even, b_forget, w_pool_mix, pool_scale, w_out_even,
           w_in_odd, diff_lambda, diff_subln_g, conv_w, w_out_odd, rel_bias_table,
           w_mem_q, w_mem_k, w_mem_v, w_mem_o, w_ff1, w_ff2, ln_g, ln_b):
    bp = x_prompt.shape[0]
    nmain = POOL_WIDTH + 3 * FOX_WIDTH
    wts = {
        "w_in_even": w_in_even[0, :, :nmain].astype(BF16),
        "w_forget": jnp.pad(jnp.tile(w_in_even[0, :, nmain:], (1, 6)),
                            ((0, 0), (0, LANES - 6 * FOX_HEADS))).astype(BF16),
        "b_forget": jnp.pad(jnp.tile(b_forget[0], 6), (0, LANES - 6 * FOX_HEADS)).reshape(1, LANES).astype(F32),
        "w_pool_mix": w_pool_mix[0].astype(BF16),
        "pool_scale": pool_scale[0],
        "w_out_even": w_out_even[0].astype(BF16),
        "w_in_odd": w_in_odd[0].astype(BF16),
        "diff_lambda": diff_lambda[0],
        "diff_subln_g": diff_subln_g[0],
        "conv_w": conv_w[0],
        "w_out_odd": w_out_odd[0].astype(BF16),
        "rel_bias_table": rel_bias_table,
        "w_mem_q": w_mem_q.astype(BF16),
        "w_mem_o": w_mem_o.astype(BF16),
        "w_ff1": w_ff1.astype(BF16),
        "w_ff2": w_ff2.astype(BF16),
        "ln_g": ln_g,
        "ln_b": ln_b,
    }
    kv = _mem_kv(mem_prompt.reshape(bp * N_MEM, D_MODEL),
                 jnp.stack([w_mem_k, w_mem_v]).astype(BF16))
    kv = kv.reshape(2, DEPTH, bp, N_MEM, D_MODEL)
    p_mem_k = kv[0].reshape(DEPTH, bp, N_MEM, MEM_HEADS, MEM_DH)
    p_mem_v = kv[1].reshape(DEPTH, bp, N_MEM, MEM_HEADS, MEM_DH)
    zeros = lambda *s: jnp.zeros(s, F32)
    (y_prompt, p_pool, p_fox_k, p_fox_v, p_fox_logf, p_diff_k, p_diff_v, p_conv) = _trunk(
        x_prompt, p_mem_k, p_mem_v,
        zeros(1, bp, POOL_HIST, POOL_WIDTH), None, None, None, None, None,
        zeros(1, bp, CONV_K - 1, CONV_CH), 0, wts)
    bs = x_sample.shape[0]
    (y_sample, s_pool, s_fox_k, s_fox_v, s_fox_logf, s_diff_k, s_diff_v, s_conv) = _trunk(
        x_sample, cache_mem_k, cache_mem_v,
        state_pool, cache_fox_k, cache_fox_v, cache_fox_logf,
        cache_diff_k, cache_diff_v, state_conv, cache_fox_k.shape[2], wts)
    return (y_prompt, y_sample,
            p_pool, p_fox_k, p_fox_v, p_fox_logf, p_diff_k, p_diff_v, p_conv, p_mem_k, p_mem_v,
            s_pool, s_fox_k, s_fox_v, s_fox_logf, s_diff_k, s_diff_v, s_conv)
```

```python
import functools
import math

import numpy as np
import jax
import jax.numpy as jnp
from jax import lax
from jax.experimental import pallas as pl
from jax.experimental.pallas import tpu as pltpu

F32 = jnp.float32
BF16 = jnp.bfloat16

D_MODEL = 1024
DEPTH = 2
CHUNK = 64
POOL_WIDTH = 512
POOL_GC = 128
POOL_WINDOWS = (2, 4, 8, 16)
POOL_HIST = 15
FOX_HEADS = 8
FOX_DH = 64
FOX_WIDTH = 512
DIFF_HEADS = 4
DIFF_DH = 64
DIFF_VD = 128
DIFF_QK = 512
DIFF_WIDTH = 512
CONV_CH = 512
CONV_K = 3
D_FF = 4096
N_MEM = 256
MEM_HEADS = 4
MEM_DH = 256
REL_BUCKETS = 32
REL_MAX_DIST = 128
DN_ALPHA = (2 * DEPTH) ** 0.25
LN_EPS = 1e-5
RMS_EPS = 1e-5
NEG = -1e30
LOG2E = math.log2(math.e)

LANES = 128
HIST_ROWS = 16
ATT_TILE = 512
VMEM_LIMIT = 56 * 1024 * 1024


def _cparams(*sem):
    return pltpu.CompilerParams(dimension_semantics=sem, vmem_limit_bytes=VMEM_LIMIT)


def _dot(a, b):
    return jnp.dot(a, b, preferred_element_type=F32)


def _dot_nt(a, b):
    return lax.dot_general(a, b, (((1,), (1,)), ((), ())), preferred_element_type=F32)


def _layer_norm(y, g, b):
    mu = jnp.mean(y, axis=-1, keepdims=True)
    d = y - mu
    var = jnp.mean(d * d, axis=-1, keepdims=True)
    return d * lax.rsqrt(var + LN_EPS) * g + b


def _row_tile(rows):
    return min(512, rows)


def _store_cache(o_ref, y):
    if o_ref.shape[0] == 1:
        o_ref[0] = y.T
        return
    cols = []
    for h in range(o_ref.shape[1]):
        c = y[:, (h // 2) * LANES:(h // 2 + 1) * LANES]
        cols.append(pltpu.roll(c, LANES // 2, 1) if h % 2 else c)
    t = jnp.swapaxes(jnp.stack(cols, axis=0), 0, 1)
    o_ref[...] = t[:, :, :o_ref.shape[2]]


def _cache_out(rows, tm, seq):
    if seq % tm == 0 and tm % LANES == 0:
        nb = seq // tm
        return (pl.BlockSpec((1, 512, tm), lambda i: (i // nb, 0, i % nb)),
                jax.ShapeDtypeStruct((rows // seq, 512, seq), F32))
    return (pl.BlockSpec((tm, 8, 64), lambda i: (i, 0, 0)),
            jax.ShapeDtypeStruct((rows, 8, 64), F32))


def _cache_logical(c, b, t, tail):
    if c.shape[0] == b and c.shape[1] == 512:
        c = jnp.transpose(c.reshape((b,) + tail + (t,)), (0, len(tail) + 1) + tuple(range(1, len(tail) + 1)))
        return c[None]
    return c.reshape((1, b, t) + tail)


def _proj_even_kernel(x_ref, w_ref, wf_ref, bf_ref,
                      u_ref, q_ref, k_ref, kb_ref, v_ref, vb_ref, lf_ref):
    xb = x_ref[...].astype(BF16)

    def mm(c):
        return _dot(xb, w_ref[:, c * 512:(c + 1) * 512])

    u_ref[...] = mm(0)
    q_ref[...] = (mm(1) * (FOX_DH ** -0.5 * LOG2E)).astype(BF16)
    k = mm(2)
    _store_cache(k_ref, k)
    kb_ref[...] = k.astype(BF16)
    v = mm(3)
    _store_cache(v_ref, v)
    vb_ref[...] = v.astype(BF16)
    z = _dot(xb, wf_ref[...]) + bf_ref[...]
    lf_ref[...] = jnp.minimum(z, 0.0) - jnp.log1p(jnp.exp(-jnp.abs(z)))


def _proj_even(x2, w, wf, bf, seq):
    rows = x2.shape[0]
    tm = _row_tile(rows)
    row = lambda n: pl.BlockSpec((tm, n), lambda i: (i, 0))
    full = lambda a: pl.BlockSpec(a.shape, lambda i: (0,) * a.ndim)
    f32o = jax.ShapeDtypeStruct((rows, 512), F32)
    bf16o = jax.ShapeDtypeStruct((rows, 512), BF16)
    heads, headso = _cache_out(rows, tm, seq)
    return pl.pallas_call(
        _proj_even_kernel,
        grid=(rows // tm,),
        in_specs=[row(D_MODEL), full(w), full(wf), full(bf)],
        out_specs=[row(512), row(512), heads, row(512), heads, row(512), row(LANES)],
        out_shape=[f32o, bf16o, headso, bf16o, headso, bf16o,
                   jax.ShapeDtypeStruct((rows, LANES), F32)],
        compiler_params=_cparams("parallel"),
        name="proj_even",
    )(x2, w, wf, bf)


def _proj_odd_kernel(x_ref, w_ref, q_ref, k_ref, kb_ref, v_ref, vb_ref, bg_ref, z_ref):
    xb = x_ref[...].astype(BF16)

    def mm(c):
        return _dot(xb, w_ref[:, c * 512:(c + 1) * 512])

    q_ref[...] = (mm(0) * (DIFF_DH ** -0.5 * LOG2E)).astype(BF16)
    k = mm(1)
    _store_cache(k_ref, k)
    kb_ref[...] = k.astype(BF16)
    v = mm(2)
    tm = v.shape[0]
    v_rows = v_ref.reshape(tm * DIFF_HEADS, DIFF_VD)
    for h in range(DIFF_HEADS):
        v_rows[pl.ds(h, tm, stride=DIFF_HEADS), :] = v[:, h * DIFF_VD:(h + 1) * DIFF_VD]
    vb_ref[...] = v.astype(BF16)
    bg_ref[...] = mm(3)
    z_ref[...] = mm(4) * mm(5)


def _proj_odd(x2, w, seq):
    rows = x2.shape[0]
    tm = _row_tile(rows)
    kspec, kshape = _cache_out(rows, tm, seq)
    row = lambda n: pl.BlockSpec((tm, n), lambda i: (i, 0))
    f32o = jax.ShapeDtypeStruct((rows, 512), F32)
    bf16o = jax.ShapeDtypeStruct((rows, 512), BF16)
    return pl.pallas_call(
        _proj_odd_kernel,
        grid=(rows // tm,),
        in_specs=[row(D_MODEL), pl.BlockSpec(w.shape, lambda i: (0, 0))],
        out_specs=[row(512), kspec, row(512),
                   pl.BlockSpec((tm, DIFF_HEADS, DIFF_VD), lambda i: (i, 0, 0))] + [row(512)] * 3,
        out_shape=[bf16o, kshape, bf16o,
                   jax.ShapeDtypeStruct((rows, DIFF_HEADS, DIFF_VD), F32), bf16o, f32o, f32o],
        compiler_params=_cparams("parallel"),
        name="proj_odd",
    )(x2, w)


def _mem_kv_kernel(x_ref, w_ref, o_ref):
    o_ref[0, 0] = _dot(x_ref[...].astype(BF16), w_ref[0, 0])


def _mem_kv(mem2, w_kv):
    rows = mem2.shape[0]
    return pl.pallas_call(
        _mem_kv_kernel,
        grid=(2, DEPTH),
        in_specs=[pl.BlockSpec((rows, D_MODEL), lambda a, l: (0, 0)),
                  pl.BlockSpec((1, 1, D_MODEL, D_MODEL), lambda a, l: (a, l, 0, 0))],
        out_specs=pl.BlockSpec((1, 1, rows, D_MODEL), lambda a, l: (a, l, 0, 0)),
        out_shape=jax.ShapeDtypeStruct((2, DEPTH, rows, D_MODEL), F32),
        compiler_params=_cparams("parallel", "parallel"),
        name="mem_kv",
    )(mem2, w_kv)


def _cumsum_kernel(x_ref, f_ref, kaug_ref, qaug_ref, carry_ref, *, rows):
    @pl.when(pl.program_id(1) == 0)
    def _():
        carry_ref[...] = jnp.zeros_like(carry_ref)

    r = lax.broadcasted_iota(jnp.int32, (LANES, LANES), 0)
    c = lax.broadcasted_iota(jnp.int32, (LANES, LANES), 1)
    tri = (r >= c).astype(F32)
    grp = c // FOX_HEADS
    carry = carry_ref[...]
    for s in range(rows // LANES):
        sl = slice(s * LANES, (s + 1) * LANES)
        cs = jnp.dot(tri, x_ref[0, sl, :], preferred_element_type=F32,
                     precision=lax.Precision.HIGHEST) + carry
        carry = cs[LANES - 1:LANES, :]
        f = cs * LOG2E
        f_ref[0, sl, :] = f
        hi = f.astype(BF16).astype(F32)
        mid = (f - hi).astype(BF16).astype(F32)
        lo = (f - hi - mid).astype(BF16).astype(F32)
        piece = jnp.where(grp % 3 == 0, hi, jnp.where(grp % 3 == 1, mid, lo))
        kaug_ref[0, sl, :] = jnp.where(grp < 3, -piece, jnp.where(grp < 6, 1.0, 0.0)).astype(BF16)
        qaug_ref[0, sl, :] = jnp.where(grp < 3, 1.0, jnp.where(grp < 6, piece, 0.0)).astype(BF16)
    carry_ref[...] = carry


def _cumsum_time(x, rows):
    b, t, _ = x.shape
    spec = pl.BlockSpec((1, rows, LANES), lambda i, j: (i, j, 0))
    return pl.pallas_call(
        functools.partial(_cumsum_kernel, rows=rows),
        grid=(b, t // rows),
        in_specs=[spec],
        out_specs=[spec] * 3,
        out_shape=[jax.ShapeDtypeStruct(x.shape, F32), jax.ShapeDtypeStruct(x.shape, BF16),
                   jax.ShapeDtypeStruct(x.shape, BF16)],
        scratch_shapes=[pltpu.VMEM((1, LANES), F32)],
        compiler_params=_cparams("parallel", "arbitrary"),
        name="cumsum_time",
    )(x)


def _online_softmax(s, m_old, shift=None):
    row_max = jnp.max(s, axis=1, keepdims=True)
    if shift is not None:
        row_max = row_max + shift
    m_new = jnp.maximum(m_old, row_max)
    m_sub = m_new if shift is None else m_new - shift
    p = jnp.exp2(s - jnp.concatenate([m_sub] * (s.shape[1] // LANES), axis=1))
    return p.astype(BF16), jnp.exp2(m_old - m_new), m_new


def _fox_kernel(q_ref, k_ref, v_ref, qaug_ref, kaug_ref, mask_ref, o_ref,
                qcat_ref, s_ref, p_ref, acc_ref, m_ref, al_ref, *, tq, tk):
    pair = pl.program_id(1)
    j_last = pl.program_id(2)
    lane = lax.broadcasted_iota(jnp.int32, (1, LANES), 1)
    upper = lane >= FOX_DH
    sels = (jnp.logical_not(upper), upper)
    q2 = q_ref[0]
    qa = qaug_ref[0]
    for hh in range(2):
        own = jnp.logical_and(lane % FOX_HEADS == 2 * pair + hh, lane < 6 * FOX_HEADS)
        qcat_ref[hh] = jnp.concatenate([jnp.where(sels[hh], q2, jnp.zeros_like(q2)),
                                        jnp.where(own, qa, jnp.zeros_like(qa))], axis=1)
    m_ref[...] = jnp.full(m_ref.shape, NEG, F32)
    acc_ref[...] = jnp.zeros(acc_ref.shape, F32)
    p_ref[1] = jnp.zeros(p_ref.shape[1:], p_ref.dtype)
    al_ref[1] = jnp.ones(al_ref.shape[1:], F32)

    def scores(hh, j):
        rows = pl.ds(pl.multiple_of(j * tk, tk), tk)
        kcat = jnp.concatenate([k_ref[0, rows, :], kaug_ref[0, rows, :]], axis=1)
        s_ref[hh] = _dot_nt(qcat_ref[hh], kcat)

    def probs(hh, masked):
        s = s_ref[hh]
        if masked:
            s = s + mask_ref[...]
        p_ref[hh], al_ref[hh], m_ref[hh] = _online_softmax(s, m_ref[hh])

    def pv(hh, j):
        vb = v_ref[0, pl.ds(pl.multiple_of(j * tk, tk), tk), :]
        va = jnp.where(sels[hh], vb, jnp.ones_like(vb))
        acc_ref[hh] = al_ref[hh] * acc_ref[hh] + _dot(p_ref[hh], va)

    scores(0, 0)

    def body(j, carry):
        scores(1, j)
        probs(0, False)
        pv(1, jnp.maximum(j - 1, 0))
        scores(0, j + 1)
        probs(1, False)
        pv(0, j)
        return carry

    lax.fori_loop(0, j_last, body, 0)
    scores(1, j_last)
    probs(0, True)
    pv(1, jnp.maximum(j_last - 1, 0))
    probs(1, True)
    pv(0, j_last)
    pv(1, j_last)

    a0 = acc_ref[0]
    a1 = acc_ref[1]
    o0 = a0 / pltpu.roll(a0, FOX_DH, 1)
    o1 = a1 / pltpu.roll(a1, FOX_DH, 1)
    o_ref[0] = jnp.where(upper, o1, o0).astype(o_ref.dtype)


def _fox_attention(qb, kb, vb, qaug, kaug, tq, tk):
    b, t, _ = qb.shape
    assert t % tq == 0 and tq == tk
    ii = np.arange(tq)[:, None]
    jj = np.arange(tk)[None, :]
    mask = jnp.asarray(np.where(jj <= ii, 0.0, NEG), F32)
    return pl.pallas_call(
        functools.partial(_fox_kernel, tq=tq, tk=tk),
        grid=(b, FOX_HEADS // 2, t // tq),
        in_specs=[
            pl.BlockSpec((1, tq, LANES), lambda bi, p, qi: (bi, qi, p)),
            pl.BlockSpec((1, t, LANES), lambda bi, p, qi: (bi, 0, p)),
            pl.BlockSpec((1, t, LANES), lambda bi, p, qi: (bi, 0, p)),
            pl.BlockSpec((1, tq, LANES), lambda bi, p, qi: (bi, qi, 0)),
            pl.BlockSpec((1, t, LANES), lambda bi, p, qi: (bi, 0, 0)),
            pl.BlockSpec((tq, tk), lambda bi, p, qi: (0, 0)),
        ],
        out_specs=pl.BlockSpec((1, tq, LANES), lambda bi, p, qi: (bi, qi, p)),
        out_shape=jax.ShapeDtypeStruct((b, t, FOX_WIDTH), BF16),
        scratch_shapes=[pltpu.VMEM((2, tq, 2 * LANES), BF16),
                        pltpu.VMEM((2, tq, tk), F32),
                        pltpu.VMEM((2, tq, tk), BF16),
                        pltpu.VMEM((2, tq, LANES), F32),
                        pltpu.VMEM((2, tq, LANES), F32),
                        pltpu.VMEM((2, tq, LANES), F32)],
        compiler_params=_cparams("parallel", "parallel", "arbitrary"),
        name="fox_attention",
    )(qb, kb, vb, qaug, kaug, mask)


def _diff_kernel(cfar_ref, q_ref, k_ref, v_ref, tiles_ref, lam_ref, g_ref, o_ref,
                 qm_ref, s_ref, p_ref, acc_ref, m_ref, al_ref, *, tq, tk, lam_init):
    j_last = pl.program_id(2)
    lane = lax.broadcasted_iota(jnp.int32, (1, LANES), 1)
    upper = lane >= DIFF_DH
    q2 = q_ref[0]
    zero = jnp.zeros_like(q2)
    qm_ref[0] = jnp.where(upper, zero, q2)
    qm_ref[1] = jnp.where(upper, q2, zero)
    cfar = cfar_ref[pl.program_id(1)]
    m_ref[...] = jnp.full(m_ref.shape, NEG, F32)
    acc_ref[...] = jnp.zeros(acc_ref.shape, F32)
    p_ref[1] = jnp.zeros(p_ref.shape[1:], p_ref.dtype)
    al_ref[1] = jnp.ones(al_ref.shape[1:], F32)

    def scores(mm, j):
        start = pl.multiple_of(j * tk, tk)
        s_ref[mm] = _dot_nt(qm_ref[mm], k_ref[0, pl.ds(start, tk), :])

    def probs(mm, tile):
        if tile is None:
            out = _online_softmax(s_ref[mm], m_ref[mm], shift=cfar)
        else:
            out = _online_softmax(s_ref[mm] + tiles_ref[0, tile], m_ref[mm])
        p_ref[mm], al_ref[mm], m_ref[mm] = out

    def pv(mm, j):
        start = pl.multiple_of(j * tk, tk)
        vb = v_ref[0, pl.ds(start, tk), :]
        va = jnp.concatenate([vb, jnp.ones_like(vb)], axis=1)
        al = al_ref[mm]
        acc_ref[mm] = jnp.concatenate([al, al], axis=1) * acc_ref[mm] + _dot(p_ref[mm], va)

    def block(j, tile):
        scores(1, j)
        probs(0, tile)
        pv(1, jnp.maximum(j - 1, 0))
        scores(0, j + 1)
        probs(1, tile)
        pv(0, j)

    scores(0, 0)

    def body(j, carry):
        block(j, None)
        return carry

    lax.fori_loop(0, jnp.maximum(j_last - 1, 0), body, 0)

    @pl.when(j_last >= 1)
    def _():
        block(j_last - 1, 1)

    scores(1, j_last)
    probs(0, 0)
    pv(1, jnp.maximum(j_last - 1, 0))
    probs(1, 0)
    pv(0, j_last)
    pv(1, j_last)

    lam = _diff_lambda(lam_ref[...], lam_init)
    a0 = acc_ref[0]
    a1 = acc_ref[1]
    o = a0[:, :LANES] / a0[:, LANES:] - lam * (a1[:, :LANES] / a1[:, LANES:])
    ms = jnp.mean(o * o, axis=1, keepdims=True)
    o = o * lax.rsqrt(ms + RMS_EPS) * g_ref[...] * (1.0 - lam_init)
    o_ref[0] = o.astype(o_ref.dtype)


def _t5_bucket(rel):
    nb = REL_BUCKETS // 2
    max_exact = nb // 2
    ret = jnp.where(rel > 0, nb, 0)
    n = jnp.abs(rel)
    nf = jnp.maximum(n, 1).astype(F32)
    large = max_exact + (jnp.log(nf / max_exact) / math.log(REL_MAX_DIST / max_exact)
                         * (nb - max_exact)).astype(jnp.int32)
    large = jnp.minimum(large, nb - 1)
    return ret + jnp.where(n < max_exact, n, large)


def _toeplitz_kernel(w_ref, mask_ref, o_ref, *, tq, width):
    n = w_ref.shape[2]
    for d in range(w_ref.shape[1]):
        w = jnp.broadcast_to(w_ref[0, d:d + 1, :], (tq, n))
        o_ref[0, d] = pltpu.roll(w, 0, 1, stride=1, stride_axis=0)[:, :width] + mask_ref[d]


def _t5_bias_tiles(rel_table, tq, width, offsets, masks):
    nt = len(offsets)
    n = -(-(tq + width) // LANES) * LANES
    m = jnp.arange(n)
    rel = jnp.where(m < width, m, m - n)
    w = jnp.stack([rel_table[_t5_bucket(rel + d)] for d in offsets])
    w = jnp.transpose(w, (2, 0, 1)).astype(F32) * LOG2E
    return pl.pallas_call(
        functools.partial(_toeplitz_kernel, tq=tq, width=width),
        grid=(DIFF_HEADS,),
        in_specs=[pl.BlockSpec((1, nt, n), lambda h: (h, 0, 0)),
                  pl.BlockSpec((nt, tq, width), lambda h: (0, 0, 0))],
        out_specs=pl.BlockSpec((1, nt, tq, width), lambda h: (h, 0, 0, 0)),
        out_shape=jax.ShapeDtypeStruct((DIFF_HEADS, nt, tq, width), F32),
        compiler_params=_cparams("parallel"),
        name="t5_bias_tiles",
    )(w, jnp.asarray(masks, F32))


def _diff_lambda(lp, lam_init):
    return (jnp.exp(jnp.sum(lp[0:1] * lp[1:2], keepdims=True))
            - jnp.exp(jnp.sum(lp[2:3] * lp[3:4], keepdims=True)) + lam_init)


def _diff_attention(qb, kb, vb, rel_table, diff_lambda, subln_g, tq, tk, lam_init):
    b, t, _ = qb.shape
    assert t % tq == 0 and tq == tk and tk >= REL_MAX_DIST and tk % CHUNK == 0
    ii = np.arange(tq)[:, None]
    jj = np.arange(tk)[None, :]
    masks = np.zeros((2, tq, tk), np.float32)
    masks[0] = np.where((jj // CHUNK) <= (ii // CHUNK), 0.0, NEG)
    tiles = _t5_bias_tiles(rel_table, tq, tk, (0, -tk), masks)
    cfar = rel_table[_t5_bucket(jnp.asarray(-2 * REL_MAX_DIST, jnp.int32))].astype(F32) * LOG2E
    return pl.pallas_call(
        functools.partial(_diff_kernel, tq=tq, tk=tk, lam_init=lam_init),
        grid=(b, DIFF_HEADS, t // tq),
        in_specs=[
            pl.BlockSpec(memory_space=pltpu.SMEM),
            pl.BlockSpec((1, tq, LANES), lambda bi, h, qi: (bi, qi, h)),
            pl.BlockSpec((1, t, LANES), lambda bi, h, qi: (bi, 0, h)),
            pl.BlockSpec((1, t, LANES), lambda bi, h, qi: (bi, 0, h)),
            pl.BlockSpec((1, 2, tq, tk), lambda bi, h, qi: (h, 0, 0, 0)),
            pl.BlockSpec((4, DIFF_DH), lambda bi, h, qi: (0, 0)),
            pl.BlockSpec((1, DIFF_VD), lambda bi, h, qi: (0, 0)),
        ],
        out_specs=pl.BlockSpec((1, tq, LANES), lambda bi, h, qi: (bi, qi, h)),
        out_shape=jax.ShapeDtypeStruct((b, t, DIFF_WIDTH), BF16),
        scratch_shapes=[pltpu.VMEM((2, tq, LANES), BF16),
                        pltpu.VMEM((2, tq, tk), F32),
                        pltpu.VMEM((2, tq, tk), BF16),
                        pltpu.VMEM((2, tq, 2 * LANES), F32),
                        pltpu.VMEM((2, tq, LANES), F32),
                        pltpu.VMEM((2, tq, LANES), F32)],
        compiler_params=_cparams("parallel", "parallel", "arbitrary"),
        name="diff_attention",
    )(cfar, qb, kb, vb, tiles, diff_lambda, subln_g.reshape(1, DIFF_VD))


def _two_part_attend(q_h, kt_c, k_n, v_c, v_n, bias_c, bias_n, v_time_minor):
    s_c = _dot(q_h, kt_c.astype(BF16)) + bias_c
    s_n = _dot_nt(q_h, k_n) + bias_n
    m = jnp.maximum(jnp.max(s_c, axis=1, keepdims=True), jnp.max(s_n, axis=1, keepdims=True))
    p_c = jnp.exp2(s_c - m).astype(BF16)
    p_n = jnp.exp2(s_n - m)
    l = jnp.sum(p_c.astype(F32), axis=1, keepdims=True) + jnp.sum(p_n, axis=1, keepdims=True)
    pv_c = _dot_nt(p_c, v_c.astype(BF16)) if v_time_minor else _dot(p_c, v_c.astype(BF16))
    return (pv_c + _dot(p_n.astype(BF16), v_n)) / l


def _fox_decode_kernel(q_ref, kn_ref, vn_ref, kc_ref, vc_ref, fq_ref, ft_ref, o_ref, *, t, past):
    q = q_ref[0]
    kn = kn_ref[0]
    vn = vn_ref[0]
    fblk = fq_ref[0]
    r = lax.broadcasted_iota(jnp.int32, (t, t), 0)
    c = lax.broadcasted_iota(jnp.int32, (t, t), 1)
    causal = jnp.where(c <= r, 0.0, NEG)
    outs = []
    for h in range(FOX_HEADS):
        sl = slice(h * FOX_DH, (h + 1) * FOX_DH)
        fq = fblk[:, h:h + 1]
        fk = ft_ref[0, h:h + 1, :]
        outs.append(_two_part_attend(q[:, sl], kc_ref[0, h], kn[:, sl], vc_ref[0, h], vn[:, sl],
                                     fq - fk[:, :past], fq - fk[:, past:past + t] + causal, True))
    o_ref[0] = jnp.concatenate(outs, axis=1).astype(o_ref.dtype)


def _decode_specs(t, past):
    new = pl.BlockSpec((1, t, 512), lambda bi: (bi, 0, 0))
    cache = pl.BlockSpec((1, 8, 64, past), lambda bi: (bi, 0, 0, 0))
    return new, cache


def _fox_decode(qb, kn, vn, kc, vc, fcum, ft, past):
    b, t, _ = qb.shape
    t_kp = fcum.shape[1]
    assert past % t == 0
    new, cache = _decode_specs(t, past)
    return pl.pallas_call(
        functools.partial(_fox_decode_kernel, t=t, past=past),
        grid=(b,),
        in_specs=[new, new, new, cache, cache,
                  pl.BlockSpec((1, t, LANES), lambda bi: (bi, past // t, 0)),
                  pl.BlockSpec((1, FOX_HEADS, t_kp), lambda bi: (bi, 0, 0))],
        out_specs=new,
        out_shape=jax.ShapeDtypeStruct((b, t, FOX_WIDTH), BF16),
        compiler_params=_cparams("parallel"),
        name="fox_decode",
    )(qb, kn, vn, kc, vc, fcum, ft)


def _diff_decode_kernel(q_ref, kn_ref, vn_ref, kc_ref, vc_ref, bias_ref, lam_ref, g_ref, o_ref,
                        *, t, past, lam_init):
    q = q_ref[0]
    kn = kn_ref[0]
    vn = vn_ref[0]
    lam = _diff_lambda(lam_ref[...], lam_init)
    outs = []
    for h in range(DIFF_HEADS):
        v_c = _head_rows(vc_ref, h)
        v_n = vn[:, h * DIFF_VD:(h + 1) * DIFF_VD]
        bias = bias_ref[h, 0]
        maps = []
        for mm in range(2):
            j = 2 * h + mm
            sl = slice(j * DIFF_DH, (j + 1) * DIFF_DH)
            maps.append(_two_part_attend(q[:, sl], kc_ref[0, j], kn[:, sl], v_c, v_n,
                                         bias[:, :past], bias[:, past:past + t], False))
        a = maps[0] - lam * maps[1]
        ms = jnp.mean(a * a, axis=1, keepdims=True)
        outs.append(a * lax.rsqrt(ms + RMS_EPS) * g_ref[...] * (1.0 - lam_init))
    o_ref[0] = jnp.concatenate(outs, axis=1).astype(o_ref.dtype)


def _diff_decode(qb, kn, vn, kc, vc, rel_table, diff_lambda, subln_g, past, lam_init):
    b, t, _ = qb.shape
    t_kp = -(-(past + t) // LANES) * LANES
    q_pos = past + np.arange(t)[:, None]
    k_pos = np.arange(t_kp)[None, :]
    visible = ((k_pos // CHUNK) <= (q_pos // CHUNK)) & (k_pos < past + t)
    tiles = _t5_bias_tiles(rel_table, t, t_kp, (-past,), np.where(visible, 0.0, NEG)[None])
    new, cache = _decode_specs(t, past)
    vcache = pl.BlockSpec((1, 1, past, DIFF_HEADS, DIFF_VD), lambda bi: (0, bi, 0, 0, 0))
    return pl.pallas_call(
        functools.partial(_diff_decode_kernel, t=t, past=past, lam_init=lam_init),
        grid=(b,),
        in_specs=[new, new, new, cache, vcache,
                  pl.BlockSpec(tiles.shape, lambda bi: (0, 0, 0, 0)),
                  pl.BlockSpec((4, DIFF_DH), lambda bi: (0, 0)),
                  pl.BlockSpec((1, DIFF_VD), lambda bi: (0, 0))],
        out_specs=new,
        out_shape=jax.ShapeDtypeStruct((b, t, DIFF_WIDTH), BF16),
        compiler_params=_cparams("parallel"),
        name="diff_decode",
    )(qb, kn, vn, kc, vc, tiles, diff_lambda, subln_g.reshape(1, DIFF_VD))


def _fill_history(ext_ref, cur, hist_ref, prev_ref):
    ext_ref[0:HIST_ROWS, :] = jnp.where(pl.program_id(1) == 0, hist_ref[0], prev_ref[0])
    ext_ref[HIST_ROWS:, :] = cur


def _halo_specs(tm, width):
    cur = pl.BlockSpec((1, tm, width), lambda b, i: (b, i, 0))
    prev = pl.BlockSpec((1, HIST_ROWS, width),
                        lambda b, i: (b, jnp.maximum(i * (tm // HIST_ROWS) - 1, 0), 0))
    hist = pl.BlockSpec((1, HIST_ROWS, width), lambda b, i: (b, 0, 0))
    return cur, prev, hist


def _mix_even_kernel(u_ref, prev_ref, hist_ref, wmix_ref, scale_ref, att_ref, w_ref, x_ref,
                     g_ref, beta_ref, o_ref, ext_ref, *, tm, past):
    u = u_ref[0]
    _fill_history(ext_ref, u, hist_ref, prev_ref)
    pos = past + pl.program_id(1) * tm + lax.broadcasted_iota(jnp.int32, (tm, 1), 0)
    groups = []
    for g, w in enumerate(POOL_WINDOWS):
        sl = slice(g * POOL_GC, (g + 1) * POOL_GC)
        ug = u[:, sl]
        wsum = ug
        for s in range(1, w):
            wsum = wsum + ext_ref[HIST_ROWS - s:HIST_ROWS - s + tm, sl]
        cnt = jnp.minimum(w, pos + 1).astype(F32)
        d = wsum / cnt - ug
        groups.append((_dot(d.astype(BF16), wmix_ref[g]) * scale_ref[:, sl]).astype(BF16))
    pool_y = jnp.concatenate(groups, axis=1)
    mix = _dot(pool_y, w_ref[0:POOL_WIDTH, :]) + _dot(att_ref[0], w_ref[POOL_WIDTH:, :])
    o_ref[0] = _layer_norm(DN_ALPHA * x_ref[0] + mix, g_ref[...], beta_ref[...])


def _mix_odd_kernel(z_ref, prev_ref, hist_ref, bg_ref, cw_ref, att_ref, w_ref, x_ref,
                    g_ref, beta_ref, o_ref, ext_ref, *, tm):
    z = z_ref[0]
    _fill_history(ext_ref, z, hist_ref, prev_ref)
    y = (ext_ref[HIST_ROWS - 2:HIST_ROWS - 2 + tm, :] * cw_ref[0:1, :]
         + ext_ref[HIST_ROWS - 1:HIST_ROWS - 1 + tm, :] * cw_ref[1:2, :]
         + z * cw_ref[2:3, :])
    conv_y = (bg_ref[0] * y).astype(BF16)
    mix = _dot(att_ref[0], w_ref[0:DIFF_WIDTH, :]) + _dot(conv_y, w_ref[DIFF_WIDTH:, :])
    o_ref[0] = _layer_norm(DN_ALPHA * x_ref[0] + mix, g_ref[...], beta_ref[...])


def _mix_ln(kernel_fn, name, local_ins, local_specs, att3, w, x3, g, beta, width):
    b, t, _ = x3.shape
    tm = min(512, t)
    cur = lambda n: pl.BlockSpec((1, tm, n), lambda bi, i: (bi, i, 0))
    vec = pl.BlockSpec((1, D_MODEL), lambda bi, i: (0, 0))
    return pl.pallas_call(
        kernel_fn,
        grid=(b, t // tm),
        in_specs=local_specs + [cur(512), pl.BlockSpec(w.shape, lambda bi, i: (0, 0)),
                                cur(D_MODEL), vec, vec],
        out_specs=cur(D_MODEL),
        out_shape=jax.ShapeDtypeStruct(x3.shape, F32),
        scratch_shapes=[pltpu.VMEM((HIST_ROWS + tm, width), F32)],
        compiler_params=_cparams("parallel", "parallel"),
        name=name,
    )(*local_ins, att3, w, x3, g.reshape(1, D_MODEL), beta.reshape(1, D_MODEL))


def _mix_even_ln(u3, hist16, wmix, scale, past, fox3, w, x3, g, beta):
    tm = min(512, u3.shape[1])
    cur, prev, hist = _halo_specs(tm, POOL_WIDTH)
    specs = [cur, prev, hist, pl.BlockSpec(wmix.shape, lambda bi, i: (0, 0, 0)),
             pl.BlockSpec((1, POOL_WIDTH), lambda bi, i: (0, 0))]
    return _mix_ln(functools.partial(_mix_even_kernel, tm=tm, past=past), "mix_even_ln",
                   [u3, u3, hist16, wmix, scale.reshape(1, POOL_WIDTH)], specs,
                   fox3, w, x3, g, beta, POOL_WIDTH)


def _mix_odd_ln(z3, hist16, bg3, conv_w, att3, w, x3, g, beta):
    tm = min(512, z3.shape[1])
    cur, prev, hist = _halo_specs(tm, CONV_CH)
    specs = [cur, prev, hist, cur, pl.BlockSpec((CONV_K, CONV_CH), lambda bi, i: (0, 0))]
    return _mix_ln(functools.partial(_mix_odd_kernel, tm=tm), "mix_odd_ln",
                   [z3, z3, hist16, bg3, conv_w], specs, att3, w, x3, g, beta, CONV_CH)


def _head_rows(c_ref, h):
    _, _, n, heads, d = c_ref.shape
    return c_ref.reshape(n * heads, d)[pl.ds(h, n, stride=heads), :]


def _mem_kernel(x_ref, wq_ref, mk0_ref, mk1_ref, mv0_ref, mv1_ref, wo_ref, g_ref, beta_ref, o_ref,
                kb_ref, vb_ref):
    @pl.when(pl.program_id(1) == 0)
    def _():
        for h in range(MEM_HEADS):
            kb_ref[h] = jnp.concatenate([_head_rows(mk0_ref, h), _head_rows(mk1_ref, h)],
                                        axis=1).astype(BF16)
            vb_ref[h] = jnp.concatenate([_head_rows(mv0_ref, h), _head_rows(mv1_ref, h)],
                                        axis=1).astype(BF16)

    x = x_ref[0]
    q = _dot(x.astype(BF16), wq_ref[0])
    qb = (q * (MEM_DH ** -0.5)).astype(BF16)
    outs = []
    for h in range(MEM_HEADS):
        sl = slice(h * MEM_DH, (h + 1) * MEM_DH)
        s = _dot_nt(qb[:, sl], kb_ref[h])
        m = jnp.max(s, axis=1, keepdims=True)
        p = jnp.exp(s - m)
        l = jnp.sum(p, axis=1, keepdims=True)
        o = _dot(p.astype(BF16), vb_ref[h]) / l
        outs.append(o.astype(BF16))
    o_all = jnp.concatenate(outs, axis=1)
    y = DN_ALPHA * x + _dot(o_all, wo_ref[0])
    o_ref[0] = _layer_norm(y, g_ref[...], beta_ref[...])


def _mem_attend_ln(x3, wq, mk, mv, layer, wo, g, beta):
    b, t, _ = x3.shape
    tm = min(512, t)
    xs = pl.BlockSpec((1, tm, D_MODEL), lambda bi, i: (bi, i, 0))
    ws = pl.BlockSpec((1, D_MODEL, D_MODEL), lambda bi, i: (layer, 0, 0))
    half = lambda c: pl.BlockSpec((1, 1, N_MEM, MEM_HEADS, LANES), lambda bi, i: (layer, bi, 0, 0, c))
    vec = pl.BlockSpec((1, D_MODEL), lambda bi, i: (0, 0))
    assert MEM_DH == 2 * LANES
    return pl.pallas_call(
        _mem_kernel,
        grid=(b, t // tm),
        in_specs=[xs, ws, half(0), half(1), half(0), half(1), ws, vec, vec],
        out_specs=xs,
        out_shape=jax.ShapeDtypeStruct(x3.shape, F32),
        scratch_shapes=[pltpu.VMEM((MEM_HEADS, N_MEM, MEM_DH), BF16),
                        pltpu.VMEM((MEM_HEADS, N_MEM, MEM_DH), BF16)],
        compiler_params=_cparams("parallel", "arbitrary"),
        name="mem_attend_ln",
    )(x3, wq, mk, mk, mv, mv, wo, g.reshape(1, D_MODEL), beta.reshape(1, D_MODEL))


def _ffn_kernel(x_ref, w1_ref, w2_ref, g_ref, beta_ref, o_ref, *, chunk):
    x = x_ref[...]
    xb = x.astype(BF16)
    acc = jnp.zeros(x.shape, F32)
    for c in range(D_FF // chunk):
        h = _dot(xb, w1_ref[0, :, c * chunk:(c + 1) * chunk])
        h = jnp.square(jnp.maximum(h, 0.0))
        acc = acc + _dot(h.astype(BF16), w2_ref[0, c * chunk:(c + 1) * chunk, :])
    o_ref[...] = _layer_norm(DN_ALPHA * x + acc, g_ref[...], beta_ref[...])


def _ffn_ln(x2, w1, w2, layer, g, beta):
    rows = x2.shape[0]
    tm = _row_tile(rows)
    row = pl.BlockSpec((tm, D_MODEL), lambda i: (i, 0))
    vec = pl.BlockSpec((1, D_MODEL), lambda i: (0, 0))
    once = pl.Buffered(1)
    return pl.pallas_call(
        functools.partial(_ffn_kernel, chunk=1024),
        grid=(rows // tm,),
        in_specs=[row,
                  pl.BlockSpec((1,) + w1.shape[1:], lambda i: (layer, 0, 0), pipeline_mode=once),
                  pl.BlockSpec((1,) + w2.shape[1:], lambda i: (layer, 0, 0), pipeline_mode=once),
                  vec, vec],
        out_specs=row,
        out_shape=jax.ShapeDtypeStruct((rows, D_MODEL), F32),
        compiler_params=_cparams("parallel"),
        name="ffn_ln",
    )(x2, w1, w2, g.reshape(1, D_MODEL), beta.reshape(1, D_MODEL))


def _pad_rows(a, total):
    pad = total - a.shape[1]
    if pad == 0:
        return a
    return jnp.pad(a, ((0, 0), (0, pad)) + ((0, 0),) * (a.ndim - 2))


def _hist16(h):
    return jnp.pad(h, ((0, 0), (HIST_ROWS - h.shape[1], 0), (0, 0)))


def _trunk(x, mem_k, mem_v, pool_h, fk_h, fv_h, flf_h, dk_h, dv_h, conv_h, past, wts):
    b, t, _ = x.shape
    rows = b * t
    t_k = past + t
    if past == 0:
        t_kp = t_k
        cs_rows = 1024
    else:
        t_kp = -(-t_k // LANES) * LANES
        cs_rows = t_kp
    x2 = x.reshape(rows, D_MODEL)

    u, qb, k, kb, v, vb, lf = _proj_even(x2, wts["w_in_even"], wts["w_forget"], wts["b_forget"], t)
    lf3 = lf.reshape(b, t, LANES)
    qb3 = qb.reshape(b, t, FOX_WIDTH)
    kb3 = kb.reshape(b, t, FOX_WIDTH)
    vb3 = vb.reshape(b, t, FOX_WIDTH)
    if past:
        lf_hist = jnp.pad(flf_h[0], ((0, 0), (0, 0), (0, LANES - FOX_HEADS)))
        lf_all = jnp.concatenate([lf_hist, lf3], axis=1)
    else:
        lf_all = lf3
    fcum, kaug, qaug = _cumsum_time(_pad_rows(lf_all, t_kp), cs_rows)
    if past:
        ft = jnp.transpose(fcum[:, :, :FOX_HEADS], (0, 2, 1))
        fox_y = _fox_decode(qb3, kb3, vb3, jnp.transpose(fk_h[0], (0, 2, 3, 1)),
                            jnp.transpose(fv_h[0], (0, 2, 3, 1)), fcum, ft, past)
    else:
        fox_y = _fox_attention(qb3, kb3, vb3, qaug, kaug, ATT_TILE, ATT_TILE)
    u3 = u.reshape(b, t, POOL_WIDTH)
    x3 = _mix_even_ln(u3, _hist16(pool_h[0]), wts["w_pool_mix"], wts["pool_scale"], past, fox_y,
                      wts["w_out_even"], x2.reshape(b, t, D_MODEL), wts["ln_g"][0, 0], wts["ln_b"][0, 0])
    x2 = _mem_attend_ln(x3, wts["w_mem_q"], mem_k, mem_v, 0,
                        wts["w_mem_o"], wts["ln_g"][0, 1], wts["ln_b"][0, 1]).reshape(rows, D_MODEL)
    x2 = _ffn_ln(x2, wts["w_ff1"], wts["w_ff2"], 0, wts["ln_g"][0, 2], wts["ln_b"][0, 2])
    n_pool = u3[:, t - POOL_HIST:][None]
    n_fk = _cache_logical(k, b, t, (FOX_HEADS, FOX_DH))
    n_fv = _cache_logical(v, b, t, (FOX_HEADS, FOX_DH))
    n_flf = lf3[:, :, :FOX_HEADS][None]

    lam_init = 0.8 - 0.6 * math.exp(-0.3 * 1)
    qb, k, kb, v, vb, bg, z = _proj_odd(x2, wts["w_in_odd"], t)
    qb3 = qb.reshape(b, t, DIFF_QK)
    kb3 = kb.reshape(b, t, DIFF_QK)
    vb3 = vb.reshape(b, t, DIFF_WIDTH)
    if past:
        att = _diff_decode(qb3, kb3, vb3,
                           jnp.transpose(dk_h[0].reshape(b, past, 2 * DIFF_HEADS, DIFF_DH), (0, 2, 3, 1)),
                           dv_h, wts["rel_bias_table"],
                           wts["diff_lambda"], wts["diff_subln_g"], past, lam_init)
    else:
        att = _diff_attention(qb3, kb3, vb3, wts["rel_bias_table"], wts["diff_lambda"],
                              wts["diff_subln_g"], ATT_TILE, ATT_TILE, lam_init)
    z3 = z.reshape(b, t, CONV_CH)
    x3 = _mix_odd_ln(z3, _hist16(conv_h[0]), bg.reshape(b, t, CONV_CH), wts["conv_w"], att,
                     wts["w_out_odd"], x2.reshape(b, t, D_MODEL), wts["ln_g"][1, 0], wts["ln_b"][1, 0])
    x2 = _mem_attend_ln(x3, wts["w_mem_q"], mem_k, mem_v, 1,
                        wts["w_mem_o"], wts["ln_g"][1, 1], wts["ln_b"][1, 1]).reshape(rows, D_MODEL)
    x2 = _ffn_ln(x2, wts["w_ff1"], wts["w_ff2"], 1, wts["ln_g"][1, 2], wts["ln_b"][1, 2])
    n_dk = _cache_logical(k, b, t, (DIFF_HEADS, 2, DIFF_DH))
    n_dv = v.reshape(1, b, t, DIFF_HEADS, DIFF_VD)
    n_conv = z3[:, t - (CONV_K - 1):][None]
    return (x2.reshape(b, t, D_MODEL), n_pool, n_fk, n_fv, n_flf, n_dk, n_dv, n_conv)


def kernel(x_prompt, x_sample, state_pool, cache_fox_k, cache_fox_v, cache_fox_logf,
           cache_diff_k, cache_diff_v, state_conv, cache_mem_k, cache_mem_v, mem_prompt,
           w_in_even, b_forget, w_pool_mix, pool_scale, w_out_even,
           w_in_odd, diff_lambda, diff_subln_g, conv_w, w_out_odd, rel_bias_table,
           w_mem_q, w_mem_k, w_mem_v, w_mem_o, w_ff1, w_ff2, ln_g, ln_b):
    bp = x_prompt.shape[0]
    nmain = POOL_WIDTH + 3 * FOX_WIDTH
    wts = {
        "w_in_even": w_in_even[0, :, :nmain].astype(BF16),
        "w_forget": jnp.pad(jnp.tile(w_in_even[0, :, nmain:], (1, 6)),
                            ((0, 0), (0, LANES - 6 * FOX_HEADS))).astype(BF16),
        "b_forget": jnp.pad(jnp.tile(b_forget[0], 6), (0, LANES - 6 * FOX_HEADS)).reshape(1, LANES).astype(F32),
        "w_pool_mix": w_pool_mix[0].astype(BF16),
        "pool_scale": pool_scale[0],
        "w_out_even": w_out_even[0].astype(BF16),
        "w_in_odd": w_in_odd[0].astype(BF16),
        "diff_lambda": diff_lambda[0],
        "diff_subln_g": diff_subln_g[0],
        "conv_w": conv_w[0],
        "w_out_odd": w_out_odd[0].astype(BF16),
        "rel_bias_table": rel_bias_table,
        "w_mem_q": w_mem_q.astype(BF16),
        "w_mem_o": w_mem_o.astype(BF16),
        "w_ff1": w_ff1.astype(BF16),
        "w_ff2": w_ff2.astype(BF16),
        "ln_g": ln_g,
        "ln_b": ln_b,
    }
    kv = _mem_kv(mem_prompt.reshape(bp * N_MEM, D_MODEL),
                 jnp.stack([w_mem_k, w_mem_v]).astype(BF16))
    kv = kv.reshape(2, DEPTH, bp, N_MEM, D_MODEL)
    p_mem_k = kv[0].reshape(DEPTH, bp, N_MEM, MEM_HEADS, MEM_DH)
    p_mem_v = kv[1].reshape(DEPTH, bp, N_MEM, MEM_HEADS, MEM_DH)
    zeros = lambda *s: jnp.zeros(s, F32)
    (y_prompt, p_pool, p_fox_k, p_fox_v, p_fox_logf, p_diff_k, p_diff_v, p_conv) = _trunk(
        x_prompt, p_mem_k, p_mem_v,
        zeros(1, bp, POOL_HIST, POOL_WIDTH), None, None, None, None, None,
        zeros(1, bp, CONV_K - 1, CONV_CH), 0, wts)
    bs = x_sample.shape[0]
    (y_sample, s_pool, s_fox_k, s_fox_v, s_fox_logf, s_diff_k, s_diff_v, s_conv) = _trunk(
        x_sample, cache_mem_k, cache_mem_v,
        state_pool, cache_fox_k, cache_fox_v, cache_fox_logf,
        cache_diff_k, cache_diff_v, state_conv, cache_fox_k.shape[2], wts)
    return (y_prompt, y_sample,
            p_pool, p_fox_k, p_fox_v, p_fox_logf, p_diff_k, p_diff_v, p_conv, p_mem_k, p_mem_v,
            s_pool, s_fox_k, s_fox_v, s_fox_logf, s_diff_k, s_diff_v, s_conv)
```

```python
import functools
import math

import numpy as np
import jax
import jax.numpy as jnp
from jax import lax
from jax.experimental import pallas as pl
from jax.experimental.pallas import tpu as pltpu

F32 = jnp.float32
BF16 = jnp.bfloat16

D_MODEL = 1024
DEPTH = 2
CHUNK = 64
POOL_WIDTH = 512
POOL_GC = 128
POOL_WINDOWS = (2, 4, 8, 16)
POOL_HIST = 15
FOX_HEADS = 8
FOX_DH = 64
FOX_WIDTH = 512
DIFF_HEADS = 4
DIFF_DH = 64
DIFF_VD = 128
DIFF_QK = 512
DIFF_WIDTH = 512
CONV_CH = 512
CONV_K = 3
D_FF = 4096
N_MEM = 256
MEM_HEADS = 4
MEM_DH = 256
REL_BUCKETS = 32
REL_MAX_DIST = 128
DN_ALPHA = (2 * DEPTH) ** 0.25
LN_EPS = 1e-5
RMS_EPS = 1e-5
NEG = -1e30
LOG2E = math.log2(math.e)

LANES = 128
HIST_ROWS = 16
ATT_TILE = 512
ATT_UNROLL = 4
VMEM_LIMIT = 56 * 1024 * 1024


def _cparams(*sem):
    return pltpu.CompilerParams(dimension_semantics=sem, vmem_limit_bytes=VMEM_LIMIT)


def _dot(a, b):
    return jnp.dot(a, b, preferred_element_type=F32)


def _dot_nt(a, b):
    return lax.dot_general(a, b, (((1,), (1,)), ((), ())), preferred_element_type=F32)


def _layer_norm(y, g, b):
    mu = jnp.mean(y, axis=-1, keepdims=True)
    d = y - mu
    var = jnp.mean(d * d, axis=-1, keepdims=True)
    return d * lax.rsqrt(var + LN_EPS) * g + b


def _row_tile(rows):
    return min(512, rows)


def _store_cache(o_ref, y):
    if o_ref.shape[0] == 1:
        o_ref[0] = y.T
        return
    cols = []
    for h in range(o_ref.shape[1]):
        c = y[:, (h // 2) * LANES:(h // 2 + 1) * LANES]
        cols.append(pltpu.roll(c, LANES // 2, 1) if h % 2 else c)
    t = jnp.swapaxes(jnp.stack(cols, axis=0), 0, 1)
    o_ref[...] = t[:, :, :o_ref.shape[2]]


def _cache_out(rows, tm, seq):
    if seq % tm == 0 and tm % LANES == 0:
        nb = seq // tm
        return (pl.BlockSpec((1, 512, tm), lambda i: (i // nb, 0, i % nb)),
                jax.ShapeDtypeStruct((rows // seq, 512, seq), F32))
    return (pl.BlockSpec((tm, 8, 64), lambda i: (i, 0, 0)),
            jax.ShapeDtypeStruct((rows, 8, 64), F32))


def _cache_logical(c, b, t, tail):
    if c.shape[0] == b and c.shape[1] == 512:
        c = jnp.transpose(c.reshape((b,) + tail + (t,)), (0, len(tail) + 1) + tuple(range(1, len(tail) + 1)))
        return c[None]
    return c.reshape((1, b, t) + tail)


def _proj_even_kernel(x_ref, w_ref, wf_ref, bf_ref,
                      u_ref, q_ref, k_ref, kb_ref, v_ref, vb_ref, lf_ref):
    xb = x_ref[...].astype(BF16)

    def mm(c):
        return _dot(xb, w_ref[:, c * 512:(c + 1) * 512])

    u_ref[...] = mm(0)
    q_ref[...] = (mm(1) * (FOX_DH ** -0.5 * LOG2E)).astype(BF16)
    k = mm(2)
    _store_cache(k_ref, k)
    kb_ref[...] = k.astype(BF16)
    v = mm(3)
    _store_cache(v_ref, v)
    vb_ref[...] = v.astype(BF16)
    z = _dot(xb, wf_ref[...]) + bf_ref[...]
    lf_ref[...] = jnp.minimum(z, 0.0) - jnp.log1p(jnp.exp(-jnp.abs(z)))


def _proj_even(x2, w, wf, bf, seq):
    rows = x2.shape[0]
    tm = _row_tile(rows)
    row = lambda n: pl.BlockSpec((tm, n), lambda i: (i, 0))
    full = lambda a: pl.BlockSpec(a.shape, lambda i: (0,) * a.ndim)
    f32o = jax.ShapeDtypeStruct((rows, 512), F32)
    bf16o = jax.ShapeDtypeStruct((rows, 512), BF16)
    heads, headso = _cache_out(rows, tm, seq)
    return pl.pallas_call(
        _proj_even_kernel,
        grid=(rows // tm,),
        in_specs=[row(D_MODEL), full(w), full(wf), full(bf)],
        out_specs=[row(512), row(512), heads, row(512), heads, row(512), row(LANES)],
        out_shape=[f32o, bf16o, headso, bf16o, headso, bf16o,
                   jax.ShapeDtypeStruct((rows, LANES), F32)],
        compiler_params=_cparams("parallel"),
        name="proj_even",
    )(x2, w, wf, bf)


def _proj_odd_kernel(x_ref, w_ref, q_ref, k_ref, kb_ref, v_ref, vb_ref, bg_ref, z_ref):
    xb = x_ref[...].astype(BF16)

    def mm(c):
        return _dot(xb, w_ref[:, c * 512:(c + 1) * 512])

    q_ref[...] = (mm(0) * (DIFF_DH ** -0.5 * LOG2E)).astype(BF16)
    k = mm(1)
    _store_cache(k_ref, k)
    kb_ref[...] = k.astype(BF16)
    v = mm(2)
    tm = v.shape[0]
    v_rows = v_ref.reshape(tm * DIFF_HEADS, DIFF_VD)
    for h in range(DIFF_HEADS):
        v_rows[pl.ds(h, tm, stride=DIFF_HEADS), :] = v[:, h * DIFF_VD:(h + 1) * DIFF_VD]
    vb_ref[...] = v.astype(BF16)
    bg_ref[...] = mm(3)
    z_ref[...] = mm(4) * mm(5)


def _proj_odd(x2, w, seq):
    rows = x2.shape[0]
    tm = _row_tile(rows)
    kspec, kshape = _cache_out(rows, tm, seq)
    row = lambda n: pl.BlockSpec((tm, n), lambda i: (i, 0))
    f32o = jax.ShapeDtypeStruct((rows, 512), F32)
    bf16o = jax.ShapeDtypeStruct((rows, 512), BF16)
    return pl.pallas_call(
        _proj_odd_kernel,
        grid=(rows // tm,),
        in_specs=[row(D_MODEL), pl.BlockSpec(w.shape, lambda i: (0, 0))],
        out_specs=[row(512), kspec, row(512),
                   pl.BlockSpec((tm, DIFF_HEADS, DIFF_VD), lambda i: (i, 0, 0))] + [row(512)] * 3,
        out_shape=[bf16o, kshape, bf16o,
                   jax.ShapeDtypeStruct((rows, DIFF_HEADS, DIFF_VD), F32), bf16o, f32o, f32o],
        compiler_params=_cparams("parallel"),
        name="proj_odd",
    )(x2, w)


def _mem_kv_kernel(x_ref, w_ref, o_ref):
    o_ref[0, 0] = _dot(x_ref[...].astype(BF16), w_ref[0, 0])


def _mem_kv(mem2, w_kv):
    rows = mem2.shape[0]
    return pl.pallas_call(
        _mem_kv_kernel,
        grid=(2, DEPTH),
        in_specs=[pl.BlockSpec((rows, D_MODEL), lambda a, l: (0, 0)),
                  pl.BlockSpec((1, 1, D_MODEL, D_MODEL), lambda a, l: (a, l, 0, 0))],
        out_specs=pl.BlockSpec((1, 1, rows, D_MODEL), lambda a, l: (a, l, 0, 0)),
        out_shape=jax.ShapeDtypeStruct((2, DEPTH, rows, D_MODEL), F32),
        compiler_params=_cparams("parallel", "parallel"),
        name="mem_kv",
    )(mem2, w_kv)


def _cumsum_kernel(x_ref, f_ref, kaug_ref, qaug_ref, carry_ref, *, rows):
    @pl.when(pl.program_id(1) == 0)
    def _():
        carry_ref[...] = jnp.zeros_like(carry_ref)

    r = lax.broadcasted_iota(jnp.int32, (LANES, LANES), 0)
    c = lax.broadcasted_iota(jnp.int32, (LANES, LANES), 1)
    tri = (r >= c).astype(F32)
    grp = c // FOX_HEADS
    carry = carry_ref[...]
    for s in range(rows // LANES):
        sl = slice(s * LANES, (s + 1) * LANES)
        cs = jnp.dot(tri, x_ref[0, sl, :], preferred_element_type=F32,
                     precision=lax.Precision.HIGHEST) + carry
        carry = cs[LANES - 1:LANES, :]
        f = cs * LOG2E
        f_ref[0, sl, :] = f
        hi = f.astype(BF16).astype(F32)
        mid = (f - hi).astype(BF16).astype(F32)
        lo = (f - hi - mid).astype(BF16).astype(F32)
        piece = jnp.where(grp % 3 == 0, hi, jnp.where(grp % 3 == 1, mid, lo))
        kaug_ref[0, sl, :] = jnp.where(grp < 3, -piece, jnp.where(grp < 6, 1.0, 0.0)).astype(BF16)
        qaug_ref[0, sl, :] = jnp.where(grp < 3, 1.0, jnp.where(grp < 6, piece, 0.0)).astype(BF16)
    carry_ref[...] = carry


def _cumsum_time(x, rows):
    b, t, _ = x.shape
    spec = pl.BlockSpec((1, rows, LANES), lambda i, j: (i, j, 0))
    return pl.pallas_call(
        functools.partial(_cumsum_kernel, rows=rows),
        grid=(b, t // rows),
        in_specs=[spec],
        out_specs=[spec] * 3,
        out_shape=[jax.ShapeDtypeStruct(x.shape, F32), jax.ShapeDtypeStruct(x.shape, BF16),
                   jax.ShapeDtypeStruct(x.shape, BF16)],
        scratch_shapes=[pltpu.VMEM((1, LANES), F32)],
        compiler_params=_cparams("parallel", "arbitrary"),
        name="cumsum_time",
    )(x)


def _loop_unrolled(block, n, unroll):
    def body(i, carry):
        for u in range(unroll):
            block(unroll * i + u)
        return carry

    lax.fori_loop(0, n // unroll, body, 0)
    group = unroll // 2
    while group >= 1:
        start = n - n % (2 * group)

        @pl.when(n % (2 * group) >= group)
        def _(start=start, group=group):
            for u in range(group):
                block(start + u)

        group //= 2


def _online_softmax(s, m_old, shift=None):
    row_max = jnp.max(s, axis=1, keepdims=True)
    if shift is not None:
        row_max = row_max + shift
    m_new = jnp.maximum(m_old, row_max)
    m_sub = m_new if shift is None else m_new - shift
    p = jnp.exp2(s - jnp.concatenate([m_sub] * (s.shape[1] // LANES), axis=1))
    return p.astype(BF16), jnp.exp2(m_old - m_new), m_new


def _fox_kernel(q_ref, k_ref, v_ref, qaug_ref, kaug_ref, mask_ref, o_ref,
                qcat_ref, s_ref, p_ref, acc_ref, m_ref, al_ref, *, tq, tk):
    pair = pl.program_id(1)
    j_last = pl.program_id(2)
    lane = lax.broadcasted_iota(jnp.int32, (1, LANES), 1)
    upper = lane >= FOX_DH
    sels = (jnp.logical_not(upper), upper)
    q2 = q_ref[0]
    qa = qaug_ref[0]
    for hh in range(2):
        own = jnp.logical_and(lane % FOX_HEADS == 2 * pair + hh, lane < 6 * FOX_HEADS)
        qcat_ref[hh] = jnp.concatenate([jnp.where(sels[hh], q2, jnp.zeros_like(q2)),
                                        jnp.where(own, qa, jnp.zeros_like(qa))], axis=1)
    m_ref[...] = jnp.full(m_ref.shape, NEG, F32)
    acc_ref[...] = jnp.zeros(acc_ref.shape, F32)
    p_ref[1] = jnp.zeros(p_ref.shape[1:], p_ref.dtype)
    al_ref[1] = jnp.ones(al_ref.shape[1:], F32)

    def scores(hh, j):
        rows = pl.ds(pl.multiple_of(j * tk, tk), tk)
        kcat = jnp.concatenate([k_ref[0, rows, :], kaug_ref[0, rows, :]], axis=1)
        s_ref[hh] = _dot_nt(qcat_ref[hh], kcat)

    def probs(hh, masked):
        s = s_ref[hh]
        if masked:
            s = s + mask_ref[...]
        p_ref[hh], al_ref[hh], m_ref[hh] = _online_softmax(s, m_ref[hh])

    def pv(hh, j):
        vb = v_ref[0, pl.ds(pl.multiple_of(j * tk, tk), tk), :]
        va = jnp.where(sels[hh], vb, jnp.ones_like(vb))
        acc_ref[hh] = al_ref[hh] * acc_ref[hh] + _dot(p_ref[hh], va)

    scores(0, 0)

    def block(j):
        scores(1, j)
        probs(0, False)
        pv(1, jnp.maximum(j - 1, 0))
        scores(0, j + 1)
        probs(1, False)
        pv(0, j)

    _loop_unrolled(block, j_last, ATT_UNROLL)
    scores(1, j_last)
    probs(0, True)
    pv(1, jnp.maximum(j_last - 1, 0))
    probs(1, True)
    pv(0, j_last)
    pv(1, j_last)

    a0 = acc_ref[0]
    a1 = acc_ref[1]
    o0 = a0 / pltpu.roll(a0, FOX_DH, 1)
    o1 = a1 / pltpu.roll(a1, FOX_DH, 1)
    o_ref[0] = jnp.where(upper, o1, o0).astype(o_ref.dtype)


def _fox_attention(qb, kb, vb, qaug, kaug, tq, tk):
    b, t, _ = qb.shape
    assert t % tq == 0 and tq == tk
    ii = np.arange(tq)[:, None]
    jj = np.arange(tk)[None, :]
    mask = jnp.asarray(np.where(jj <= ii, 0.0, NEG), F32)
    return pl.pallas_call(
        functools.partial(_fox_kernel, tq=tq, tk=tk),
        grid=(b, FOX_HEADS // 2, t // tq),
        in_specs=[
            pl.BlockSpec((1, tq, LANES), lambda bi, p, qi: (bi, qi, p)),
            pl.BlockSpec((1, t, LANES), lambda bi, p, qi: (bi, 0, p)),
            pl.BlockSpec((1, t, LANES), lambda bi, p, qi: (bi, 0, p)),
            pl.BlockSpec((1, tq, LANES), lambda bi, p, qi: (bi, qi, 0)),
            pl.BlockSpec((1, t, LANES), lambda bi, p, qi: (bi, 0, 0)),
            pl.BlockSpec((tq, tk), lambda bi, p, qi: (0, 0)),
        ],
        out_specs=pl.BlockSpec((1, tq, LANES), lambda bi, p, qi: (bi, qi, p)),
        out_shape=jax.ShapeDtypeStruct((b, t, FOX_WIDTH), BF16),
        scratch_shapes=[pltpu.VMEM((2, tq, 2 * LANES), BF16),
                        pltpu.VMEM((2, tq, tk), F32),
                        pltpu.VMEM((2, tq, tk), BF16),
                        pltpu.VMEM((2, tq, LANES), F32),
                        pltpu.VMEM((2, tq, LANES), F32),
                        pltpu.VMEM((2, tq, LANES), F32)],
        compiler_params=_cparams("parallel", "parallel", "arbitrary"),
        name="fox_attention",
    )(qb, kb, vb, qaug, kaug, mask)


def _diff_kernel(cfar_ref, q_ref, k_ref, v_ref, tiles_ref, lam_ref, g_ref, o_ref,
                 qm_ref, s_ref, p_ref, acc_ref, m_ref, al_ref, *, tq, tk, lam_init):
    j_last = pl.program_id(2)
    lane = lax.broadcasted_iota(jnp.int32, (1, LANES), 1)
    upper = lane >= DIFF_DH
    q2 = q_ref[0]
    zero = jnp.zeros_like(q2)
    qm_ref[0] = jnp.where(upper, zero, q2)
    qm_ref[1] = jnp.where(upper, q2, zero)
    cfar = cfar_ref[pl.program_id(1)]
    m_ref[...] = jnp.full(m_ref.shape, NEG, F32)
    acc_ref[...] = jnp.zeros(acc_ref.shape, F32)
    p_ref[1] = jnp.zeros(p_ref.shape[1:], p_ref.dtype)
    al_ref[1] = jnp.ones(al_ref.shape[1:], F32)

    def scores(mm, j):
        start = pl.multiple_of(j * tk, tk)
        s_ref[mm] = _dot_nt(qm_ref[mm], k_ref[0, pl.ds(start, tk), :])

    def probs(mm, tile):
        if tile is None:
            out = _online_softmax(s_ref[mm], m_ref[mm], shift=cfar)
        else:
            out = _online_softmax(s_ref[mm] + tiles_ref[0, tile], m_ref[mm])
        p_ref[mm], al_ref[mm], m_ref[mm] = out

    def pv(mm, j):
        start = pl.multiple_of(j * tk, tk)
        vb = v_ref[0, pl.ds(start, tk), :]
        va = jnp.concatenate([vb, jnp.ones_like(vb)], axis=1)
        al = al_ref[mm]
        acc_ref[mm] = jnp.concatenate([al, al], axis=1) * acc_ref[mm] + _dot(p_ref[mm], va)

    def block(j, tile):
        scores(1, j)
        probs(0, tile)
        pv(1, jnp.maximum(j - 1, 0))
        scores(0, j + 1)
        probs(1, tile)
        pv(0, j)

    scores(0, 0)
    _loop_unrolled(lambda j: block(j, None), jnp.maximum(j_last - 1, 0), ATT_UNROLL)

    @pl.when(j_last >= 1)
    def _():
        block(j_last - 1, 1)

    scores(1, j_last)
    probs(0, 0)
    pv(1, jnp.maximum(j_last - 1, 0))
    probs(1, 0)
    pv(0, j_last)
    pv(1, j_last)

    lam = _diff_lambda(lam_ref[...], lam_init)
    a0 = acc_ref[0]
    a1 = acc_ref[1]
    o = a0[:, :LANES] / a0[:, LANES:] - lam * (a1[:, :LANES] / a1[:, LANES:])
    ms = jnp.mean(o * o, axis=1, keepdims=True)
    o = o * lax.rsqrt(ms + RMS_EPS) * g_ref[...] * (1.0 - lam_init)
    o_ref[0] = o.astype(o_ref.dtype)


def _t5_bucket(rel):
    nb = REL_BUCKETS // 2
    max_exact = nb // 2
    ret = jnp.where(rel > 0, nb, 0)
    n = jnp.abs(rel)
    nf = jnp.maximum(n, 1).astype(F32)
    large = max_exact + (jnp.log(nf / max_exact) / math.log(REL_MAX_DIST / max_exact)
                         * (nb - max_exact)).astype(jnp.int32)
    large = jnp.minimum(large, nb - 1)
    return ret + jnp.where(n < max_exact, n, large)


def _toeplitz_kernel(w_ref, mask_ref, o_ref, *, tq, width):
    n = w_ref.shape[2]
    for d in range(w_ref.shape[1]):
        w = jnp.broadcast_to(w_ref[0, d:d + 1, :], (tq, n))
        o_ref[0, d] = pltpu.roll(w, 0, 1, stride=1, stride_axis=0)[:, :width] + mask_ref[d]


def _t5_bias_tiles(rel_table, tq, width, offsets, masks):
    nt = len(offsets)
    n = -(-(tq + width) // LANES) * LANES
    m = jnp.arange(n)
    rel = jnp.where(m < width, m, m - n)
    w = jnp.stack([rel_table[_t5_bucket(rel + d)] for d in offsets])
    w = jnp.transpose(w, (2, 0, 1)).astype(F32) * LOG2E
    return pl.pallas_call(
        functools.partial(_toeplitz_kernel, tq=tq, width=width),
        grid=(DIFF_HEADS,),
        in_specs=[pl.BlockSpec((1, nt, n), lambda h: (h, 0, 0)),
                  pl.BlockSpec((nt, tq, width), lambda h: (0, 0, 0))],
        out_specs=pl.BlockSpec((1, nt, tq, width), lambda h: (h, 0, 0, 0)),
        out_shape=jax.ShapeDtypeStruct((DIFF_HEADS, nt, tq, width), F32),
        compiler_params=_cparams("parallel"),
        name="t5_bias_tiles",
    )(w, jnp.asarray(masks, F32))


def _diff_lambda(lp, lam_init):
    return (jnp.exp(jnp.sum(lp[0:1] * lp[1:2], keepdims=True))
            - jnp.exp(jnp.sum(lp[2:3] * lp[3:4], keepdims=True)) + lam_init)


def _diff_attention(qb, kb, vb, rel_table, diff_lambda, subln_g, tq, tk, lam_init):
    b, t, _ = qb.shape
    assert t % tq == 0 and tq == tk and tk >= REL_MAX_DIST and tk % CHUNK == 0
    ii = np.arange(tq)[:, None]
    jj = np.arange(tk)[None, :]
    masks = np.zeros((2, tq, tk), np.float32)
    masks[0] = np.where((jj // CHUNK) <= (ii // CHUNK), 0.0, NEG)
    tiles = _t5_bias_tiles(rel_table, tq, tk, (0, -tk), masks)
    cfar = rel_table[_t5_bucket(jnp.asarray(-2 * REL_MAX_DIST, jnp.int32))].astype(F32) * LOG2E
    return pl.pallas_call(
        functools.partial(_diff_kernel, tq=tq, tk=tk, lam_init=lam_init),
        grid=(b, DIFF_HEADS, t // tq),
        in_specs=[
            pl.BlockSpec(memory_space=pltpu.SMEM),
            pl.BlockSpec((1, tq, LANES), lambda bi, h, qi: (bi, qi, h)),
            pl.BlockSpec((1, t, LANES), lambda bi, h, qi: (bi, 0, h)),
            pl.BlockSpec((1, t, LANES), lambda bi, h, qi: (bi, 0, h)),
            pl.BlockSpec((1, 2, tq, tk), lambda bi, h, qi: (h, 0, 0, 0)),
            pl.BlockSpec((4, DIFF_DH), lambda bi, h, qi: (0, 0)),
            pl.BlockSpec((1, DIFF_VD), lambda bi, h, qi: (0, 0)),
        ],
        out_specs=pl.BlockSpec((1, tq, LANES), lambda bi, h, qi: (bi, qi, h)),
        out_shape=jax.ShapeDtypeStruct((b, t, DIFF_WIDTH), BF16),
        scratch_shapes=[pltpu.VMEM((2, tq, LANES), BF16),
                        pltpu.VMEM((2, tq, tk), F32),
                        pltpu.VMEM((2, tq, tk), BF16),
                        pltpu.VMEM((2, tq, 2 * LANES), F32),
                        pltpu.VMEM((2, tq, LANES), F32),
                        pltpu.VMEM((2, tq, LANES), F32)],
        compiler_params=_cparams("parallel", "parallel", "arbitrary"),
        name="diff_attention",
    )(cfar, qb, kb, vb, tiles, diff_lambda, subln_g.reshape(1, DIFF_VD))


def _two_part_attend(q_h, kt_c, k_n, v_c, v_n, bias_c, bias_n, v_time_minor):
    s_c = _dot(q_h, kt_c.astype(BF16)) + bias_c
    s_n = _dot_nt(q_h, k_n) + bias_n
    m = jnp.maximum(jnp.max(s_c, axis=1, keepdims=True), jnp.max(s_n, axis=1, keepdims=True))
    p_c = jnp.exp2(s_c - m).astype(BF16)
    p_n = jnp.exp2(s_n - m)
    l = jnp.sum(p_c.astype(F32), axis=1, keepdims=True) + jnp.sum(p_n, axis=1, keepdims=True)
    pv_c = _dot_nt(p_c, v_c.astype(BF16)) if v_time_minor else _dot(p_c, v_c.astype(BF16))
    return (pv_c + _dot(p_n.astype(BF16), v_n)) / l


def _fox_decode_kernel(q_ref, kn_ref, vn_ref, kc_ref, vc_ref, fq_ref, ft_ref, o_ref, *, t, past):
    q = q_ref[0]
    kn = kn_ref[0]
    vn = vn_ref[0]
    fblk = fq_ref[0]
    r = lax.broadcasted_iota(jnp.int32, (t, t), 0)
    c = lax.broadcasted_iota(jnp.int32, (t, t), 1)
    causal = jnp.where(c <= r, 0.0, NEG)
    outs = []
    for h in range(FOX_HEADS):
        sl = slice(h * FOX_DH, (h + 1) * FOX_DH)
        fq = fblk[:, h:h + 1]
        fk = ft_ref[0, h:h + 1, :]
        outs.append(_two_part_attend(q[:, sl], kc_ref[0, h], kn[:, sl], vc_ref[0, h], vn[:, sl],
                                     fq - fk[:, :past], fq - fk[:, past:past + t] + causal, True))
    o_ref[0] = jnp.concatenate(outs, axis=1).astype(o_ref.dtype)


def _decode_specs(t, past):
    new = pl.BlockSpec((1, t, 512), lambda bi: (bi, 0, 0))
    cache = pl.BlockSpec((1, 8, 64, past), lambda bi: (bi, 0, 0, 0))
    return new, cache


def _fox_decode(qb, kn, vn, kc, vc, fcum, ft, past):
    b, t, _ = qb.shape
    t_kp = fcum.shape[1]
    assert past % t == 0
    new, cache = _decode_specs(t, past)
    return pl.pallas_call(
        functools.partial(_fox_decode_kernel, t=t, past=past),
        grid=(b,),
        in_specs=[new, new, new, cache, cache,
                  pl.BlockSpec((1, t, LANES), lambda bi: (bi, past // t, 0)),
                  pl.BlockSpec((1, FOX_HEADS, t_kp), lambda bi: (bi, 0, 0))],
        out_specs=new,
        out_shape=jax.ShapeDtypeStruct((b, t, FOX_WIDTH), BF16),
        compiler_params=_cparams("parallel"),
        name="fox_decode",
    )(qb, kn, vn, kc, vc, fcum, ft)


def _diff_decode_kernel(q_ref, kn_ref, vn_ref, kc_ref, vc_ref, bias_ref, lam_ref, g_ref, o_ref,
                        *, t, past, lam_init):
    q = q_ref[0]
    kn = kn_ref[0]
    vn = vn_ref[0]
    lam = _diff_lambda(lam_ref[...], lam_init)
    outs = []
    for h in range(DIFF_HEADS):
        v_c = _head_rows(vc_ref, h)
        v_n = vn[:, h * DIFF_VD:(h + 1) * DIFF_VD]
        bias = bias_ref[h, 0]
        maps = []
        for mm in range(2):
            j = 2 * h + mm
            sl = slice(j * DIFF_DH, (j + 1) * DIFF_DH)
            maps.append(_two_part_attend(q[:, sl], kc_ref[0, j], kn[:, sl], v_c, v_n,
                                         bias[:, :past], bias[:, past:past + t], False))
        a = maps[0] - lam * maps[1]
        ms = jnp.mean(a * a, axis=1, keepdims=True)
        outs.append(a * lax.rsqrt(ms + RMS_EPS) * g_ref[...] * (1.0 - lam_init))
    o_ref[0] = jnp.concatenate(outs, axis=1).astype(o_ref.dtype)


def _diff_decode(qb, kn, vn, kc, vc, rel_table, diff_lambda, subln_g, past, lam_init):
    b, t, _ = qb.shape
    t_kp = -(-(past + t) // LANES) * LANES
    q_pos = past + np.arange(t)[:, None]
    k_pos = np.arange(t_kp)[None, :]
    visible = ((k_pos // CHUNK) <= (q_pos // CHUNK)) & (k_pos < past + t)
    tiles = _t5_bias_tiles(rel_table, t, t_kp, (-past,), np.where(visible, 0.0, NEG)[None])
    new, cache = _decode_specs(t, past)
    vcache = pl.BlockSpec((1, 1, past, DIFF_HEADS, DIFF_VD), lambda bi: (0, bi, 0, 0, 0))
    return pl.pallas_call(
        functools.partial(_diff_decode_kernel, t=t, past=past, lam_init=lam_init),
        grid=(b,),
        in_specs=[new, new, new, cache, vcache,
                  pl.BlockSpec(tiles.shape, lambda bi: (0, 0, 0, 0)),
                  pl.BlockSpec((4, DIFF_DH), lambda bi: (0, 0)),
                  pl.BlockSpec((1, DIFF_VD), lambda bi: (0, 0))],
        out_specs=new,
        out_shape=jax.ShapeDtypeStruct((b, t, DIFF_WIDTH), BF16),
        compiler_params=_cparams("parallel"),
        name="diff_decode",
    )(qb, kn, vn, kc, vc, tiles, diff_lambda, subln_g.reshape(1, DIFF_VD))


def _fill_history(ext_ref, cur, hist_ref, prev_ref):
    ext_ref[0:HIST_ROWS, :] = jnp.where(pl.program_id(1) == 0, hist_ref[0], prev_ref[0])
    ext_ref[HIST_ROWS:, :] = cur


def _halo_specs(tm, width):
    cur = pl.BlockSpec((1, tm, width), lambda b, i: (b, i, 0))
    prev = pl.BlockSpec((1, HIST_ROWS, width),
                        lambda b, i: (b, jnp.maximum(i * (tm // HIST_ROWS) - 1, 0), 0))
    hist = pl.BlockSpec((1, HIST_ROWS, width), lambda b, i: (b, 0, 0))
    return cur, prev, hist


def _mix_even_kernel(u_ref, prev_ref, hist_ref, wmix_ref, scale_ref, att_ref, w_ref, x_ref,
                     g_ref, beta_ref, o_ref, ext_ref, *, tm, past):
    u = u_ref[0]
    _fill_history(ext_ref, u, hist_ref, prev_ref)
    pos = past + pl.program_id(1) * tm + lax.broadcasted_iota(jnp.int32, (tm, 1), 0)
    groups = []
    for g, w in enumerate(POOL_WINDOWS):
        sl = slice(g * POOL_GC, (g + 1) * POOL_GC)
        ug = u[:, sl]
        wsum = ug
        for s in range(1, w):
            wsum = wsum + ext_ref[HIST_ROWS - s:HIST_ROWS - s + tm, sl]
        cnt = jnp.minimum(w, pos + 1).astype(F32)
        d = wsum / cnt - ug
        groups.append((_dot(d.astype(BF16), wmix_ref[g]) * scale_ref[:, sl]).astype(BF16))
    pool_y = jnp.concatenate(groups, axis=1)
    mix = _dot(pool_y, w_ref[0:POOL_WIDTH, :]) + _dot(att_ref[0], w_ref[POOL_WIDTH:, :])
    o_ref[0] = _layer_norm(DN_ALPHA * x_ref[0] + mix, g_ref[...], beta_ref[...])


def _mix_odd_kernel(z_ref, prev_ref, hist_ref, bg_ref, cw_ref, att_ref, w_ref, x_ref,
                    g_ref, beta_ref, o_ref, ext_ref, *, tm):
    z = z_ref[0]
    _fill_history(ext_ref, z, hist_ref, prev_ref)
    y = (ext_ref[HIST_ROWS - 2:HIST_ROWS - 2 + tm, :] * cw_ref[0:1, :]
         + ext_ref[HIST_ROWS - 1:HIST_ROWS - 1 + tm, :] * cw_ref[1:2, :]
         + z * cw_ref[2:3, :])
    conv_y = (bg_ref[0] * y).astype(BF16)
    mix = _dot(att_ref[0], w_ref[0:DIFF_WIDTH, :]) + _dot(conv_y, w_ref[DIFF_WIDTH:, :])
    o_ref[0] = _layer_norm(DN_ALPHA * x_ref[0] + mix, g_ref[...], beta_ref[...])


def _mix_ln(kernel_fn, name, local_ins, local_specs, att3, w, x3, g, beta, width):
    b, t, _ = x3.shape
    tm = min(512, t)
    cur = lambda n: pl.BlockSpec((1, tm, n), lambda bi, i: (bi, i, 0))
    vec = pl.BlockSpec((1, D_MODEL), lambda bi, i: (0, 0))
    return pl.pallas_call(
        kernel_fn,
        grid=(b, t // tm),
        in_specs=local_specs + [cur(512), pl.BlockSpec(w.shape, lambda bi, i: (0, 0)),
                                cur(D_MODEL), vec, vec],
        out_specs=cur(D_MODEL),
        out_shape=jax.ShapeDtypeStruct(x3.shape, F32),
        scratch_shapes=[pltpu.VMEM((HIST_ROWS + tm, width), F32)],
        compiler_params=_cparams("parallel", "parallel"),
        name=name,
    )(*local_ins, att3, w, x3, g.reshape(1, D_MODEL), beta.reshape(1, D_MODEL))


def _mix_even_ln(u3, hist16, wmix, scale, past, fox3, w, x3, g, beta):
    tm = min(512, u3.shape[1])
    cur, prev, hist = _halo_specs(tm, POOL_WIDTH)
    specs = [cur, prev, hist, pl.BlockSpec(wmix.shape, lambda bi, i: (0, 0, 0)),
             pl.BlockSpec((1, POOL_WIDTH), lambda bi, i: (0, 0))]
    return _mix_ln(functools.partial(_mix_even_kernel, tm=tm, past=past), "mix_even_ln",
                   [u3, u3, hist16, wmix, scale.reshape(1, POOL_WIDTH)], specs,
                   fox3, w, x3, g, beta, POOL_WIDTH)


def _mix_odd_ln(z3, hist16, bg3, conv_w, att3, w, x3, g, beta):
    tm = min(512, z3.shape[1])
    cur, prev, hist = _halo_specs(tm, CONV_CH)
    specs = [cur, prev, hist, cur, pl.BlockSpec((CONV_K, CONV_CH), lambda bi, i: (0, 0))]
    return _mix_ln(functools.partial(_mix_odd_kernel, tm=tm), "mix_odd_ln",
                   [z3, z3, hist16, bg3, conv_w], specs, att3, w, x3, g, beta, CONV_CH)


def _head_rows(c_ref, h):
    _, _, n, heads, d = c_ref.shape
    return c_ref.reshape(n * heads, d)[pl.ds(h, n, stride=heads), :]


def _mem_kernel(x_ref, wq_ref, mk0_ref, mk1_ref, mv0_ref, mv1_ref, wo_ref, g_ref, beta_ref, o_ref,
                kb_ref, vb_ref):
    @pl.when(pl.program_id(1) == 0)
    def _():
        for h in range(MEM_HEADS):
            kb_ref[h] = jnp.concatenate([_head_rows(mk0_ref, h), _head_rows(mk1_ref, h)],
                                        axis=1).astype(BF16)
            vb_ref[h] = jnp.concatenate([_head_rows(mv0_ref, h), _head_rows(mv1_ref, h)],
                                        axis=1).astype(BF16)

    x = x_ref[0]
    q = _dot(x.astype(BF16), wq_ref[0])
    qb = (q * (MEM_DH ** -0.5)).astype(BF16)
    outs = []
    for h in range(MEM_HEADS):
        sl = slice(h * MEM_DH, (h + 1) * MEM_DH)
        s = _dot_nt(qb[:, sl], kb_ref[h])
        m = jnp.max(s, axis=1, keepdims=True)
        p = jnp.exp(s - m)
        l = jnp.sum(p, axis=1, keepdims=True)
        o = _dot(p.astype(BF16), vb_ref[h]) / l
        outs.append(o.astype(BF16))
    o_all = jnp.concatenate(outs, axis=1)
    y = DN_ALPHA * x + _dot(o_all, wo_ref[0])
    o_ref[0] = _layer_norm(y, g_ref[...], beta_ref[...])


def _mem_attend_ln(x3, wq, mk, mv, layer, wo, g, beta):
    b, t, _ = x3.shape
    tm = min(512, t)
    xs = pl.BlockSpec((1, tm, D_MODEL), lambda bi, i: (bi, i, 0))
    ws = pl.BlockSpec((1, D_MODEL, D_MODEL), lambda bi, i: (layer, 0, 0))
    half = lambda c: pl.BlockSpec((1, 1, N_MEM, MEM_HEADS, LANES), lambda bi, i: (layer, bi, 0, 0, c))
    vec = pl.BlockSpec((1, D_MODEL), lambda bi, i: (0, 0))
    assert MEM_DH == 2 * LANES
    return pl.pallas_call(
        _mem_kernel,
        grid=(b, t // tm),
        in_specs=[xs, ws, half(0), half(1), half(0), half(1), ws, vec, vec],
        out_specs=xs,
        out_shape=jax.ShapeDtypeStruct(x3.shape, F32),
        scratch_shapes=[pltpu.VMEM((MEM_HEADS, N_MEM, MEM_DH), BF16),
                        pltpu.VMEM((MEM_HEADS, N_MEM, MEM_DH), BF16)],
        compiler_params=_cparams("parallel", "arbitrary"),
        name="mem_attend_ln",
    )(x3, wq, mk, mk, mv, mv, wo, g.reshape(1, D_MODEL), beta.reshape(1, D_MODEL))


def _ffn_kernel(x_ref, w1_ref, w2_ref, g_ref, beta_ref, o_ref, *, chunk):
    x = x_ref[...]
    xb = x.astype(BF16)
    acc = jnp.zeros(x.shape, F32)
    for c in range(D_FF // chunk):
        h = _dot(xb, w1_ref[0, :, c * chunk:(c + 1) * chunk])
        h = jnp.square(jnp.maximum(h, 0.0))
        acc = acc + _dot(h.astype(BF16), w2_ref[0, c * chunk:(c + 1) * chunk, :])
    o_ref[...] = _layer_norm(DN_ALPHA * x + acc, g_ref[...], beta_ref[...])


def _ffn_ln(x2, w1, w2, layer, g, beta):
    rows = x2.shape[0]
    tm = _row_tile(rows)
    row = pl.BlockSpec((tm, D_MODEL), lambda i: (i, 0))
    vec = pl.BlockSpec((1, D_MODEL), lambda i: (0, 0))
    once = pl.Buffered(1)
    return pl.pallas_call(
        functools.partial(_ffn_kernel, chunk=1024),
        grid=(rows // tm,),
        in_specs=[row,
                  pl.BlockSpec((1,) + w1.shape[1:], lambda i: (layer, 0, 0), pipeline_mode=once),
                  pl.BlockSpec((1,) + w2.shape[1:], lambda i: (layer, 0, 0), pipeline_mode=once),
                  vec, vec],
        out_specs=row,
        out_shape=jax.ShapeDtypeStruct((rows, D_MODEL), F32),
        compiler_params=_cparams("parallel"),
        name="ffn_ln",
    )(x2, w1, w2, g.reshape(1, D_MODEL), beta.reshape(1, D_MODEL))


def _pad_rows(a, total):
    pad = total - a.shape[1]
    if pad == 0:
        return a
    return jnp.pad(a, ((0, 0), (0, pad)) + ((0, 0),) * (a.ndim - 2))


def _hist16(h):
    return jnp.pad(h, ((0, 0), (HIST_ROWS - h.shape[1], 0), (0, 0)))


def _trunk(x, mem_k, mem_v, pool_h, fk_h, fv_h, flf_h, dk_h, dv_h, conv_h, past, wts):
    b, t, _ = x.shape
    rows = b * t
    t_k = past + t
    if past == 0:
        t_kp = t_k
        cs_rows = 1024
    else:
        t_kp = -(-t_k // LANES) * LANES
        cs_rows = t_kp
    x2 = x.reshape(rows, D_MODEL)

    u, qb, k, kb, v, vb, lf = _proj_even(x2, wts["w_in_even"], wts["w_forget"], wts["b_forget"], t)
    lf3 = lf.reshape(b, t, LANES)
    qb3 = qb.reshape(b, t, FOX_WIDTH)
    kb3 = kb.reshape(b, t, FOX_WIDTH)
    vb3 = vb.reshape(b, t, FOX_WIDTH)
    if past:
        lf_hist = jnp.pad(flf_h[0], ((0, 0), (0, 0), (0, LANES - FOX_HEADS)))
        lf_all = jnp.concatenate([lf_hist, lf3], axis=1)
    else:
        lf_all = lf3
    fcum, kaug, qaug = _cumsum_time(_pad_rows(lf_all, t_kp), cs_rows)
    if past:
        ft = jnp.transpose(fcum[:, :, :FOX_HEADS], (0, 2, 1))
        fox_y = _fox_decode(qb3, kb3, vb3, jnp.transpose(fk_h[0], (0, 2, 3, 1)),
                            jnp.transpose(fv_h[0], (0, 2, 3, 1)), fcum, ft, past)
    else:
        fox_y = _fox_attention(qb3, kb3, vb3, qaug, kaug, ATT_TILE, ATT_TILE)
    u3 = u.reshape(b, t, POOL_WIDTH)
    x3 = _mix_even_ln(u3, _hist16(pool_h[0]), wts["w_pool_mix"], wts["pool_scale"], past, fox_y,
                      wts["w_out_even"], x2.reshape(b, t, D_MODEL), wts["ln_g"][0, 0], wts["ln_b"][0, 0])
    x2 = _mem_attend_ln(x3, wts["w_mem_q"], mem_k, mem_v, 0,
                        wts["w_mem_o"], wts["ln_g"][0, 1], wts["ln_b"][0, 1]).reshape(rows, D_MODEL)
    x2 = _ffn_ln(x2, wts["w_ff1"], wts["w_ff2"], 0, wts["ln_g"][0, 2], wts["ln_b"][0, 2])
    n_pool = u3[:, t - POOL_HIST:][None]
    n_fk = _cache_logical(k, b, t, (FOX_HEADS, FOX_DH))
    n_fv = _cache_logical(v, b, t, (FOX_HEADS, FOX_DH))
    n_flf = lf3[:, :, :FOX_HEADS][None]

    lam_init = 0.8 - 0.6 * math.exp(-0.3 * 1)
    qb, k, kb, v, vb, bg, z = _proj_odd(x2, wts["w_in_odd"], t)
    qb3 = qb.reshape(b, t, DIFF_QK)
    kb3 = kb.reshape(b, t, DIFF_QK)
    vb3 = vb.reshape(b, t, DIFF_WIDTH)
    if past:
        att = _diff_decode(qb3, kb3, vb3,
                           jnp.transpose(dk_h[0].reshape(b, past, 2 * DIFF_HEADS, DIFF_DH), (0, 2, 3, 1)),
                           dv_h, wts["rel_bias_table"],
                           wts["diff_lambda"], wts["diff_subln_g"], past, lam_init)
    else:
        att = _diff_attention(qb3, kb3, vb3, wts["rel_bias_table"], wts["diff_lambda"],
                              wts["diff_subln_g"], ATT_TILE, ATT_TILE, lam_init)
    z3 = z.reshape(b, t, CONV_CH)
    x3 = _mix_odd_ln(z3, _hist16(conv_h[0]), bg.reshape(b, t, CONV_CH), wts["conv_w"], att,
                     wts["w_out_odd"], x2.reshape(b, t, D_MODEL), wts["ln_g"][1, 0], wts["ln_b"][1, 0])
    x2 = _mem_attend_ln(x3, wts["w_mem_q"], mem_k, mem_v, 1,
                        wts["w_mem_o"], wts["ln_g"][1, 1], wts["ln_b"][1, 1]).reshape(rows, D_MODEL)
    x2 = _ffn_ln(x2, wts["w_ff1"], wts["w_ff2"], 1, wts["ln_g"][1, 2], wts["ln_b"][1, 2])
    n_dk = _cache_logical(k, b, t, (DIFF_HEADS, 2, DIFF_DH))
    n_dv = v.reshape(1, b, t, DIFF_HEADS, DIFF_VD)
    n_conv = z3[:, t - (CONV_K - 1):][None]
    return (x2.reshape(b, t, D_MODEL), n_pool, n_fk, n_fv, n_flf, n_dk, n_dv, n_conv)


def kernel(x_prompt, x_sample, state_pool, cache_fox_k, cache_fox_v, cache_fox_logf,
           cache_diff_k, cache_diff_v, state_conv, cache_mem_k, cache_mem_v, mem_prompt,
           w_in_even, b_forget, w_pool_mix, pool_scale, w_out_even,
           w_in_odd, diff_lambda, diff_subln_g, conv_w, w_out_odd, rel_bias_table,
           w_mem_q, w_mem_k, w_mem_v, w_mem_o, w_ff1, w_ff2, ln_g, ln_b):
    bp = x_prompt.shape[0]
    nmain = POOL_WIDTH + 3 * FOX_WIDTH
    wts = {
        "w_in_even": w_in_even[0, :, :nmain].astype(BF16),
        "w_forget": jnp.pad(jnp.tile(w_in_even[0, :, nmain:], (1, 6)),
                            ((0, 0), (0, LANES - 6 * FOX_HEADS))).astype(BF16),
        "b_forget": jnp.pad(jnp.tile(b_forget[0], 6), (0, LANES - 6 * FOX_HEADS)).reshape(1, LANES).astype(F32),
        "w_pool_mix": w_pool_mix[0].astype(BF16),
        "pool_scale": pool_scale[0],
        "w_out_even": w_out_even[0].astype(BF16),
        "w_in_odd": w_in_odd[0].astype(BF16),
        "diff_lambda": diff_lambda[0],
        "diff_subln_g": diff_subln_g[0],
        "conv_w": conv_w[0],
        "w_out_odd": w_out_odd[0].astype(BF16),
        "rel_bias_table": rel_bias_table,
        "w_mem_q": w_mem_q.astype(BF16),
        "w_mem_o": w_mem_o.astype(BF16),
        "w_ff1": w_ff1.astype(BF16),
        "w_ff2": w_ff2.astype(BF16),
        "ln_g": ln_g,
        "ln_b": ln_b,
    }
    kv = _mem_kv(mem_prompt.reshape(bp * N_MEM, D_MODEL),
                 jnp.stack([w_mem_k, w_mem_v]).astype(BF16))
    kv = kv.reshape(2, DEPTH, bp, N_MEM, D_MODEL)
    p_mem_k = kv[0].reshape(DEPTH, bp, N_MEM, MEM_HEADS, MEM_DH)
    p_mem_v = kv[1].reshape(DEPTH, bp, N_MEM, MEM_HEADS, MEM_DH)
    zeros = lambda *s: jnp.zeros(s, F32)
    (y_prompt, p_pool, p_fox_k, p_fox_v, p_fox_logf, p_diff_k, p_diff_v, p_conv) = _trunk(
        x_prompt, p_mem_k, p_mem_v,
        zeros(1, bp, POOL_HIST, POOL_WIDTH), None, None, None, None, None,
        zeros(1, bp, CONV_K - 1, CONV_CH), 0, wts)
    bs = x_sample.shape[0]
    (y_sample, s_pool, s_fox_k, s_fox_v, s_fox_logf, s_diff_k, s_diff_v, s_conv) = _trunk(
        x_sample, cache_mem_k, cache_mem_v,
        state_pool, cache_fox_k, cache_fox_v, cache_fox_logf,
        cache_diff_k, cache_diff_v, state_conv, cache_fox_k.shape[2], wts)
    return (y_prompt, y_sample,
            p_pool, p_fox_k, p_fox_v, p_fox_logf, p_diff_k, p_diff_v, p_conv, p_mem_k, p_mem_v,
            s_pool, s_fox_k, s_fox_v, s_fox_logf, s_diff_k, s_diff_v, s_conv)
```

```python
import functools
import math

import numpy as np
import jax
import jax.numpy as jnp
from jax import lax
from jax.experimental import pallas as pl
from jax.experimental.pallas import tpu as pltpu

F32 = jnp.float32
BF16 = jnp.bfloat16

D_MODEL = 1024
DEPTH = 2
CHUNK = 64
POOL_WIDTH = 512
POOL_GC = 128
POOL_WINDOWS = (2, 4, 8, 16)
POOL_HIST = 15
FOX_HEADS = 8
FOX_DH = 64
FOX_WIDTH = 512
DIFF_HEADS = 4
DIFF_DH = 64
DIFF_VD = 128
DIFF_QK = 512
DIFF_WIDTH = 512
CONV_CH = 512
CONV_K = 3
D_FF = 4096
N_MEM = 256
MEM_HEADS = 4
MEM_DH = 256
REL_BUCKETS = 32
REL_MAX_DIST = 128
DN_ALPHA = (2 * DEPTH) ** 0.25
LN_EPS = 1e-5
RMS_EPS = 1e-5
NEG = -1e30
LOG2E = math.log2(math.e)

LANES = 128
HIST_ROWS = 16
ATT_TILE = 512
ATT_UNROLL = 4
MEM_ROW_TILE = 1024
MEM_ROWS_MIN = 64
VMEM_LIMIT = 56 * 1024 * 1024


def _cparams(*sem):
    return pltpu.CompilerParams(dimension_semantics=sem, vmem_limit_bytes=VMEM_LIMIT)


def _dot(a, b):
    return jnp.dot(a, b, preferred_element_type=F32)


def _dot_nt(a, b):
    return lax.dot_general(a, b, (((1,), (1,)), ((), ())), preferred_element_type=F32)


def _layer_norm(y, g, b):
    mu = jnp.mean(y, axis=-1, keepdims=True)
    d = y - mu
    var = jnp.mean(d * d, axis=-1, keepdims=True)
    return d * lax.rsqrt(var + LN_EPS) * g + b


def _row_tile(rows):
    return min(512, rows)


def _store_cache(o_ref, y):
    if o_ref.shape[0] == 1:
        o_ref[0] = y.T
        return
    cols = []
    for h in range(o_ref.shape[1]):
        c = y[:, (h // 2) * LANES:(h // 2 + 1) * LANES]
        cols.append(pltpu.roll(c, LANES // 2, 1) if h % 2 else c)
    t = jnp.swapaxes(jnp.stack(cols, axis=0), 0, 1)
    o_ref[...] = t[:, :, :o_ref.shape[2]]


def _cache_out(rows, tm, seq):
    if seq % tm == 0 and tm % LANES == 0:
        nb = seq // tm
        return (pl.BlockSpec((1, 512, tm), lambda i: (i // nb, 0, i % nb)),
                jax.ShapeDtypeStruct((rows // seq, 512, seq), F32))
    return (pl.BlockSpec((tm, 8, 64), lambda i: (i, 0, 0)),
            jax.ShapeDtypeStruct((rows, 8, 64), F32))


def _cache_logical(c, b, t, tail):
    if c.shape[0] == b and c.shape[1] == 512:
        c = jnp.transpose(c.reshape((b,) + tail + (t,)), (0, len(tail) + 1) + tuple(range(1, len(tail) + 1)))
        return c[None]
    return c.reshape((1, b, t) + tail)


def _proj_even_kernel(x_ref, w_ref, wf_ref, bf_ref,
                      u_ref, q_ref, k_ref, kb_ref, v_ref, vb_ref, lf_ref):
    xb = x_ref[...].astype(BF16)

    def mm(c):
        return _dot(xb, w_ref[:, c * 512:(c + 1) * 512])

    u_ref[...] = mm(0)
    q_ref[...] = (mm(1) * (FOX_DH ** -0.5 * LOG2E)).astype(BF16)
    k = mm(2)
    _store_cache(k_ref, k)
    kb_ref[...] = k.astype(BF16)
    v = mm(3)
    _store_cache(v_ref, v)
    vb_ref[...] = v.astype(BF16)
    z = _dot(xb, wf_ref[...]) + bf_ref[...]
    lf_ref[...] = jnp.minimum(z, 0.0) - jnp.log1p(jnp.exp(-jnp.abs(z)))


def _proj_even(x2, w, wf, bf, seq):
    rows = x2.shape[0]
    tm = _row_tile(rows)
    row = lambda n: pl.BlockSpec((tm, n), lambda i: (i, 0))
    full = lambda a: pl.BlockSpec(a.shape, lambda i: (0,) * a.ndim)
    f32o = jax.ShapeDtypeStruct((rows, 512), F32)
    bf16o = jax.ShapeDtypeStruct((rows, 512), BF16)
    heads, headso = _cache_out(rows, tm, seq)
    return pl.pallas_call(
        _proj_even_kernel,
        grid=(rows // tm,),
        in_specs=[row(D_MODEL), full(w), full(wf), full(bf)],
        out_specs=[row(512), row(512), heads, row(512), heads, row(512), row(LANES)],
        out_shape=[f32o, bf16o, headso, bf16o, headso, bf16o,
                   jax.ShapeDtypeStruct((rows, LANES), F32)],
        compiler_params=_cparams("parallel"),
        name="proj_even",
    )(x2, w, wf, bf)


def _proj_odd_kernel(x_ref, w_ref, q_ref, k_ref, kb_ref, v_ref, vb_ref, bg_ref, z_ref):
    xb = x_ref[...].astype(BF16)

    def mm(c):
        return _dot(xb, w_ref[:, c * 512:(c + 1) * 512])

    q_ref[...] = (mm(0) * (DIFF_DH ** -0.5 * LOG2E)).astype(BF16)
    k = mm(1)
    _store_cache(k_ref, k)
    kb_ref[...] = k.astype(BF16)
    v = mm(2)
    tm = v.shape[0]
    v_rows = v_ref.reshape(tm * DIFF_HEADS, DIFF_VD)
    for h in range(DIFF_HEADS):
        v_rows[pl.ds(h, tm, stride=DIFF_HEADS), :] = v[:, h * DIFF_VD:(h + 1) * DIFF_VD]
    vb_ref[...] = v.astype(BF16)
    bg_ref[...] = mm(3)
    z_ref[...] = mm(4) * mm(5)


def _proj_odd(x2, w, seq):
    rows = x2.shape[0]
    tm = _row_tile(rows)
    kspec, kshape = _cache_out(rows, tm, seq)
    row = lambda n: pl.BlockSpec((tm, n), lambda i: (i, 0))
    f32o = jax.ShapeDtypeStruct((rows, 512), F32)
    bf16o = jax.ShapeDtypeStruct((rows, 512), BF16)
    return pl.pallas_call(
        _proj_odd_kernel,
        grid=(rows // tm,),
        in_specs=[row(D_MODEL), pl.BlockSpec(w.shape, lambda i: (0, 0))],
        out_specs=[row(512), kspec, row(512),
                   pl.BlockSpec((tm, DIFF_HEADS, DIFF_VD), lambda i: (i, 0, 0))] + [row(512)] * 3,
        out_shape=[bf16o, kshape, bf16o,
                   jax.ShapeDtypeStruct((rows, DIFF_HEADS, DIFF_VD), F32), bf16o, f32o, f32o],
        compiler_params=_cparams("parallel"),
        name="proj_odd",
    )(x2, w)


def _mem_kv_kernel(x_ref, w_ref, o_ref):
    o_ref[0, 0] = _dot(x_ref[...].astype(BF16), w_ref[0, 0])


def _mem_kv(mem2, w_kv):
    rows = mem2.shape[0]
    return pl.pallas_call(
        _mem_kv_kernel,
        grid=(2, DEPTH),
        in_specs=[pl.BlockSpec((rows, D_MODEL), lambda a, l: (0, 0)),
                  pl.BlockSpec((1, 1, D_MODEL, D_MODEL), lambda a, l: (a, l, 0, 0))],
        out_specs=pl.BlockSpec((1, 1, rows, D_MODEL), lambda a, l: (a, l, 0, 0)),
        out_shape=jax.ShapeDtypeStruct((2, DEPTH, rows, D_MODEL), F32),
        compiler_params=_cparams("parallel", "parallel"),
        name="mem_kv",
    )(mem2, w_kv)


def _cumsum_kernel(x_ref, f_ref, kaug_ref, qaug_ref, carry_ref, *, rows):
    @pl.when(pl.program_id(1) == 0)
    def _():
        carry_ref[...] = jnp.zeros_like(carry_ref)

    r = lax.broadcasted_iota(jnp.int32, (LANES, LANES), 0)
    c = lax.broadcasted_iota(jnp.int32, (LANES, LANES), 1)
    tri = (r >= c).astype(F32)
    grp = c // FOX_HEADS
    carry = carry_ref[...]
    for s in range(rows // LANES):
        sl = slice(s * LANES, (s + 1) * LANES)
        cs = jnp.dot(tri, x_ref[0, sl, :], preferred_element_type=F32,
                     precision=lax.Precision.HIGHEST) + carry
        carry = cs[LANES - 1:LANES, :]
        f = cs * LOG2E
        f_ref[0, sl, :] = f
        hi = f.astype(BF16).astype(F32)
        mid = (f - hi).astype(BF16).astype(F32)
        lo = (f - hi - mid).astype(BF16).astype(F32)
        piece = jnp.where(grp % 3 == 0, hi, jnp.where(grp % 3 == 1, mid, lo))
        kaug_ref[0, sl, :] = jnp.where(grp < 3, -piece, jnp.where(grp < 6, 1.0, 0.0)).astype(BF16)
        qaug_ref[0, sl, :] = jnp.where(grp < 3, 1.0, jnp.where(grp < 6, piece, 0.0)).astype(BF16)
    carry_ref[...] = carry


def _cumsum_time(x, rows):
    b, t, _ = x.shape
    spec = pl.BlockSpec((1, rows, LANES), lambda i, j: (i, j, 0))
    return pl.pallas_call(
        functools.partial(_cumsum_kernel, rows=rows),
        grid=(b, t // rows),
        in_specs=[spec],
        out_specs=[spec] * 3,
        out_shape=[jax.ShapeDtypeStruct(x.shape, F32), jax.ShapeDtypeStruct(x.shape, BF16),
                   jax.ShapeDtypeStruct(x.shape, BF16)],
        scratch_shapes=[pltpu.VMEM((1, LANES), F32)],
        compiler_params=_cparams("parallel", "arbitrary"),
        name="cumsum_time",
    )(x)


def _loop_unrolled(block, n, unroll):
    def body(i, carry):
        for u in range(unroll):
            block(unroll * i + u)
        return carry

    lax.fori_loop(0, n // unroll, body, 0)
    group = unroll // 2
    while group >= 1:
        start = n - n % (2 * group)

        @pl.when(n % (2 * group) >= group)
        def _(start=start, group=group):
            for u in range(group):
                block(start + u)

        group //= 2


def _online_softmax(s, m_old, shift=None):
    row_max = jnp.max(s, axis=1, keepdims=True)
    if shift is not None:
        row_max = row_max + shift
    m_new = jnp.maximum(m_old, row_max)
    m_sub = m_new if shift is None else m_new - shift
    p = jnp.exp2(s - jnp.concatenate([m_sub] * (s.shape[1] // LANES), axis=1))
    return p.astype(BF16), jnp.exp2(m_old - m_new), m_new


def _fox_kernel(q_ref, k_ref, v_ref, qaug_ref, kaug_ref, mask_ref, o_ref,
                qcat_ref, s_ref, p_ref, acc_ref, m_ref, al_ref, *, tq, tk):
    pair = pl.program_id(1)
    j_last = pl.program_id(2)
    lane = lax.broadcasted_iota(jnp.int32, (1, LANES), 1)
    upper = lane >= FOX_DH
    sels = (jnp.logical_not(upper), upper)
    q2 = q_ref[0]
    qa = qaug_ref[0]
    for hh in range(2):
        own = jnp.logical_and(lane % FOX_HEADS == 2 * pair + hh, lane < 6 * FOX_HEADS)
        qcat_ref[hh] = jnp.concatenate([jnp.where(sels[hh], q2, jnp.zeros_like(q2)),
                                        jnp.where(own, qa, jnp.zeros_like(qa))], axis=1)
    m_ref[...] = jnp.full(m_ref.shape, NEG, F32)
    acc_ref[...] = jnp.zeros(acc_ref.shape, F32)
    p_ref[1] = jnp.zeros(p_ref.shape[1:], p_ref.dtype)
    al_ref[1] = jnp.ones(al_ref.shape[1:], F32)

    def scores(hh, j):
        rows = pl.ds(pl.multiple_of(j * tk, tk), tk)
        kcat = jnp.concatenate([k_ref[0, rows, :], kaug_ref[0, rows, :]], axis=1)
        s_ref[hh] = _dot_nt(qcat_ref[hh], kcat)

    def probs(hh, masked):
        s = s_ref[hh]
        if masked:
            s = s + mask_ref[...]
        p_ref[hh], al_ref[hh], m_ref[hh] = _online_softmax(s, m_ref[hh])

    def pv(hh, j):
        vb = v_ref[0, pl.ds(pl.multiple_of(j * tk, tk), tk), :]
        va = jnp.where(sels[hh], vb, jnp.ones_like(vb))
        acc_ref[hh] = al_ref[hh] * acc_ref[hh] + _dot(p_ref[hh], va)

    scores(0, 0)

    def block(j):
        scores(1, j)
        probs(0, False)
        pv(1, jnp.maximum(j - 1, 0))
        scores(0, j + 1)
        probs(1, False)
        pv(0, j)

    _loop_unrolled(block, j_last, ATT_UNROLL)
    scores(1, j_last)
    probs(0, True)
    pv(1, jnp.maximum(j_last - 1, 0))
    probs(1, True)
    pv(0, j_last)
    pv(1, j_last)

    a0 = acc_ref[0]
    a1 = acc_ref[1]
    o0 = a0 / pltpu.roll(a0, FOX_DH, 1)
    o1 = a1 / pltpu.roll(a1, FOX_DH, 1)
    o_ref[0] = jnp.where(upper, o1, o0).astype(o_ref.dtype)


def _fox_attention(qb, kb, vb, qaug, kaug, tq, tk):
    b, t, _ = qb.shape
    assert t % tq == 0 and tq == tk
    ii = np.arange(tq)[:, None]
    jj = np.arange(tk)[None, :]
    mask = jnp.asarray(np.where(jj <= ii, 0.0, NEG), F32)
    return pl.pallas_call(
        functools.partial(_fox_kernel, tq=tq, tk=tk),
        grid=(b, FOX_HEADS // 2, t // tq),
        in_specs=[
            pl.BlockSpec((1, tq, LANES), lambda bi, p, qi: (bi, qi, p)),
            pl.BlockSpec((1, t, LANES), lambda bi, p, qi: (bi, 0, p)),
            pl.BlockSpec((1, t, LANES), lambda bi, p, qi: (bi, 0, p)),
            pl.BlockSpec((1, tq, LANES), lambda bi, p, qi: (bi, qi, 0)),
            pl.BlockSpec((1, t, LANES), lambda bi, p, qi: (bi, 0, 0)),
            pl.BlockSpec((tq, tk), lambda bi, p, qi: (0, 0)),
        ],
        out_specs=pl.BlockSpec((1, tq, LANES), lambda bi, p, qi: (bi, qi, p)),
        out_shape=jax.ShapeDtypeStruct((b, t, FOX_WIDTH), BF16),
        scratch_shapes=[pltpu.VMEM((2, tq, 2 * LANES), BF16),
                        pltpu.VMEM((2, tq, tk), F32),
                        pltpu.VMEM((2, tq, tk), BF16),
                        pltpu.VMEM((2, tq, LANES), F32),
                        pltpu.VMEM((2, tq, LANES), F32),
                        pltpu.VMEM((2, tq, LANES), F32)],
        compiler_params=_cparams("parallel", "parallel", "arbitrary"),
        name="fox_attention",
    )(qb, kb, vb, qaug, kaug, mask)


def _diff_kernel(cfar_ref, q_ref, k_ref, v_ref, tiles_ref, lam_ref, g_ref, o_ref,
                 qm_ref, s_ref, p_ref, acc_ref, m_ref, al_ref, *, tq, tk, lam_init):
    j_last = pl.program_id(2)
    lane = lax.broadcasted_iota(jnp.int32, (1, LANES), 1)
    upper = lane >= DIFF_DH
    q2 = q_ref[0]
    zero = jnp.zeros_like(q2)
    qm_ref[0] = jnp.where(upper, zero, q2)
    qm_ref[1] = jnp.where(upper, q2, zero)
    cfar = cfar_ref[pl.program_id(1)]
    m_ref[...] = jnp.full(m_ref.shape, NEG, F32)
    acc_ref[...] = jnp.zeros(acc_ref.shape, F32)
    p_ref[1] = jnp.zeros(p_ref.shape[1:], p_ref.dtype)
    al_ref[1] = jnp.ones(al_ref.shape[1:], F32)

    def scores(mm, j):
        start = pl.multiple_of(j * tk, tk)
        s_ref[mm] = _dot_nt(qm_ref[mm], k_ref[0, pl.ds(start, tk), :])

    def probs(mm, tile):
        if tile is None:
            out = _online_softmax(s_ref[mm], m_ref[mm], shift=cfar)
        else:
            out = _online_softmax(s_ref[mm] + tiles_ref[0, tile], m_ref[mm])
        p_ref[mm], al_ref[mm], m_ref[mm] = out

    def pv(mm, j):
        start = pl.multiple_of(j * tk, tk)
        vb = v_ref[0, pl.ds(start, tk), :]
        va = jnp.concatenate([vb, jnp.ones_like(vb)], axis=1)
        al = al_ref[mm]
        acc_ref[mm] = jnp.concatenate([al, al], axis=1) * acc_ref[mm] + _dot(p_ref[mm], va)

    def block(j, tile):
        scores(1, j)
        probs(0, tile)
        pv(1, jnp.maximum(j - 1, 0))
        scores(0, j + 1)
        probs(1, tile)
        pv(0, j)

    scores(0, 0)
    _loop_unrolled(lambda j: block(j, None), jnp.maximum(j_last - 1, 0), ATT_UNROLL)

    @pl.when(j_last >= 1)
    def _():
        block(j_last - 1, 1)

    scores(1, j_last)
    probs(0, 0)
    pv(1, jnp.maximum(j_last - 1, 0))
    probs(1, 0)
    pv(0, j_last)
    pv(1, j_last)

    lam = _diff_lambda(lam_ref[...], lam_init)
    a0 = acc_ref[0]
    a1 = acc_ref[1]
    o = a0[:, :LANES] / a0[:, LANES:] - lam * (a1[:, :LANES] / a1[:, LANES:])
    ms = jnp.mean(o * o, axis=1, keepdims=True)
    o = o * lax.rsqrt(ms + RMS_EPS) * g_ref[...] * (1.0 - lam_init)
    o_ref[0] = o.astype(o_ref.dtype)


def _t5_bucket(rel):
    nb = REL_BUCKETS // 2
    max_exact = nb // 2
    ret = jnp.where(rel > 0, nb, 0)
    n = jnp.abs(rel)
    nf = jnp.maximum(n, 1).astype(F32)
    large = max_exact + (jnp.log(nf / max_exact) / math.log(REL_MAX_DIST / max_exact)
                         * (nb - max_exact)).astype(jnp.int32)
    large = jnp.minimum(large, nb - 1)
    return ret + jnp.where(n < max_exact, n, large)


def _toeplitz_kernel(w_ref, mask_ref, o_ref, *, tq, width):
    n = w_ref.shape[2]
    for d in range(w_ref.shape[1]):
        w = jnp.broadcast_to(w_ref[0, d:d + 1, :], (tq, n))
        o_ref[0, d] = pltpu.roll(w, 0, 1, stride=1, stride_axis=0)[:, :width] + mask_ref[d]


def _t5_bias_tiles(rel_table, tq, width, offsets, masks):
    nt = len(offsets)
    n = -(-(tq + width) // LANES) * LANES
    m = jnp.arange(n)
    rel = jnp.where(m < width, m, m - n)
    w = jnp.stack([rel_table[_t5_bucket(rel + d)] for d in offsets])
    w = jnp.transpose(w, (2, 0, 1)).astype(F32) * LOG2E
    return pl.pallas_call(
        functools.partial(_toeplitz_kernel, tq=tq, width=width),
        grid=(DIFF_HEADS,),
        in_specs=[pl.BlockSpec((1, nt, n), lambda h: (h, 0, 0)),
                  pl.BlockSpec((nt, tq, width), lambda h: (0, 0, 0))],
        out_specs=pl.BlockSpec((1, nt, tq, width), lambda h: (h, 0, 0, 0)),
        out_shape=jax.ShapeDtypeStruct((DIFF_HEADS, nt, tq, width), F32),
        compiler_params=_cparams("parallel"),
        name="t5_bias_tiles",
    )(w, jnp.asarray(masks, F32))


def _diff_lambda(lp, lam_init):
    return (jnp.exp(jnp.sum(lp[0:1] * lp[1:2], keepdims=True))
            - jnp.exp(jnp.sum(lp[2:3] * lp[3:4], keepdims=True)) + lam_init)


def _diff_attention(qb, kb, vb, rel_table, diff_lambda, subln_g, tq, tk, lam_init):
    b, t, _ = qb.shape
    assert t % tq == 0 and tq == tk and tk >= REL_MAX_DIST and tk % CHUNK == 0
    ii = np.arange(tq)[:, None]
    jj = np.arange(tk)[None, :]
    masks = np.zeros((2, tq, tk), np.float32)
    masks[0] = np.where((jj // CHUNK) <= (ii // CHUNK), 0.0, NEG)
    tiles = _t5_bias_tiles(rel_table, tq, tk, (0, -tk), masks)
    cfar = rel_table[_t5_bucket(jnp.asarray(-2 * REL_MAX_DIST, jnp.int32))].astype(F32) * LOG2E
    return pl.pallas_call(
        functools.partial(_diff_kernel, tq=tq, tk=tk, lam_init=lam_init),
        grid=(b, DIFF_HEADS, t // tq),
        in_specs=[
            pl.BlockSpec(memory_space=pltpu.SMEM),
            pl.BlockSpec((1, tq, LANES), lambda bi, h, qi: (bi, qi, h)),
            pl.BlockSpec((1, t, LANES), lambda bi, h, qi: (bi, 0, h)),
            pl.BlockSpec((1, t, LANES), lambda bi, h, qi: (bi, 0, h)),
            pl.BlockSpec((1, 2, tq, tk), lambda bi, h, qi: (h, 0, 0, 0)),
            pl.BlockSpec((4, DIFF_DH), lambda bi, h, qi: (0, 0)),
            pl.BlockSpec((1, DIFF_VD), lambda bi, h, qi: (0, 0)),
        ],
        out_specs=pl.BlockSpec((1, tq, LANES), lambda bi, h, qi: (bi, qi, h)),
        out_shape=jax.ShapeDtypeStruct((b, t, DIFF_WIDTH), BF16),
        scratch_shapes=[pltpu.VMEM((2, tq, LANES), BF16),
                        pltpu.VMEM((2, tq, tk), F32),
                        pltpu.VMEM((2, tq, tk), BF16),
                        pltpu.VMEM((2, tq, 2 * LANES), F32),
                        pltpu.VMEM((2, tq, LANES), F32),
                        pltpu.VMEM((2, tq, LANES), F32)],
        compiler_params=_cparams("parallel", "parallel", "arbitrary"),
        name="diff_attention",
    )(cfar, qb, kb, vb, tiles, diff_lambda, subln_g.reshape(1, DIFF_VD))


def _two_part_attend(q_h, kt_c, k_n, v_c, v_n, bias_c, bias_n, v_time_minor):
    s_c = _dot(q_h, kt_c.astype(BF16)) + bias_c
    s_n = _dot_nt(q_h, k_n) + bias_n
    m = jnp.maximum(jnp.max(s_c, axis=1, keepdims=True), jnp.max(s_n, axis=1, keepdims=True))
    p_c = jnp.exp2(s_c - m).astype(BF16)
    p_n = jnp.exp2(s_n - m)
    l = jnp.sum(p_c.astype(F32), axis=1, keepdims=True) + jnp.sum(p_n, axis=1, keepdims=True)
    pv_c = _dot_nt(p_c, v_c.astype(BF16)) if v_time_minor else _dot(p_c, v_c.astype(BF16))
    return (pv_c + _dot(p_n.astype(BF16), v_n)) / l


def _fox_decode_kernel(q_ref, kn_ref, vn_ref, kc_ref, vc_ref, fq_ref, ft_ref, o_ref, *, t, past):
    q = q_ref[0]
    kn = kn_ref[0]
    vn = vn_ref[0]
    fblk = fq_ref[0]
    r = lax.broadcasted_iota(jnp.int32, (t, t), 0)
    c = lax.broadcasted_iota(jnp.int32, (t, t), 1)
    causal = jnp.where(c <= r, 0.0, NEG)
    outs = []
    for h in range(FOX_HEADS):
        sl = slice(h * FOX_DH, (h + 1) * FOX_DH)
        fq = fblk[:, h:h + 1]
        fk = ft_ref[0, h:h + 1, :]
        outs.append(_two_part_attend(q[:, sl], kc_ref[0, h], kn[:, sl], vc_ref[0, h], vn[:, sl],
                                     fq - fk[:, :past], fq - fk[:, past:past + t] + causal, True))
    o_ref[0] = jnp.concatenate(outs, axis=1).astype(o_ref.dtype)


def _decode_specs(t, past):
    new = pl.BlockSpec((1, t, 512), lambda bi: (bi, 0, 0))
    cache = pl.BlockSpec((1, 8, 64, past), lambda bi: (bi, 0, 0, 0))
    return new, cache


def _fox_decode(qb, kn, vn, kc, vc, fcum, ft, past):
    b, t, _ = qb.shape
    t_kp = fcum.shape[1]
    assert past % t == 0
    new, cache = _decode_specs(t, past)
    return pl.pallas_call(
        functools.partial(_fox_decode_kernel, t=t, past=past),
        grid=(b,),
        in_specs=[new, new, new, cache, cache,
                  pl.BlockSpec((1, t, LANES), lambda bi: (bi, past // t, 0)),
                  pl.BlockSpec((1, FOX_HEADS, t_kp), lambda bi: (bi, 0, 0))],
        out_specs=new,
        out_shape=jax.ShapeDtypeStruct((b, t, FOX_WIDTH), BF16),
        compiler_params=_cparams("parallel"),
        name="fox_decode",
    )(qb, kn, vn, kc, vc, fcum, ft)


def _diff_decode_kernel(q_ref, kn_ref, vn_ref, kc_ref, vc_ref, bias_ref, lam_ref, g_ref, o_ref,
                        *, t, past, lam_init):
    q = q_ref[0]
    kn = kn_ref[0]
    vn = vn_ref[0]
    lam = _diff_lambda(lam_ref[...], lam_init)
    outs = []
    for h in range(DIFF_HEADS):
        v_c = _head_rows(vc_ref, h)
        v_n = vn[:, h * DIFF_VD:(h + 1) * DIFF_VD]
        bias = bias_ref[h, 0]
        maps = []
        for mm in range(2):
            j = 2 * h + mm
            sl = slice(j * DIFF_DH, (j + 1) * DIFF_DH)
            maps.append(_two_part_attend(q[:, sl], kc_ref[0, j], kn[:, sl], v_c, v_n,
                                         bias[:, :past], bias[:, past:past + t], False))
        a = maps[0] - lam * maps[1]
        ms = jnp.mean(a * a, axis=1, keepdims=True)
        outs.append(a * lax.rsqrt(ms + RMS_EPS) * g_ref[...] * (1.0 - lam_init))
    o_ref[0] = jnp.concatenate(outs, axis=1).astype(o_ref.dtype)


def _diff_decode(qb, kn, vn, kc, vc, rel_table, diff_lambda, subln_g, past, lam_init):
    b, t, _ = qb.shape
    t_kp = -(-(past + t) // LANES) * LANES
    q_pos = past + np.arange(t)[:, None]
    k_pos = np.arange(t_kp)[None, :]
    visible = ((k_pos // CHUNK) <= (q_pos // CHUNK)) & (k_pos < past + t)
    tiles = _t5_bias_tiles(rel_table, t, t_kp, (-past,), np.where(visible, 0.0, NEG)[None])
    new, cache = _decode_specs(t, past)
    vcache = pl.BlockSpec((1, 1, past, DIFF_HEADS, DIFF_VD), lambda bi: (0, bi, 0, 0, 0))
    return pl.pallas_call(
        functools.partial(_diff_decode_kernel, t=t, past=past, lam_init=lam_init),
        grid=(b,),
        in_specs=[new, new, new, cache, vcache,
                  pl.BlockSpec(tiles.shape, lambda bi: (0, 0, 0, 0)),
                  pl.BlockSpec((4, DIFF_DH), lambda bi: (0, 0)),
                  pl.BlockSpec((1, DIFF_VD), lambda bi: (0, 0))],
        out_specs=new,
        out_shape=jax.ShapeDtypeStruct((b, t, DIFF_WIDTH), BF16),
        compiler_params=_cparams("parallel"),
        name="diff_decode",
    )(qb, kn, vn, kc, vc, tiles, diff_lambda, subln_g.reshape(1, DIFF_VD))


def _fill_history(ext_ref, cur, hist_ref, prev_ref):
    ext_ref[:, 0:HIST_ROWS, :] = jnp.where(pl.program_id(1) == 0, hist_ref[...], prev_ref[...])
    ext_ref[:, HIST_ROWS:, :] = cur


def _halo_specs(bb, tm, width):
    cur = pl.BlockSpec((bb, tm, width), lambda b, i: (b, i, 0))
    prev = pl.BlockSpec((bb, HIST_ROWS, width),
                        lambda b, i: (b, jnp.maximum(i * (tm // HIST_ROWS) - 1, 0), 0))
    hist = pl.BlockSpec((bb, HIST_ROWS, width), lambda b, i: (b, 0, 0))
    return cur, prev, hist


def _mix_out(o_ref, x_ref, mix, g_ref, beta_ref):
    bb, tm, d = x_ref.shape
    y = DN_ALPHA * x_ref[...].reshape(bb * tm, d) + mix
    o_ref[...] = _layer_norm(y, g_ref[...], beta_ref[...]).reshape(bb, tm, d)


def _mix_even_kernel(u_ref, prev_ref, hist_ref, wmix_ref, scale_ref, att_ref, w_ref, x_ref,
                     g_ref, beta_ref, o_ref, ext_ref, *, past):
    bb, tm, _ = u_ref.shape
    u = u_ref[...]
    _fill_history(ext_ref, u, hist_ref, prev_ref)
    pos = past + pl.program_id(1) * tm + lax.broadcasted_iota(jnp.int32, (1, tm, 1), 1)
    groups = []
    for g, w in enumerate(POOL_WINDOWS):
        sl = slice(g * POOL_GC, (g + 1) * POOL_GC)
        ug = u[:, :, sl]
        wsum = ug
        for s in range(1, w):
            wsum = wsum + ext_ref[:, HIST_ROWS - s:HIST_ROWS - s + tm, sl]
        cnt = jnp.minimum(w, pos + 1).astype(F32)
        d = (wsum / cnt - ug).reshape(bb * tm, POOL_GC)
        groups.append((_dot(d.astype(BF16), wmix_ref[g]) * scale_ref[:, sl]).astype(BF16))
    pool_y = jnp.concatenate(groups, axis=1)
    att = att_ref[...].reshape(bb * tm, FOX_WIDTH)
    mix = _dot(pool_y, w_ref[0:POOL_WIDTH, :]) + _dot(att, w_ref[POOL_WIDTH:, :])
    _mix_out(o_ref, x_ref, mix, g_ref, beta_ref)


def _mix_odd_kernel(z_ref, prev_ref, hist_ref, bg_ref, cw_ref, att_ref, w_ref, x_ref,
                    g_ref, beta_ref, o_ref, ext_ref):
    bb, tm, _ = z_ref.shape
    z = z_ref[...]
    _fill_history(ext_ref, z, hist_ref, prev_ref)
    y = (ext_ref[:, HIST_ROWS - 2:HIST_ROWS - 2 + tm, :] * cw_ref[0:1, :]
         + ext_ref[:, HIST_ROWS - 1:HIST_ROWS - 1 + tm, :] * cw_ref[1:2, :]
         + z * cw_ref[2:3, :])
    conv_y = (bg_ref[...] * y).astype(BF16).reshape(bb * tm, CONV_CH)
    att = att_ref[...].reshape(bb * tm, DIFF_WIDTH)
    mix = _dot(att, w_ref[0:DIFF_WIDTH, :]) + _dot(conv_y, w_ref[DIFF_WIDTH:, :])
    _mix_out(o_ref, x_ref, mix, g_ref, beta_ref)


def _mix_tiles(b, t):
    tm = min(512, t)
    return max(1, min(b, 512 // tm)), tm


def _mix_ln(kernel_fn, name, local_ins, local_specs, att3, w, x3, g, beta, width):
    b, t, _ = x3.shape
    bb, tm = _mix_tiles(b, t)
    cur = lambda n: pl.BlockSpec((bb, tm, n), lambda bi, i: (bi, i, 0))
    vec = pl.BlockSpec((1, D_MODEL), lambda bi, i: (0, 0))
    return pl.pallas_call(
        kernel_fn,
        grid=(b // bb, t // tm),
        in_specs=local_specs + [cur(512), pl.BlockSpec(w.shape, lambda bi, i: (0, 0)),
                                cur(D_MODEL), vec, vec],
        out_specs=cur(D_MODEL),
        out_shape=jax.ShapeDtypeStruct(x3.shape, F32),
        scratch_shapes=[pltpu.VMEM((bb, HIST_ROWS + tm, width), F32)],
        compiler_params=_cparams("parallel", "parallel"),
        name=name,
    )(*local_ins, att3, w, x3, g.reshape(1, D_MODEL), beta.reshape(1, D_MODEL))


def _mix_even_ln(u3, hist16, wmix, scale, past, fox3, w, x3, g, beta):
    cur, prev, hist = _halo_specs(*_mix_tiles(*u3.shape[:2]), POOL_WIDTH)
    specs = [cur, prev, hist, pl.BlockSpec(wmix.shape, lambda bi, i: (0, 0, 0)),
             pl.BlockSpec((1, POOL_WIDTH), lambda bi, i: (0, 0))]
    return _mix_ln(functools.partial(_mix_even_kernel, past=past), "mix_even_ln",
                   [u3, u3, hist16, wmix, scale.reshape(1, POOL_WIDTH)], specs,
                   fox3, w, x3, g, beta, POOL_WIDTH)


def _mix_odd_ln(z3, hist16, bg3, conv_w, att3, w, x3, g, beta):
    cur, prev, hist = _halo_specs(*_mix_tiles(*z3.shape[:2]), CONV_CH)
    specs = [cur, prev, hist, cur, pl.BlockSpec((CONV_K, CONV_CH), lambda bi, i: (0, 0))]
    return _mix_ln(_mix_odd_kernel, "mix_odd_ln",
                   [z3, z3, hist16, bg3, conv_w], specs, att3, w, x3, g, beta, CONV_CH)


def _head_rows(c_ref, h, bi=0):
    _, bb, n, heads, d = c_ref.shape
    return c_ref.reshape(bb * n * heads, d)[pl.ds(bi * n * heads + h, n, stride=heads), :]


def _mem_kernel(x_ref, wq_ref, mk0_ref, mk1_ref, mv0_ref, mv1_ref, wo_ref, g_ref, beta_ref, o_ref,
                kb_ref, vb_ref):
    bb, tm, _ = x_ref.shape

    @pl.when(pl.program_id(1) == 0)
    def _():
        for bi in range(bb):
            for h in range(MEM_HEADS):
                kb_ref[bi, h] = jnp.concatenate(
                    [_head_rows(mk0_ref, h, bi), _head_rows(mk1_ref, h, bi)], axis=1).astype(BF16)
                vb_ref[bi, h] = jnp.concatenate(
                    [_head_rows(mv0_ref, h, bi), _head_rows(mv1_ref, h, bi)], axis=1).astype(BF16)

    x = x_ref[...].reshape(bb * tm, D_MODEL)
    q = _dot(x.astype(BF16), wq_ref[0])
    qb = (q * (MEM_DH ** -0.5)).astype(BF16)
    rows = []
    for bi in range(bb):
        outs = []
        for h in range(MEM_HEADS):
            s = _dot_nt(qb[bi * tm:(bi + 1) * tm, h * MEM_DH:(h + 1) * MEM_DH], kb_ref[bi, h])
            m = jnp.max(s, axis=1, keepdims=True)
            p = jnp.exp(s - m)
            l = jnp.sum(p, axis=1, keepdims=True)
            o = _dot(p.astype(BF16), vb_ref[bi, h]) / l
            outs.append(o.astype(BF16))
        rows.append(jnp.concatenate(outs, axis=1))
    o_all = rows[0] if bb == 1 else jnp.concatenate(rows, axis=0)
    y = DN_ALPHA * x + _dot(o_all, wo_ref[0])
    o_ref[...] = _layer_norm(y, g_ref[...], beta_ref[...]).reshape(bb, tm, D_MODEL)


def _mem_attend_ln(x3, wq, mk, mv, layer, wo, g, beta):
    b, t, _ = x3.shape
    tm = min(MEM_ROW_TILE, t)
    bb = max(1, min(b, MEM_ROWS_MIN // tm))
    assert b % bb == 0 and MEM_DH == 2 * LANES
    xs = pl.BlockSpec((bb, tm, D_MODEL), lambda bi, i: (bi, i, 0))
    ws = pl.BlockSpec((1, D_MODEL, D_MODEL), lambda bi, i: (layer, 0, 0))
    half = lambda c: pl.BlockSpec((1, bb, N_MEM, MEM_HEADS, LANES), lambda bi, i: (layer, bi, 0, 0, c))
    vec = pl.BlockSpec((1, D_MODEL), lambda bi, i: (0, 0))
    return pl.pallas_call(
        _mem_kernel,
        grid=(b // bb, t // tm),
        in_specs=[xs, ws, half(0), half(1), half(0), half(1), ws, vec, vec],
        out_specs=xs,
        out_shape=jax.ShapeDtypeStruct(x3.shape, F32),
        scratch_shapes=[pltpu.VMEM((bb, MEM_HEADS, N_MEM, MEM_DH), BF16),
                        pltpu.VMEM((bb, MEM_HEADS, N_MEM, MEM_DH), BF16)],
        compiler_params=_cparams("parallel", "arbitrary"),
        name="mem_attend_ln",
    )(x3, wq, mk, mk, mv, mv, wo, g.reshape(1, D_MODEL), beta.reshape(1, D_MODEL))


def _ffn_kernel(x_ref, w1_ref, w2_ref, g_ref, beta_ref, o_ref, *, chunk):
    x = x_ref[...]
    xb = x.astype(BF16)
    acc = jnp.zeros(x.shape, F32)
    for c in range(D_FF // chunk):
        h = _dot(xb, w1_ref[0, :, c * chunk:(c + 1) * chunk])
        h = jnp.square(jnp.maximum(h, 0.0))
        acc = acc + _dot(h.astype(BF16), w2_ref[0, c * chunk:(c + 1) * chunk, :])
    o_ref[...] = _layer_norm(DN_ALPHA * x + acc, g_ref[...], beta_ref[...])


def _ffn_ln(x2, w1, w2, layer, g, beta):
    rows = x2.shape[0]
    tm = _row_tile(rows)
    row = pl.BlockSpec((tm, D_MODEL), lambda i: (i, 0))
    vec = pl.BlockSpec((1, D_MODEL), lambda i: (0, 0))
    once = pl.Buffered(1)
    return pl.pallas_call(
        functools.partial(_ffn_kernel, chunk=1024),
        grid=(rows // tm,),
        in_specs=[row,
                  pl.BlockSpec((1,) + w1.shape[1:], lambda i: (layer, 0, 0), pipeline_mode=once),
                  pl.BlockSpec((1,) + w2.shape[1:], lambda i: (layer, 0, 0), pipeline_mode=once),
                  vec, vec],
        out_specs=row,
        out_shape=jax.ShapeDtypeStruct((rows, D_MODEL), F32),
        compiler_params=_cparams("parallel"),
        name="ffn_ln",
    )(x2, w1, w2, g.reshape(1, D_MODEL), beta.reshape(1, D_MODEL))


def _pad_rows(a, total):
    pad = total - a.shape[1]
    if pad == 0:
        return a
    return jnp.pad(a, ((0, 0), (0, pad)) + ((0, 0),) * (a.ndim - 2))


def _hist16(h):
    return jnp.pad(h, ((0, 0), (HIST_ROWS - h.shape[1], 0), (0, 0)))


def _trunk(x, mem_k, mem_v, pool_h, fk_h, fv_h, flf_h, dk_h, dv_h, conv_h, past, wts):
    b, t, _ = x.shape
    rows = b * t
    t_k = past + t
    if past == 0:
        t_kp = t_k
        cs_rows = 1024
    else:
        t_kp = -(-t_k // LANES) * LANES
        cs_rows = t_kp
    x2 = x.reshape(rows, D_MODEL)

    u, qb, k, kb, v, vb, lf = _proj_even(x2, wts["w_in_even"], wts["w_forget"], wts["b_forget"], t)
    lf3 = lf.reshape(b, t, LANES)
    qb3 = qb.reshape(b, t, FOX_WIDTH)
    kb3 = kb.reshape(b, t, FOX_WIDTH)
    vb3 = vb.reshape(b, t, FOX_WIDTH)
    if past:
        lf_hist = jnp.pad(flf_h[0], ((0, 0), (0, 0), (0, LANES - FOX_HEADS)))
        lf_all = jnp.concatenate([lf_hist, lf3], axis=1)
    else:
        lf_all = lf3
    fcum, kaug, qaug = _cumsum_time(_pad_rows(lf_all, t_kp), cs_rows)
    if past:
        ft = jnp.transpose(fcum[:, :, :FOX_HEADS], (0, 2, 1))
        fox_y = _fox_decode(qb3, kb3, vb3, jnp.transpose(fk_h[0], (0, 2, 3, 1)),
                            jnp.transpose(fv_h[0], (0, 2, 3, 1)), fcum, ft, past)
    else:
        fox_y = _fox_attention(qb3, kb3, vb3, qaug, kaug, ATT_TILE, ATT_TILE)
    u3 = u.reshape(b, t, POOL_WIDTH)
    x3 = _mix_even_ln(u3, _hist16(pool_h[0]), wts["w_pool_mix"], wts["pool_scale"], past, fox_y,
                      wts["w_out_even"], x2.reshape(b, t, D_MODEL), wts["ln_g"][0, 0], wts["ln_b"][0, 0])
    x2 = _mem_attend_ln(x3, wts["w_mem_q"], mem_k, mem_v, 0,
                        wts["w_mem_o"], wts["ln_g"][0, 1], wts["ln_b"][0, 1]).reshape(rows, D_MODEL)
    x2 = _ffn_ln(x2, wts["w_ff1"], wts["w_ff2"], 0, wts["ln_g"][0, 2], wts["ln_b"][0, 2])
    n_pool = u3[:, t - POOL_HIST:][None]
    n_fk = _cache_logical(k, b, t, (FOX_HEADS, FOX_DH))
    n_fv = _cache_logical(v, b, t, (FOX_HEADS, FOX_DH))
    n_flf = lf3[:, :, :FOX_HEADS][None]

    lam_init = 0.8 - 0.6 * math.exp(-0.3 * 1)
    qb, k, kb, v, vb, bg, z = _proj_odd(x2, wts["w_in_odd"], t)
    qb3 = qb.reshape(b, t, DIFF_QK)
    kb3 = kb.reshape(b, t, DIFF_QK)
    vb3 = vb.reshape(b, t, DIFF_WIDTH)
    if past:
        att = _diff_decode(qb3, kb3, vb3,
                           jnp.transpose(dk_h[0].reshape(b, past, 2 * DIFF_HEADS, DIFF_DH), (0, 2, 3, 1)),
                           dv_h, wts["rel_bias_table"],
                           wts["diff_lambda"], wts["diff_subln_g"], past, lam_init)
    else:
        att = _diff_attention(qb3, kb3, vb3, wts["rel_bias_table"], wts["diff_lambda"],
                              wts["diff_subln_g"], ATT_TILE, ATT_TILE, lam_init)
    z3 = z.reshape(b, t, CONV_CH)
    x3 = _mix_odd_ln(z3, _hist16(conv_h[0]), bg.reshape(b, t, CONV_CH), wts["conv_w"], att,
                     wts["w_out_odd"], x2.reshape(b, t, D_MODEL), wts["ln_g"][1, 0], wts["ln_b"][1, 0])
    x2 = _mem_attend_ln(x3, wts["w_mem_q"], mem_k, mem_v, 1,
                        wts["w_mem_o"], wts["ln_g"][1, 1], wts["ln_b"][1, 1]).reshape(rows, D_MODEL)
    x2 = _ffn_ln(x2, wts["w_ff1"], wts["w_ff2"], 1, wts["ln_g"][1, 2], wts["ln_b"][1, 2])
    n_dk = _cache_logical(k, b, t, (DIFF_HEADS, 2, DIFF_DH))
    n_dv = v.reshape(1, b, t, DIFF_HEADS, DIFF_VD)
    n_conv = z3[:, t - (CONV_K - 1):][None]
    return (x2.reshape(b, t, D_MODEL), n_pool, n_fk, n_fv, n_flf, n_dk, n_dv, n_conv)


def kernel(x_prompt, x_sample, state_pool, cache_fox_k, cache_fox_v, cache_fox_logf,
           cache_diff_k, cache_diff_v, state_conv, cache_mem_k, cache_mem_v, mem_prompt,
           w_in_even, b_forget, w_pool_mix, pool_scale, w_out_even,
           w_in_odd, diff_lambda, diff_subln_g, conv_w, w_out_odd, rel_bias_table,
           w_mem_q, w_mem_k, w_mem_v, w_mem_o, w_ff1, w_ff2, ln_g, ln_b):
    bp = x_prompt.shape[0]
    nmain = POOL_WIDTH + 3 * FOX_WIDTH
    wts = {
        "w_in_even": w_in_even[0, :, :nmain].astype(BF16),
        "w_forget": jnp.pad(jnp.tile(w_in_even[0, :, nmain:], (1, 6)),
                            ((0, 0), (0, LANES - 6 * FOX_HEADS))).astype(BF16),
        "b_forget": jnp.pad(jnp.tile(b_forget[0], 6), (0, LANES - 6 * FOX_HEADS)).reshape(1, LANES).astype(F32),
        "w_pool_mix": w_pool_mix[0].astype(BF16),
        "pool_scale": pool_scale[0],
        "w_out_even": w_out_even[0].astype(BF16),
        "w_in_odd": w_in_odd[0].astype(BF16),
        "diff_lambda": diff_lambda[0],
        "diff_subln_g": diff_subln_g[0],
        "conv_w": conv_w[0],
        "w_out_odd": w_out_odd[0].astype(BF16),
        "rel_bias_table": rel_bias_table,
        "w_mem_q": w_mem_q.astype(BF16),
        "w_mem_o": w_mem_o.astype(BF16),
        "w_ff1": w_ff1.astype(BF16),
        "w_ff2": w_ff2.astype(BF16),
        "ln_g": ln_g,
        "ln_b": ln_b,
    }
    kv = _mem_kv(mem_prompt.reshape(bp * N_MEM, D_MODEL),
                 jnp.stack([w_mem_k, w_mem_v]).astype(BF16))
    kv = kv.reshape(2, DEPTH, bp, N_MEM, D_MODEL)
    p_mem_k = kv[0].reshape(DEPTH, bp, N_MEM, MEM_HEADS, MEM_DH)
    p_mem_v = kv[1].reshape(DEPTH, bp, N_MEM, MEM_HEADS, MEM_DH)
    zeros = lambda *s: jnp.zeros(s, F32)
    (y_prompt, p_pool, p_fox_k, p_fox_v, p_fox_logf, p_diff_k, p_diff_v, p_conv) = _trunk(
        x_prompt, p_mem_k, p_mem_v,
        zeros(1, bp, POOL_HIST, POOL_WIDTH), None, None, None, None, None,
        zeros(1, bp, CONV_K - 1, CONV_CH), 0, wts)
    bs = x_sample.shape[0]
    (y_sample, s_pool, s_fox_k, s_fox_v, s_fox_logf, s_diff_k, s_diff_v, s_conv) = _trunk(
        x_sample, cache_mem_k, cache_mem_v,
        state_pool, cache_fox_k, cache_fox_v, cache_fox_logf,
        cache_diff_k, cache_diff_v, state_conv, cache_fox_k.shape[2], wts)
    return (y_prompt, y_sample,
            p_pool, p_fox_k, p_fox_v, p_fox_logf, p_diff_k, p_diff_v, p_conv, p_mem_k, p_mem_v,
            s_pool, s_fox_k, s_fox_v, s_fox_logf, s_diff_k, s_diff_v, s_conv)
```

```python
import functools
import math

import numpy as np
import jax
import jax.numpy as jnp
from jax import lax
from jax.experimental import pallas as pl
from jax.experimental.pallas import tpu as pltpu

F32 = jnp.float32
BF16 = jnp.bfloat16

D_MODEL = 1024
DEPTH = 2
CHUNK = 64
POOL_WIDTH = 512
POOL_GC = 128
POOL_WINDOWS = (2, 4, 8, 16)
POOL_HIST = 15
FOX_HEADS = 8
FOX_DH = 64
FOX_WIDTH = 512
DIFF_HEADS = 4
DIFF_DH = 64
DIFF_VD = 128
DIFF_QK = 512
DIFF_WIDTH = 512
CONV_CH = 512
CONV_K = 3
D_FF = 4096
N_MEM = 256
MEM_HEADS = 4
MEM_DH = 256
REL_BUCKETS = 32
REL_MAX_DIST = 128
DN_ALPHA = (2 * DEPTH) ** 0.25
LN_EPS = 1e-5
RMS_EPS = 1e-5
NEG = -1e30
LOG2E = math.log2(math.e)

LANES = 128
HIST_ROWS = 16
ATT_TILE = 512
ATT_UNROLL = 4
MEM_ROW_TILE = 1024
MEM_ROWS_MIN = 64
VMEM_LIMIT = 56 * 1024 * 1024


def _cparams(*sem):
    return pltpu.CompilerParams(dimension_semantics=sem, vmem_limit_bytes=VMEM_LIMIT)


def _dot(a, b):
    return jnp.dot(a, b, preferred_element_type=F32)


def _dot_nt(a, b):
    return lax.dot_general(a, b, (((1,), (1,)), ((), ())), preferred_element_type=F32)


def _layer_norm(y, g, b):
    mu = jnp.mean(y, axis=-1, keepdims=True)
    d = y - mu
    var = jnp.mean(d * d, axis=-1, keepdims=True)
    return d * lax.rsqrt(var + LN_EPS) * g + b


def _row_tile(rows):
    return min(512, rows)


def _store_cache(o_ref, y):
    if o_ref.shape[0] == 1:
        o_ref[0] = y.T
        return
    cols = []
    for h in range(o_ref.shape[1]):
        c = y[:, (h // 2) * LANES:(h // 2 + 1) * LANES]
        cols.append(pltpu.roll(c, LANES // 2, 1) if h % 2 else c)
    t = jnp.swapaxes(jnp.stack(cols, axis=0), 0, 1)
    o_ref[...] = t[:, :, :o_ref.shape[2]]


def _cache_out(rows, tm, seq):
    if seq % tm == 0 and tm % LANES == 0:
        nb = seq // tm
        return (pl.BlockSpec((1, 512, tm), lambda i: (i // nb, 0, i % nb)),
                jax.ShapeDtypeStruct((rows // seq, 512, seq), F32))
    return (pl.BlockSpec((tm, 8, 64), lambda i: (i, 0, 0)),
            jax.ShapeDtypeStruct((rows, 8, 64), F32))


def _cache_logical(c, b, t, tail):
    if c.shape[0] == b and c.shape[1] == 512:
        c = jnp.transpose(c.reshape((b,) + tail + (t,)), (0, len(tail) + 1) + tuple(range(1, len(tail) + 1)))
        return c[None]
    return c.reshape((1, b, t) + tail)


def _proj_even_kernel(x_ref, w_ref, wf_ref, bf_ref,
                      u_ref, q_ref, k_ref, kb_ref, v_ref, vb_ref, lf_ref):
    xb = x_ref[...].astype(BF16)

    def mm(c):
        return _dot(xb, w_ref[:, c * 512:(c + 1) * 512])

    u_ref[...] = mm(0)
    q_ref[...] = (mm(1) * (FOX_DH ** -0.5 * LOG2E)).astype(BF16)
    k = mm(2)
    _store_cache(k_ref, k)
    kb_ref[...] = k.astype(BF16)
    v = mm(3)
    _store_cache(v_ref, v)
    vb_ref[...] = v.astype(BF16)
    z = _dot(xb, wf_ref[...]) + bf_ref[...]
    lf_ref[...] = jnp.minimum(z, 0.0) - jnp.log1p(jnp.exp(-jnp.abs(z)))


def _proj_even(x2, w, wf, bf, seq):
    rows = x2.shape[0]
    tm = _row_tile(rows)
    row = lambda n: pl.BlockSpec((tm, n), lambda i: (i, 0))
    full = lambda a: pl.BlockSpec(a.shape, lambda i: (0,) * a.ndim)
    f32o = jax.ShapeDtypeStruct((rows, 512), F32)
    bf16o = jax.ShapeDtypeStruct((rows, 512), BF16)
    heads, headso = _cache_out(rows, tm, seq)
    return pl.pallas_call(
        _proj_even_kernel,
        grid=(rows // tm,),
        in_specs=[row(D_MODEL), full(w), full(wf), full(bf)],
        out_specs=[row(512), row(512), heads, row(512), heads, row(512), row(LANES)],
        out_shape=[f32o, bf16o, headso, bf16o, headso, bf16o,
                   jax.ShapeDtypeStruct((rows, LANES), F32)],
        compiler_params=_cparams("parallel"),
        name="proj_even",
    )(x2, w, wf, bf)


def _proj_odd_kernel(x_ref, w_ref, q_ref, k_ref, kb_ref, v_ref, vb_ref, bg_ref, z_ref):
    xb = x_ref[...].astype(BF16)

    def mm(c):
        return _dot(xb, w_ref[:, c * 512:(c + 1) * 512])

    q_ref[...] = (mm(0) * (DIFF_DH ** -0.5 * LOG2E)).astype(BF16)
    k = mm(1)
    _store_cache(k_ref, k)
    kb_ref[...] = k.astype(BF16)
    v = mm(2)
    tm = v.shape[0]
    v_rows = v_ref.reshape(tm * DIFF_HEADS, DIFF_VD)
    for h in range(DIFF_HEADS):
        v_rows[pl.ds(h, tm, stride=DIFF_HEADS), :] = v[:, h * DIFF_VD:(h + 1) * DIFF_VD]
    vb_ref[...] = v.astype(BF16)
    bg_ref[...] = mm(3)
    z_ref[...] = mm(4) * mm(5)


def _proj_odd(x2, w, seq):
    rows = x2.shape[0]
    tm = _row_tile(rows)
    kspec, kshape = _cache_out(rows, tm, seq)
    row = lambda n: pl.BlockSpec((tm, n), lambda i: (i, 0))
    f32o = jax.ShapeDtypeStruct((rows, 512), F32)
    bf16o = jax.ShapeDtypeStruct((rows, 512), BF16)
    return pl.pallas_call(
        _proj_odd_kernel,
        grid=(rows // tm,),
        in_specs=[row(D_MODEL), pl.BlockSpec(w.shape, lambda i: (0, 0))],
        out_specs=[row(512), kspec, row(512),
                   pl.BlockSpec((tm, DIFF_HEADS, DIFF_VD), lambda i: (i, 0, 0))] + [row(512)] * 3,
        out_shape=[bf16o, kshape, bf16o,
                   jax.ShapeDtypeStruct((rows, DIFF_HEADS, DIFF_VD), F32), bf16o, f32o, f32o],
        compiler_params=_cparams("parallel"),
        name="proj_odd",
    )(x2, w)


def _mem_kv_kernel(x_ref, w_ref, o_ref):
    o_ref[0, 0] = _dot(x_ref[...].astype(BF16), w_ref[0, 0])


def _mem_kv(mem2, w_kv):
    rows = mem2.shape[0]
    return pl.pallas_call(
        _mem_kv_kernel,
        grid=(2, DEPTH),
        in_specs=[pl.BlockSpec((rows, D_MODEL), lambda a, l: (0, 0)),
                  pl.BlockSpec((1, 1, D_MODEL, D_MODEL), lambda a, l: (a, l, 0, 0))],
        out_specs=pl.BlockSpec((1, 1, rows, D_MODEL), lambda a, l: (a, l, 0, 0)),
        out_shape=jax.ShapeDtypeStruct((2, DEPTH, rows, D_MODEL), F32),
        compiler_params=_cparams("parallel", "parallel"),
        name="mem_kv",
    )(mem2, w_kv)


def _cumsum_kernel(x_ref, f_ref, kaug_ref, qaug_ref, carry_ref, *, rows):
    @pl.when(pl.program_id(1) == 0)
    def _():
        carry_ref[...] = jnp.zeros_like(carry_ref)

    r = lax.broadcasted_iota(jnp.int32, (LANES, LANES), 0)
    c = lax.broadcasted_iota(jnp.int32, (LANES, LANES), 1)
    tri = (r >= c).astype(F32)
    grp = c // FOX_HEADS
    carry = carry_ref[...]
    for s in range(rows // LANES):
        sl = slice(s * LANES, (s + 1) * LANES)
        cs = jnp.dot(tri, x_ref[0, sl, :], preferred_element_type=F32,
                     precision=lax.Precision.HIGHEST) + carry
        carry = cs[LANES - 1:LANES, :]
        f = cs * LOG2E
        f_ref[0, sl, :] = f
        hi = f.astype(BF16).astype(F32)
        mid = (f - hi).astype(BF16).astype(F32)
        lo = (f - hi - mid).astype(BF16).astype(F32)
        piece = jnp.where(grp % 3 == 0, hi, jnp.where(grp % 3 == 1, mid, lo))
        kaug_ref[0, sl, :] = jnp.where(grp < 3, -piece, jnp.where(grp < 6, 1.0, 0.0)).astype(BF16)
        qaug_ref[0, sl, :] = jnp.where(grp < 3, 1.0, jnp.where(grp < 6, piece, 0.0)).astype(BF16)
    carry_ref[...] = carry


def _cumsum_time(x, rows):
    b, t, _ = x.shape
    spec = pl.BlockSpec((1, rows, LANES), lambda i, j: (i, j, 0))
    return pl.pallas_call(
        functools.partial(_cumsum_kernel, rows=rows),
        grid=(b, t // rows),
        in_specs=[spec],
        out_specs=[spec] * 3,
        out_shape=[jax.ShapeDtypeStruct(x.shape, F32), jax.ShapeDtypeStruct(x.shape, BF16),
                   jax.ShapeDtypeStruct(x.shape, BF16)],
        scratch_shapes=[pltpu.VMEM((1, LANES), F32)],
        compiler_params=_cparams("parallel", "arbitrary"),
        name="cumsum_time",
    )(x)


def _loop_unrolled(block, n, unroll, tail):
    def body(i, carry):
        for u in range(unroll):
            block(unroll * i + u)
        return carry

    lax.fori_loop(0, n // unroll, body, 0)
    for left in range(unroll):
        @pl.when(n % unroll == left)
        def _(left=left):
            for u in range(left):
                block(n - left + u)
            tail()


def _online_softmax(s, m_old, shift=None):
    row_max = jnp.max(s, axis=1, keepdims=True)
    if shift is not None:
        row_max = row_max + shift
    m_new = jnp.maximum(m_old, row_max)
    m_sub = m_new if shift is None else m_new - shift
    p = jnp.exp2(s - jnp.concatenate([m_sub] * (s.shape[1] // LANES), axis=1))
    return p.astype(BF16), jnp.exp2(m_old - m_new), m_new


def _fox_kernel(q_ref, k_ref, v_ref, qaug_ref, kaug_ref, mask_ref, o_ref,
                qcat_ref, s_ref, p_ref, acc_ref, m_ref, al_ref, *, tq, tk):
    pair = pl.program_id(1)
    j_last = pl.program_id(2)
    lane = lax.broadcasted_iota(jnp.int32, (1, LANES), 1)
    upper = lane >= FOX_DH
    sels = (jnp.logical_not(upper), upper)
    q2 = q_ref[0]
    qa = qaug_ref[0]
    for hh in range(2):
        own = jnp.logical_and(lane % FOX_HEADS == 2 * pair + hh, lane < 6 * FOX_HEADS)
        qcat_ref[hh] = jnp.concatenate([jnp.where(sels[hh], q2, jnp.zeros_like(q2)),
                                        jnp.where(own, qa, jnp.zeros_like(qa))], axis=1)
    m_ref[...] = jnp.full(m_ref.shape, NEG, F32)
    acc_ref[...] = jnp.zeros(acc_ref.shape, F32)
    p_ref[1] = jnp.zeros(p_ref.shape[1:], p_ref.dtype)
    al_ref[1] = jnp.ones(al_ref.shape[1:], F32)

    def scores(hh, j):
        rows = pl.ds(pl.multiple_of(j * tk, tk), tk)
        kcat = jnp.concatenate([k_ref[0, rows, :], kaug_ref[0, rows, :]], axis=1)
        s_ref[hh] = _dot_nt(qcat_ref[hh], kcat)

    def probs(hh, masked):
        s = s_ref[hh]
        if masked:
            s = s + mask_ref[...]
        p_ref[hh], al_ref[hh], m_ref[hh] = _online_softmax(s, m_ref[hh])

    def pv(hh, j):
        vb = v_ref[0, pl.ds(pl.multiple_of(j * tk, tk), tk), :]
        va = jnp.where(sels[hh], vb, jnp.ones_like(vb))
        acc_ref[hh] = al_ref[hh] * acc_ref[hh] + _dot(p_ref[hh], va)

    scores(0, 0)

    def block(j):
        scores(1, j)
        probs(0, False)
        pv(1, jnp.maximum(j - 1, 0))
        scores(0, j + 1)
        probs(1, False)
        pv(0, j)

    def tail():
        scores(1, j_last)
        probs(0, True)
        pv(1, jnp.maximum(j_last - 1, 0))
        probs(1, True)
        pv(0, j_last)
        pv(1, j_last)
        a0 = acc_ref[0]
        a1 = acc_ref[1]
        o0 = a0 / pltpu.roll(a0, FOX_DH, 1)
        o1 = a1 / pltpu.roll(a1, FOX_DH, 1)
        o_ref[0] = jnp.where(upper, o1, o0).astype(o_ref.dtype)

    _loop_unrolled(block, j_last, ATT_UNROLL, tail)


def _fox_attention(qb, kb, vb, qaug, kaug, tq, tk):
    b, t, _ = qb.shape
    assert t % tq == 0 and tq == tk
    ii = np.arange(tq)[:, None]
    jj = np.arange(tk)[None, :]
    mask = jnp.asarray(np.where(jj <= ii, 0.0, NEG), F32)
    return pl.pallas_call(
        functools.partial(_fox_kernel, tq=tq, tk=tk),
        grid=(b, FOX_HEADS // 2, t // tq),
        in_specs=[
            pl.BlockSpec((1, tq, LANES), lambda bi, p, qi: (bi, qi, p)),
            pl.BlockSpec((1, t, LANES), lambda bi, p, qi: (bi, 0, p)),
            pl.BlockSpec((1, t, LANES), lambda bi, p, qi: (bi, 0, p)),
            pl.BlockSpec((1, tq, LANES), lambda bi, p, qi: (bi, qi, 0)),
            pl.BlockSpec((1, t, LANES), lambda bi, p, qi: (bi, 0, 0)),
            pl.BlockSpec((tq, tk), lambda bi, p, qi: (0, 0)),
        ],
        out_specs=pl.BlockSpec((1, tq, LANES), lambda bi, p, qi: (bi, qi, p)),
        out_shape=jax.ShapeDtypeStruct((b, t, FOX_WIDTH), BF16),
        scratch_shapes=[pltpu.VMEM((2, tq, 2 * LANES), BF16),
                        pltpu.VMEM((2, tq, tk), F32),
                        pltpu.VMEM((2, tq, tk), BF16),
                        pltpu.VMEM((2, tq, LANES), F32),
                        pltpu.VMEM((2, tq, LANES), F32),
                        pltpu.VMEM((2, tq, LANES), F32)],
        compiler_params=_cparams("parallel", "parallel", "arbitrary"),
        name="fox_attention",
    )(qb, kb, vb, qaug, kaug, mask)


def _diff_kernel(cfar_ref, q_ref, k_ref, v_ref, tiles_ref, lam_ref, g_ref, o_ref,
                 qm_ref, s_ref, p_ref, acc_ref, m_ref, al_ref, *, tq, tk, lam_init):
    j_last = pl.program_id(2)
    lane = lax.broadcasted_iota(jnp.int32, (1, LANES), 1)
    upper = lane >= DIFF_DH
    q2 = q_ref[0]
    zero = jnp.zeros_like(q2)
    qm_ref[0] = jnp.where(upper, zero, q2)
    qm_ref[1] = jnp.where(upper, q2, zero)
    cfar = cfar_ref[pl.program_id(1)]
    m_ref[...] = jnp.full(m_ref.shape, NEG, F32)
    acc_ref[...] = jnp.zeros(acc_ref.shape, F32)
    p_ref[1] = jnp.zeros(p_ref.shape[1:], p_ref.dtype)
    al_ref[1] = jnp.ones(al_ref.shape[1:], F32)

    def scores(mm, j):
        start = pl.multiple_of(j * tk, tk)
        s_ref[mm] = _dot_nt(qm_ref[mm], k_ref[0, pl.ds(start, tk), :])

    def probs(mm, tile):
        if tile is None:
            out = _online_softmax(s_ref[mm], m_ref[mm], shift=cfar)
        else:
            out = _online_softmax(s_ref[mm] + tiles_ref[0, tile], m_ref[mm])
        p_ref[mm], al_ref[mm], m_ref[mm] = out

    def pv(mm, j):
        start = pl.multiple_of(j * tk, tk)
        vb = v_ref[0, pl.ds(start, tk), :]
        va = jnp.concatenate([vb, jnp.ones_like(vb)], axis=1)
        al = al_ref[mm]
        acc_ref[mm] = jnp.concatenate([al, al], axis=1) * acc_ref[mm] + _dot(p_ref[mm], va)

    def block(j, tile):
        scores(1, j)
        probs(0, tile)
        pv(1, jnp.maximum(j - 1, 0))
        scores(0, j + 1)
        probs(1, tile)
        pv(0, j)

    def tail(with_previous):
        if with_previous:
            block(j_last - 1, 1)
        scores(1, j_last)
        probs(0, 0)
        pv(1, jnp.maximum(j_last - 1, 0))
        probs(1, 0)
        pv(0, j_last)
        pv(1, j_last)
        lam = _diff_lambda(lam_ref[...], lam_init)
        a0 = acc_ref[0]
        a1 = acc_ref[1]
        o = a0[:, :LANES] / a0[:, LANES:] - lam * (a1[:, :LANES] / a1[:, LANES:])
        ms = jnp.mean(o * o, axis=1, keepdims=True)
        o = o * lax.rsqrt(ms + RMS_EPS) * g_ref[...] * (1.0 - lam_init)
        o_ref[0] = o.astype(o_ref.dtype)

    scores(0, 0)

    @pl.when(j_last == 0)
    def _():
        tail(False)

    @pl.when(j_last >= 1)
    def _():
        _loop_unrolled(lambda j: block(j, None), j_last - 1, ATT_UNROLL, lambda: tail(True))


def _t5_bucket(rel):
    nb = REL_BUCKETS // 2
    max_exact = nb // 2
    ret = jnp.where(rel > 0, nb, 0)
    n = jnp.abs(rel)
    nf = jnp.maximum(n, 1).astype(F32)
    large = max_exact + (jnp.log(nf / max_exact) / math.log(REL_MAX_DIST / max_exact)
                         * (nb - max_exact)).astype(jnp.int32)
    large = jnp.minimum(large, nb - 1)
    return ret + jnp.where(n < max_exact, n, large)


def _toeplitz_kernel(w_ref, mask_ref, o_ref, *, tq, width):
    n = w_ref.shape[2]
    for d in range(w_ref.shape[1]):
        w = jnp.broadcast_to(w_ref[0, d:d + 1, :], (tq, n))
        o_ref[0, d] = pltpu.roll(w, 0, 1, stride=1, stride_axis=0)[:, :width] + mask_ref[d]


def _t5_bias_tiles(rel_table, tq, width, offsets, masks):
    nt = len(offsets)
    n = -(-(tq + width) // LANES) * LANES
    m = jnp.arange(n)
    rel = jnp.where(m < width, m, m - n)
    w = jnp.stack([rel_table[_t5_bucket(rel + d)] for d in offsets])
    w = jnp.transpose(w, (2, 0, 1)).astype(F32) * LOG2E
    return pl.pallas_call(
        functools.partial(_toeplitz_kernel, tq=tq, width=width),
        grid=(DIFF_HEADS,),
        in_specs=[pl.BlockSpec((1, nt, n), lambda h: (h, 0, 0)),
                  pl.BlockSpec((nt, tq, width), lambda h: (0, 0, 0))],
        out_specs=pl.BlockSpec((1, nt, tq, width), lambda h: (h, 0, 0, 0)),
        out_shape=jax.ShapeDtypeStruct((DIFF_HEADS, nt, tq, width), F32),
        compiler_params=_cparams("parallel"),
        name="t5_bias_tiles",
    )(w, jnp.asarray(masks, F32))


def _diff_lambda(lp, lam_init):
    return (jnp.exp(jnp.sum(lp[0:1] * lp[1:2], keepdims=True))
            - jnp.exp(jnp.sum(lp[2:3] * lp[3:4], keepdims=True)) + lam_init)


def _diff_attention(qb, kb, vb, rel_table, diff_lambda, subln_g, tq, tk, lam_init):
    b, t, _ = qb.shape
    assert t % tq == 0 and tq == tk and tk >= REL_MAX_DIST and tk % CHUNK == 0
    ii = np.arange(tq)[:, None]
    jj = np.arange(tk)[None, :]
    masks = np.zeros((2, tq, tk), np.float32)
    masks[0] = np.where((jj // CHUNK) <= (ii // CHUNK), 0.0, NEG)
    tiles = _t5_bias_tiles(rel_table, tq, tk, (0, -tk), masks)
    cfar = rel_table[_t5_bucket(jnp.asarray(-2 * REL_MAX_DIST, jnp.int32))].astype(F32) * LOG2E
    return pl.pallas_call(
        functools.partial(_diff_kernel, tq=tq, tk=tk, lam_init=lam_init),
        grid=(b, DIFF_HEADS, t // tq),
        in_specs=[
            pl.BlockSpec(memory_space=pltpu.SMEM),
            pl.BlockSpec((1, tq, LANES), lambda bi, h, qi: (bi, qi, h)),
            pl.BlockSpec((1, t, LANES), lambda bi, h, qi: (bi, 0, h)),
            pl.BlockSpec((1, t, LANES), lambda bi, h, qi: (bi, 0, h)),
            pl.BlockSpec((1, 2, tq, tk), lambda bi, h, qi: (h, 0, 0, 0)),
            pl.BlockSpec((4, DIFF_DH), lambda bi, h, qi: (0, 0)),
            pl.BlockSpec((1, DIFF_VD), lambda bi, h, qi: (0, 0)),
        ],
        out_specs=pl.BlockSpec((1, tq, LANES), lambda bi, h, qi: (bi, qi, h)),
        out_shape=jax.ShapeDtypeStruct((b, t, DIFF_WIDTH), BF16),
        scratch_shapes=[pltpu.VMEM((2, tq, LANES), BF16),
                        pltpu.VMEM((2, tq, tk), F32),
                        pltpu.VMEM((2, tq, tk), BF16),
                        pltpu.VMEM((2, tq, 2 * LANES), F32),
                        pltpu.VMEM((2, tq, LANES), F32),
                        pltpu.VMEM((2, tq, LANES), F32)],
        compiler_params=_cparams("parallel", "parallel", "arbitrary"),
        name="diff_attention",
    )(cfar, qb, kb, vb, tiles, diff_lambda, subln_g.reshape(1, DIFF_VD))


def _two_part_attend(q_h, kt_c, k_n, v_c, v_n, bias_c, bias_n, v_time_minor):
    s_c = _dot(q_h, kt_c.astype(BF16)) + bias_c
    s_n = _dot_nt(q_h, k_n) + bias_n
    m = jnp.maximum(jnp.max(s_c, axis=1, keepdims=True), jnp.max(s_n, axis=1, keepdims=True))
    p_c = jnp.exp2(s_c - m).astype(BF16)
    p_n = jnp.exp2(s_n - m)
    l = jnp.sum(p_c.astype(F32), axis=1, keepdims=True) + jnp.sum(p_n, axis=1, keepdims=True)
    pv_c = _dot_nt(p_c, v_c.astype(BF16)) if v_time_minor else _dot(p_c, v_c.astype(BF16))
    return (pv_c + _dot(p_n.astype(BF16), v_n)) / l


def _fox_decode_kernel(q_ref, kn_ref, vn_ref, kc_ref, vc_ref, fq_ref, ft_ref, o_ref, *, t, past):
    q = q_ref[0]
    kn = kn_ref[0]
    vn = vn_ref[0]
    fblk = fq_ref[0]
    r = lax.broadcasted_iota(jnp.int32, (t, t), 0)
    c = lax.broadcasted_iota(jnp.int32, (t, t), 1)
    causal = jnp.where(c <= r, 0.0, NEG)
    outs = []
    for h in range(FOX_HEADS):
        sl = slice(h * FOX_DH, (h + 1) * FOX_DH)
        fq = fblk[:, h:h + 1]
        fk = ft_ref[0, h:h + 1, :]
        outs.append(_two_part_attend(q[:, sl], kc_ref[0, h], kn[:, sl], vc_ref[0, h], vn[:, sl],
                                     fq - fk[:, :past], fq - fk[:, past:past + t] + causal, True))
    o_ref[0] = jnp.concatenate(outs, axis=1).astype(o_ref.dtype)


def _decode_specs(t, past):
    new = pl.BlockSpec((1, t, 512), lambda bi: (bi, 0, 0))
    cache = pl.BlockSpec((1, 8, 64, past), lambda bi: (bi, 0, 0, 0))
    return new, cache


def _fox_decode(qb, kn, vn, kc, vc, fcum, ft, past):
    b, t, _ = qb.shape
    t_kp = fcum.shape[1]
    assert past % t == 0
    new, cache = _decode_specs(t, past)
    return pl.pallas_call(
        functools.partial(_fox_decode_kernel, t=t, past=past),
        grid=(b,),
        in_specs=[new, new, new, cache, cache,
                  pl.BlockSpec((1, t, LANES), lambda bi: (bi, past // t, 0)),
                  pl.BlockSpec((1, FOX_HEADS, t_kp), lambda bi: (bi, 0, 0))],
        out_specs=new,
        out_shape=jax.ShapeDtypeStruct((b, t, FOX_WIDTH), BF16),
        compiler_params=_cparams("parallel"),
        name="fox_decode",
    )(qb, kn, vn, kc, vc, fcum, ft)


def _diff_decode_kernel(q_ref, kn_ref, vn_ref, kc_ref, vc_ref, bias_ref, lam_ref, g_ref, o_ref,
                        *, t, past, lam_init):
    q = q_ref[0]
    kn = kn_ref[0]
    vn = vn_ref[0]
    lam = _diff_lambda(lam_ref[...], lam_init)
    outs = []
    for h in range(DIFF_HEADS):
        v_c = _head_rows(vc_ref, h)
        v_n = vn[:, h * DIFF_VD:(h + 1) * DIFF_VD]
        bias = bias_ref[h, 0]
        maps = []
        for mm in range(2):
            j = 2 * h + mm
            sl = slice(j * DIFF_DH, (j + 1) * DIFF_DH)
            maps.append(_two_part_attend(q[:, sl], kc_ref[0, j], kn[:, sl], v_c, v_n,
                                         bias[:, :past], bias[:, past:past + t], False))
        a = maps[0] - lam * maps[1]
        ms = jnp.mean(a * a, axis=1, keepdims=True)
        outs.append(a * lax.rsqrt(ms + RMS_EPS) * g_ref[...] * (1.0 - lam_init))
    o_ref[0] = jnp.concatenate(outs, axis=1).astype(o_ref.dtype)


def _diff_decode(qb, kn, vn, kc, vc, rel_table, diff_lambda, subln_g, past, lam_init):
    b, t, _ = qb.shape
    t_kp = -(-(past + t) // LANES) * LANES
    q_pos = past + np.arange(t)[:, None]
    k_pos = np.arange(t_kp)[None, :]
    visible = ((k_pos // CHUNK) <= (q_pos // CHUNK)) & (k_pos < past + t)
    tiles = _t5_bias_tiles(rel_table, t, t_kp, (-past,), np.where(visible, 0.0, NEG)[None])
    new, cache = _decode_specs(t, past)
    vcache = pl.BlockSpec((1, 1, past, DIFF_HEADS, DIFF_VD), lambda bi: (0, bi, 0, 0, 0))
    return pl.pallas_call(
        functools.partial(_diff_decode_kernel, t=t, past=past, lam_init=lam_init),
        grid=(b,),
        in_specs=[new, new, new, cache, vcache,
                  pl.BlockSpec(tiles.shape, lambda bi: (0, 0, 0, 0)),
                  pl.BlockSpec((4, DIFF_DH), lambda bi: (0, 0)),
                  pl.BlockSpec((1, DIFF_VD), lambda bi: (0, 0))],
        out_specs=new,
        out_shape=jax.ShapeDtypeStruct((b, t, DIFF_WIDTH), BF16),
        compiler_params=_cparams("parallel"),
        name="diff_decode",
    )(qb, kn, vn, kc, vc, tiles, diff_lambda, subln_g.reshape(1, DIFF_VD))


def _fill_history(ext_ref, cur, hist_ref, prev_ref):
    ext_ref[:, 0:HIST_ROWS, :] = jnp.where(pl.program_id(1) == 0, hist_ref[...], prev_ref[...])
    ext_ref[:, HIST_ROWS:, :] = cur


def _halo_specs(bb, tm, width):
    cur = pl.BlockSpec((bb, tm, width), lambda b, i: (b, i, 0))
    prev = pl.BlockSpec((bb, HIST_ROWS, width),
                        lambda b, i: (b, jnp.maximum(i * (tm // HIST_ROWS) - 1, 0), 0))
    hist = pl.BlockSpec((bb, HIST_ROWS, width), lambda b, i: (b, 0, 0))
    return cur, prev, hist


def _mix_out(o_ref, x_ref, mix, g_ref, beta_ref):
    bb, tm, d = x_ref.shape
    y = DN_ALPHA * x_ref[...].reshape(bb * tm, d) + mix
    o_ref[...] = _layer_norm(y, g_ref[...], beta_ref[...]).reshape(bb, tm, d)


def _mix_even_kernel(u_ref, prev_ref, hist_ref, wmix_ref, scale_ref, att_ref, w_ref, x_ref,
                     g_ref, beta_ref, o_ref, ext_ref, *, past):
    bb, tm, _ = u_ref.shape
    u = u_ref[...]
    _fill_history(ext_ref, u, hist_ref, prev_ref)
    pos = past + pl.program_id(1) * tm + lax.broadcasted_iota(jnp.int32, (1, tm, 1), 1)
    groups = []
    for g, w in enumerate(POOL_WINDOWS):
        sl = slice(g * POOL_GC, (g + 1) * POOL_GC)
        ug = u[:, :, sl]
        wsum = ug
        for s in range(1, w):
            wsum = wsum + ext_ref[:, HIST_ROWS - s:HIST_ROWS - s + tm, sl]
        cnt = jnp.minimum(w, pos + 1).astype(F32)
        d = (wsum / cnt - ug).reshape(bb * tm, POOL_GC)
        groups.append((_dot(d.astype(BF16), wmix_ref[g]) * scale_ref[:, sl]).astype(BF16))
    pool_y = jnp.concatenate(groups, axis=1)
    att = att_ref[...].reshape(bb * tm, FOX_WIDTH)
    mix = _dot(pool_y, w_ref[0:POOL_WIDTH, :]) + _dot(att, w_ref[POOL_WIDTH:, :])
    _mix_out(o_ref, x_ref, mix, g_ref, beta_ref)


def _mix_odd_kernel(z_ref, prev_ref, hist_ref, bg_ref, cw_ref, att_ref, w_ref, x_ref,
                    g_ref, beta_ref, o_ref, ext_ref):
    bb, tm, _ = z_ref.shape
    z = z_ref[...]
    _fill_history(ext_ref, z, hist_ref, prev_ref)
    y = (ext_ref[:, HIST_ROWS - 2:HIST_ROWS - 2 + tm, :] * cw_ref[0:1, :]
         + ext_ref[:, HIST_ROWS - 1:HIST_ROWS - 1 + tm, :] * cw_ref[1:2, :]
         + z * cw_ref[2:3, :])
    conv_y = (bg_ref[...] * y).astype(BF16).reshape(bb * tm, CONV_CH)
    att = att_ref[...].reshape(bb * tm, DIFF_WIDTH)
    mix = _dot(att, w_ref[0:DIFF_WIDTH, :]) + _dot(conv_y, w_ref[DIFF_WIDTH:, :])
    _mix_out(o_ref, x_ref, mix, g_ref, beta_ref)


def _mix_tiles(b, t):
    tm = min(512, t)
    return max(1, min(b, 512 // tm)), tm


def _mix_ln(kernel_fn, name, local_ins, local_specs, att3, w, x3, g, beta, width):
    b, t, _ = x3.shape
    bb, tm = _mix_tiles(b, t)
    cur = lambda n: pl.BlockSpec((bb, tm, n), lambda bi, i: (bi, i, 0))
    vec = pl.BlockSpec((1, D_MODEL), lambda bi, i: (0, 0))
    return pl.pallas_call(
        kernel_fn,
        grid=(b // bb, t // tm),
        in_specs=local_specs + [cur(512), pl.BlockSpec(w.shape, lambda bi, i: (0, 0)),
                                cur(D_MODEL), vec, vec],
        out_specs=cur(D_MODEL),
        out_shape=jax.ShapeDtypeStruct(x3.shape, F32),
        scratch_shapes=[pltpu.VMEM((bb, HIST_ROWS + tm, width), F32)],
        compiler_params=_cparams("parallel", "parallel"),
        name=name,
    )(*local_ins, att3, w, x3, g.reshape(1, D_MODEL), beta.reshape(1, D_MODEL))


def _mix_even_ln(u3, hist16, wmix, scale, past, fox3, w, x3, g, beta):
    cur, prev, hist = _halo_specs(*_mix_tiles(*u3.shape[:2]), POOL_WIDTH)
    specs = [cur, prev, hist, pl.BlockSpec(wmix.shape, lambda bi, i: (0, 0, 0)),
             pl.BlockSpec((1, POOL_WIDTH), lambda bi, i: (0, 0))]
    return _mix_ln(functools.partial(_mix_even_kernel, past=past), "mix_even_ln",
                   [u3, u3, hist16, wmix, scale.reshape(1, POOL_WIDTH)], specs,
                   fox3, w, x3, g, beta, POOL_WIDTH)


def _mix_odd_ln(z3, hist16, bg3, conv_w, att3, w, x3, g, beta):
    cur, prev, hist = _halo_specs(*_mix_tiles(*z3.shape[:2]), CONV_CH)
    specs = [cur, prev, hist, cur, pl.BlockSpec((CONV_K, CONV_CH), lambda bi, i: (0, 0))]
    return _mix_ln(_mix_odd_kernel, "mix_odd_ln",
                   [z3, z3, hist16, bg3, conv_w], specs, att3, w, x3, g, beta, CONV_CH)


def _head_rows(c_ref, h, bi=0):
    _, bb, n, heads, d = c_ref.shape
    return c_ref.reshape(bb * n * heads, d)[pl.ds(bi * n * heads + h, n, stride=heads), :]


def _mem_kernel(x_ref, wq_ref, mk0_ref, mk1_ref, mv0_ref, mv1_ref, wo_ref, g_ref, beta_ref, o_ref,
                kb_ref, vb_ref):
    bb, tm, _ = x_ref.shape

    @pl.when(pl.program_id(1) == 0)
    def _():
        for bi in range(bb):
            for h in range(MEM_HEADS):
                kb_ref[bi, h] = jnp.concatenate(
                    [_head_rows(mk0_ref, h, bi), _head_rows(mk1_ref, h, bi)], axis=1).astype(BF16)
                vb_ref[bi, h] = jnp.concatenate(
                    [_head_rows(mv0_ref, h, bi), _head_rows(mv1_ref, h, bi)], axis=1).astype(BF16)

    x = x_ref[...].reshape(bb * tm, D_MODEL)
    q = _dot(x.astype(BF16), wq_ref[0])
    qb = (q * (MEM_DH ** -0.5)).astype(BF16)
    rows = []
    for bi in range(bb):
        outs = []
        for h in range(MEM_HEADS):
            s = _dot_nt(qb[bi * tm:(bi + 1) * tm, h * MEM_DH:(h + 1) * MEM_DH], kb_ref[bi, h])
            m = jnp.max(s, axis=1, keepdims=True)
            p = jnp.exp(s - m)
            l = jnp.sum(p, axis=1, keepdims=True)
            o = _dot(p.astype(BF16), vb_ref[bi, h]) / l
            outs.append(o.astype(BF16))
        rows.append(jnp.concatenate(outs, axis=1))
    o_all = rows[0] if bb == 1 else jnp.concatenate(rows, axis=0)
    y = DN_ALPHA * x + _dot(o_all, wo_ref[0])
    o_ref[...] = _layer_norm(y, g_ref[...], beta_ref[...]).reshape(bb, tm, D_MODEL)


def _mem_attend_ln(x3, wq, mk, mv, layer, wo, g, beta):
    b, t, _ = x3.shape
    tm = min(MEM_ROW_TILE, t)
    bb = max(1, min(b, MEM_ROWS_MIN // tm))
    assert b % bb == 0 and MEM_DH == 2 * LANES
    xs = pl.BlockSpec((bb, tm, D_MODEL), lambda bi, i: (bi, i, 0))
    ws = pl.BlockSpec((1, D_MODEL, D_MODEL), lambda bi, i: (layer, 0, 0))
    half = lambda c: pl.BlockSpec((1, bb, N_MEM, MEM_HEADS, LANES), lambda bi, i: (layer, bi, 0, 0, c))
    vec = pl.BlockSpec((1, D_MODEL), lambda bi, i: (0, 0))
    return pl.pallas_call(
        _mem_kernel,
        grid=(b // bb, t // tm),
        in_specs=[xs, ws, half(0), half(1), half(0), half(1), ws, vec, vec],
        out_specs=xs,
        out_shape=jax.ShapeDtypeStruct(x3.shape, F32),
        scratch_shapes=[pltpu.VMEM((bb, MEM_HEADS, N_MEM, MEM_DH), BF16),
                        pltpu.VMEM((bb, MEM_HEADS, N_MEM, MEM_DH), BF16)],
        compiler_params=_cparams("parallel", "arbitrary"),
        name="mem_attend_ln",
    )(x3, wq, mk, mk, mv, mv, wo, g.reshape(1, D_MODEL), beta.reshape(1, D_MODEL))


def _ffn_kernel(x_ref, w1_ref, w2_ref, g_ref, beta_ref, o_ref, *, chunk):
    x = x_ref[...]
    xb = x.astype(BF16)
    acc = jnp.zeros(x.shape, F32)
    for c in range(D_FF // chunk):
        h = _dot(xb, w1_ref[0, :, c * chunk:(c + 1) * chunk])
        h = jnp.square(jnp.maximum(h, 0.0))
        acc = acc + _dot(h.astype(BF16), w2_ref[0, c * chunk:(c + 1) * chunk, :])
    o_ref[...] = _layer_norm(DN_ALPHA * x + acc, g_ref[...], beta_ref[...])


def _ffn_ln(x2, w1, w2, layer, g, beta):
    rows = x2.shape[0]
    tm = _row_tile(rows)
    row = pl.BlockSpec((tm, D_MODEL), lambda i: (i, 0))
    vec = pl.BlockSpec((1, D_MODEL), lambda i: (0, 0))
    once = pl.Buffered(1)
    return pl.pallas_call(
        functools.partial(_ffn_kernel, chunk=1024),
        grid=(rows // tm,),
        in_specs=[row,
                  pl.BlockSpec((1,) + w1.shape[1:], lambda i: (layer, 0, 0), pipeline_mode=once),
                  pl.BlockSpec((1,) + w2.shape[1:], lambda i: (layer, 0, 0), pipeline_mode=once),
                  vec, vec],
        out_specs=row,
        out_shape=jax.ShapeDtypeStruct((rows, D_MODEL), F32),
        compiler_params=_cparams("parallel"),
        name="ffn_ln",
    )(x2, w1, w2, g.reshape(1, D_MODEL), beta.reshape(1, D_MODEL))


def _pad_rows(a, total):
    pad = total - a.shape[1]
    if pad == 0:
        return a
    return jnp.pad(a, ((0, 0), (0, pad)) + ((0, 0),) * (a.ndim - 2))


def _hist16(h):
    return jnp.pad(h, ((0, 0), (HIST_ROWS - h.shape[1], 0), (0, 0)))


def _trunk(x, mem_k, mem_v, pool_h, fk_h, fv_h, flf_h, dk_h, dv_h, conv_h, past, wts):
    b, t, _ = x.shape
    rows = b * t
    t_k = past + t
    if past == 0:
        t_kp = t_k
        cs_rows = 1024
    else:
        t_kp = -(-t_k // LANES) * LANES
        cs_rows = t_kp
    x2 = x.reshape(rows, D_MODEL)

    u, qb, k, kb, v, vb, lf = _proj_even(x2, wts["w_in_even"], wts["w_forget"], wts["b_forget"], t)
    lf3 = lf.reshape(b, t, LANES)
    qb3 = qb.reshape(b, t, FOX_WIDTH)
    kb3 = kb.reshape(b, t, FOX_WIDTH)
    vb3 = vb.reshape(b, t, FOX_WIDTH)
    if past:
        lf_hist = jnp.pad(flf_h[0], ((0, 0), (0, 0), (0, LANES - FOX_HEADS)))
        lf_all = jnp.concatenate([lf_hist, lf3], axis=1)
    else:
        lf_all = lf3
    fcum, kaug, qaug = _cumsum_time(_pad_rows(lf_all, t_kp), cs_rows)
    if past:
        ft = jnp.transpose(fcum[:, :, :FOX_HEADS], (0, 2, 1))
        fox_y = _fox_decode(qb3, kb3, vb3, jnp.transpose(fk_h[0], (0, 2, 3, 1)),
                            jnp.transpose(fv_h[0], (0, 2, 3, 1)), fcum, ft, past)
    else:
        fox_y = _fox_attention(qb3, kb3, vb3, qaug, kaug, ATT_TILE, ATT_TILE)
    u3 = u.reshape(b, t, POOL_WIDTH)
    x3 = _mix_even_ln(u3, _hist16(pool_h[0]), wts["w_pool_mix"], wts["pool_scale"], past, fox_y,
                      wts["w_out_even"], x2.reshape(b, t, D_MODEL), wts["ln_g"][0, 0], wts["ln_b"][0, 0])
    x2 = _mem_attend_ln(x3, wts["w_mem_q"], mem_k, mem_v, 0,
                        wts["w_mem_o"], wts["ln_g"][0, 1], wts["ln_b"][0, 1]).reshape(rows, D_MODEL)
    x2 = _ffn_ln(x2, wts["w_ff1"], wts["w_ff2"], 0, wts["ln_g"][0, 2], wts["ln_b"][0, 2])
    n_pool = u3[:, t - POOL_HIST:][None]
    n_fk = _cache_logical(k, b, t, (FOX_HEADS, FOX_DH))
    n_fv = _cache_logical(v, b, t, (FOX_HEADS, FOX_DH))
    n_flf = lf3[:, :, :FOX_HEADS][None]

    lam_init = 0.8 - 0.6 * math.exp(-0.3 * 1)
    qb, k, kb, v, vb, bg, z = _proj_odd(x2, wts["w_in_odd"], t)
    qb3 = qb.reshape(b, t, DIFF_QK)
    kb3 = kb.reshape(b, t, DIFF_QK)
    vb3 = vb.reshape(b, t, DIFF_WIDTH)
    if past:
        att = _diff_decode(qb3, kb3, vb3,
                           jnp.transpose(dk_h[0].reshape(b, past, 2 * DIFF_HEADS, DIFF_DH), (0, 2, 3, 1)),
                           dv_h, wts["rel_bias_table"],
                           wts["diff_lambda"], wts["diff_subln_g"], past, lam_init)
    else:
        att = _diff_attention(qb3, kb3, vb3, wts["rel_bias_table"], wts["diff_lambda"],
                              wts["diff_subln_g"], ATT_TILE, ATT_TILE, lam_init)
    z3 = z.reshape(b, t, CONV_CH)
    x3 = _mix_odd_ln(z3, _hist16(conv_h[0]), bg.reshape(b, t, CONV_CH), wts["conv_w"], att,
                     wts["w_out_odd"], x2.reshape(b, t, D_MODEL), wts["ln_g"][1, 0], wts["ln_b"][1, 0])
    x2 = _mem_attend_ln(x3, wts["w_mem_q"], mem_k, mem_v, 1,
                        wts["w_mem_o"], wts["ln_g"][1, 1], wts["ln_b"][1, 1]).reshape(rows, D_MODEL)
    x2 = _ffn_ln(x2, wts["w_ff1"], wts["w_ff2"], 1, wts["ln_g"][1, 2], wts["ln_b"][1, 2])
    n_dk = _cache_logical(k, b, t, (DIFF_HEADS, 2, DIFF_DH))
    n_dv = v.reshape(1, b, t, DIFF_HEADS, DIFF_VD)
    n_conv = z3[:, t - (CONV_K - 1):][None]
    return (x2.reshape(b, t, D_MODEL), n_pool, n_fk, n_fv, n_flf, n_dk, n_dv, n_conv)


def kernel(x_prompt, x_sample, state_pool, cache_fox_k, cache_fox_v, cache_fox_logf,
           cache_diff_k, cache_diff_v, state_conv, cache_mem_k, cache_mem_v, mem_prompt,
           w_in_even, b_forget, w_pool_mix, pool_scale, w_out_even,
           w_in_odd, diff_lambda, diff_subln_g, conv_w, w_out_odd, rel_bias_table,
           w_mem_q, w_mem_k, w_mem_v, w_mem_o, w_ff1, w_ff2, ln_g, ln_b):
    bp = x_prompt.shape[0]
    nmain = POOL_WIDTH + 3 * FOX_WIDTH
    wts = {
        "w_in_even": w_in_even[0, :, :nmain].astype(BF16),
        "w_forget": jnp.pad(jnp.tile(w_in_even[0, :, nmain:], (1, 6)),
                            ((0, 0), (0, LANES - 6 * FOX_HEADS))).astype(BF16),
        "b_forget": jnp.pad(jnp.tile(b_forget[0], 6), (0, LANES - 6 * FOX_HEADS)).reshape(1, LANES).astype(F32),
        "w_pool_mix": w_pool_mix[0].astype(BF16),
        "pool_scale": pool_scale[0],
        "w_out_even": w_out_even[0].astype(BF16),
        "w_in_odd": w_in_odd[0].astype(BF16),
        "diff_lambda": diff_lambda[0],
        "diff_subln_g": diff_subln_g[0],
        "conv_w": conv_w[0],
        "w_out_odd": w_out_odd[0].astype(BF16),
        "rel_bias_table": rel_bias_table,
        "w_mem_q": w_mem_q.astype(BF16),
        "w_mem_o": w_mem_o.astype(BF16),
        "w_ff1": w_ff1.astype(BF16),
        "w_ff2": w_ff2.astype(BF16),
        "ln_g": ln_g,
        "ln_b": ln_b,
    }
    kv = _mem_kv(mem_prompt.reshape(bp * N_MEM, D_MODEL),
                 jnp.stack([w_mem_k, w_mem_v]).astype(BF16))
    kv = kv.reshape(2, DEPTH, bp, N_MEM, D_MODEL)
    p_mem_k = kv[0].reshape(DEPTH, bp, N_MEM, MEM_HEADS, MEM_DH)
    p_mem_v = kv[1].reshape(DEPTH, bp, N_MEM, MEM_HEADS, MEM_DH)
    zeros = lambda *s: jnp.zeros(s, F32)
    (y_prompt, p_pool, p_fox_k, p_fox_v, p_fox_logf, p_diff_k, p_diff_v, p_conv) = _trunk(
        x_prompt, p_mem_k, p_mem_v,
        zeros(1, bp, POOL_HIST, POOL_WIDTH), None, None, None, None, None,
        zeros(1, bp, CONV_K - 1, CONV_CH), 0, wts)
    bs = x_sample.shape[0]
    (y_sample, s_pool, s_fox_k, s_fox_v, s_fox_logf, s_diff_k, s_diff_v, s_conv) = _trunk(
        x_sample, cache_mem_k, cache_mem_v,
        state_pool, cache_fox_k, cache_fox_v, cache_fox_logf,
        cache_diff_k, cache_diff_v, state_conv, cache_fox_k.shape[2], wts)
    return (y_prompt, y_sample,
            p_pool, p_fox_k, p_fox_v, p_fox_logf, p_diff_k, p_diff_v, p_conv, p_mem_k, p_mem_v,
            s_pool, s_fox_k, s_fox_v, s_fox_logf, s_diff_k, s_diff_v, s_conv)
```

```python
import functools
import math

import numpy as np
import jax
import jax.numpy as jnp
from jax import lax
from jax.experimental import pallas as pl
from jax.experimental.pallas import tpu as pltpu

F32 = jnp.float32
BF16 = jnp.bfloat16

D_MODEL = 1024
DEPTH = 2
CHUNK = 64
POOL_WIDTH = 512
POOL_GC = 128
POOL_WINDOWS = (2, 4, 8, 16)
POOL_HIST = 15
FOX_HEADS = 8
FOX_DH = 64
FOX_WIDTH = 512
DIFF_HEADS = 4
DIFF_DH = 64
DIFF_VD = 128
DIFF_QK = 512
DIFF_WIDTH = 512
CONV_CH = 512
CONV_K = 3
D_FF = 4096
N_MEM = 256
MEM_HEADS = 4
MEM_DH = 256
REL_BUCKETS = 32
REL_MAX_DIST = 128
DN_ALPHA = (2 * DEPTH) ** 0.25
LN_EPS = 1e-5
RMS_EPS = 1e-5
NEG = -1e30
LOG2E = math.log2(math.e)

LANES = 128
HIST_ROWS = 16
ATT_TILE = 512
ATT_UNROLL = 4
MEM_ROW_TILE = 1024
MEM_ROWS_MIN = 64
VMEM_LIMIT = 56 * 1024 * 1024


def _cparams(*sem):
    return pltpu.CompilerParams(dimension_semantics=sem, vmem_limit_bytes=VMEM_LIMIT)


def _dot(a, b):
    return jnp.dot(a, b, preferred_element_type=F32)


def _dot_nt(a, b):
    return lax.dot_general(a, b, (((1,), (1,)), ((), ())), preferred_element_type=F32)


def _layer_norm(y, g, b):
    mu = jnp.mean(y, axis=-1, keepdims=True)
    d = y - mu
    var = jnp.mean(d * d, axis=-1, keepdims=True)
    return d * lax.rsqrt(var + LN_EPS) * g + b


def _row_tile(rows):
    return min(512, rows)


def _store_cache(o_ref, y):
    if o_ref.shape[0] == 1:
        o_ref[0] = y.T
        return
    cols = []
    for h in range(o_ref.shape[1]):
        c = y[:, (h // 2) * LANES:(h // 2 + 1) * LANES]
        cols.append(pltpu.roll(c, LANES // 2, 1) if h % 2 else c)
    t = jnp.swapaxes(jnp.stack(cols, axis=0), 0, 1)
    o_ref[...] = t[:, :, :o_ref.shape[2]]


def _cache_out(rows, tm, seq):
    if seq % tm == 0 and tm % LANES == 0:
        nb = seq // tm
        return (pl.BlockSpec((1, 512, tm), lambda i: (i // nb, 0, i % nb)),
                jax.ShapeDtypeStruct((rows // seq, 512, seq), F32))
    return (pl.BlockSpec((tm, 8, 64), lambda i: (i, 0, 0)),
            jax.ShapeDtypeStruct((rows, 8, 64), F32))


def _cache_logical(c, b, t, tail):
    if c.shape[0] == b and c.shape[1] == 512:
        c = jnp.transpose(c.reshape((b,) + tail + (t,)), (0, len(tail) + 1) + tuple(range(1, len(tail) + 1)))
        return c[None]
    return c.reshape((1, b, t) + tail)


def _proj_even_kernel(x_ref, w_ref, wf_ref, bf_ref,
                      u_ref, q_ref, k_ref, kb_ref, v_ref, vb_ref, lf_ref):
    xb = x_ref[...].astype(BF16)

    def mm(c):
        return _dot(xb, w_ref[:, c * 512:(c + 1) * 512])

    u_ref[...] = mm(0)
    q_ref[...] = (mm(1) * (FOX_DH ** -0.5 * LOG2E)).astype(BF16)
    k = mm(2)
    _store_cache(k_ref, k)
    kb_ref[...] = k.astype(BF16)
    v = mm(3)
    _store_cache(v_ref, v)
    vb_ref[...] = v.astype(BF16)
    z = _dot(xb, wf_ref[...]) + bf_ref[...]
    lf_ref[...] = jnp.minimum(z, 0.0) - jnp.log1p(jnp.exp(-jnp.abs(z)))


def _proj_even(x2, w, wf, bf, seq):
    rows = x2.shape[0]
    tm = _row_tile(rows)
    row = lambda n: pl.BlockSpec((tm, n), lambda i: (i, 0))
    full = lambda a: pl.BlockSpec(a.shape, lambda i: (0,) * a.ndim)
    f32o = jax.ShapeDtypeStruct((rows, 512), F32)
    bf16o = jax.ShapeDtypeStruct((rows, 512), BF16)
    heads, headso = _cache_out(rows, tm, seq)
    return pl.pallas_call(
        _proj_even_kernel,
        grid=(rows // tm,),
        in_specs=[row(D_MODEL), full(w), full(wf), full(bf)],
        out_specs=[row(512), row(512), heads, row(512), heads, row(512), row(LANES)],
        out_shape=[f32o, bf16o, headso, bf16o, headso, bf16o,
                   jax.ShapeDtypeStruct((rows, LANES), F32)],
        compiler_params=_cparams("parallel"),
        name="proj_even",
    )(x2, w, wf, bf)


def _proj_odd_kernel(x_ref, w_ref, q_ref, k_ref, kb_ref, v_ref, vb_ref, bg_ref, z_ref):
    xb = x_ref[...].astype(BF16)

    def mm(c):
        return _dot(xb, w_ref[:, c * 512:(c + 1) * 512])

    q_ref[...] = (mm(0) * (DIFF_DH ** -0.5 * LOG2E)).astype(BF16)
    k = mm(1)
    _store_cache(k_ref, k)
    kb_ref[...] = k.astype(BF16)
    v = mm(2)
    tm = v.shape[0]
    v_rows = v_ref.reshape(tm * DIFF_HEADS, DIFF_VD)
    for h in range(DIFF_HEADS):
        v_rows[pl.ds(h, tm, stride=DIFF_HEADS), :] = v[:, h * DIFF_VD:(h + 1) * DIFF_VD]
    vb_ref[...] = v.astype(BF16)
    bg_ref[...] = mm(3)
    z_ref[...] = mm(4) * mm(5)


def _proj_odd(x2, w, seq):
    rows = x2.shape[0]
    tm = _row_tile(rows)
    kspec, kshape = _cache_out(rows, tm, seq)
    row = lambda n: pl.BlockSpec((tm, n), lambda i: (i, 0))
    f32o = jax.ShapeDtypeStruct((rows, 512), F32)
    bf16o = jax.ShapeDtypeStruct((rows, 512), BF16)
    return pl.pallas_call(
        _proj_odd_kernel,
        grid=(rows // tm,),
        in_specs=[row(D_MODEL), pl.BlockSpec(w.shape, lambda i: (0, 0))],
        out_specs=[row(512), kspec, row(512),
                   pl.BlockSpec((tm, DIFF_HEADS, DIFF_VD), lambda i: (i, 0, 0))] + [row(512)] * 3,
        out_shape=[bf16o, kshape, bf16o,
                   jax.ShapeDtypeStruct((rows, DIFF_HEADS, DIFF_VD), F32), bf16o, f32o, f32o],
        compiler_params=_cparams("parallel"),
        name="proj_odd",
    )(x2, w)


def _mem_kv_kernel(x_ref, w_ref, o_ref):
    o_ref[0, 0] = _dot(x_ref[...].astype(BF16), w_ref[0, 0])


def _mem_kv(mem2, w_kv):
    rows = mem2.shape[0]
    return pl.pallas_call(
        _mem_kv_kernel,
        grid=(2, DEPTH),
        in_specs=[pl.BlockSpec((rows, D_MODEL), lambda a, l: (0, 0)),
                  pl.BlockSpec((1, 1, D_MODEL, D_MODEL), lambda a, l: (a, l, 0, 0))],
        out_specs=pl.BlockSpec((1, 1, rows, D_MODEL), lambda a, l: (a, l, 0, 0)),
        out_shape=jax.ShapeDtypeStruct((2, DEPTH, rows, D_MODEL), F32),
        compiler_params=_cparams("parallel", "parallel"),
        name="mem_kv",
    )(mem2, w_kv)


def _cumsum_kernel(x_ref, f_ref, kaug_ref, qaug_ref, carry_ref, *, rows):
    @pl.when(pl.program_id(1) == 0)
    def _():
        carry_ref[...] = jnp.zeros_like(carry_ref)

    r = lax.broadcasted_iota(jnp.int32, (LANES, LANES), 0)
    c = lax.broadcasted_iota(jnp.int32, (LANES, LANES), 1)
    tri = (r >= c).astype(F32)
    grp = c // FOX_HEADS
    carry = carry_ref[...]
    for s in range(rows // LANES):
        sl = slice(s * LANES, (s + 1) * LANES)
        cs = jnp.dot(tri, x_ref[0, sl, :], preferred_element_type=F32,
                     precision=lax.Precision.HIGHEST) + carry
        carry = cs[LANES - 1:LANES, :]
        f = cs * LOG2E
        f_ref[0, sl, :] = f
        hi = f.astype(BF16).astype(F32)
        mid = (f - hi).astype(BF16).astype(F32)
        lo = (f - hi - mid).astype(BF16).astype(F32)
        piece = jnp.where(grp % 3 == 0, hi, jnp.where(grp % 3 == 1, mid, lo))
        kaug_ref[0, sl, :] = jnp.where(grp < 3, -piece, jnp.where(grp < 6, 1.0, 0.0)).astype(BF16)
        qaug_ref[0, sl, :] = jnp.where(grp < 3, 1.0, jnp.where(grp < 6, piece, 0.0)).astype(BF16)
    carry_ref[...] = carry


def _cumsum_time(x, rows):
    b, t, _ = x.shape
    spec = pl.BlockSpec((1, rows, LANES), lambda i, j: (i, j, 0))
    return pl.pallas_call(
        functools.partial(_cumsum_kernel, rows=rows),
        grid=(b, t // rows),
        in_specs=[spec],
        out_specs=[spec] * 3,
        out_shape=[jax.ShapeDtypeStruct(x.shape, F32), jax.ShapeDtypeStruct(x.shape, BF16),
                   jax.ShapeDtypeStruct(x.shape, BF16)],
        scratch_shapes=[pltpu.VMEM((1, LANES), F32)],
        compiler_params=_cparams("parallel", "arbitrary"),
        name="cumsum_time",
    )(x)


def _loop_unrolled(block, n, unroll, tail):
    def body(i, carry):
        for u in range(unroll):
            block(unroll * i + u)
        return carry

    lax.fori_loop(0, n // unroll, body, 0)
    for left in range(unroll):
        @pl.when(n % unroll == left)
        def _(left=left):
            for u in range(left):
                block(n - left + u)
            tail()


def _online_softmax(s, m_old, shift=None):
    row_max = jnp.max(s, axis=1, keepdims=True)
    if shift is not None:
        row_max = row_max + shift
    m_new = jnp.maximum(m_old, row_max)
    m_sub = m_new if shift is None else m_new - shift
    p = jnp.exp2(s - jnp.concatenate([m_sub] * (s.shape[1] // LANES), axis=1))
    return p.astype(BF16), jnp.exp2(m_old - m_new), m_new


def _fox_kernel(q_ref, k_ref, v_ref, qaug_ref, kaug_ref, mask_ref, o_ref,
                qcat_ref, s_ref, p_ref, acc_ref, m_ref, al_ref, *, tq, tk):
    pair = pl.program_id(1)
    j_last = pl.program_id(2)
    lane = lax.broadcasted_iota(jnp.int32, (1, LANES), 1)
    upper = lane >= FOX_DH
    sels = (jnp.logical_not(upper), upper)
    q2 = q_ref[0]
    qa = qaug_ref[0]
    for hh in range(2):
        own = jnp.logical_and(lane % FOX_HEADS == 2 * pair + hh, lane < 6 * FOX_HEADS)
        qcat_ref[hh] = jnp.concatenate([jnp.where(sels[hh], q2, jnp.zeros_like(q2)),
                                        jnp.where(own, qa, jnp.zeros_like(qa))], axis=1)
    m_ref[...] = jnp.full(m_ref.shape, NEG, F32)
    acc_ref[...] = jnp.zeros(acc_ref.shape, F32)
    p_ref[1] = jnp.zeros(p_ref.shape[1:], p_ref.dtype)
    al_ref[1] = jnp.ones(al_ref.shape[1:], F32)

    def scores(hh, j):
        rows = pl.ds(pl.multiple_of(j * tk, tk), tk)
        kcat = jnp.concatenate([k_ref[0, rows, :], kaug_ref[0, rows, :]], axis=1)
        s_ref[hh] = _dot_nt(qcat_ref[hh], kcat)

    def probs(hh, masked):
        s = s_ref[hh]
        if masked:
            s = s + mask_ref[...]
        p_ref[hh], al_ref[hh], m_ref[hh] = _online_softmax(s, m_ref[hh])

    def pv(hh, j):
        vb = v_ref[0, pl.ds(pl.multiple_of(j * tk, tk), tk), :]
        va = jnp.where(sels[hh], vb, jnp.ones_like(vb))
        acc_ref[hh] = al_ref[hh] * acc_ref[hh] + _dot(p_ref[hh], va)

    scores(0, 0)

    def block(j):
        scores(1, j)
        probs(0, False)
        pv(1, jnp.maximum(j - 1, 0))
        scores(0, j + 1)
        probs(1, False)
        pv(0, j)

    def tail():
        scores(1, j_last)
        probs(0, True)
        pv(1, jnp.maximum(j_last - 1, 0))
        probs(1, True)
        pv(0, j_last)
        pv(1, j_last)
        a0 = acc_ref[0]
        a1 = acc_ref[1]
        o0 = a0 / pltpu.roll(a0, FOX_DH, 1)
        o1 = a1 / pltpu.roll(a1, FOX_DH, 1)
        o_ref[0] = jnp.where(upper, o1, o0).astype(o_ref.dtype)

    _loop_unrolled(block, j_last, ATT_UNROLL, tail)


def _fox_attention(qb, kb, vb, qaug, kaug, tq, tk):
    b, t, _ = qb.shape
    assert t % tq == 0 and tq == tk
    ii = np.arange(tq)[:, None]
    jj = np.arange(tk)[None, :]
    mask = jnp.asarray(np.where(jj <= ii, 0.0, NEG), F32)
    return pl.pallas_call(
        functools.partial(_fox_kernel, tq=tq, tk=tk),
        grid=(b, FOX_HEADS // 2, t // tq),
        in_specs=[
            pl.BlockSpec((1, tq, LANES), lambda bi, p, qi: (bi, qi, p)),
            pl.BlockSpec((1, t, LANES), lambda bi, p, qi: (bi, 0, p)),
            pl.BlockSpec((1, t, LANES), lambda bi, p, qi: (bi, 0, p)),
            pl.BlockSpec((1, tq, LANES), lambda bi, p, qi: (bi, qi, 0)),
            pl.BlockSpec((1, t, LANES), lambda bi, p, qi: (bi, 0, 0)),
            pl.BlockSpec((tq, tk), lambda bi, p, qi: (0, 0)),
        ],
        out_specs=pl.BlockSpec((1, tq, LANES), lambda bi, p, qi: (bi, qi, p)),
        out_shape=jax.ShapeDtypeStruct((b, t, FOX_WIDTH), BF16),
        scratch_shapes=[pltpu.VMEM((2, tq, 2 * LANES), BF16),
                        pltpu.VMEM((2, tq, tk), F32),
                        pltpu.VMEM((2, tq, tk), BF16),
                        pltpu.VMEM((2, tq, LANES), F32),
                        pltpu.VMEM((2, tq, LANES), F32),
                        pltpu.VMEM((2, tq, LANES), F32)],
        compiler_params=_cparams("parallel", "parallel", "arbitrary"),
        name="fox_attention",
    )(qb, kb, vb, qaug, kaug, mask)


def _diff_kernel(cfar_ref, q_ref, k_ref, v_ref, tiles_ref, lam_ref, g_ref, o_ref,
                 qm_ref, s_ref, p_ref, acc_ref, m_ref, al_ref, *, tq, tk, lam_init):
    j_last = pl.program_id(2)
    lane = lax.broadcasted_iota(jnp.int32, (1, LANES), 1)
    upper = lane >= DIFF_DH
    q2 = q_ref[0]
    zero = jnp.zeros_like(q2)
    qm_ref[0] = jnp.where(upper, zero, q2)
    qm_ref[1] = jnp.where(upper, q2, zero)
    cfar = cfar_ref[pl.program_id(1)]
    m_ref[...] = jnp.full(m_ref.shape, NEG, F32)
    acc_ref[...] = jnp.zeros(acc_ref.shape, F32)
    p_ref[1] = jnp.zeros(p_ref.shape[1:], p_ref.dtype)
    al_ref[1] = jnp.ones(al_ref.shape[1:], F32)

    def scores(mm, j):
        start = pl.multiple_of(j * tk, tk)
        s_ref[mm] = _dot_nt(qm_ref[mm], k_ref[0, pl.ds(start, tk), :])

    def probs(mm, tile):
        if tile is None:
            out = _online_softmax(s_ref[mm], m_ref[mm], shift=cfar)
        else:
            out = _online_softmax(s_ref[mm] + tiles_ref[0, tile], m_ref[mm])
        p_ref[mm], al_ref[mm], m_ref[mm] = out

    def pv(mm, j):
        start = pl.multiple_of(j * tk, tk)
        vb = v_ref[0, pl.ds(start, tk), :]
        va = jnp.concatenate([vb, jnp.ones_like(vb)], axis=1)
        al = al_ref[mm]
        acc_ref[mm] = jnp.concatenate([al, al], axis=1) * acc_ref[mm] + _dot(p_ref[mm], va)

    def block(j, tile):
        scores(1, j)
        probs(0, tile)
        pv(1, jnp.maximum(j - 1, 0))
        scores(0, j + 1)
        probs(1, tile)
        pv(0, j)

    def tail(with_previous):
        if with_previous:
            block(j_last - 1, 1)
        scores(1, j_last)
        probs(0, 0)
        pv(1, jnp.maximum(j_last - 1, 0))
        probs(1, 0)
        pv(0, j_last)
        pv(1, j_last)
        lam = _diff_lambda(lam_ref[...], lam_init)
        a0 = acc_ref[0]
        a1 = acc_ref[1]
        o = a0[:, :LANES] / a0[:, LANES:] - lam * (a1[:, :LANES] / a1[:, LANES:])
        ms = jnp.mean(o * o, axis=1, keepdims=True)
        o = o * lax.rsqrt(ms + RMS_EPS) * g_ref[...] * (1.0 - lam_init)
        o_ref[0] = o.astype(o_ref.dtype)

    scores(0, 0)

    @pl.when(j_last == 0)
    def _():
        tail(False)

    @pl.when(j_last >= 1)
    def _():
        _loop_unrolled(lambda j: block(j, None), j_last - 1, ATT_UNROLL, lambda: tail(True))


def _t5_bucket(rel):
    nb = REL_BUCKETS // 2
    max_exact = nb // 2
    ret = jnp.where(rel > 0, nb, 0)
    n = jnp.abs(rel)
    nf = jnp.maximum(n, 1).astype(F32)
    large = max_exact + (jnp.log(nf / max_exact) / math.log(REL_MAX_DIST / max_exact)
                         * (nb - max_exact)).astype(jnp.int32)
    large = jnp.minimum(large, nb - 1)
    return ret + jnp.where(n < max_exact, n, large)


def _toeplitz_kernel(w_ref, mask_ref, o_ref, *, tq, width):
    n = w_ref.shape[2]
    for d in range(w_ref.shape[1]):
        w = jnp.broadcast_to(w_ref[0, d:d + 1, :], (tq, n))
        o_ref[0, d] = pltpu.roll(w, 0, 1, stride=1, stride_axis=0)[:, :width] + mask_ref[d]


def _t5_bias_tiles(rel_table, tq, width, offsets, masks):
    nt = len(offsets)
    n = -(-(tq + width) // LANES) * LANES
    m = jnp.arange(n)
    rel = jnp.where(m < width, m, m - n)
    w = jnp.stack([rel_table[_t5_bucket(rel + d)] for d in offsets])
    w = jnp.transpose(w, (2, 0, 1)).astype(F32) * LOG2E
    return pl.pallas_call(
        functools.partial(_toeplitz_kernel, tq=tq, width=width),
        grid=(DIFF_HEADS,),
        in_specs=[pl.BlockSpec((1, nt, n), lambda h: (h, 0, 0)),
                  pl.BlockSpec((nt, tq, width), lambda h: (0, 0, 0))],
        out_specs=pl.BlockSpec((1, nt, tq, width), lambda h: (h, 0, 0, 0)),
        out_shape=jax.ShapeDtypeStruct((DIFF_HEADS, nt, tq, width), F32),
        compiler_params=_cparams("parallel"),
        name="t5_bias_tiles",
    )(w, jnp.asarray(masks, F32))


def _diff_lambda(lp, lam_init):
    return (jnp.exp(jnp.sum(lp[0:1] * lp[1:2], keepdims=True))
            - jnp.exp(jnp.sum(lp[2:3] * lp[3:4], keepdims=True)) + lam_init)


def _diff_attention(qb, kb, vb, rel_table, diff_lambda, subln_g, tq, tk, lam_init):
    b, t, _ = qb.shape
    assert t % tq == 0 and tq == tk and tk >= REL_MAX_DIST and tk % CHUNK == 0
    ii = np.arange(tq)[:, None]
    jj = np.arange(tk)[None, :]
    masks = np.zeros((2, tq, tk), np.float32)
    masks[0] = np.where((jj // CHUNK) <= (ii // CHUNK), 0.0, NEG)
    tiles = _t5_bias_tiles(rel_table, tq, tk, (0, -tk), masks)
    cfar = rel_table[_t5_bucket(jnp.asarray(-2 * REL_MAX_DIST, jnp.int32))].astype(F32) * LOG2E
    return pl.pallas_call(
        functools.partial(_diff_kernel, tq=tq, tk=tk, lam_init=lam_init),
        grid=(b, DIFF_HEADS, t // tq),
        in_specs=[
            pl.BlockSpec(memory_space=pltpu.SMEM),
            pl.BlockSpec((1, tq, LANES), lambda bi, h, qi: (bi, qi, h)),
            pl.BlockSpec((1, t, LANES), lambda bi, h, qi: (bi, 0, h)),
            pl.BlockSpec((1, t, LANES), lambda bi, h, qi: (bi, 0, h)),
            pl.BlockSpec((1, 2, tq, tk), lambda bi, h, qi: (h, 0, 0, 0)),
            pl.BlockSpec((4, DIFF_DH), lambda bi, h, qi: (0, 0)),
            pl.BlockSpec((1, DIFF_VD), lambda bi, h, qi: (0, 0)),
        ],
        out_specs=pl.BlockSpec((1, tq, LANES), lambda bi, h, qi: (bi, qi, h)),
        out_shape=jax.ShapeDtypeStruct((b, t, DIFF_WIDTH), BF16),
        scratch_shapes=[pltpu.VMEM((2, tq, LANES), BF16),
                        pltpu.VMEM((2, tq, tk), F32),
                        pltpu.VMEM((2, tq, tk), BF16),
                        pltpu.VMEM((2, tq, 2 * LANES), F32),
                        pltpu.VMEM((2, tq, LANES), F32),
                        pltpu.VMEM((2, tq, LANES), F32)],
        compiler_params=_cparams("parallel", "parallel", "arbitrary"),
        name="diff_attention",
    )(cfar, qb, kb, vb, tiles, diff_lambda, subln_g.reshape(1, DIFF_VD))


def _two_part_attend(q_h, kt_c, k_n, v_c, v_n, bias_c, bias_n, v_time_minor):
    s_c = _dot(q_h, kt_c.astype(BF16)) + bias_c
    s_n = _dot_nt(q_h, k_n) + bias_n
    m = jnp.maximum(jnp.max(s_c, axis=1, keepdims=True), jnp.max(s_n, axis=1, keepdims=True))
    p_c = jnp.exp2(s_c - m).astype(BF16)
    p_n = jnp.exp2(s_n - m)
    l = jnp.sum(p_c.astype(F32), axis=1, keepdims=True) + jnp.sum(p_n, axis=1, keepdims=True)
    pv_c = _dot_nt(p_c, v_c.astype(BF16)) if v_time_minor else _dot(p_c, v_c.astype(BF16))
    return (pv_c + _dot(p_n.astype(BF16), v_n)) / l


def _fox_decode_kernel(q_ref, kn_ref, vn_ref, kc_ref, vc_ref, fq_ref, ft_ref, o_ref, *, t, past):
    q = q_ref[0]
    kn = kn_ref[0]
    vn = vn_ref[0]
    fblk = fq_ref[0]
    r = lax.broadcasted_iota(jnp.int32, (t, t), 0)
    c = lax.broadcasted_iota(jnp.int32, (t, t), 1)
    causal = jnp.where(c <= r, 0.0, NEG)
    outs = []
    for h in range(FOX_HEADS):
        sl = slice(h * FOX_DH, (h + 1) * FOX_DH)
        fq = fblk[:, h:h + 1]
        fk = ft_ref[0, h:h + 1, :]
        outs.append(_two_part_attend(q[:, sl], kc_ref[0, h], kn[:, sl], vc_ref[0, h], vn[:, sl],
                                     fq - fk[:, :past], fq - fk[:, past:past + t] + causal, True))
    o_ref[0] = jnp.concatenate(outs, axis=1).astype(o_ref.dtype)


def _decode_specs(t, past):
    new = pl.BlockSpec((1, t, 512), lambda bi: (bi, 0, 0))
    cache = pl.BlockSpec((1, 8, 64, past), lambda bi: (bi, 0, 0, 0))
    return new, cache


def _fox_decode(qb, kn, vn, kc, vc, fq, ft, past):
    b, t, _ = qb.shape
    t_kp = ft.shape[2]
    new, cache = _decode_specs(t, past)
    return pl.pallas_call(
        functools.partial(_fox_decode_kernel, t=t, past=past),
        grid=(b,),
        in_specs=[new, new, new, cache, cache,
                  pl.BlockSpec((1, t, FOX_HEADS), lambda bi: (bi, 0, 0)),
                  pl.BlockSpec((1, FOX_HEADS, t_kp), lambda bi: (bi, 0, 0))],
        out_specs=new,
        out_shape=jax.ShapeDtypeStruct((b, t, FOX_WIDTH), BF16),
        compiler_params=_cparams("parallel"),
        name="fox_decode",
    )(qb, kn, vn, kc, vc, fq, ft)


def _diff_decode_kernel(q_ref, kn_ref, vn_ref, kc_ref, vc_ref, bias_ref, lam_ref, g_ref, o_ref,
                        *, t, past, lam_init):
    q = q_ref[0]
    kn = kn_ref[0]
    vn = vn_ref[0]
    lam = _diff_lambda(lam_ref[...], lam_init)
    outs = []
    for h in range(DIFF_HEADS):
        v_c = _head_rows(vc_ref, h)
        v_n = vn[:, h * DIFF_VD:(h + 1) * DIFF_VD]
        bias = bias_ref[h, 0]
        maps = []
        for mm in range(2):
            j = 2 * h + mm
            sl = slice(j * DIFF_DH, (j + 1) * DIFF_DH)
            maps.append(_two_part_attend(q[:, sl], kc_ref[0, j], kn[:, sl], v_c, v_n,
                                         bias[:, :past], bias[:, past:past + t], False))
        a = maps[0] - lam * maps[1]
        ms = jnp.mean(a * a, axis=1, keepdims=True)
        outs.append(a * lax.rsqrt(ms + RMS_EPS) * g_ref[...] * (1.0 - lam_init))
    o_ref[0] = jnp.concatenate(outs, axis=1).astype(o_ref.dtype)


def _diff_decode(qb, kn, vn, kc, vc, rel_table, diff_lambda, subln_g, past, lam_init):
    b, t, _ = qb.shape
    t_kp = -(-(past + t) // LANES) * LANES
    q_pos = past + np.arange(t)[:, None]
    k_pos = np.arange(t_kp)[None, :]
    visible = ((k_pos // CHUNK) <= (q_pos // CHUNK)) & (k_pos < past + t)
    tiles = _t5_bias_tiles(rel_table, t, t_kp, (-past,), np.where(visible, 0.0, NEG)[None])
    new, cache = _decode_specs(t, past)
    vcache = pl.BlockSpec((1, 1, past, DIFF_HEADS, DIFF_VD), lambda bi: (0, bi, 0, 0, 0))
    return pl.pallas_call(
        functools.partial(_diff_decode_kernel, t=t, past=past, lam_init=lam_init),
        grid=(b,),
        in_specs=[new, new, new, cache, vcache,
                  pl.BlockSpec(tiles.shape, lambda bi: (0, 0, 0, 0)),
                  pl.BlockSpec((4, DIFF_DH), lambda bi: (0, 0)),
                  pl.BlockSpec((1, DIFF_VD), lambda bi: (0, 0))],
        out_specs=new,
        out_shape=jax.ShapeDtypeStruct((b, t, DIFF_WIDTH), BF16),
        compiler_params=_cparams("parallel"),
        name="diff_decode",
    )(qb, kn, vn, kc, vc, tiles, diff_lambda, subln_g.reshape(1, DIFF_VD))


def _fill_history(ext_ref, cur, hist_ref, prev_ref):
    ext_ref[:, 0:HIST_ROWS, :] = jnp.where(pl.program_id(1) == 0, hist_ref[...], prev_ref[...])
    ext_ref[:, HIST_ROWS:, :] = cur


def _halo_specs(bb, tm, width):
    cur = pl.BlockSpec((bb, tm, width), lambda b, i: (b, i, 0))
    prev = pl.BlockSpec((bb, HIST_ROWS, width),
                        lambda b, i: (b, jnp.maximum(i * (tm // HIST_ROWS) - 1, 0), 0))
    hist = pl.BlockSpec((bb, HIST_ROWS, width), lambda b, i: (b, 0, 0))
    return cur, prev, hist


def _mix_out(o_ref, x_ref, mix, g_ref, beta_ref):
    bb, tm, d = x_ref.shape
    y = DN_ALPHA * x_ref[...].reshape(bb * tm, d) + mix
    o_ref[...] = _layer_norm(y, g_ref[...], beta_ref[...]).reshape(bb, tm, d)


def _mix_even_kernel(u_ref, prev_ref, hist_ref, wmix_ref, scale_ref, att_ref, w_ref, x_ref,
                     g_ref, beta_ref, o_ref, ext_ref, *, past):
    bb, tm, _ = u_ref.shape
    u = u_ref[...]
    _fill_history(ext_ref, u, hist_ref, prev_ref)
    pos = past + pl.program_id(1) * tm + lax.broadcasted_iota(jnp.int32, (1, tm, 1), 1)
    groups = []
    for g, w in enumerate(POOL_WINDOWS):
        sl = slice(g * POOL_GC, (g + 1) * POOL_GC)
        ug = u[:, :, sl]
        wsum = ug
        for s in range(1, w):
            wsum = wsum + ext_ref[:, HIST_ROWS - s:HIST_ROWS - s + tm, sl]
        cnt = jnp.minimum(w, pos + 1).astype(F32)
        d = (wsum / cnt - ug).reshape(bb * tm, POOL_GC)
        groups.append((_dot(d.astype(BF16), wmix_ref[g]) * scale_ref[:, sl]).astype(BF16))
    pool_y = jnp.concatenate(groups, axis=1)
    att = att_ref[...].reshape(bb * tm, FOX_WIDTH)
    mix = _dot(pool_y, w_ref[0:POOL_WIDTH, :]) + _dot(att, w_ref[POOL_WIDTH:, :])
    _mix_out(o_ref, x_ref, mix, g_ref, beta_ref)


def _mix_odd_kernel(z_ref, prev_ref, hist_ref, bg_ref, cw_ref, att_ref, w_ref, x_ref,
                    g_ref, beta_ref, o_ref, ext_ref):
    bb, tm, _ = z_ref.shape
    z = z_ref[...]
    _fill_history(ext_ref, z, hist_ref, prev_ref)
    y = (ext_ref[:, HIST_ROWS - 2:HIST_ROWS - 2 + tm, :] * cw_ref[0:1, :]
         + ext_ref[:, HIST_ROWS - 1:HIST_ROWS - 1 + tm, :] * cw_ref[1:2, :]
         + z * cw_ref[2:3, :])
    conv_y = (bg_ref[...] * y).astype(BF16).reshape(bb * tm, CONV_CH)
    att = att_ref[...].reshape(bb * tm, DIFF_WIDTH)
    mix = _dot(att, w_ref[0:DIFF_WIDTH, :]) + _dot(conv_y, w_ref[DIFF_WIDTH:, :])
    _mix_out(o_ref, x_ref, mix, g_ref, beta_ref)


def _mix_tiles(b, t):
    tm = min(512, t)
    return max(1, min(b, 512 // tm)), tm


def _mix_ln(kernel_fn, name, local_ins, local_specs, att3, w, x3, g, beta, width):
    b, t, _ = x3.shape
    bb, tm = _mix_tiles(b, t)
    cur = lambda n: pl.BlockSpec((bb, tm, n), lambda bi, i: (bi, i, 0))
    vec = pl.BlockSpec((1, D_MODEL), lambda bi, i: (0, 0))
    return pl.pallas_call(
        kernel_fn,
        grid=(b // bb, t // tm),
        in_specs=local_specs + [cur(512), pl.BlockSpec(w.shape, lambda bi, i: (0, 0)),
                                cur(D_MODEL), vec, vec],
        out_specs=cur(D_MODEL),
        out_shape=jax.ShapeDtypeStruct(x3.shape, F32),
        scratch_shapes=[pltpu.VMEM((bb, HIST_ROWS + tm, width), F32)],
        compiler_params=_cparams("parallel", "parallel"),
        name=name,
    )(*local_ins, att3, w, x3, g.reshape(1, D_MODEL), beta.reshape(1, D_MODEL))


def _mix_even_ln(u3, hist16, wmix, scale, past, fox3, w, x3, g, beta):
    cur, prev, hist = _halo_specs(*_mix_tiles(*u3.shape[:2]), POOL_WIDTH)
    specs = [cur, prev, hist, pl.BlockSpec(wmix.shape, lambda bi, i: (0, 0, 0)),
             pl.BlockSpec((1, POOL_WIDTH), lambda bi, i: (0, 0))]
    return _mix_ln(functools.partial(_mix_even_kernel, past=past), "mix_even_ln",
                   [u3, u3, hist16, wmix, scale.reshape(1, POOL_WIDTH)], specs,
                   fox3, w, x3, g, beta, POOL_WIDTH)


def _mix_odd_ln(z3, hist16, bg3, conv_w, att3, w, x3, g, beta):
    cur, prev, hist = _halo_specs(*_mix_tiles(*z3.shape[:2]), CONV_CH)
    specs = [cur, prev, hist, cur, pl.BlockSpec((CONV_K, CONV_CH), lambda bi, i: (0, 0))]
    return _mix_ln(_mix_odd_kernel, "mix_odd_ln",
                   [z3, z3, hist16, bg3, conv_w], specs, att3, w, x3, g, beta, CONV_CH)


def _head_rows(c_ref, h, bi=0):
    _, bb, n, heads, d = c_ref.shape
    return c_ref.reshape(bb * n * heads, d)[pl.ds(bi * n * heads + h, n, stride=heads), :]


def _mem_kernel(x_ref, wq_ref, mk0_ref, mk1_ref, mv0_ref, mv1_ref, wo_ref, g_ref, beta_ref, o_ref,
                kb_ref, vb_ref):
    bb, tm, _ = x_ref.shape

    @pl.when(pl.program_id(1) == 0)
    def _():
        for bi in range(bb):
            for h in range(MEM_HEADS):
                kb_ref[bi, h] = jnp.concatenate(
                    [_head_rows(mk0_ref, h, bi), _head_rows(mk1_ref, h, bi)], axis=1).astype(BF16)
                vb_ref[bi, h] = jnp.concatenate(
                    [_head_rows(mv0_ref, h, bi), _head_rows(mv1_ref, h, bi)], axis=1).astype(BF16)

    x = x_ref[...].reshape(bb * tm, D_MODEL)
    q = _dot(x.astype(BF16), wq_ref[0])
    qb = (q * (MEM_DH ** -0.5)).astype(BF16)
    rows = []
    for bi in range(bb):
        outs = []
        for h in range(MEM_HEADS):
            s = _dot_nt(qb[bi * tm:(bi + 1) * tm, h * MEM_DH:(h + 1) * MEM_DH], kb_ref[bi, h])
            m = jnp.max(s, axis=1, keepdims=True)
            p = jnp.exp(s - m)
            l = jnp.sum(p, axis=1, keepdims=True)
            o = _dot(p.astype(BF16), vb_ref[bi, h]) / l
            outs.append(o.astype(BF16))
        rows.append(jnp.concatenate(outs, axis=1))
    o_all = rows[0] if bb == 1 else jnp.concatenate(rows, axis=0)
    y = DN_ALPHA * x + _dot(o_all, wo_ref[0])
    o_ref[...] = _layer_norm(y, g_ref[...], beta_ref[...]).reshape(bb, tm, D_MODEL)


def _mem_attend_ln(x3, wq, mk, mv, layer, wo, g, beta):
    b, t, _ = x3.shape
    tm = min(MEM_ROW_TILE, t)
    bb = max(1, min(b, MEM_ROWS_MIN // tm))
    assert b % bb == 0 and MEM_DH == 2 * LANES
    xs = pl.BlockSpec((bb, tm, D_MODEL), lambda bi, i: (bi, i, 0))
    ws = pl.BlockSpec((1, D_MODEL, D_MODEL), lambda bi, i: (layer, 0, 0))
    half = lambda c: pl.BlockSpec((1, bb, N_MEM, MEM_HEADS, LANES), lambda bi, i: (layer, bi, 0, 0, c))
    vec = pl.BlockSpec((1, D_MODEL), lambda bi, i: (0, 0))
    return pl.pallas_call(
        _mem_kernel,
        grid=(b // bb, t // tm),
        in_specs=[xs, ws, half(0), half(1), half(0), half(1), ws, vec, vec],
        out_specs=xs,
        out_shape=jax.ShapeDtypeStruct(x3.shape, F32),
        scratch_shapes=[pltpu.VMEM((bb, MEM_HEADS, N_MEM, MEM_DH), BF16),
                        pltpu.VMEM((bb, MEM_HEADS, N_MEM, MEM_DH), BF16)],
        compiler_params=_cparams("parallel", "arbitrary"),
        name="mem_attend_ln",
    )(x3, wq, mk, mk, mv, mv, wo, g.reshape(1, D_MODEL), beta.reshape(1, D_MODEL))


def _ffn_kernel(x_ref, w1_ref, w2_ref, g_ref, beta_ref, o_ref, *, chunk):
    x = x_ref[...]
    xb = x.astype(BF16)
    acc = jnp.zeros(x.shape, F32)
    for c in range(D_FF // chunk):
        h = _dot(xb, w1_ref[0, :, c * chunk:(c + 1) * chunk])
        h = jnp.square(jnp.maximum(h, 0.0))
        acc = acc + _dot(h.astype(BF16), w2_ref[0, c * chunk:(c + 1) * chunk, :])
    o_ref[...] = _layer_norm(DN_ALPHA * x + acc, g_ref[...], beta_ref[...])


def _ffn_ln(x2, w1, w2, layer, g, beta):
    rows = x2.shape[0]
    tm = _row_tile(rows)
    row = pl.BlockSpec((tm, D_MODEL), lambda i: (i, 0))
    vec = pl.BlockSpec((1, D_MODEL), lambda i: (0, 0))
    once = pl.Buffered(1)
    return pl.pallas_call(
        functools.partial(_ffn_kernel, chunk=1024),
        grid=(rows // tm,),
        in_specs=[row,
                  pl.BlockSpec((1,) + w1.shape[1:], lambda i: (layer, 0, 0), pipeline_mode=once),
                  pl.BlockSpec((1,) + w2.shape[1:], lambda i: (layer, 0, 0), pipeline_mode=once),
                  vec, vec],
        out_specs=row,
        out_shape=jax.ShapeDtypeStruct((rows, D_MODEL), F32),
        compiler_params=_cparams("parallel"),
        name="ffn_ln",
    )(x2, w1, w2, g.reshape(1, D_MODEL), beta.reshape(1, D_MODEL))


def _pad_rows(a, total):
    pad = total - a.shape[1]
    if pad == 0:
        return a
    return jnp.pad(a, ((0, 0), (0, pad)) + ((0, 0),) * (a.ndim - 2))


def _hist16(h):
    return jnp.pad(h, ((0, 0), (HIST_ROWS - h.shape[1], 0), (0, 0)))


def _trunk(x, mem_k, mem_v, pool_h, fk_h, fv_h, flf_h, dk_h, dv_h, conv_h, past, wts):
    b, t, _ = x.shape
    rows = b * t
    t_k = past + t
    if past == 0:
        t_kp = t_k
        cs_rows = 1024
    else:
        t_kp = -(-t_k // LANES) * LANES
        cs_rows = t_kp
    x2 = x.reshape(rows, D_MODEL)

    u, qb, k, kb, v, vb, lf = _proj_even(x2, wts["w_in_even"], wts["w_forget"], wts["b_forget"], t)
    lf3 = lf.reshape(b, t, LANES)
    qb3 = qb.reshape(b, t, FOX_WIDTH)
    kb3 = kb.reshape(b, t, FOX_WIDTH)
    vb3 = vb.reshape(b, t, FOX_WIDTH)
    if past:
        groups = b * FOX_HEADS // LANES
        assert groups * LANES == b * FOX_HEADS
        lf_all = jnp.concatenate([flf_h[0], lf3[:, :, :FOX_HEADS]], axis=1)
        lf_t = jnp.transpose(lf_all, (1, 0, 2)).reshape(t_k, groups, LANES)
        fcum = _cumsum_time(_pad_rows(jnp.transpose(lf_t, (1, 0, 2)), t_kp), cs_rows)[0]
        f_tbh = jnp.transpose(fcum, (1, 0, 2)).reshape(t_kp, b, FOX_HEADS)
        fox_y = _fox_decode(qb3, kb3, vb3, jnp.transpose(fk_h[0], (0, 2, 3, 1)),
                            jnp.transpose(fv_h[0], (0, 2, 3, 1)),
                            jnp.transpose(f_tbh[past:past + t], (1, 0, 2)),
                            jnp.transpose(f_tbh, (1, 2, 0)), past)
    else:
        _, kaug, qaug = _cumsum_time(lf3, cs_rows)
        fox_y = _fox_attention(qb3, kb3, vb3, qaug, kaug, ATT_TILE, ATT_TILE)
    u3 = u.reshape(b, t, POOL_WIDTH)
    x3 = _mix_even_ln(u3, _hist16(pool_h[0]), wts["w_pool_mix"], wts["pool_scale"], past, fox_y,
                      wts["w_out_even"], x2.reshape(b, t, D_MODEL), wts["ln_g"][0, 0], wts["ln_b"][0, 0])
    x2 = _mem_attend_ln(x3, wts["w_mem_q"], mem_k, mem_v, 0,
                        wts["w_mem_o"], wts["ln_g"][0, 1], wts["ln_b"][0, 1]).reshape(rows, D_MODEL)
    x2 = _ffn_ln(x2, wts["w_ff1"], wts["w_ff2"], 0, wts["ln_g"][0, 2], wts["ln_b"][0, 2])
    n_pool = u3[:, t - POOL_HIST:][None]
    n_fk = _cache_logical(k, b, t, (FOX_HEADS, FOX_DH))
    n_fv = _cache_logical(v, b, t, (FOX_HEADS, FOX_DH))
    n_flf = lf3[:, :, :FOX_HEADS][None]

    lam_init = 0.8 - 0.6 * math.exp(-0.3 * 1)
    qb, k, kb, v, vb, bg, z = _proj_odd(x2, wts["w_in_odd"], t)
    qb3 = qb.reshape(b, t, DIFF_QK)
    kb3 = kb.reshape(b, t, DIFF_QK)
    vb3 = vb.reshape(b, t, DIFF_WIDTH)
    if past:
        att = _diff_decode(qb3, kb3, vb3,
                           jnp.transpose(dk_h[0].reshape(b, past, 2 * DIFF_HEADS, DIFF_DH), (0, 2, 3, 1)),
                           dv_h, wts["rel_bias_table"],
                           wts["diff_lambda"], wts["diff_subln_g"], past, lam_init)
    else:
        att = _diff_attention(qb3, kb3, vb3, wts["rel_bias_table"], wts["diff_lambda"],
                              wts["diff_subln_g"], ATT_TILE, ATT_TILE, lam_init)
    z3 = z.reshape(b, t, CONV_CH)
    x3 = _mix_odd_ln(z3, _hist16(conv_h[0]), bg.reshape(b, t, CONV_CH), wts["conv_w"], att,
                     wts["w_out_odd"], x2.reshape(b, t, D_MODEL), wts["ln_g"][1, 0], wts["ln_b"][1, 0])
    x2 = _mem_attend_ln(x3, wts["w_mem_q"], mem_k, mem_v, 1,
                        wts["w_mem_o"], wts["ln_g"][1, 1], wts["ln_b"][1, 1]).reshape(rows, D_MODEL)
    x2 = _ffn_ln(x2, wts["w_ff1"], wts["w_ff2"], 1, wts["ln_g"][1, 2], wts["ln_b"][1, 2])
    n_dk = _cache_logical(k, b, t, (DIFF_HEADS, 2, DIFF_DH))
    n_dv = v.reshape(1, b, t, DIFF_HEADS, DIFF_VD)
    n_conv = z3[:, t - (CONV_K - 1):][None]
    return (x2.reshape(b, t, D_MODEL), n_pool, n_fk, n_fv, n_flf, n_dk, n_dv, n_conv)


def kernel(x_prompt, x_sample, state_pool, cache_fox_k, cache_fox_v, cache_fox_logf,
           cache_diff_k, cache_diff_v, state_conv, cache_mem_k, cache_mem_v, mem_prompt,
           w_in_even, b_forget, w_pool_mix, pool_scale, w_out_even,
           w_in_odd, diff_lambda, diff_subln_g, conv_w, w_out_odd, rel_bias_table,
           w_mem_q, w_mem_k, w_mem_v, w_mem_o, w_ff1, w_ff2, ln_g, ln_b):
    bp = x_prompt.shape[0]
    nmain = POOL_WIDTH + 3 * FOX_WIDTH
    wts = {
        "w_in_even": w_in_even[0, :, :nmain].astype(BF16),
        "w_forget": jnp.pad(jnp.tile(w_in_even[0, :, nmain:], (1, 6)),
                            ((0, 0), (0, LANES - 6 * FOX_HEADS))).astype(BF16),
        "b_forget": jnp.pad(jnp.tile(b_forget[0], 6), (0, LANES - 6 * FOX_HEADS)).reshape(1, LANES).astype(F32),
        "w_pool_mix": w_pool_mix[0].astype(BF16),
        "pool_scale": pool_scale[0],
        "w_out_even": w_out_even[0].astype(BF16),
        "w_in_odd": w_in_odd[0].astype(BF16),
        "diff_lambda": diff_lambda[0],
        "diff_subln_g": diff_subln_g[0],
        "conv_w": conv_w[0],
        "w_out_odd": w_out_odd[0].astype(BF16),
        "rel_bias_table": rel_bias_table,
        "w_mem_q": w_mem_q.astype(BF16),
        "w_mem_o": w_mem_o.astype(BF16),
        "w_ff1": w_ff1.astype(BF16),
        "w_ff2": w_ff2.astype(BF16),
        "ln_g": ln_g,
        "ln_b": ln_b,
    }
    kv = _mem_kv(mem_prompt.reshape(bp * N_MEM, D_MODEL),
                 jnp.stack([w_mem_k, w_mem_v]).astype(BF16))
    kv = kv.reshape(2, DEPTH, bp, N_MEM, D_MODEL)
    p_mem_k = kv[0].reshape(DEPTH, bp, N_MEM, MEM_HEADS, MEM_DH)
    p_mem_v = kv[1].reshape(DEPTH, bp, N_MEM, MEM_HEADS, MEM_DH)
    zeros = lambda *s: jnp.zeros(s, F32)
    (y_prompt, p_pool, p_fox_k, p_fox_v, p_fox_logf, p_diff_k, p_diff_v, p_conv) = _trunk(
        x_prompt, p_mem_k, p_mem_v,
        zeros(1, bp, POOL_HIST, POOL_WIDTH), None, None, None, None, None,
        zeros(1, bp, CONV_K - 1, CONV_CH), 0, wts)
    bs = x_sample.shape[0]
    (y_sample, s_pool, s_fox_k, s_fox_v, s_fox_logf, s_diff_k, s_diff_v, s_conv) = _trunk(
        x_sample, cache_mem_k, cache_mem_v,
        state_pool, cache_fox_k, cache_fox_v, cache_fox_logf,
        cache_diff_k, cache_diff_v, state_conv, cache_fox_k.shape[2], wts)
    return (y_prompt, y_sample,
            p_pool, p_fox_k, p_fox_v, p_fox_logf, p_diff_k, p_diff_v, p_conv, p_mem_k, p_mem_v,
            s_pool, s_fox_k, s_fox_v, s_fox_logf, s_diff_k, s_diff_v, s_conv)
```

```python
import functools
import math

import numpy as np
import jax
import jax.numpy as jnp
from jax import lax
from jax.experimental import pallas as pl
from jax.experimental.pallas import tpu as pltpu

F32 = jnp.float32
BF16 = jnp.bfloat16

D_MODEL = 1024
DEPTH = 2
CHUNK = 64
POOL_WIDTH = 512
POOL_GC = 128
POOL_WINDOWS = (2, 4, 8, 16)
POOL_HIST = 15
FOX_HEADS = 8
FOX_DH = 64
FOX_WIDTH = 512
DIFF_HEADS = 4
DIFF_DH = 64
DIFF_VD = 128
DIFF_QK = 512
DIFF_WIDTH = 512
CONV_CH = 512
CONV_K = 3
D_FF = 4096
N_MEM = 256
MEM_HEADS = 4
MEM_DH = 256
REL_BUCKETS = 32
REL_MAX_DIST = 128
DN_ALPHA = (2 * DEPTH) ** 0.25
LN_EPS = 1e-5
RMS_EPS = 1e-5
NEG = -1e30
LOG2E = math.log2(math.e)

LANES = 128
HIST_ROWS = 16
ATT_TILE = 512
ATT_UNROLL = 4
MEM_ROW_TILE = 1024
MEM_ROWS_MIN = 64
VMEM_LIMIT = 56 * 1024 * 1024


def _cparams(*sem):
    return pltpu.CompilerParams(dimension_semantics=sem, vmem_limit_bytes=VMEM_LIMIT)


def _dot(a, b):
    return jnp.dot(a, b, preferred_element_type=F32)


def _dot_nt(a, b):
    return lax.dot_general(a, b, (((1,), (1,)), ((), ())), preferred_element_type=F32)


def _layer_norm(y, g, b):
    mu = jnp.mean(y, axis=-1, keepdims=True)
    d = y - mu
    var = jnp.mean(d * d, axis=-1, keepdims=True)
    return d * lax.rsqrt(var + LN_EPS) * g + b


def _row_tile(rows):
    return min(512, rows)


def _store_cache(o_ref, y):
    if o_ref.shape[0] == 1:
        o_ref[0] = y.T
        return
    cols = []
    for h in range(o_ref.shape[1]):
        c = y[:, (h // 2) * LANES:(h // 2 + 1) * LANES]
        cols.append(pltpu.roll(c, LANES // 2, 1) if h % 2 else c)
    t = jnp.swapaxes(jnp.stack(cols, axis=0), 0, 1)
    o_ref[...] = t[:, :, :o_ref.shape[2]]


def _cache_out(rows, tm, seq):
    if seq % tm == 0 and tm % LANES == 0:
        nb = seq // tm
        return (pl.BlockSpec((1, 512, tm), lambda i: (i // nb, 0, i % nb)),
                jax.ShapeDtypeStruct((rows // seq, 512, seq), F32))
    return (pl.BlockSpec((tm, 8, 64), lambda i: (i, 0, 0)),
            jax.ShapeDtypeStruct((rows, 8, 64), F32))


def _cache_logical(c, b, t, tail):
    if c.shape[0] == b and c.shape[1] == 512:
        c = jnp.transpose(c.reshape((b,) + tail + (t,)), (0, len(tail) + 1) + tuple(range(1, len(tail) + 1)))
        return c[None]
    return c.reshape((1, b, t) + tail)


def _proj_even_kernel(x_ref, w_ref, wf_ref, bf_ref,
                      u_ref, q_ref, k_ref, kb_ref, v_ref, vb_ref, lf_ref):
    xb = x_ref[...].astype(BF16)

    def mm(c):
        return _dot(xb, w_ref[:, c * 512:(c + 1) * 512])

    u_ref[...] = mm(0)
    q_ref[...] = (mm(1) * (FOX_DH ** -0.5 * LOG2E)).astype(BF16)
    k = mm(2)
    _store_cache(k_ref, k)
    kb_ref[...] = k.astype(BF16)
    v = mm(3)
    _store_cache(v_ref, v)
    vb_ref[...] = v.astype(BF16)
    z = _dot(xb, wf_ref[...]) + bf_ref[...]
    lf_ref[...] = jnp.minimum(z, 0.0) - jnp.log1p(jnp.exp(-jnp.abs(z)))


def _proj_even(x2, w, wf, bf, seq):
    rows = x2.shape[0]
    tm = _row_tile(rows)
    row = lambda n: pl.BlockSpec((tm, n), lambda i: (i, 0))
    full = lambda a: pl.BlockSpec(a.shape, lambda i: (0,) * a.ndim)
    f32o = jax.ShapeDtypeStruct((rows, 512), F32)
    bf16o = jax.ShapeDtypeStruct((rows, 512), BF16)
    heads, headso = _cache_out(rows, tm, seq)
    return pl.pallas_call(
        _proj_even_kernel,
        grid=(rows // tm,),
        in_specs=[row(D_MODEL), full(w), full(wf), full(bf)],
        out_specs=[row(512), row(512), heads, row(512), heads, row(512), row(LANES)],
        out_shape=[f32o, bf16o, headso, bf16o, headso, bf16o,
                   jax.ShapeDtypeStruct((rows, LANES), F32)],
        compiler_params=_cparams("parallel"),
        name="proj_even",
    )(x2, w, wf, bf)


def _proj_odd_kernel(x_ref, w_ref, q_ref, k_ref, kb_ref, v_ref, vb_ref, bg_ref, z_ref):
    xb = x_ref[...].astype(BF16)

    def mm(c):
        return _dot(xb, w_ref[:, c * 512:(c + 1) * 512])

    q_ref[...] = (mm(0) * (DIFF_DH ** -0.5 * LOG2E)).astype(BF16)
    k = mm(1)
    _store_cache(k_ref, k)
    kb_ref[...] = k.astype(BF16)
    v = mm(2)
    tm = v.shape[0]
    v_rows = v_ref.reshape(tm * DIFF_HEADS, DIFF_VD)
    for h in range(DIFF_HEADS):
        v_rows[pl.ds(h, tm, stride=DIFF_HEADS), :] = v[:, h * DIFF_VD:(h + 1) * DIFF_VD]
    vb_ref[...] = v.astype(BF16)
    bg_ref[...] = mm(3)
    z_ref[...] = mm(4) * mm(5)


def _proj_odd(x2, w, seq):
    rows = x2.shape[0]
    tm = _row_tile(rows)
    kspec, kshape = _cache_out(rows, tm, seq)
    row = lambda n: pl.BlockSpec((tm, n), lambda i: (i, 0))
    f32o = jax.ShapeDtypeStruct((rows, 512), F32)
    bf16o = jax.ShapeDtypeStruct((rows, 512), BF16)
    return pl.pallas_call(
        _proj_odd_kernel,
        grid=(rows // tm,),
        in_specs=[row(D_MODEL), pl.BlockSpec(w.shape, lambda i: (0, 0))],
        out_specs=[row(512), kspec, row(512),
                   pl.BlockSpec((tm, DIFF_HEADS, DIFF_VD), lambda i: (i, 0, 0))] + [row(512)] * 3,
        out_shape=[bf16o, kshape, bf16o,
                   jax.ShapeDtypeStruct((rows, DIFF_HEADS, DIFF_VD), F32), bf16o, f32o, f32o],
        compiler_params=_cparams("parallel"),
        name="proj_odd",
    )(x2, w)


def _mem_kv_kernel(x_ref, w_ref, o_ref):
    o_ref[0, 0] = _dot(x_ref[...].astype(BF16), w_ref[0, 0])


def _mem_kv(mem2, w_kv):
    rows = mem2.shape[0]
    return pl.pallas_call(
        _mem_kv_kernel,
        grid=(2, DEPTH),
        in_specs=[pl.BlockSpec((rows, D_MODEL), lambda a, l: (0, 0)),
                  pl.BlockSpec((1, 1, D_MODEL, D_MODEL), lambda a, l: (a, l, 0, 0))],
        out_specs=pl.BlockSpec((1, 1, rows, D_MODEL), lambda a, l: (a, l, 0, 0)),
        out_shape=jax.ShapeDtypeStruct((2, DEPTH, rows, D_MODEL), F32),
        compiler_params=_cparams("parallel", "parallel"),
        name="mem_kv",
    )(mem2, w_kv)


def _cumsum_kernel(x_ref, f_ref, kaug_ref, qaug_ref, carry_ref, *, rows):
    @pl.when(pl.program_id(1) == 0)
    def _():
        carry_ref[...] = jnp.zeros_like(carry_ref)

    r = lax.broadcasted_iota(jnp.int32, (LANES, LANES), 0)
    c = lax.broadcasted_iota(jnp.int32, (LANES, LANES), 1)
    tri = (r >= c).astype(F32)
    grp = c // FOX_HEADS
    carry = carry_ref[...]
    for s in range(rows // LANES):
        sl = slice(s * LANES, (s + 1) * LANES)
        cs = jnp.dot(tri, x_ref[0, sl, :], preferred_element_type=F32,
                     precision=lax.Precision.HIGHEST) + carry
        carry = cs[LANES - 1:LANES, :]
        f = cs * LOG2E
        f_ref[0, sl, :] = f
        hi = f.astype(BF16).astype(F32)
        mid = (f - hi).astype(BF16).astype(F32)
        lo = (f - hi - mid).astype(BF16).astype(F32)
        piece = jnp.where(grp % 3 == 0, hi, jnp.where(grp % 3 == 1, mid, lo))
        kaug_ref[0, sl, :] = jnp.where(grp < 3, -piece, jnp.where(grp < 6, 1.0, 0.0)).astype(BF16)
        qaug_ref[0, sl, :] = jnp.where(grp < 3, 1.0, jnp.where(grp < 6, piece, 0.0)).astype(BF16)
    carry_ref[...] = carry


def _cumsum_time(x, rows):
    b, t, _ = x.shape
    spec = pl.BlockSpec((1, rows, LANES), lambda i, j: (i, j, 0))
    return pl.pallas_call(
        functools.partial(_cumsum_kernel, rows=rows),
        grid=(b, t // rows),
        in_specs=[spec],
        out_specs=[spec] * 3,
        out_shape=[jax.ShapeDtypeStruct(x.shape, F32), jax.ShapeDtypeStruct(x.shape, BF16),
                   jax.ShapeDtypeStruct(x.shape, BF16)],
        scratch_shapes=[pltpu.VMEM((1, LANES), F32)],
        compiler_params=_cparams("parallel", "arbitrary"),
        name="cumsum_time",
    )(x)


def _loop_unrolled(head, block, n, unroll, tail):
    def body(i, carry):
        for u in range(unroll):
            block(unroll * i + u)
        return carry

    @pl.when(n >= unroll)
    def _():
        head()
        body(0, 0)

    @pl.when(n < unroll)
    def _():
        head()

    lax.fori_loop(1, n // unroll, body, 0)
    for left in range(unroll):
        @pl.when(n % unroll == left)
        def _(left=left):
            for u in range(left):
                block(n - left + u)
            tail()


def _online_softmax(s, m_old, shift=None):
    row_max = jnp.max(s, axis=1, keepdims=True)
    if shift is not None:
        row_max = row_max + shift
    m_new = jnp.maximum(m_old, row_max)
    m_sub = m_new if shift is None else m_new - shift
    p = jnp.exp2(s - jnp.concatenate([m_sub] * (s.shape[1] // LANES), axis=1))
    return p.astype(BF16), jnp.exp2(m_old - m_new), m_new


def _fox_kernel(q_ref, k_ref, v_ref, qaug_ref, kaug_ref, mask_ref, o_ref,
                qcat_ref, s_ref, p_ref, acc_ref, m_ref, al_ref, *, tq, tk):
    pair = pl.program_id(1)
    j_last = pl.program_id(2)
    lane = lax.broadcasted_iota(jnp.int32, (1, LANES), 1)
    upper = lane >= FOX_DH
    sels = (jnp.logical_not(upper), upper)

    def head():
        q2 = q_ref[0]
        qa = qaug_ref[0]
        for hh in range(2):
            own = jnp.logical_and(lane % FOX_HEADS == 2 * pair + hh, lane < 6 * FOX_HEADS)
            qcat_ref[hh] = jnp.concatenate([jnp.where(sels[hh], q2, jnp.zeros_like(q2)),
                                            jnp.where(own, qa, jnp.zeros_like(qa))], axis=1)
        m_ref[...] = jnp.full(m_ref.shape, NEG, F32)
        acc_ref[...] = jnp.zeros(acc_ref.shape, F32)
        p_ref[1] = jnp.zeros(p_ref.shape[1:], p_ref.dtype)
        al_ref[1] = jnp.ones(al_ref.shape[1:], F32)
        scores(0, 0)

    def scores(hh, j):
        rows = pl.ds(pl.multiple_of(j * tk, tk), tk)
        kcat = jnp.concatenate([k_ref[0, rows, :], kaug_ref[0, rows, :]], axis=1)
        s_ref[hh] = _dot_nt(qcat_ref[hh], kcat)

    def probs(hh, masked):
        s = s_ref[hh]
        if masked:
            s = s + mask_ref[...]
        p_ref[hh], al_ref[hh], m_ref[hh] = _online_softmax(s, m_ref[hh])

    def pv(hh, j):
        vb = v_ref[0, pl.ds(pl.multiple_of(j * tk, tk), tk), :]
        va = jnp.where(sels[hh], vb, jnp.ones_like(vb))
        acc_ref[hh] = al_ref[hh] * acc_ref[hh] + _dot(p_ref[hh], va)

    def block(j):
        scores(1, j)
        probs(0, False)
        pv(1, jnp.maximum(j - 1, 0))
        scores(0, j + 1)
        probs(1, False)
        pv(0, j)

    def tail():
        scores(1, j_last)
        probs(0, True)
        pv(1, jnp.maximum(j_last - 1, 0))
        probs(1, True)
        pv(0, j_last)
        pv(1, j_last)
        a0 = acc_ref[0]
        a1 = acc_ref[1]
        o0 = a0 / pltpu.roll(a0, FOX_DH, 1)
        o1 = a1 / pltpu.roll(a1, FOX_DH, 1)
        o_ref[0] = jnp.where(upper, o1, o0).astype(o_ref.dtype)

    _loop_unrolled(head, block, j_last, ATT_UNROLL, tail)


def _fox_attention(qb, kb, vb, qaug, kaug, tq, tk):
    b, t, _ = qb.shape
    assert t % tq == 0 and tq == tk
    ii = np.arange(tq)[:, None]
    jj = np.arange(tk)[None, :]
    mask = jnp.asarray(np.where(jj <= ii, 0.0, NEG), F32)
    return pl.pallas_call(
        functools.partial(_fox_kernel, tq=tq, tk=tk),
        grid=(b, FOX_HEADS // 2, t // tq),
        in_specs=[
            pl.BlockSpec((1, tq, LANES), lambda bi, p, qi: (bi, qi, p)),
            pl.BlockSpec((1, t, LANES), lambda bi, p, qi: (bi, 0, p)),
            pl.BlockSpec((1, t, LANES), lambda bi, p, qi: (bi, 0, p)),
            pl.BlockSpec((1, tq, LANES), lambda bi, p, qi: (bi, qi, 0)),
            pl.BlockSpec((1, t, LANES), lambda bi, p, qi: (bi, 0, 0)),
            pl.BlockSpec((tq, tk), lambda bi, p, qi: (0, 0)),
        ],
        out_specs=pl.BlockSpec((1, tq, LANES), lambda bi, p, qi: (bi, qi, p)),
        out_shape=jax.ShapeDtypeStruct((b, t, FOX_WIDTH), BF16),
        scratch_shapes=[pltpu.VMEM((2, tq, 2 * LANES), BF16),
                        pltpu.VMEM((2, tq, tk), F32),
                        pltpu.VMEM((2, tq, tk), BF16),
                        pltpu.VMEM((2, tq, LANES), F32),
                        pltpu.VMEM((2, tq, LANES), F32),
                        pltpu.VMEM((2, tq, LANES), F32)],
        compiler_params=_cparams("parallel", "parallel", "arbitrary"),
        name="fox_attention",
    )(qb, kb, vb, qaug, kaug, mask)


def _diff_kernel(cfar_ref, q_ref, k_ref, v_ref, tiles_ref, lam_ref, g_ref, o_ref,
                 qm_ref, s_ref, p_ref, acc_ref, m_ref, al_ref, *, tq, tk, lam_init):
    j_last = pl.program_id(2)
    lane = lax.broadcasted_iota(jnp.int32, (1, LANES), 1)
    upper = lane >= DIFF_DH
    cfar = cfar_ref[pl.program_id(1)]

    def head():
        q2 = q_ref[0]
        zero = jnp.zeros_like(q2)
        qm_ref[0] = jnp.where(upper, zero, q2)
        qm_ref[1] = jnp.where(upper, q2, zero)
        m_ref[...] = jnp.full(m_ref.shape, NEG, F32)
        acc_ref[...] = jnp.zeros(acc_ref.shape, F32)
        p_ref[1] = jnp.zeros(p_ref.shape[1:], p_ref.dtype)
        al_ref[1] = jnp.ones(al_ref.shape[1:], F32)
        scores(0, 0)

    def scores(mm, j):
        start = pl.multiple_of(j * tk, tk)
        s_ref[mm] = _dot_nt(qm_ref[mm], k_ref[0, pl.ds(start, tk), :])

    def probs(mm, tile):
        if tile is None:
            out = _online_softmax(s_ref[mm], m_ref[mm], shift=cfar)
        else:
            out = _online_softmax(s_ref[mm] + tiles_ref[0, tile], m_ref[mm])
        p_ref[mm], al_ref[mm], m_ref[mm] = out

    def pv(mm, j):
        start = pl.multiple_of(j * tk, tk)
        vb = v_ref[0, pl.ds(start, tk), :]
        va = jnp.concatenate([vb, jnp.ones_like(vb)], axis=1)
        al = al_ref[mm]
        acc_ref[mm] = jnp.concatenate([al, al], axis=1) * acc_ref[mm] + _dot(p_ref[mm], va)

    def block(j, tile):
        scores(1, j)
        probs(0, tile)
        pv(1, jnp.maximum(j - 1, 0))
        scores(0, j + 1)
        probs(1, tile)
        pv(0, j)

    def tail(with_previous):
        if with_previous:
            block(j_last - 1, 1)
        scores(1, j_last)
        probs(0, 0)
        pv(1, jnp.maximum(j_last - 1, 0))
        probs(1, 0)
        pv(0, j_last)
        pv(1, j_last)
        lam = _diff_lambda(lam_ref[...], lam_init)
        a0 = acc_ref[0]
        a1 = acc_ref[1]
        o = a0[:, :LANES] / a0[:, LANES:] - lam * (a1[:, :LANES] / a1[:, LANES:])
        ms = jnp.mean(o * o, axis=1, keepdims=True)
        o = o * lax.rsqrt(ms + RMS_EPS) * g_ref[...] * (1.0 - lam_init)
        o_ref[0] = o.astype(o_ref.dtype)

    @pl.when(j_last == 0)
    def _():
        head()
        tail(False)

    @pl.when(j_last >= 1)
    def _():
        _loop_unrolled(head, lambda j: block(j, None), j_last - 1, ATT_UNROLL, lambda: tail(True))


def _t5_bucket(rel):
    nb = REL_BUCKETS // 2
    max_exact = nb // 2
    ret = jnp.where(rel > 0, nb, 0)
    n = jnp.abs(rel)
    nf = jnp.maximum(n, 1).astype(F32)
    large = max_exact + (jnp.log(nf / max_exact) / math.log(REL_MAX_DIST / max_exact)
                         * (nb - max_exact)).astype(jnp.int32)
    large = jnp.minimum(large, nb - 1)
    return ret + jnp.where(n < max_exact, n, large)


def _toeplitz_kernel(w_ref, mask_ref, o_ref, *, tq, width):
    n = w_ref.shape[2]
    for d in range(w_ref.shape[1]):
        w = jnp.broadcast_to(w_ref[0, d:d + 1, :], (tq, n))
        o_ref[0, d] = pltpu.roll(w, 0, 1, stride=1, stride_axis=0)[:, :width] + mask_ref[d]


def _t5_bias_tiles(rel_table, tq, width, offsets, masks):
    nt = len(offsets)
    n = -(-(tq + width) // LANES) * LANES
    m = jnp.arange(n)
    rel = jnp.where(m < width, m, m - n)
    w = jnp.stack([rel_table[_t5_bucket(rel + d)] for d in offsets])
    w = jnp.transpose(w, (2, 0, 1)).astype(F32) * LOG2E
    return pl.pallas_call(
        functools.partial(_toeplitz_kernel, tq=tq, width=width),
        grid=(DIFF_HEADS,),
        in_specs=[pl.BlockSpec((1, nt, n), lambda h: (h, 0, 0)),
                  pl.BlockSpec((nt, tq, width), lambda h: (0, 0, 0))],
        out_specs=pl.BlockSpec((1, nt, tq, width), lambda h: (h, 0, 0, 0)),
        out_shape=jax.ShapeDtypeStruct((DIFF_HEADS, nt, tq, width), F32),
        compiler_params=_cparams("parallel"),
        name="t5_bias_tiles",
    )(w, jnp.asarray(masks, F32))


def _diff_lambda(lp, lam_init):
    return (jnp.exp(jnp.sum(lp[0:1] * lp[1:2], keepdims=True))
            - jnp.exp(jnp.sum(lp[2:3] * lp[3:4], keepdims=True)) + lam_init)


def _diff_attention(qb, kb, vb, rel_table, diff_lambda, subln_g, tq, tk, lam_init):
    b, t, _ = qb.shape
    assert t % tq == 0 and tq == tk and tk >= REL_MAX_DIST and tk % CHUNK == 0
    ii = np.arange(tq)[:, None]
    jj = np.arange(tk)[None, :]
    masks = np.zeros((2, tq, tk), np.float32)
    masks[0] = np.where((jj // CHUNK) <= (ii // CHUNK), 0.0, NEG)
    tiles = _t5_bias_tiles(rel_table, tq, tk, (0, -tk), masks)
    cfar = rel_table[_t5_bucket(jnp.asarray(-2 * REL_MAX_DIST, jnp.int32))].astype(F32) * LOG2E
    return pl.pallas_call(
        functools.partial(_diff_kernel, tq=tq, tk=tk, lam_init=lam_init),
        grid=(b, DIFF_HEADS, t // tq),
        in_specs=[
            pl.BlockSpec(memory_space=pltpu.SMEM),
            pl.BlockSpec((1, tq, LANES), lambda bi, h, qi: (bi, qi, h)),
            pl.BlockSpec((1, t, LANES), lambda bi, h, qi: (bi, 0, h)),
            pl.BlockSpec((1, t, LANES), lambda bi, h, qi: (bi, 0, h)),
            pl.BlockSpec((1, 2, tq, tk), lambda bi, h, qi: (h, 0, 0, 0)),
            pl.BlockSpec((4, DIFF_DH), lambda bi, h, qi: (0, 0)),
            pl.BlockSpec((1, DIFF_VD), lambda bi, h, qi: (0, 0)),
        ],
        out_specs=pl.BlockSpec((1, tq, LANES), lambda bi, h, qi: (bi, qi, h)),
        out_shape=jax.ShapeDtypeStruct((b, t, DIFF_WIDTH), BF16),
        scratch_shapes=[pltpu.VMEM((2, tq, LANES), BF16),
                        pltpu.VMEM((2, tq, tk), F32),
                        pltpu.VMEM((2, tq, tk), BF16),
                        pltpu.VMEM((2, tq, 2 * LANES), F32),
                        pltpu.VMEM((2, tq, LANES), F32),
                        pltpu.VMEM((2, tq, LANES), F32)],
        compiler_params=_cparams("parallel", "parallel", "arbitrary"),
        name="diff_attention",
    )(cfar, qb, kb, vb, tiles, diff_lambda, subln_g.reshape(1, DIFF_VD))


def _two_part_attend(q_h, kt_c, k_n, v_c, v_n, bias_c, bias_n, v_time_minor):
    s_c = _dot(q_h, kt_c.astype(BF16)) + bias_c
    s_n = _dot_nt(q_h, k_n) + bias_n
    m = jnp.maximum(jnp.max(s_c, axis=1, keepdims=True), jnp.max(s_n, axis=1, keepdims=True))
    p_c = jnp.exp2(s_c - m).astype(BF16)
    p_n = jnp.exp2(s_n - m)
    l = jnp.sum(p_c.astype(F32), axis=1, keepdims=True) + jnp.sum(p_n, axis=1, keepdims=True)
    pv_c = _dot_nt(p_c, v_c.astype(BF16)) if v_time_minor else _dot(p_c, v_c.astype(BF16))
    return (pv_c + _dot(p_n.astype(BF16), v_n)) / l


def _fox_decode_kernel(q_ref, kn_ref, vn_ref, kc_ref, vc_ref, fq_ref, ft_ref, o_ref, *, t, past):
    q = q_ref[0]
    kn = kn_ref[0]
    vn = vn_ref[0]
    fblk = fq_ref[0]
    r = lax.broadcasted_iota(jnp.int32, (t, t), 0)
    c = lax.broadcasted_iota(jnp.int32, (t, t), 1)
    causal = jnp.where(c <= r, 0.0, NEG)
    outs = []
    for h in range(FOX_HEADS):
        sl = slice(h * FOX_DH, (h + 1) * FOX_DH)
        fq = fblk[:, h:h + 1]
        fk = ft_ref[0, h:h + 1, :]
        outs.append(_two_part_attend(q[:, sl], kc_ref[0, h], kn[:, sl], vc_ref[0, h], vn[:, sl],
                                     fq - fk[:, :past], fq - fk[:, past:past + t] + causal, True))
    o_ref[0] = jnp.concatenate(outs, axis=1).astype(o_ref.dtype)


def _decode_specs(t, past):
    new = pl.BlockSpec((1, t, 512), lambda bi: (bi, 0, 0))
    cache = pl.BlockSpec((1, 8, 64, past), lambda bi: (bi, 0, 0, 0))
    return new, cache


def _fox_decode(qb, kn, vn, kc, vc, fq, ft, past):
    b, t, _ = qb.shape
    t_kp = ft.shape[2]
    new, cache = _decode_specs(t, past)
    return pl.pallas_call(
        functools.partial(_fox_decode_kernel, t=t, past=past),
        grid=(b,),
        in_specs=[new, new, new, cache, cache,
                  pl.BlockSpec((1, t, FOX_HEADS), lambda bi: (bi, 0, 0)),
                  pl.BlockSpec((1, FOX_HEADS, t_kp), lambda bi: (bi, 0, 0))],
        out_specs=new,
        out_shape=jax.ShapeDtypeStruct((b, t, FOX_WIDTH), BF16),
        compiler_params=_cparams("parallel"),
        name="fox_decode",
    )(qb, kn, vn, kc, vc, fq, ft)


def _diff_decode_kernel(q_ref, kn_ref, vn_ref, kc_ref, vc_ref, bias_ref, lam_ref, g_ref, o_ref,
                        *, t, past, lam_init):
    q = q_ref[0]
    kn = kn_ref[0]
    vn = vn_ref[0]
    lam = _diff_lambda(lam_ref[...], lam_init)
    outs = []
    for h in range(DIFF_HEADS):
        v_c = _head_rows(vc_ref, h)
        v_n = vn[:, h * DIFF_VD:(h + 1) * DIFF_VD]
        bias = bias_ref[h, 0]
        maps = []
        for mm in range(2):
            j = 2 * h + mm
            sl = slice(j * DIFF_DH, (j + 1) * DIFF_DH)
            maps.append(_two_part_attend(q[:, sl], kc_ref[0, j], kn[:, sl], v_c, v_n,
                                         bias[:, :past], bias[:, past:past + t], False))
        a = maps[0] - lam * maps[1]
        ms = jnp.mean(a * a, axis=1, keepdims=True)
        outs.append(a * lax.rsqrt(ms + RMS_EPS) * g_ref[...] * (1.0 - lam_init))
    o_ref[0] = jnp.concatenate(outs, axis=1).astype(o_ref.dtype)


def _diff_decode(qb, kn, vn, kc, vc, rel_table, diff_lambda, subln_g, past, lam_init):
    b, t, _ = qb.shape
    t_kp = -(-(past + t) // LANES) * LANES
    q_pos = past + np.arange(t)[:, None]
    k_pos = np.arange(t_kp)[None, :]
    visible = ((k_pos // CHUNK) <= (q_pos // CHUNK)) & (k_pos < past + t)
    tiles = _t5_bias_tiles(rel_table, t, t_kp, (-past,), np.where(visible, 0.0, NEG)[None])
    new, cache = _decode_specs(t, past)
    vcache = pl.BlockSpec((1, 1, past, DIFF_HEADS, DIFF_VD), lambda bi: (0, bi, 0, 0, 0))
    return pl.pallas_call(
        functools.partial(_diff_decode_kernel, t=t, past=past, lam_init=lam_init),
        grid=(b,),
        in_specs=[new, new, new, cache, vcache,
                  pl.BlockSpec(tiles.shape, lambda bi: (0, 0, 0, 0)),
                  pl.BlockSpec((4, DIFF_DH), lambda bi: (0, 0)),
                  pl.BlockSpec((1, DIFF_VD), lambda bi: (0, 0))],
        out_specs=new,
        out_shape=jax.ShapeDtypeStruct((b, t, DIFF_WIDTH), BF16),
        compiler_params=_cparams("parallel"),
        name="diff_decode",
    )(qb, kn, vn, kc, vc, tiles, diff_lambda, subln_g.reshape(1, DIFF_VD))


def _fill_history(ext_ref, cur, hist_ref, prev_ref):
    ext_ref[:, 0:HIST_ROWS, :] = jnp.where(pl.program_id(1) == 0, hist_ref[...], prev_ref[...])
    ext_ref[:, HIST_ROWS:, :] = cur


def _halo_specs(bb, tm, width):
    cur = pl.BlockSpec((bb, tm, width), lambda b, i: (b, i, 0))
    prev = pl.BlockSpec((bb, HIST_ROWS, width),
                        lambda b, i: (b, jnp.maximum(i * (tm // HIST_ROWS) - 1, 0), 0))
    hist = pl.BlockSpec((bb, HIST_ROWS, width), lambda b, i: (b, 0, 0))
    return cur, prev, hist


def _mix_out(o_ref, x_ref, mix, g_ref, beta_ref):
    bb, tm, d = x_ref.shape
    y = DN_ALPHA * x_ref[...].reshape(bb * tm, d) + mix
    o_ref[...] = _layer_norm(y, g_ref[...], beta_ref[...]).reshape(bb, tm, d)


def _mix_even_kernel(u_ref, prev_ref, hist_ref, wmix_ref, scale_ref, att_ref, w_ref, x_ref,
                     g_ref, beta_ref, o_ref, ext_ref, *, past):
    bb, tm, _ = u_ref.shape
    u = u_ref[...]
    _fill_history(ext_ref, u, hist_ref, prev_ref)
    pos = past + pl.program_id(1) * tm + lax.broadcasted_iota(jnp.int32, (1, tm, 1), 1)
    groups = []
    for g, w in enumerate(POOL_WINDOWS):
        sl = slice(g * POOL_GC, (g + 1) * POOL_GC)
        ug = u[:, :, sl]
        wsum = ug
        for s in range(1, w):
            wsum = wsum + ext_ref[:, HIST_ROWS - s:HIST_ROWS - s + tm, sl]
        cnt = jnp.minimum(w, pos + 1).astype(F32)
        d = (wsum / cnt - ug).reshape(bb * tm, POOL_GC)
        groups.append((_dot(d.astype(BF16), wmix_ref[g]) * scale_ref[:, sl]).astype(BF16))
    pool_y = jnp.concatenate(groups, axis=1)
    att = att_ref[...].reshape(bb * tm, FOX_WIDTH)
    mix = _dot(pool_y, w_ref[0:POOL_WIDTH, :]) + _dot(att, w_ref[POOL_WIDTH:, :])
    _mix_out(o_ref, x_ref, mix, g_ref, beta_ref)


def _mix_odd_kernel(z_ref, prev_ref, hist_ref, bg_ref, cw_ref, att_ref, w_ref, x_ref,
                    g_ref, beta_ref, o_ref, ext_ref):
    bb, tm, _ = z_ref.shape
    z = z_ref[...]
    _fill_history(ext_ref, z, hist_ref, prev_ref)
    y = (ext_ref[:, HIST_ROWS - 2:HIST_ROWS - 2 + tm, :] * cw_ref[0:1, :]
         + ext_ref[:, HIST_ROWS - 1:HIST_ROWS - 1 + tm, :] * cw_ref[1:2, :]
         + z * cw_ref[2:3, :])
    conv_y = (bg_ref[...] * y).astype(BF16).reshape(bb * tm, CONV_CH)
    att = att_ref[...].reshape(bb * tm, DIFF_WIDTH)
    mix = _dot(att, w_ref[0:DIFF_WIDTH, :]) + _dot(conv_y, w_ref[DIFF_WIDTH:, :])
    _mix_out(o_ref, x_ref, mix, g_ref, beta_ref)


def _mix_tiles(b, t):
    tm = min(512, t)
    return max(1, min(b, 512 // tm)), tm


def _mix_ln(kernel_fn, name, local_ins, local_specs, att3, w, x3, g, beta, width):
    b, t, _ = x3.shape
    bb, tm = _mix_tiles(b, t)
    cur = lambda n: pl.BlockSpec((bb, tm, n), lambda bi, i: (bi, i, 0))
    vec = pl.BlockSpec((1, D_MODEL), lambda bi, i: (0, 0))
    return pl.pallas_call(
        kernel_fn,
        grid=(b // bb, t // tm),
        in_specs=local_specs + [cur(512), pl.BlockSpec(w.shape, lambda bi, i: (0, 0)),
                                cur(D_MODEL), vec, vec],
        out_specs=cur(D_MODEL),
        out_shape=jax.ShapeDtypeStruct(x3.shape, F32),
        scratch_shapes=[pltpu.VMEM((bb, HIST_ROWS + tm, width), F32)],
        compiler_params=_cparams("parallel", "parallel"),
        name=name,
    )(*local_ins, att3, w, x3, g.reshape(1, D_MODEL), beta.reshape(1, D_MODEL))


def _mix_even_ln(u3, hist16, wmix, scale, past, fox3, w, x3, g, beta):
    cur, prev, hist = _halo_specs(*_mix_tiles(*u3.shape[:2]), POOL_WIDTH)
    specs = [cur, prev, hist, pl.BlockSpec(wmix.shape, lambda bi, i: (0, 0, 0)),
             pl.BlockSpec((1, POOL_WIDTH), lambda bi, i: (0, 0))]
    return _mix_ln(functools.partial(_mix_even_kernel, past=past), "mix_even_ln",
                   [u3, u3, hist16, wmix, scale.reshape(1, POOL_WIDTH)], specs,
                   fox3, w, x3, g, beta, POOL_WIDTH)


def _mix_odd_ln(z3, hist16, bg3, conv_w, att3, w, x3, g, beta):
    cur, prev, hist = _halo_specs(*_mix_tiles(*z3.shape[:2]), CONV_CH)
    specs = [cur, prev, hist, cur, pl.BlockSpec((CONV_K, CONV_CH), lambda bi, i: (0, 0))]
    return _mix_ln(_mix_odd_kernel, "mix_odd_ln",
                   [z3, z3, hist16, bg3, conv_w], specs, att3, w, x3, g, beta, CONV_CH)


def _head_rows(c_ref, h, bi=0):
    _, bb, n, heads, d = c_ref.shape
    return c_ref.reshape(bb * n * heads, d)[pl.ds(bi * n * heads + h, n, stride=heads), :]


def _mem_kernel(x_ref, wq_ref, mk0_ref, mk1_ref, mv0_ref, mv1_ref, wo_ref, g_ref, beta_ref, o_ref,
                kb_ref, vb_ref):
    bb, tm, _ = x_ref.shape

    @pl.when(pl.program_id(1) == 0)
    def _():
        for bi in range(bb):
            for h in range(MEM_HEADS):
                kb_ref[bi, h] = jnp.concatenate(
                    [_head_rows(mk0_ref, h, bi), _head_rows(mk1_ref, h, bi)], axis=1).astype(BF16)
                vb_ref[bi, h] = jnp.concatenate(
                    [_head_rows(mv0_ref, h, bi), _head_rows(mv1_ref, h, bi)], axis=1).astype(BF16)

    x = x_ref[...].reshape(bb * tm, D_MODEL)
    q = _dot(x.astype(BF16), wq_ref[0])
    qb = (q * (MEM_DH ** -0.5)).astype(BF16)
    rows = []
    for bi in range(bb):
        outs = []
        for h in range(MEM_HEADS):
            s = _dot_nt(qb[bi * tm:(bi + 1) * tm, h * MEM_DH:(h + 1) * MEM_DH], kb_ref[bi, h])
            m = jnp.max(s, axis=1, keepdims=True)
            p = jnp.exp(s - m)
            l = jnp.sum(p, axis=1, keepdims=True)
            o = _dot(p.astype(BF16), vb_ref[bi, h]) / l
            outs.append(o.astype(BF16))
        rows.append(jnp.concatenate(outs, axis=1))
    o_all = rows[0] if bb == 1 else jnp.concatenate(rows, axis=0)
    y = DN_ALPHA * x + _dot(o_all, wo_ref[0])
    o_ref[...] = _layer_norm(y, g_ref[...], beta_ref[...]).reshape(bb, tm, D_MODEL)


def _mem_attend_ln(x3, wq, mk, mv, layer, wo, g, beta):
    b, t, _ = x3.shape
    tm = min(MEM_ROW_TILE, t)
    bb = max(1, min(b, MEM_ROWS_MIN // tm))
    assert b % bb == 0 and MEM_DH == 2 * LANES
    xs = pl.BlockSpec((bb, tm, D_MODEL), lambda bi, i: (bi, i, 0))
    ws = pl.BlockSpec((1, D_MODEL, D_MODEL), lambda bi, i: (layer, 0, 0))
    half = lambda c: pl.BlockSpec((1, bb, N_MEM, MEM_HEADS, LANES), lambda bi, i: (layer, bi, 0, 0, c))
    vec = pl.BlockSpec((1, D_MODEL), lambda bi, i: (0, 0))
    return pl.pallas_call(
        _mem_kernel,
        grid=(b // bb, t // tm),
        in_specs=[xs, ws, half(0), half(1), half(0), half(1), ws, vec, vec],
        out_specs=xs,
        out_shape=jax.ShapeDtypeStruct(x3.shape, F32),
        scratch_shapes=[pltpu.VMEM((bb, MEM_HEADS, N_MEM, MEM_DH), BF16),
                        pltpu.VMEM((bb, MEM_HEADS, N_MEM, MEM_DH), BF16)],
        compiler_params=_cparams("parallel", "arbitrary"),
        name="mem_attend_ln",
    )(x3, wq, mk, mk, mv, mv, wo, g.reshape(1, D_MODEL), beta.reshape(1, D_MODEL))


def _ffn_kernel(x_ref, w1_ref, w2_ref, g_ref, beta_ref, o_ref, *, chunk):
    x = x_ref[...]
    xb = x.astype(BF16)
    acc = jnp.zeros(x.shape, F32)
    for c in range(D_FF // chunk):
        h = _dot(xb, w1_ref[0, :, c * chunk:(c + 1) * chunk])
        h = jnp.square(jnp.maximum(h, 0.0))
        acc = acc + _dot(h.astype(BF16), w2_ref[0, c * chunk:(c + 1) * chunk, :])
    o_ref[...] = _layer_norm(DN_ALPHA * x + acc, g_ref[...], beta_ref[...])


def _ffn_ln(x2, w1, w2, layer, g, beta):
    rows = x2.shape[0]
    tm = _row_tile(rows)
    row = pl.BlockSpec((tm, D_MODEL), lambda i: (i, 0))
    vec = pl.BlockSpec((1, D_MODEL), lambda i: (0, 0))
    once = pl.Buffered(1)
    return pl.pallas_call(
        functools.partial(_ffn_kernel, chunk=1024),
        grid=(rows // tm,),
        in_specs=[row,
                  pl.BlockSpec((1,) + w1.shape[1:], lambda i: (layer, 0, 0), pipeline_mode=once),
                  pl.BlockSpec((1,) + w2.shape[1:], lambda i: (layer, 0, 0), pipeline_mode=once),
                  vec, vec],
        out_specs=row,
        out_shape=jax.ShapeDtypeStruct((rows, D_MODEL), F32),
        compiler_params=_cparams("parallel"),
        name="ffn_ln",
    )(x2, w1, w2, g.reshape(1, D_MODEL), beta.reshape(1, D_MODEL))


def _pad_rows(a, total):
    pad = total - a.shape[1]
    if pad == 0:
        return a
    return jnp.pad(a, ((0, 0), (0, pad)) + ((0, 0),) * (a.ndim - 2))


def _hist16(h):
    return jnp.pad(h, ((0, 0), (HIST_ROWS - h.shape[1], 0), (0, 0)))


def _trunk(x, mem_k, mem_v, pool_h, fk_h, fv_h, flf_h, dk_h, dv_h, conv_h, past, wts):
    b, t, _ = x.shape
    rows = b * t
    t_k = past + t
    if past == 0:
        t_kp = t_k
        cs_rows = 1024
    else:
        t_kp = -(-t_k // LANES) * LANES
        cs_rows = t_kp
    x2 = x.reshape(rows, D_MODEL)

    u, qb, k, kb, v, vb, lf = _proj_even(x2, wts["w_in_even"], wts["w_forget"], wts["b_forget"], t)
    lf3 = lf.reshape(b, t, LANES)
    qb3 = qb.reshape(b, t, FOX_WIDTH)
    kb3 = kb.reshape(b, t, FOX_WIDTH)
    vb3 = vb.reshape(b, t, FOX_WIDTH)
    if past:
        groups = b * FOX_HEADS // LANES
        assert groups * LANES == b * FOX_HEADS
        lf_all = jnp.concatenate([flf_h[0], lf3[:, :, :FOX_HEADS]], axis=1)
        lf_t = jnp.transpose(lf_all, (1, 0, 2)).reshape(t_k, groups, LANES)
        fcum = _cumsum_time(_pad_rows(jnp.transpose(lf_t, (1, 0, 2)), t_kp), cs_rows)[0]
        f_tbh = jnp.transpose(fcum, (1, 0, 2)).reshape(t_kp, b, FOX_HEADS)
        fox_y = _fox_decode(qb3, kb3, vb3, jnp.transpose(fk_h[0], (0, 2, 3, 1)),
                            jnp.transpose(fv_h[0], (0, 2, 3, 1)),
                            jnp.transpose(f_tbh[past:past + t], (1, 0, 2)),
                            jnp.transpose(f_tbh, (1, 2, 0)), past)
    else:
        _, kaug, qaug = _cumsum_time(lf3, cs_rows)
        fox_y = _fox_attention(qb3, kb3, vb3, qaug, kaug, ATT_TILE, ATT_TILE)
    u3 = u.reshape(b, t, POOL_WIDTH)
    x3 = _mix_even_ln(u3, _hist16(pool_h[0]), wts["w_pool_mix"], wts["pool_scale"], past, fox_y,
                      wts["w_out_even"], x2.reshape(b, t, D_MODEL), wts["ln_g"][0, 0], wts["ln_b"][0, 0])
    x2 = _mem_attend_ln(x3, wts["w_mem_q"], mem_k, mem_v, 0,
                        wts["w_mem_o"], wts["ln_g"][0, 1], wts["ln_b"][0, 1]).reshape(rows, D_MODEL)
    x2 = _ffn_ln(x2, wts["w_ff1"], wts["w_ff2"], 0, wts["ln_g"][0, 2], wts["ln_b"][0, 2])
    n_pool = u3[:, t - POOL_HIST:][None]
    n_fk = _cache_logical(k, b, t, (FOX_HEADS, FOX_DH))
    n_fv = _cache_logical(v, b, t, (FOX_HEADS, FOX_DH))
    n_flf = lf3[:, :, :FOX_HEADS][None]

    lam_init = 0.8 - 0.6 * math.exp(-0.3 * 1)
    qb, k, kb, v, vb, bg, z = _proj_odd(x2, wts["w_in_odd"], t)
    qb3 = qb.reshape(b, t, DIFF_QK)
    kb3 = kb.reshape(b, t, DIFF_QK)
    vb3 = vb.reshape(b, t, DIFF_WIDTH)
    if past:
        att = _diff_decode(qb3, kb3, vb3,
                           jnp.transpose(dk_h[0].reshape(b, past, 2 * DIFF_HEADS, DIFF_DH), (0, 2, 3, 1)),
                           dv_h, wts["rel_bias_table"],
                           wts["diff_lambda"], wts["diff_subln_g"], past, lam_init)
    else:
        att = _diff_attention(qb3, kb3, vb3, wts["rel_bias_table"], wts["diff_lambda"],
                              wts["diff_subln_g"], ATT_TILE, ATT_TILE, lam_init)
    z3 = z.reshape(b, t, CONV_CH)
    x3 = _mix_odd_ln(z3, _hist16(conv_h[0]), bg.reshape(b, t, CONV_CH), wts["conv_w"], att,
                     wts["w_out_odd"], x2.reshape(b, t, D_MODEL), wts["ln_g"][1, 0], wts["ln_b"][1, 0])
    x2 = _mem_attend_ln(x3, wts["w_mem_q"], mem_k, mem_v, 1,
                        wts["w_mem_o"], wts["ln_g"][1, 1], wts["ln_b"][1, 1]).reshape(rows, D_MODEL)
    x2 = _ffn_ln(x2, wts["w_ff1"], wts["w_ff2"], 1, wts["ln_g"][1, 2], wts["ln_b"][1, 2])
    n_dk = _cache_logical(k, b, t, (DIFF_HEADS, 2, DIFF_DH))
    n_dv = v.reshape(1, b, t, DIFF_HEADS, DIFF_VD)
    n_conv = z3[:, t - (CONV_K - 1):][None]
    return (x2.reshape(b, t, D_MODEL), n_pool, n_fk, n_fv, n_flf, n_dk, n_dv, n_conv)


def kernel(x_prompt, x_sample, state_pool, cache_fox_k, cache_fox_v, cache_fox_logf,
           cache_diff_k, cache_diff_v, state_conv, cache_mem_k, cache_mem_v, mem_prompt,
           w_in_even, b_forget, w_pool_mix, pool_scale, w_out_even,
           w_in_odd, diff_lambda, diff_subln_g, conv_w, w_out_odd, rel_bias_table,
           w_mem_q, w_mem_k, w_mem_v, w_mem_o, w_ff1, w_ff2, ln_g, ln_b):
    bp = x_prompt.shape[0]
    nmain = POOL_WIDTH + 3 * FOX_WIDTH
    wts = {
        "w_in_even": w_in_even[0, :, :nmain].astype(BF16),
        "w_forget": jnp.pad(jnp.tile(w_in_even[0, :, nmain:], (1, 6)),
                            ((0, 0), (0, LANES - 6 * FOX_HEADS))).astype(BF16),
        "b_forget": jnp.pad(jnp.tile(b_forget[0], 6), (0, LANES - 6 * FOX_HEADS)).reshape(1, LANES).astype(F32),
        "w_pool_mix": w_pool_mix[0].astype(BF16),
        "pool_scale": pool_scale[0],
        "w_out_even": w_out_even[0].astype(BF16),
        "w_in_odd": w_in_odd[0].astype(BF16),
        "diff_lambda": diff_lambda[0],
        "diff_subln_g": diff_subln_g[0],
        "conv_w": conv_w[0],
        "w_out_odd": w_out_odd[0].astype(BF16),
        "rel_bias_table": rel_bias_table,
        "w_mem_q": w_mem_q.astype(BF16),
        "w_mem_o": w_mem_o.astype(BF16),
        "w_ff1": w_ff1.astype(BF16),
        "w_ff2": w_ff2.astype(BF16),
        "ln_g": ln_g,
        "ln_b": ln_b,
    }
    kv = _mem_kv(mem_prompt.reshape(bp * N_MEM, D_MODEL),
                 jnp.stack([w_mem_k, w_mem_v]).astype(BF16))
    kv = kv.reshape(2, DEPTH, bp, N_MEM, D_MODEL)
    p_mem_k = kv[0].reshape(DEPTH, bp, N_MEM, MEM_HEADS, MEM_DH)
    p_mem_v = kv[1].reshape(DEPTH, bp, N_MEM, MEM_HEADS, MEM_DH)
    zeros = lambda *s: jnp.zeros(s, F32)
    (y_prompt, p_pool, p_fox_k, p_fox_v, p_fox_logf, p_diff_k, p_diff_v, p_conv) = _trunk(
        x_prompt, p_mem_k, p_mem_v,
        zeros(1, bp, POOL_HIST, POOL_WIDTH), None, None, None, None, None,
        zeros(1, bp, CONV_K - 1, CONV_CH), 0, wts)
    bs = x_sample.shape[0]
    (y_sample, s_pool, s_fox_k, s_fox_v, s_fox_logf, s_diff_k, s_diff_v, s_conv) = _trunk(
        x_sample, cache_mem_k, cache_mem_v,
        state_pool, cache_fox_k, cache_fox_v, cache_fox_logf,
        cache_diff_k, cache_diff_v, state_conv, cache_fox_k.shape[2], wts)
    return (y_prompt, y_sample,
            p_pool, p_fox_k, p_fox_v, p_fox_logf, p_diff_k, p_diff_v, p_conv, p_mem_k, p_mem_v,
            s_pool, s_fox_k, s_fox_v, s_fox_logf, s_diff_k, s_diff_v, s_conv)
```

```python
import functools
import math

import numpy as np
import jax
import jax.numpy as jnp
from jax import lax
from jax.experimental import pallas as pl
from jax.experimental.pallas import tpu as pltpu

F32 = jnp.float32
BF16 = jnp.bfloat16

D_MODEL = 1024
DEPTH = 2
CHUNK = 64
POOL_WIDTH = 512
POOL_GC = 128
POOL_WINDOWS = (2, 4, 8, 16)
POOL_HIST = 15
FOX_HEADS = 8
FOX_DH = 64
FOX_WIDTH = 512
DIFF_HEADS = 4
DIFF_DH = 64
DIFF_VD = 128
DIFF_QK = 512
DIFF_WIDTH = 512
CONV_CH = 512
CONV_K = 3
D_FF = 4096
N_MEM = 256
MEM_HEADS = 4
MEM_DH = 256
REL_BUCKETS = 32
REL_MAX_DIST = 128
DN_ALPHA = (2 * DEPTH) ** 0.25
LN_EPS = 1e-5
RMS_EPS = 1e-5
NEG = -1e30
LOG2E = math.log2(math.e)

LANES = 128
HIST_ROWS = 16
ATT_TILE = 512
ATT_UNROLL = 4
MEM_ROW_TILE = 1024
MEM_ROWS_MIN = 64
VMEM_LIMIT = 56 * 1024 * 1024


def _cparams(*sem):
    return pltpu.CompilerParams(dimension_semantics=sem, vmem_limit_bytes=VMEM_LIMIT)


def _dot(a, b):
    return jnp.dot(a, b, preferred_element_type=F32)


def _dot_nt(a, b):
    return lax.dot_general(a, b, (((1,), (1,)), ((), ())), preferred_element_type=F32)


def _layer_norm(y, g, b):
    mu = jnp.mean(y, axis=-1, keepdims=True)
    d = y - mu
    var = jnp.mean(d * d, axis=-1, keepdims=True)
    return d * lax.rsqrt(var + LN_EPS) * g + b


def _row_tile(rows):
    return min(512, rows)


def _store_cache(o_ref, y):
    if o_ref.shape[0] == 1:
        o_ref[0] = y.T
        return
    cols = []
    for h in range(o_ref.shape[1]):
        c = y[:, (h // 2) * LANES:(h // 2 + 1) * LANES]
        cols.append(pltpu.roll(c, LANES // 2, 1) if h % 2 else c)
    t = jnp.swapaxes(jnp.stack(cols, axis=0), 0, 1)
    o_ref[...] = t[:, :, :o_ref.shape[2]]


def _cache_out(rows, tm, seq):
    if seq % tm == 0 and tm % LANES == 0:
        nb = seq // tm
        return (pl.BlockSpec((1, 512, tm), lambda i: (i // nb, 0, i % nb)),
                jax.ShapeDtypeStruct((rows // seq, 512, seq), F32))
    return (pl.BlockSpec((tm, 8, 64), lambda i: (i, 0, 0)),
            jax.ShapeDtypeStruct((rows, 8, 64), F32))


def _cache_logical(c, b, t, tail):
    if c.shape[0] == b and c.shape[1] == 512:
        c = jnp.transpose(c.reshape((b,) + tail + (t,)), (0, len(tail) + 1) + tuple(range(1, len(tail) + 1)))
        return c[None]
    return c.reshape((1, b, t) + tail)


def _proj_even_kernel(x_ref, w_ref, wf_ref, bf_ref,
                      u_ref, q_ref, k_ref, kb_ref, v_ref, vb_ref, lf_ref):
    xb = x_ref[...].astype(BF16)

    def mm(c):
        return _dot(xb, w_ref[:, c * 512:(c + 1) * 512])

    u_ref[...] = mm(0)
    q_ref[...] = (mm(1) * (FOX_DH ** -0.5 * LOG2E)).astype(BF16)
    k = mm(2)
    _store_cache(k_ref, k)
    kb_ref[...] = k.astype(BF16)
    v = mm(3)
    _store_cache(v_ref, v)
    vb_ref[...] = v.astype(BF16)
    z = _dot(xb, wf_ref[...]) + bf_ref[...]
    lf_ref[...] = jnp.minimum(z, 0.0) - jnp.log1p(jnp.exp(-jnp.abs(z)))


def _proj_even(x2, w, wf, bf, seq):
    rows = x2.shape[0]
    tm = _row_tile(rows)
    row = lambda n: pl.BlockSpec((tm, n), lambda i: (i, 0))
    full = lambda a: pl.BlockSpec(a.shape, lambda i: (0,) * a.ndim)
    f32o = jax.ShapeDtypeStruct((rows, 512), F32)
    bf16o = jax.ShapeDtypeStruct((rows, 512), BF16)
    heads, headso = _cache_out(rows, tm, seq)
    return pl.pallas_call(
        _proj_even_kernel,
        grid=(rows // tm,),
        in_specs=[row(D_MODEL), full(w), full(wf), full(bf)],
        out_specs=[row(512), row(512), heads, row(512), heads, row(512), row(LANES)],
        out_shape=[f32o, bf16o, headso, bf16o, headso, bf16o,
                   jax.ShapeDtypeStruct((rows, LANES), F32)],
        compiler_params=_cparams("parallel"),
        name="proj_even",
    )(x2, w, wf, bf)


def _proj_odd_kernel(x_ref, w_ref, q_ref, k_ref, kb_ref, v_ref, vb_ref, bg_ref, z_ref):
    xb = x_ref[...].astype(BF16)

    def mm(c):
        return _dot(xb, w_ref[:, c * 512:(c + 1) * 512])

    q_ref[...] = (mm(0) * (DIFF_DH ** -0.5 * LOG2E)).astype(BF16)
    k = mm(1)
    _store_cache(k_ref, k)
    kb_ref[...] = k.astype(BF16)
    v = mm(2)
    tm = v.shape[0]
    v_rows = v_ref.reshape(tm * DIFF_HEADS, DIFF_VD)
    for h in range(DIFF_HEADS):
        v_rows[pl.ds(h, tm, stride=DIFF_HEADS), :] = v[:, h * DIFF_VD:(h + 1) * DIFF_VD]
    vb_ref[...] = v.astype(BF16)
    bg_ref[...] = mm(3)
    z_ref[...] = mm(4) * mm(5)


def _proj_odd(x2, w, seq):
    rows = x2.shape[0]
    tm = _row_tile(rows)
    kspec, kshape = _cache_out(rows, tm, seq)
    row = lambda n: pl.BlockSpec((tm, n), lambda i: (i, 0))
    f32o = jax.ShapeDtypeStruct((rows, 512), F32)
    bf16o = jax.ShapeDtypeStruct((rows, 512), BF16)
    return pl.pallas_call(
        _proj_odd_kernel,
        grid=(rows // tm,),
        in_specs=[row(D_MODEL), pl.BlockSpec(w.shape, lambda i: (0, 0))],
        out_specs=[row(512), kspec, row(512),
                   pl.BlockSpec((tm, DIFF_HEADS, DIFF_VD), lambda i: (i, 0, 0))] + [row(512)] * 3,
        out_shape=[bf16o, kshape, bf16o,
                   jax.ShapeDtypeStruct((rows, DIFF_HEADS, DIFF_VD), F32), bf16o, f32o, f32o],
        compiler_params=_cparams("parallel"),
        name="proj_odd",
    )(x2, w)


def _mem_kv_kernel(x_ref, wk_ref, wv_ref, k_ref, v_ref):
    xb = x_ref[...].astype(BF16)
    k_ref[0] = _dot(xb, wk_ref[0].astype(BF16))
    v_ref[0] = _dot(xb, wv_ref[0].astype(BF16))


def _mem_kv(mem2, w_k, w_v):
    rows = mem2.shape[0]
    ws = pl.BlockSpec((1, D_MODEL, D_MODEL), lambda l: (l, 0, 0))
    os = pl.BlockSpec((1, rows, D_MODEL), lambda l: (l, 0, 0))
    out = jax.ShapeDtypeStruct((DEPTH, rows, D_MODEL), F32)
    return pl.pallas_call(
        _mem_kv_kernel,
        grid=(DEPTH,),
        in_specs=[pl.BlockSpec((rows, D_MODEL), lambda l: (0, 0)), ws, ws],
        out_specs=[os, os],
        out_shape=[out, out],
        compiler_params=_cparams("parallel"),
        name="mem_kv",
    )(mem2, w_k, w_v)


def _cumsum_kernel(x_ref, f_ref, kaug_ref, qaug_ref, carry_ref, *, rows):
    @pl.when(pl.program_id(1) == 0)
    def _():
        carry_ref[...] = jnp.zeros_like(carry_ref)

    r = lax.broadcasted_iota(jnp.int32, (LANES, LANES), 0)
    c = lax.broadcasted_iota(jnp.int32, (LANES, LANES), 1)
    tri = (r >= c).astype(F32)
    grp = c // FOX_HEADS
    carry = carry_ref[...]
    for s in range(rows // LANES):
        sl = slice(s * LANES, (s + 1) * LANES)
        cs = jnp.dot(tri, x_ref[0, sl, :], preferred_element_type=F32,
                     precision=lax.Precision.HIGHEST) + carry
        carry = cs[LANES - 1:LANES, :]
        f = cs * LOG2E
        f_ref[0, sl, :] = f
        hi = f.astype(BF16).astype(F32)
        mid = (f - hi).astype(BF16).astype(F32)
        lo = (f - hi - mid).astype(BF16).astype(F32)
        piece = jnp.where(grp % 3 == 0, hi, jnp.where(grp % 3 == 1, mid, lo))
        kaug_ref[0, sl, :] = jnp.where(grp < 3, -piece, jnp.where(grp < 6, 1.0, 0.0)).astype(BF16)
        qaug_ref[0, sl, :] = jnp.where(grp < 3, 1.0, jnp.where(grp < 6, piece, 0.0)).astype(BF16)
    carry_ref[...] = carry


def _cumsum_time(x, rows):
    b, t, _ = x.shape
    spec = pl.BlockSpec((1, rows, LANES), lambda i, j: (i, j, 0))
    return pl.pallas_call(
        functools.partial(_cumsum_kernel, rows=rows),
        grid=(b, t // rows),
        in_specs=[spec],
        out_specs=[spec] * 3,
        out_shape=[jax.ShapeDtypeStruct(x.shape, F32), jax.ShapeDtypeStruct(x.shape, BF16),
                   jax.ShapeDtypeStruct(x.shape, BF16)],
        scratch_shapes=[pltpu.VMEM((1, LANES), F32)],
        compiler_params=_cparams("parallel", "arbitrary"),
        name="cumsum_time",
    )(x)


def _loop_unrolled(head, block, n, unroll, tail):
    def body(i, carry):
        for u in range(unroll):
            block(unroll * i + u)
        return carry

    @pl.when(n >= unroll)
    def _():
        head()
        body(0, 0)

    @pl.when(n < unroll)
    def _():
        head()

    lax.fori_loop(1, n // unroll, body, 0)
    for left in range(unroll):
        @pl.when(n % unroll == left)
        def _(left=left):
            for u in range(left):
                block(n - left + u)
            tail()


def _online_softmax(s, m_old, shift=None):
    row_max = jnp.max(s, axis=1, keepdims=True)
    if shift is not None:
        row_max = row_max + shift
    m_new = jnp.maximum(m_old, row_max)
    m_sub = m_new if shift is None else m_new - shift
    p = jnp.exp2(s - jnp.concatenate([m_sub] * (s.shape[1] // LANES), axis=1))
    return p.astype(BF16), jnp.exp2(m_old - m_new), m_new


def _fox_kernel(q_ref, k_ref, v_ref, qaug_ref, kaug_ref, mask_ref, o_ref,
                qcat_ref, s_ref, p_ref, acc_ref, m_ref, al_ref, *, tq, tk):
    pair = pl.program_id(1)
    j_last = pl.program_id(2)
    lane = lax.broadcasted_iota(jnp.int32, (1, LANES), 1)
    upper = lane >= FOX_DH
    sels = (jnp.logical_not(upper), upper)

    def head():
        q2 = q_ref[0]
        qa = qaug_ref[0]
        for hh in range(2):
            own = jnp.logical_and(lane % FOX_HEADS == 2 * pair + hh, lane < 6 * FOX_HEADS)
            qcat_ref[hh] = jnp.concatenate([jnp.where(sels[hh], q2, jnp.zeros_like(q2)),
                                            jnp.where(own, qa, jnp.zeros_like(qa))], axis=1)
        m_ref[...] = jnp.full(m_ref.shape, NEG, F32)
        acc_ref[...] = jnp.zeros(acc_ref.shape, F32)
        p_ref[1] = jnp.zeros(p_ref.shape[1:], p_ref.dtype)
        al_ref[1] = jnp.ones(al_ref.shape[1:], F32)
        scores(0, 0)

    def scores(hh, j):
        rows = pl.ds(pl.multiple_of(j * tk, tk), tk)
        kcat = jnp.concatenate([k_ref[0, rows, :], kaug_ref[0, rows, :]], axis=1)
        s_ref[hh] = _dot_nt(qcat_ref[hh], kcat)

    def probs(hh, masked):
        s = s_ref[hh]
        if masked:
            s = s + mask_ref[...]
        p_ref[hh], al_ref[hh], m_ref[hh] = _online_softmax(s, m_ref[hh])

    def pv(hh, j):
        vb = v_ref[0, pl.ds(pl.multiple_of(j * tk, tk), tk), :]
        va = jnp.where(sels[hh], vb, jnp.ones_like(vb))
        acc_ref[hh] = al_ref[hh] * acc_ref[hh] + _dot(p_ref[hh], va)

    def block(j):
        scores(1, j)
        probs(0, False)
        pv(1, jnp.maximum(j - 1, 0))
        scores(0, j + 1)
        probs(1, False)
        pv(0, j)

    def tail():
        scores(1, j_last)
        probs(0, True)
        pv(1, jnp.maximum(j_last - 1, 0))
        probs(1, True)
        pv(0, j_last)
        pv(1, j_last)
        a0 = acc_ref[0]
        a1 = acc_ref[1]
        o0 = a0 / pltpu.roll(a0, FOX_DH, 1)
        o1 = a1 / pltpu.roll(a1, FOX_DH, 1)
        o_ref[0] = jnp.where(upper, o1, o0).astype(o_ref.dtype)

    _loop_unrolled(head, block, j_last, ATT_UNROLL, tail)


def _fox_attention(qb, kb, vb, qaug, kaug, tq, tk):
    b, t, _ = qb.shape
    assert t % tq == 0 and tq == tk
    ii = np.arange(tq)[:, None]
    jj = np.arange(tk)[None, :]
    mask = jnp.asarray(np.where(jj <= ii, 0.0, NEG), F32)
    return pl.pallas_call(
        functools.partial(_fox_kernel, tq=tq, tk=tk),
        grid=(b, FOX_HEADS // 2, t // tq),
        in_specs=[
            pl.BlockSpec((1, tq, LANES), lambda bi, p, qi: (bi, qi, p)),
            pl.BlockSpec((1, t, LANES), lambda bi, p, qi: (bi, 0, p)),
            pl.BlockSpec((1, t, LANES), lambda bi, p, qi: (bi, 0, p)),
            pl.BlockSpec((1, tq, LANES), lambda bi, p, qi: (bi, qi, 0)),
            pl.BlockSpec((1, t, LANES), lambda bi, p, qi: (bi, 0, 0)),
            pl.BlockSpec((tq, tk), lambda bi, p, qi: (0, 0)),
        ],
        out_specs=pl.BlockSpec((1, tq, LANES), lambda bi, p, qi: (bi, qi, p)),
        out_shape=jax.ShapeDtypeStruct((b, t, FOX_WIDTH), BF16),
        scratch_shapes=[pltpu.VMEM((2, tq, 2 * LANES), BF16),
                        pltpu.VMEM((2, tq, tk), F32),
                        pltpu.VMEM((2, tq, tk), BF16),
                        pltpu.VMEM((2, tq, LANES), F32),
                        pltpu.VMEM((2, tq, LANES), F32),
                        pltpu.VMEM((2, tq, LANES), F32)],
        compiler_params=_cparams("parallel", "parallel", "arbitrary"),
        name="fox_attention",
    )(qb, kb, vb, qaug, kaug, mask)


def _diff_kernel(cfar_ref, q_ref, k_ref, v_ref, tiles_ref, lam_ref, g_ref, o_ref,
                 qm_ref, s_ref, p_ref, acc_ref, m_ref, al_ref, *, tq, tk, lam_init):
    j_last = pl.program_id(2)
    lane = lax.broadcasted_iota(jnp.int32, (1, LANES), 1)
    upper = lane >= DIFF_DH
    cfar = cfar_ref[pl.program_id(1)]

    def head():
        q2 = q_ref[0]
        zero = jnp.zeros_like(q2)
        qm_ref[0] = jnp.where(upper, zero, q2)
        qm_ref[1] = jnp.where(upper, q2, zero)
        m_ref[...] = jnp.full(m_ref.shape, NEG, F32)
        acc_ref[...] = jnp.zeros(acc_ref.shape, F32)
        p_ref[1] = jnp.zeros(p_ref.shape[1:], p_ref.dtype)
        al_ref[1] = jnp.ones(al_ref.shape[1:], F32)
        scores(0, 0)

    def scores(mm, j):
        start = pl.multiple_of(j * tk, tk)
        s_ref[mm] = _dot_nt(qm_ref[mm], k_ref[0, pl.ds(start, tk), :])

    def probs(mm, tile):
        if tile is None:
            out = _online_softmax(s_ref[mm], m_ref[mm], shift=cfar)
        else:
            out = _online_softmax(s_ref[mm] + tiles_ref[0, tile], m_ref[mm])
        p_ref[mm], al_ref[mm], m_ref[mm] = out

    def pv(mm, j):
        start = pl.multiple_of(j * tk, tk)
        vb = v_ref[0, pl.ds(start, tk), :]
        va = jnp.concatenate([vb, jnp.ones_like(vb)], axis=1)
        al = al_ref[mm]
        acc_ref[mm] = jnp.concatenate([al, al], axis=1) * acc_ref[mm] + _dot(p_ref[mm], va)

    def block(j, tile):
        scores(1, j)
        probs(0, tile)
        pv(1, jnp.maximum(j - 1, 0))
        scores(0, j + 1)
        probs(1, tile)
        pv(0, j)

    def tail(with_previous):
        if with_previous:
            block(j_last - 1, 1)
        scores(1, j_last)
        probs(0, 0)
        pv(1, jnp.maximum(j_last - 1, 0))
        probs(1, 0)
        pv(0, j_last)
        pv(1, j_last)
        lam = _diff_lambda(lam_ref[...], lam_init)
        a0 = acc_ref[0]
        a1 = acc_ref[1]
        o = a0[:, :LANES] / a0[:, LANES:] - lam * (a1[:, :LANES] / a1[:, LANES:])
        ms = jnp.mean(o * o, axis=1, keepdims=True)
        o = o * lax.rsqrt(ms + RMS_EPS) * g_ref[...] * (1.0 - lam_init)
        o_ref[0] = o.astype(o_ref.dtype)

    @pl.when(j_last == 0)
    def _():
        head()
        tail(False)

    @pl.when(j_last >= 1)
    def _():
        _loop_unrolled(head, lambda j: block(j, None), j_last - 1, ATT_UNROLL, lambda: tail(True))


def _t5_bucket(rel):
    nb = REL_BUCKETS // 2
    max_exact = nb // 2
    ret = jnp.where(rel > 0, nb, 0)
    n = jnp.abs(rel)
    nf = jnp.maximum(n, 1).astype(F32)
    large = max_exact + (jnp.log(nf / max_exact) / math.log(REL_MAX_DIST / max_exact)
                         * (nb - max_exact)).astype(jnp.int32)
    large = jnp.minimum(large, nb - 1)
    return ret + jnp.where(n < max_exact, n, large)


def _toeplitz_kernel(w_ref, mask_ref, o_ref, *, tq, width):
    n = w_ref.shape[2]
    for d in range(w_ref.shape[1]):
        w = jnp.broadcast_to(w_ref[0, d:d + 1, :], (tq, n))
        o_ref[0, d] = pltpu.roll(w, 0, 1, stride=1, stride_axis=0)[:, :width] + mask_ref[d]


def _t5_bias_tiles(rel_table, tq, width, offsets, masks):
    nt = len(offsets)
    n = -(-(tq + width) // LANES) * LANES
    m = jnp.arange(n)
    rel = jnp.where(m < width, m, m - n)
    w = jnp.stack([rel_table[_t5_bucket(rel + d)] for d in offsets])
    w = jnp.transpose(w, (2, 0, 1)).astype(F32) * LOG2E
    return pl.pallas_call(
        functools.partial(_toeplitz_kernel, tq=tq, width=width),
        grid=(DIFF_HEADS,),
        in_specs=[pl.BlockSpec((1, nt, n), lambda h: (h, 0, 0)),
                  pl.BlockSpec((nt, tq, width), lambda h: (0, 0, 0))],
        out_specs=pl.BlockSpec((1, nt, tq, width), lambda h: (h, 0, 0, 0)),
        out_shape=jax.ShapeDtypeStruct((DIFF_HEADS, nt, tq, width), F32),
        compiler_params=_cparams("parallel"),
        name="t5_bias_tiles",
    )(w, jnp.asarray(masks, F32))


def _diff_lambda(lp, lam_init):
    return (jnp.exp(jnp.sum(lp[0:1] * lp[1:2], keepdims=True))
            - jnp.exp(jnp.sum(lp[2:3] * lp[3:4], keepdims=True)) + lam_init)


def _diff_attention(qb, kb, vb, rel_table, diff_lambda, subln_g, tq, tk, lam_init):
    b, t, _ = qb.shape
    assert t % tq == 0 and tq == tk and tk >= REL_MAX_DIST and tk % CHUNK == 0
    ii = np.arange(tq)[:, None]
    jj = np.arange(tk)[None, :]
    masks = np.zeros((2, tq, tk), np.float32)
    masks[0] = np.where((jj // CHUNK) <= (ii // CHUNK), 0.0, NEG)
    tiles = _t5_bias_tiles(rel_table, tq, tk, (0, -tk), masks)
    cfar = rel_table[_t5_bucket(jnp.asarray(-2 * REL_MAX_DIST, jnp.int32))].astype(F32) * LOG2E
    return pl.pallas_call(
        functools.partial(_diff_kernel, tq=tq, tk=tk, lam_init=lam_init),
        grid=(b, DIFF_HEADS, t // tq),
        in_specs=[
            pl.BlockSpec(memory_space=pltpu.SMEM),
            pl.BlockSpec((1, tq, LANES), lambda bi, h, qi: (bi, qi, h)),
            pl.BlockSpec((1, t, LANES), lambda bi, h, qi: (bi, 0, h)),
            pl.BlockSpec((1, t, LANES), lambda bi, h, qi: (bi, 0, h)),
            pl.BlockSpec((1, 2, tq, tk), lambda bi, h, qi: (h, 0, 0, 0)),
            pl.BlockSpec((4, DIFF_DH), lambda bi, h, qi: (0, 0)),
            pl.BlockSpec((1, DIFF_VD), lambda bi, h, qi: (0, 0)),
        ],
        out_specs=pl.BlockSpec((1, tq, LANES), lambda bi, h, qi: (bi, qi, h)),
        out_shape=jax.ShapeDtypeStruct((b, t, DIFF_WIDTH), BF16),
        scratch_shapes=[pltpu.VMEM((2, tq, LANES), BF16),
                        pltpu.VMEM((2, tq, tk), F32),
                        pltpu.VMEM((2, tq, tk), BF16),
                        pltpu.VMEM((2, tq, 2 * LANES), F32),
                        pltpu.VMEM((2, tq, LANES), F32),
                        pltpu.VMEM((2, tq, LANES), F32)],
        compiler_params=_cparams("parallel", "parallel", "arbitrary"),
        name="diff_attention",
    )(cfar, qb, kb, vb, tiles, diff_lambda, subln_g.reshape(1, DIFF_VD))


def _two_part_attend(q_h, kt_c, k_n, v_c, v_n, bias_c, bias_n, v_time_minor):
    s_c = _dot(q_h, kt_c.astype(BF16)) + bias_c
    s_n = _dot_nt(q_h, k_n) + bias_n
    m = jnp.maximum(jnp.max(s_c, axis=1, keepdims=True), jnp.max(s_n, axis=1, keepdims=True))
    p_c = jnp.exp2(s_c - m).astype(BF16)
    p_n = jnp.exp2(s_n - m)
    l = jnp.sum(p_c.astype(F32), axis=1, keepdims=True) + jnp.sum(p_n, axis=1, keepdims=True)
    pv_c = _dot_nt(p_c, v_c.astype(BF16)) if v_time_minor else _dot(p_c, v_c.astype(BF16))
    return (pv_c + _dot(p_n.astype(BF16), v_n)) / l


def _fox_decode_kernel(q_ref, kn_ref, vn_ref, kc_ref, vc_ref, fq_ref, ft_ref, o_ref, *, t, past):
    q = q_ref[0]
    kn = kn_ref[0]
    vn = vn_ref[0]
    fblk = fq_ref[0]
    r = lax.broadcasted_iota(jnp.int32, (t, t), 0)
    c = lax.broadcasted_iota(jnp.int32, (t, t), 1)
    causal = jnp.where(c <= r, 0.0, NEG)
    outs = []
    for h in range(FOX_HEADS):
        sl = slice(h * FOX_DH, (h + 1) * FOX_DH)
        fq = fblk[:, h:h + 1]
        fk = ft_ref[0, h:h + 1, :]
        outs.append(_two_part_attend(q[:, sl], kc_ref[0, h], kn[:, sl], vc_ref[0, h], vn[:, sl],
                                     fq - fk[:, :past], fq - fk[:, past:past + t] + causal, True))
    o_ref[0] = jnp.concatenate(outs, axis=1).astype(o_ref.dtype)


def _decode_specs(t, past):
    new = pl.BlockSpec((1, t, 512), lambda bi: (bi, 0, 0))
    cache = pl.BlockSpec((1, 8, 64, past), lambda bi: (bi, 0, 0, 0))
    return new, cache


def _fox_decode(qb, kn, vn, kc, vc, fq, ft, past):
    b, t, _ = qb.shape
    t_kp = ft.shape[2]
    new, cache = _decode_specs(t, past)
    return pl.pallas_call(
        functools.partial(_fox_decode_kernel, t=t, past=past),
        grid=(b,),
        in_specs=[new, new, new, cache, cache,
                  pl.BlockSpec((1, t, FOX_HEADS), lambda bi: (bi, 0, 0)),
                  pl.BlockSpec((1, FOX_HEADS, t_kp), lambda bi: (bi, 0, 0))],
        out_specs=new,
        out_shape=jax.ShapeDtypeStruct((b, t, FOX_WIDTH), BF16),
        compiler_params=_cparams("parallel"),
        name="fox_decode",
    )(qb, kn, vn, kc, vc, fq, ft)


def _diff_decode_kernel(q_ref, kn_ref, vn_ref, kc_ref, vc_ref, bias_ref, lam_ref, g_ref, o_ref,
                        *, t, past, lam_init):
    q = q_ref[0]
    kn = kn_ref[0]
    vn = vn_ref[0]
    lam = _diff_lambda(lam_ref[...], lam_init)
    outs = []
    for h in range(DIFF_HEADS):
        v_c = _head_rows(vc_ref, h)
        v_n = vn[:, h * DIFF_VD:(h + 1) * DIFF_VD]
        bias = bias_ref[h, 0]
        maps = []
        for mm in range(2):
            j = 2 * h + mm
            sl = slice(j * DIFF_DH, (j + 1) * DIFF_DH)
            maps.append(_two_part_attend(q[:, sl], kc_ref[0, j], kn[:, sl], v_c, v_n,
                                         bias[:, :past], bias[:, past:past + t], False))
        a = maps[0] - lam * maps[1]
        ms = jnp.mean(a * a, axis=1, keepdims=True)
        outs.append(a * lax.rsqrt(ms + RMS_EPS) * g_ref[...] * (1.0 - lam_init))
    o_ref[0] = jnp.concatenate(outs, axis=1).astype(o_ref.dtype)


def _diff_decode(qb, kn, vn, kc, vc, rel_table, diff_lambda, subln_g, past, lam_init):
    b, t, _ = qb.shape
    t_kp = -(-(past + t) // LANES) * LANES
    q_pos = past + np.arange(t)[:, None]
    k_pos = np.arange(t_kp)[None, :]
    visible = ((k_pos // CHUNK) <= (q_pos // CHUNK)) & (k_pos < past + t)
    tiles = _t5_bias_tiles(rel_table, t, t_kp, (-past,), np.where(visible, 0.0, NEG)[None])
    new, cache = _decode_specs(t, past)
    vcache = pl.BlockSpec((1, 1, past, DIFF_HEADS, DIFF_VD), lambda bi: (0, bi, 0, 0, 0))
    return pl.pallas_call(
        functools.partial(_diff_decode_kernel, t=t, past=past, lam_init=lam_init),
        grid=(b,),
        in_specs=[new, new, new, cache, vcache,
                  pl.BlockSpec(tiles.shape, lambda bi: (0, 0, 0, 0)),
                  pl.BlockSpec((4, DIFF_DH), lambda bi: (0, 0)),
                  pl.BlockSpec((1, DIFF_VD), lambda bi: (0, 0))],
        out_specs=new,
        out_shape=jax.ShapeDtypeStruct((b, t, DIFF_WIDTH), BF16),
        compiler_params=_cparams("parallel"),
        name="diff_decode",
    )(qb, kn, vn, kc, vc, tiles, diff_lambda, subln_g.reshape(1, DIFF_VD))


def _fill_history(ext_ref, cur, hist_ref, prev_ref):
    ext_ref[:, 0:HIST_ROWS, :] = jnp.where(pl.program_id(1) == 0, hist_ref[...], prev_ref[...])
    ext_ref[:, HIST_ROWS:, :] = cur


def _halo_specs(bb, tm, width):
    cur = pl.BlockSpec((bb, tm, width), lambda b, i: (b, i, 0))
    prev = pl.BlockSpec((bb, HIST_ROWS, width),
                        lambda b, i: (b, jnp.maximum(i * (tm // HIST_ROWS) - 1, 0), 0))
    hist = pl.BlockSpec((bb, HIST_ROWS, width), lambda b, i: (b, 0, 0))
    return cur, prev, hist


def _mix_out(o_ref, x_ref, mix, g_ref, beta_ref):
    bb, tm, d = x_ref.shape
    y = DN_ALPHA * x_ref[...].reshape(bb * tm, d) + mix
    o_ref[...] = _layer_norm(y, g_ref[...], beta_ref[...]).reshape(bb, tm, d)


def _mix_even_kernel(u_ref, prev_ref, hist_ref, wmix_ref, scale_ref, att_ref, w_ref, x_ref,
                     g_ref, beta_ref, o_ref, ext_ref, *, past):
    bb, tm, _ = u_ref.shape
    u = u_ref[...]
    _fill_history(ext_ref, u, hist_ref, prev_ref)
    pos = past + pl.program_id(1) * tm + lax.broadcasted_iota(jnp.int32, (1, tm, 1), 1)
    groups = []
    for g, w in enumerate(POOL_WINDOWS):
        sl = slice(g * POOL_GC, (g + 1) * POOL_GC)
        ug = u[:, :, sl]
        wsum = ug
        for s in range(1, w):
            wsum = wsum + ext_ref[:, HIST_ROWS - s:HIST_ROWS - s + tm, sl]
        cnt = jnp.minimum(w, pos + 1).astype(F32)
        d = (wsum / cnt - ug).reshape(bb * tm, POOL_GC)
        groups.append((_dot(d.astype(BF16), wmix_ref[g]) * scale_ref[:, sl]).astype(BF16))
    pool_y = jnp.concatenate(groups, axis=1)
    att = att_ref[...].reshape(bb * tm, FOX_WIDTH)
    mix = _dot(pool_y, w_ref[0:POOL_WIDTH, :]) + _dot(att, w_ref[POOL_WIDTH:, :])
    _mix_out(o_ref, x_ref, mix, g_ref, beta_ref)


def _mix_odd_kernel(z_ref, prev_ref, hist_ref, bg_ref, cw_ref, att_ref, w_ref, x_ref,
                    g_ref, beta_ref, o_ref, ext_ref):
    bb, tm, _ = z_ref.shape
    z = z_ref[...]
    _fill_history(ext_ref, z, hist_ref, prev_ref)
    y = (ext_ref[:, HIST_ROWS - 2:HIST_ROWS - 2 + tm, :] * cw_ref[0:1, :]
         + ext_ref[:, HIST_ROWS - 1:HIST_ROWS - 1 + tm, :] * cw_ref[1:2, :]
         + z * cw_ref[2:3, :])
    conv_y = (bg_ref[...] * y).astype(BF16).reshape(bb * tm, CONV_CH)
    att = att_ref[...].reshape(bb * tm, DIFF_WIDTH)
    mix = _dot(att, w_ref[0:DIFF_WIDTH, :]) + _dot(conv_y, w_ref[DIFF_WIDTH:, :])
    _mix_out(o_ref, x_ref, mix, g_ref, beta_ref)


def _mix_tiles(b, t):
    tm = min(512, t)
    return max(1, min(b, 512 // tm)), tm


def _mix_ln(kernel_fn, name, local_ins, local_specs, att3, w, x3, g, beta, width):
    b, t, _ = x3.shape
    bb, tm = _mix_tiles(b, t)
    cur = lambda n: pl.BlockSpec((bb, tm, n), lambda bi, i: (bi, i, 0))
    vec = pl.BlockSpec((1, D_MODEL), lambda bi, i: (0, 0))
    return pl.pallas_call(
        kernel_fn,
        grid=(b // bb, t // tm),
        in_specs=local_specs + [cur(512), pl.BlockSpec(w.shape, lambda bi, i: (0, 0)),
                                cur(D_MODEL), vec, vec],
        out_specs=cur(D_MODEL),
        out_shape=jax.ShapeDtypeStruct(x3.shape, F32),
        scratch_shapes=[pltpu.VMEM((bb, HIST_ROWS + tm, width), F32)],
        compiler_params=_cparams("parallel", "parallel"),
        name=name,
    )(*local_ins, att3, w, x3, g.reshape(1, D_MODEL), beta.reshape(1, D_MODEL))


def _mix_even_ln(u3, hist16, wmix, scale, past, fox3, w, x3, g, beta):
    cur, prev, hist = _halo_specs(*_mix_tiles(*u3.shape[:2]), POOL_WIDTH)
    specs = [cur, prev, hist, pl.BlockSpec(wmix.shape, lambda bi, i: (0, 0, 0)),
             pl.BlockSpec((1, POOL_WIDTH), lambda bi, i: (0, 0))]
    return _mix_ln(functools.partial(_mix_even_kernel, past=past), "mix_even_ln",
                   [u3, u3, hist16, wmix, scale.reshape(1, POOL_WIDTH)], specs,
                   fox3, w, x3, g, beta, POOL_WIDTH)


def _mix_odd_ln(z3, hist16, bg3, conv_w, att3, w, x3, g, beta):
    cur, prev, hist = _halo_specs(*_mix_tiles(*z3.shape[:2]), CONV_CH)
    specs = [cur, prev, hist, cur, pl.BlockSpec((CONV_K, CONV_CH), lambda bi, i: (0, 0))]
    return _mix_ln(_mix_odd_kernel, "mix_odd_ln",
                   [z3, z3, hist16, bg3, conv_w], specs, att3, w, x3, g, beta, CONV_CH)


def _head_rows(c_ref, h, bi=0):
    _, bb, n, heads, d = c_ref.shape
    return c_ref.reshape(bb * n * heads, d)[pl.ds(bi * n * heads + h, n, stride=heads), :]


def _mem_kernel(x_ref, wq_ref, mk0_ref, mk1_ref, mv0_ref, mv1_ref, wo_ref, g_ref, beta_ref, o_ref,
                kb_ref, vb_ref):
    bb, tm, _ = x_ref.shape

    @pl.when(pl.program_id(1) == 0)
    def _():
        for bi in range(bb):
            for h in range(MEM_HEADS):
                kb_ref[bi, h] = jnp.concatenate(
                    [_head_rows(mk0_ref, h, bi), _head_rows(mk1_ref, h, bi)], axis=1).astype(BF16)
                vb_ref[bi, h] = jnp.concatenate(
                    [_head_rows(mv0_ref, h, bi), _head_rows(mv1_ref, h, bi)], axis=1).astype(BF16)

    x = x_ref[...].reshape(bb * tm, D_MODEL)
    q = _dot(x.astype(BF16), wq_ref[0])
    qb = (q * (MEM_DH ** -0.5)).astype(BF16)
    rows = []
    for bi in range(bb):
        outs = []
        for h in range(MEM_HEADS):
            s = _dot_nt(qb[bi * tm:(bi + 1) * tm, h * MEM_DH:(h + 1) * MEM_DH], kb_ref[bi, h])
            m = jnp.max(s, axis=1, keepdims=True)
            p = jnp.exp(s - m)
            l = jnp.sum(p, axis=1, keepdims=True)
            o = _dot(p.astype(BF16), vb_ref[bi, h]) / l
            outs.append(o.astype(BF16))
        rows.append(jnp.concatenate(outs, axis=1))
    o_all = rows[0] if bb == 1 else jnp.concatenate(rows, axis=0)
    y = DN_ALPHA * x + _dot(o_all, wo_ref[0])
    o_ref[...] = _layer_norm(y, g_ref[...], beta_ref[...]).reshape(bb, tm, D_MODEL)


def _mem_attend_ln(x3, wq, mk, mv, layer, wo, g, beta):
    b, t, _ = x3.shape
    tm = min(MEM_ROW_TILE, t)
    bb = max(1, min(b, MEM_ROWS_MIN // tm))
    assert b % bb == 0 and MEM_DH == 2 * LANES
    xs = pl.BlockSpec((bb, tm, D_MODEL), lambda bi, i: (bi, i, 0))
    ws = pl.BlockSpec((1, D_MODEL, D_MODEL), lambda bi, i: (layer, 0, 0))
    half = lambda c: pl.BlockSpec((1, bb, N_MEM, MEM_HEADS, LANES), lambda bi, i: (layer, bi, 0, 0, c))
    vec = pl.BlockSpec((1, D_MODEL), lambda bi, i: (0, 0))
    return pl.pallas_call(
        _mem_kernel,
        grid=(b // bb, t // tm),
        in_specs=[xs, ws, half(0), half(1), half(0), half(1), ws, vec, vec],
        out_specs=xs,
        out_shape=jax.ShapeDtypeStruct(x3.shape, F32),
        scratch_shapes=[pltpu.VMEM((bb, MEM_HEADS, N_MEM, MEM_DH), BF16),
                        pltpu.VMEM((bb, MEM_HEADS, N_MEM, MEM_DH), BF16)],
        compiler_params=_cparams("parallel", "arbitrary"),
        name="mem_attend_ln",
    )(x3, wq, mk, mk, mv, mv, wo, g.reshape(1, D_MODEL), beta.reshape(1, D_MODEL))


def _ffn_kernel(x_ref, w1_ref, w2_ref, g_ref, beta_ref, o_ref, *, chunk):
    x = x_ref[...]
    xb = x.astype(BF16)
    acc = jnp.zeros(x.shape, F32)
    for c in range(D_FF // chunk):
        h = _dot(xb, w1_ref[0, :, c * chunk:(c + 1) * chunk])
        h = jnp.square(jnp.maximum(h, 0.0))
        acc = acc + _dot(h.astype(BF16), w2_ref[0, c * chunk:(c + 1) * chunk, :])
    o_ref[...] = _layer_norm(DN_ALPHA * x + acc, g_ref[...], beta_ref[...])


def _ffn_ln(x2, w1, w2, layer, g, beta):
    rows = x2.shape[0]
    tm = _row_tile(rows)
    row = pl.BlockSpec((tm, D_MODEL), lambda i: (i, 0))
    vec = pl.BlockSpec((1, D_MODEL), lambda i: (0, 0))
    once = pl.Buffered(1)
    return pl.pallas_call(
        functools.partial(_ffn_kernel, chunk=1024),
        grid=(rows // tm,),
        in_specs=[row,
                  pl.BlockSpec((1,) + w1.shape[1:], lambda i: (layer, 0, 0), pipeline_mode=once),
                  pl.BlockSpec((1,) + w2.shape[1:], lambda i: (layer, 0, 0), pipeline_mode=once),
                  vec, vec],
        out_specs=row,
        out_shape=jax.ShapeDtypeStruct((rows, D_MODEL), F32),
        compiler_params=_cparams("parallel"),
        name="ffn_ln",
    )(x2, w1, w2, g.reshape(1, D_MODEL), beta.reshape(1, D_MODEL))


def _pad_rows(a, total):
    pad = total - a.shape[1]
    if pad == 0:
        return a
    return jnp.pad(a, ((0, 0), (0, pad)) + ((0, 0),) * (a.ndim - 2))


def _hist16(h):
    return jnp.pad(h, ((0, 0), (HIST_ROWS - h.shape[1], 0), (0, 0)))


def _trunk(x, mem_k, mem_v, pool_h, fk_h, fv_h, flf_h, dk_h, dv_h, conv_h, past, wts):
    b, t, _ = x.shape
    rows = b * t
    t_k = past + t
    if past == 0:
        t_kp = t_k
        cs_rows = 1024
    else:
        t_kp = -(-t_k // LANES) * LANES
        cs_rows = t_kp
    x2 = x.reshape(rows, D_MODEL)

    u, qb, k, kb, v, vb, lf = _proj_even(x2, wts["w_in_even"], wts["w_forget"], wts["b_forget"], t)
    lf3 = lf.reshape(b, t, LANES)
    qb3 = qb.reshape(b, t, FOX_WIDTH)
    kb3 = kb.reshape(b, t, FOX_WIDTH)
    vb3 = vb.reshape(b, t, FOX_WIDTH)
    if past:
        groups = b * FOX_HEADS // LANES
        assert groups * LANES == b * FOX_HEADS
        lf_all = jnp.concatenate([flf_h[0], lf3[:, :, :FOX_HEADS]], axis=1)
        lf_t = jnp.transpose(lf_all, (1, 0, 2)).reshape(t_k, groups, LANES)
        fcum = _cumsum_time(_pad_rows(jnp.transpose(lf_t, (1, 0, 2)), t_kp), cs_rows)[0]
        f_tbh = jnp.transpose(fcum, (1, 0, 2)).reshape(t_kp, b, FOX_HEADS)
        fox_y = _fox_decode(qb3, kb3, vb3, jnp.transpose(fk_h[0], (0, 2, 3, 1)),
                            jnp.transpose(fv_h[0], (0, 2, 3, 1)),
                            jnp.transpose(f_tbh[past:past + t], (1, 0, 2)),
                            jnp.transpose(f_tbh, (1, 2, 0)), past)
    else:
        _, kaug, qaug = _cumsum_time(lf3, cs_rows)
        fox_y = _fox_attention(qb3, kb3, vb3, qaug, kaug, ATT_TILE, ATT_TILE)
    u3 = u.reshape(b, t, POOL_WIDTH)
    x3 = _mix_even_ln(u3, _hist16(pool_h[0]), wts["w_pool_mix"], wts["pool_scale"], past, fox_y,
                      wts["w_out_even"], x2.reshape(b, t, D_MODEL), wts["ln_g"][0, 0], wts["ln_b"][0, 0])
    x2 = _mem_attend_ln(x3, wts["w_mem_q"], mem_k, mem_v, 0,
                        wts["w_mem_o"], wts["ln_g"][0, 1], wts["ln_b"][0, 1]).reshape(rows, D_MODEL)
    x2 = _ffn_ln(x2, wts["w_ff1"], wts["w_ff2"], 0, wts["ln_g"][0, 2], wts["ln_b"][0, 2])
    n_pool = u3[:, t - POOL_HIST:][None]
    n_fk = _cache_logical(k, b, t, (FOX_HEADS, FOX_DH))
    n_fv = _cache_logical(v, b, t, (FOX_HEADS, FOX_DH))
    n_flf = lf3[:, :, :FOX_HEADS][None]

    lam_init = 0.8 - 0.6 * math.exp(-0.3 * 1)
    qb, k, kb, v, vb, bg, z = _proj_odd(x2, wts["w_in_odd"], t)
    qb3 = qb.reshape(b, t, DIFF_QK)
    kb3 = kb.reshape(b, t, DIFF_QK)
    vb3 = vb.reshape(b, t, DIFF_WIDTH)
    if past:
        att = _diff_decode(qb3, kb3, vb3,
                           jnp.transpose(dk_h[0].reshape(b, past, 2 * DIFF_HEADS, DIFF_DH), (0, 2, 3, 1)),
                           dv_h, wts["rel_bias_table"],
                           wts["diff_lambda"], wts["diff_subln_g"], past, lam_init)
    else:
        att = _diff_attention(qb3, kb3, vb3, wts["rel_bias_table"], wts["diff_lambda"],
                              wts["diff_subln_g"], ATT_TILE, ATT_TILE, lam_init)
    z3 = z.reshape(b, t, CONV_CH)
    x3 = _mix_odd_ln(z3, _hist16(conv_h[0]), bg.reshape(b, t, CONV_CH), wts["conv_w"], att,
                     wts["w_out_odd"], x2.reshape(b, t, D_MODEL), wts["ln_g"][1, 0], wts["ln_b"][1, 0])
    x2 = _mem_attend_ln(x3, wts["w_mem_q"], mem_k, mem_v, 1,
                        wts["w_mem_o"], wts["ln_g"][1, 1], wts["ln_b"][1, 1]).reshape(rows, D_MODEL)
    x2 = _ffn_ln(x2, wts["w_ff1"], wts["w_ff2"], 1, wts["ln_g"][1, 2], wts["ln_b"][1, 2])
    n_dk = _cache_logical(k, b, t, (DIFF_HEADS, 2, DIFF_DH))
    n_dv = v.reshape(1, b, t, DIFF_HEADS, DIFF_VD)
    n_conv = z3[:, t - (CONV_K - 1):][None]
    return (x2.reshape(b, t, D_MODEL), n_pool, n_fk, n_fv, n_flf, n_dk, n_dv, n_conv)


def kernel(x_prompt, x_sample, state_pool, cache_fox_k, cache_fox_v, cache_fox_logf,
           cache_diff_k, cache_diff_v, state_conv, cache_mem_k, cache_mem_v, mem_prompt,
           w_in_even, b_forget, w_pool_mix, pool_scale, w_out_even,
           w_in_odd, diff_lambda, diff_subln_g, conv_w, w_out_odd, rel_bias_table,
           w_mem_q, w_mem_k, w_mem_v, w_mem_o, w_ff1, w_ff2, ln_g, ln_b):
    bp = x_prompt.shape[0]
    nmain = POOL_WIDTH + 3 * FOX_WIDTH
    wts = {
        "w_in_even": w_in_even[0, :, :nmain].astype(BF16),
        "w_forget": jnp.pad(jnp.tile(w_in_even[0, :, nmain:], (1, 6)),
                            ((0, 0), (0, LANES - 6 * FOX_HEADS))).astype(BF16),
        "b_forget": jnp.pad(jnp.tile(b_forget[0], 6), (0, LANES - 6 * FOX_HEADS)).reshape(1, LANES).astype(F32),
        "w_pool_mix": w_pool_mix[0].astype(BF16),
        "pool_scale": pool_scale[0],
        "w_out_even": w_out_even[0].astype(BF16),
        "w_in_odd": w_in_odd[0].astype(BF16),
        "diff_lambda": diff_lambda[0],
        "diff_subln_g": diff_subln_g[0],
        "conv_w": conv_w[0],
        "w_out_odd": w_out_odd[0].astype(BF16),
        "rel_bias_table": rel_bias_table,
        "w_mem_q": w_mem_q.astype(BF16),
        "w_mem_o": w_mem_o.astype(BF16),
        "w_ff1": w_ff1.astype(BF16),
        "w_ff2": w_ff2.astype(BF16),
        "ln_g": ln_g,
        "ln_b": ln_b,
    }
    mem_k, mem_v = _mem_kv(mem_prompt.reshape(bp * N_MEM, D_MODEL), w_mem_k, w_mem_v)
    p_mem_k = mem_k.reshape(DEPTH, bp, N_MEM, MEM_HEADS, MEM_DH)
    p_mem_v = mem_v.reshape(DEPTH, bp, N_MEM, MEM_HEADS, MEM_DH)
    zeros = lambda *s: jnp.zeros(s, F32)
    (y_prompt, p_pool, p_fox_k, p_fox_v, p_fox_logf, p_diff_k, p_diff_v, p_conv) = _trunk(
        x_prompt, p_mem_k, p_mem_v,
        zeros(1, bp, POOL_HIST, POOL_WIDTH), None, None, None, None, None,
        zeros(1, bp, CONV_K - 1, CONV_CH), 0, wts)
    bs = x_sample.shape[0]
    (y_sample, s_pool, s_fox_k, s_fox_v, s_fox_logf, s_diff_k, s_diff_v, s_conv) = _trunk(
        x_sample, cache_mem_k, cache_mem_v,
        state_pool, cache_fox_k, cache_fox_v, cache_fox_logf,
        cache_diff_k, cache_diff_v, state_conv, cache_fox_k.shape[2], wts)
    return (y_prompt, y_sample,
            p_pool, p_fox_k, p_fox_v, p_fox_logf, p_diff_k, p_diff_v, p_conv, p_mem_k, p_mem_v,
            s_pool, s_fox_k, s_fox_v, s_fox_logf, s_diff_k, s_diff_v, s_conv)
```
